```python
import math, functools
import jax, jax.numpy as jnp
from jax import lax
import numpy as np

D_MODEL = 1024
BATCH = 4
SEQ = 4096
DEPTH = 1
DEC_BATCH = 128
DEC_SEQ = 1
PAST_LEN = 2048
PAGE_SIZE = 128

N_META = 16
N_HEADS = 16
N_KV_HEADS = 4
HEAD_DIM = 64
ATTN_WIDTH = N_HEADS * HEAD_DIM
IDX_HEADS = 4
IDX_DIM = 64
TOP_K_MAX = 256
Q_BLOCK = 128
D_INNER = 2 * D_MODEL
SSD_HEAD_DIM = 64
SSD_HEADS = D_INNER // SSD_HEAD_DIM
SSD_GROUPS = 4
HEADS_PER_GROUP = SSD_HEADS // SSD_GROUPS
D_STATE = 128
CONV_W = 4
CONV_DIM = D_INNER + 2 * SSD_GROUPS * D_STATE
CHUNK = 128
D_FF = ((8 * D_MODEL // 3 + 127) // 128) * 128
EPS = 1e-6
IN_SPLITS = (ATTN_WIDTH, N_KV_HEADS * HEAD_DIM, N_KV_HEADS * HEAD_DIM, IDX_HEADS * IDX_DIM, IDX_DIM, IDX_HEADS,
             D_INNER, CONV_DIM, SSD_HEADS, D_MODEL, D_MODEL)
IN_COLS = sum(IN_SPLITS)

kernel_name = "hybrid_dsa_ssd_macaron_step"


def rmsnorm(x, g):
    xf = x.astype(jnp.float32)
    y = xf * lax.rsqrt(jnp.mean(xf * xf, axis=-1, keepdims=True) + EPS)
    return (y * g.astype(jnp.float32)).astype(x.dtype)


def swiglu(x, g, w_gate, w_up, w_down):
    h = rmsnorm(x, g)
    return (jax.nn.silu(h @ w_gate) * (h @ w_up)) @ w_down


def index_scores(qi, ki, wi, qpos, kpos):
    dots = jnp.einsum('bqhd,bsd->bqhs', qi, ki).astype(jnp.float32) * IDX_DIM ** -0.5
    score = jnp.einsum('bqh,bqhs->bqs', wi.astype(jnp.float32) * IDX_HEADS ** -0.5, jax.nn.relu(dots))
    return jnp.where(kpos[None, None, :] <= qpos[None, :, None], score, -jnp.inf)


def sparse_attend(q, ks, vs, valid):
    b, nq = q.shape[:2]
    qg = q.reshape(b, nq, N_KV_HEADS, N_HEADS // N_KV_HEADS, HEAD_DIM)
    s = jnp.einsum('bqhgd,bqnhd->bqhgn', qg, ks).astype(jnp.float32) * HEAD_DIM ** -0.5
    s = jnp.where(valid[:, :, None, None, :], s, -jnp.inf)
    p = jax.nn.softmax(s, axis=-1).astype(vs.dtype)
    o = jnp.einsum('bqhgn,bqnhd->bqhgd', p, vs)
    return o.reshape(b, nq, ATTN_WIDTH)


def dsa_prompt(q, k, v, qi, ki, wi):
    b, L = q.shape[:2]
    k_top = min(TOP_K_MAX, L // 4)
    n_blk = -(-L // Q_BLOCK)
    pad = n_blk * Q_BLOCK - L

    def blocks(a):
        a = jnp.pad(a, [(0, 0), (0, pad)] + [(0, 0)] * (a.ndim - 2))
        return jnp.moveaxis(a.reshape((b, n_blk, Q_BLOCK) + a.shape[2:]), 1, 0)

    kpos = jnp.arange(L)
    bidx = jnp.arange(b)[:, None, None]

    def one_block(args):
        qb, qib, wib, start = args
        qpos = start + jnp.arange(Q_BLOCK)
        _, idx = lax.top_k(index_scores(qib, ki, wib, qpos, kpos), k_top)
        valid = idx <= qpos[None, :, None]
        return sparse_attend(qb, k[bidx, idx], v[bidx, idx], valid)

    starts = jnp.arange(n_blk) * Q_BLOCK
    out = lax.map(one_block, (blocks(q), blocks(qi), blocks(wi), starts))
    return jnp.moveaxis(out, 0, 1).reshape(b, n_blk * Q_BLOCK, ATTN_WIDTH)[:, :L]


def dsa_sample(q, k, v, qi, ki, wi, cache_k, cache_v, cache_ki, page_table):
    b, T = q.shape[:2]
    P = page_table.shape[1] * PAGE_SIZE
    S = P + T
    k_top = min(TOP_K_MAX, S // 4)
    ki_all = jnp.concatenate([cache_ki[page_table].reshape(b, P, IDX_DIM), ki.astype(cache_ki.dtype)], axis=1)
    qpos = P + jnp.arange(T)
    _, idx = lax.top_k(index_scores(qi, ki_all, wi, qpos, jnp.arange(S)), k_top)
    valid = idx <= qpos[None, :, None]
    bidx = jnp.arange(b)[:, None, None]
    pidx = jnp.minimum(idx, P - 1)
    phys = page_table[bidx, pidx // PAGE_SIZE]
    off = pidx % PAGE_SIZE
    nidx = jnp.clip(idx - P, 0, T - 1)
    from_past = (idx < P)[..., None, None]
    ks = jnp.where(from_past, cache_k[phys, off], k[bidx, nidx])
    vs = jnp.where(from_past, cache_v[phys, off], v[bidx, nidx])
    return sparse_attend(q, ks, vs, valid)


def causal_conv(xbc, prev, w, bias):
    T = xbc.shape[1]
    xp = jnp.concatenate([prev.astype(xbc.dtype), xbc], axis=1)
    out = bias
    for j in range(CONV_W):
        out = out + xp[:, j:j + T] * w[j]
    return jax.nn.silu(out), xp[:, xp.shape[1] - (CONV_W - 1):]


def ssd_scan(x, dt, A, Bm, Cm, h0, chunk):
    b, T, G, E, P = x.shape
    N = Bm.shape[-1]
    nc = T // chunk
    f32 = jnp.float32
    x = x.astype(f32).reshape(b, nc, chunk, G, E, P)
    dt = dt.reshape(b, nc, chunk, G, E)
    Bm = Bm.astype(f32).reshape(b, nc, chunk, G, N)
    Cm = Cm.astype(f32).reshape(b, nc, chunk, G, N)
    acs = jnp.cumsum(dt * A, axis=2)
    causal = jnp.tril(jnp.ones((chunk, chunk), bool))
    seg = acs[:, :, :, None] - acs[:, :, None, :]
    Lmat = jnp.exp(jnp.where(causal[:, :, None, None], seg, -jnp.inf))
    CB = jnp.einsum('bzign,bzjgn->bzijg', Cm, Bm)
    M = CB[..., None] * Lmat * dt[:, :, None]
    y_diag = jnp.einsum('bzijge,bzjgep->bzigep', M, x)
    xw = x * (jnp.exp(acs[:, :, -1:] - acs) * dt)[..., None]
    states = jnp.einsum('bzjgn,bzjgep->bzgepn', Bm, xw)
    chunk_decay = jnp.exp(acs[:, :, -1])

    def step(h, inp):
        s, d = inp
        return h * d[..., None, None] + s, h

    h_last, h_prev = lax.scan(step, h0.astype(f32), (jnp.moveaxis(states, 1, 0), jnp.moveaxis(chunk_decay, 1, 0)))
    h_prev = jnp.moveaxis(h_prev, 0, 1)
    y_off = jnp.einsum('bzign,bzgepn->bzigep', Cm, h_prev) * jnp.exp(acs)[..., None]
    return (y_diag + y_off).reshape(b, T, G, E, P), h_last


def ssd_mixer(xbc, dt_raw, z, h0, dt_bias, a_log, d_skip, g_norm, pad_front, chunk):
    b, T = xbc.shape[:2]
    f32 = jnp.float32
    xs, Bm, Cm = jnp.split(xbc, [D_INNER, D_INNER + SSD_GROUPS * D_STATE], axis=-1)
    xs = xs.astype(f32).reshape(b, T, SSD_GROUPS, HEADS_PER_GROUP, SSD_HEAD_DIM)
    Bm = Bm.reshape(b, T, SSD_GROUPS, D_STATE)
    Cm = Cm.reshape(b, T, SSD_GROUPS, D_STATE)
    dt = jax.nn.softplus(dt_raw.astype(f32) + dt_bias.astype(f32)).reshape(b, T, SSD_GROUPS, HEADS_PER_GROUP)
    A = -jnp.exp(a_log.astype(f32)).reshape(SSD_GROUPS, HEADS_PER_GROUP)
    pad_back = (-(pad_front + T)) % chunk

    def padt(a):
        return jnp.pad(a, [(0, 0), (pad_front, pad_back)] + [(0, 0)] * (a.ndim - 2))

    h0g = h0.reshape(b, SSD_GROUPS, HEADS_PER_GROUP, SSD_HEAD_DIM, D_STATE)
    y, h_last = ssd_scan(padt(xs), padt(dt), A, padt(Bm), padt(Cm), h0g, chunk)
    y = y[:, pad_front:pad_front + T] + d_skip.astype(f32).reshape(SSD_GROUPS, HEADS_PER_GROUP)[:, :, None] * xs
    y = y.reshape(b, T, D_INNER) * jax.nn.silu(z.astype(f32))
    yg = y.reshape(b, T, SSD_GROUPS, D_INNER // SSD_GROUPS)
    yg = yg * lax.rsqrt(jnp.mean(yg * yg, axis=-1, keepdims=True) + EPS)
    y = yg.reshape(b, T, D_INNER) * g_norm.astype(f32)
    return y.astype(xbc.dtype), h_last.reshape(b, SSD_HEADS, SSD_HEAD_DIM, D_STATE).astype(h0.dtype)


def layer_forward(x, attend, conv_prev, h0, pad_front, chunk,
                  g_ffn1, w_ffn1_gate, w_ffn1_up, w_ffn1_down, g_mix, w_in, g_q, g_k,
                  conv_w, conv_b, dt_bias, a_log, d_skip, g_ssd_norm,
                  w_attn_out, w_ssd_out, w_o, g_ffn2, w_ffn2_gate, w_ffn2_up, w_ffn2_down):
    x = x + 0.5 * swiglu(x, g_ffn1, w_ffn1_gate, w_ffn1_up, w_ffn1_down)
    h = rmsnorm(x, g_mix)
    b, T = h.shape[:2]
    offsets = np.cumsum(IN_SPLITS)[:-1].tolist()
    q, k, v, qi, ki, wi, z, xbc, dt_raw, ga, gs = jnp.split(h @ w_in, offsets, axis=-1)
    q = rmsnorm(q.reshape(b, T, N_HEADS, HEAD_DIM), g_q)
    k = rmsnorm(k.reshape(b, T, N_KV_HEADS, HEAD_DIM), g_k)
    v = v.reshape(b, T, N_KV_HEADS, HEAD_DIM)
    qi = qi.reshape(b, T, IDX_HEADS, IDX_DIM)
    attn = attend(q, k, v, qi, ki, wi)
    xbc, conv_state = causal_conv(xbc, conv_prev, conv_w, conv_b)
    ssd, ssm_state = ssd_mixer(xbc, dt_raw, z, h0, dt_bias, a_log, d_skip, g_ssd_norm, pad_front, chunk)
    mixed = jax.nn.sigmoid(ga) * (attn @ w_attn_out) + jax.nn.sigmoid(gs) * (ssd @ w_ssd_out)
    x = x + mixed @ w_o
    x = x + 0.5 * swiglu(x, g_ffn2, w_ffn2_gate, w_ffn2_up, w_ffn2_down)
    return x, k, v, ki, ssm_state, conv_state


def setup_inputs(seed: int = 0) -> dict:
    key = jax.random.key(seed)
    keys = iter(jax.random.split(key, 40))

    def nrm(shape, scale=1.0):
        return scale * jax.random.normal(next(keys), shape, jnp.float32)

    def gain(shape):
        return 1.0 + nrm(shape, 0.05)

    n_pages = PAST_LEN // PAGE_SIZE
    n_used = DEC_BATCH * n_pages
    n_pool = n_used + (n_used + 3) // 4
    page_table = jax.random.permutation(next(keys), n_pool)[:n_used].reshape(DEC_BATCH, n_pages).astype(jnp.int32)
    dt0 = jnp.exp(jax.random.uniform(next(keys), (DEPTH, SSD_HEADS), jnp.float32, math.log(1e-3), math.log(1e-1)))
    dt_bias = dt0 + jnp.log(-jnp.expm1(-dt0))
    a_log = jnp.log(jax.random.uniform(next(keys), (DEPTH, SSD_HEADS), jnp.float32, 1.0, 16.0))
    return {
        "x_prompt": nrm((BATCH, SEQ, D_MODEL)),
        "x_sample": nrm((DEC_BATCH, DEC_SEQ, D_MODEL)),
        "cache_k": nrm((DEPTH, n_pool, PAGE_SIZE, N_KV_HEADS, HEAD_DIM)),
        "cache_v": nrm((DEPTH, n_pool, PAGE_SIZE, N_KV_HEADS, HEAD_DIM)),
        "cache_kidx": nrm((DEPTH, n_pool, PAGE_SIZE, IDX_DIM)),
        "state_ssm": nrm((DEPTH, DEC_BATCH, SSD_HEADS, SSD_HEAD_DIM, D_STATE), 0.1),
        "state_conv": nrm((DEPTH, DEC_BATCH, CONV_W - 1, CONV_DIM)),
        "page_table": page_table,
        "meta_tokens": nrm((N_META, D_MODEL)),
        "g_ffn1": gain((DEPTH, D_MODEL)),
        "w_ffn1_gate": nrm((DEPTH, D_MODEL, D_FF), D_MODEL ** -0.5),
        "w_ffn1_up": nrm((DEPTH, D_MODEL, D_FF), D_MODEL ** -0.5),
        "w_ffn1_down": nrm((DEPTH, D_FF, D_MODEL), D_FF ** -0.5),
        "g_mix": gain((DEPTH, D_MODEL)),
        "w_in": nrm((DEPTH, D_MODEL, IN_COLS), D_MODEL ** -0.5),
        "g_q": gain((DEPTH, HEAD_DIM)),
        "g_k": gain((DEPTH, HEAD_DIM)),
        "conv_w": nrm((DEPTH, CONV_W, CONV_DIM), CONV_W ** -0.5),
        "conv_b": nrm((DEPTH, CONV_DIM), 0.02),
        "dt_bias": dt_bias,
        "a_log": a_log,
        "d_skip": gain((DEPTH, SSD_HEADS)),
        "g_ssd_norm": gain((DEPTH, D_INNER)),
        "w_attn_out": nrm((DEPTH, ATTN_WIDTH, D_MODEL), ATTN_WIDTH ** -0.5),
        "w_ssd_out": nrm((DEPTH, D_INNER, D_MODEL), D_INNER ** -0.5),
        "w_o": nrm((DEPTH, D_MODEL, D_MODEL), D_MODEL ** -0.5),
        "g_ffn2": gain((DEPTH, D_MODEL)),
        "w_ffn2_gate": nrm((DEPTH, D_MODEL, D_FF), D_MODEL ** -0.5),
        "w_ffn2_up": nrm((DEPTH, D_MODEL, D_FF), D_MODEL ** -0.5),
        "w_ffn2_down": nrm((DEPTH, D_FF, D_MODEL), D_FF ** -0.5),
    }


def reference(x_prompt, x_sample, cache_k, cache_v, cache_kidx, state_ssm, state_conv, page_table,
              meta_tokens, g_ffn1, w_ffn1_gate, w_ffn1_up, w_ffn1_down, g_mix, w_in, g_q, g_k,
              conv_w, conv_b, dt_bias, a_log, d_skip, g_ssd_norm, w_attn_out, w_ssd_out, w_o,
              g_ffn2, w_ffn2_gate, w_ffn2_up, w_ffn2_down):
    b = x_prompt.shape[0]
    xp = jnp.concatenate([jnp.broadcast_to(meta_tokens[None].astype(x_prompt.dtype), (b, N_META, D_MODEL)), x_prompt], axis=1)
    xs = x_sample
    kp, vp, kip, hp, cp = [], [], [], [], []
    ks, vs, kis, hs, cs = [], [], [], [], []
    for l in range(DEPTH):
        lw = (g_ffn1[l], w_ffn1_gate[l], w_ffn1_up[l], w_ffn1_down[l], g_mix[l], w_in[l], g_q[l], g_k[l],
              conv_w[l], conv_b[l], dt_bias[l], a_log[l], d_skip[l], g_ssd_norm[l],
              w_attn_out[l], w_ssd_out[l], w_o[l], g_ffn2[l], w_ffn2_gate[l], w_ffn2_up[l], w_ffn2_down[l])
        xp, k1, v1, ki1, h1, c1 = layer_forward(
            xp, dsa_prompt, jnp.zeros((b, CONV_W - 1, CONV_DIM), xp.dtype),
            jnp.zeros((b, SSD_HEADS, SSD_HEAD_DIM, D_STATE), xp.dtype), (-N_META) % CHUNK, CHUNK, *lw)
        kp.append(k1); vp.append(v1); kip.append(ki1); hp.append(h1); cp.append(c1)
        attend_s = functools.partial(dsa_sample, cache_k=cache_k[l], cache_v=cache_v[l],
                                     cache_ki=cache_kidx[l], page_table=page_table)
        xs, k2, v2, ki2, h2, c2 = layer_forward(
            xs, attend_s, state_conv[l], state_ssm[l], 0, min(CHUNK, xs.shape[1]), *lw)
        ks.append(k2); vs.append(v2); kis.append(ki2); hs.append(h2); cs.append(c2)
    y_prompt = xp[:, N_META:]
    y_sample = xs
    return (y_prompt, y_sample,
            jnp.stack(kp), jnp.stack(vp), jnp.stack(kip), jnp.stack(hp), jnp.stack(cp),
            jnp.stack(ks), jnp.stack(vs), jnp.stack(kis), jnp.stack(hs), jnp.stack(cs))
```

```python
import functools

import jax
import jax.numpy as jnp
from jax import lax
from jax.experimental import pallas as pl
from jax.experimental.pallas import tpu as pltpu

F32 = jnp.float32
BF16 = jnp.bfloat16
I32 = jnp.int32
HIGHEST = lax.Precision.HIGHEST

EPS = 1e-6
N_META = 16
N_HEADS = 16
N_KV_HEADS = 4
HEAD_DIM = 64
Q_PER_KV = N_HEADS // N_KV_HEADS
IDX_HEADS = 4
IDX_DIM = 64
TOP_K_MAX = 256
SSD_HEAD_DIM = 64
SSD_GROUPS = 4
D_STATE = 128
CONV_W = 4
TILE = 128
PAD_FRONT = TILE - N_META
INT_MIN = -(2 ** 31)
NEG_BIG = -1e30
VMEM_LIMIT = 56 * 1024 * 1024


def _cparams(sem):
    return pltpu.CompilerParams(dimension_semantics=sem, vmem_limit_bytes=VMEM_LIMIT)


def _const_spec(shape):
    nd = len(shape)
    return pl.BlockSpec(shape, lambda *_: (0,) * nd, pipeline_mode=pl.Buffered(1))


def _rms(x, g):
    return x * lax.rsqrt(jnp.mean(x * x, axis=-1, keepdims=True) + EPS) * g


def _silu(x):
    return x * jax.nn.sigmoid(x)


def _row_tile(n_tiles, cap):
    for k in range(cap, 0, -1):
        if n_tiles % k == 0:
            return k * TILE
    return TILE


def _ffn_kernel(x_ref, g_ref, g2_ref, wg_ref, wu_ref, wd_ref, o_ref, h2_ref, *, ck):
    x = x_ref[...]
    h = _rms(x, g_ref[...]).astype(BF16)
    acc = jnp.zeros(x.shape, F32)
    for c in range(wg_ref.shape[1] // ck):
        sl = slice(c * ck, (c + 1) * ck)
        gate = jnp.dot(h, wg_ref[:, sl], preferred_element_type=F32)
        up = jnp.dot(h, wu_ref[:, sl], preferred_element_type=F32)
        act = (_silu(gate) * up).astype(BF16)
        acc = acc + jnp.dot(act, wd_ref[sl, :], preferred_element_type=F32)
    y = x + 0.5 * acc
    o_ref[...] = y
    h2_ref[...] = _rms(y, g2_ref[...]).astype(BF16)


def _ffn(x, g, g2, wg, wu, wd, tm):
    r, d = x.shape
    dff = wg.shape[1]
    row = lambda i: (i, 0)
    return pl.pallas_call(
        functools.partial(_ffn_kernel, ck=256),
        grid=(r // tm,),
        in_specs=[pl.BlockSpec((tm, d), row), _const_spec((1, d)), _const_spec((1, d)),
                  _const_spec((d, dff)), _const_spec((d, dff)), _const_spec((dff, d))],
        out_specs=[pl.BlockSpec((tm, d), row), pl.BlockSpec((tm, d), row)],
        out_shape=[jax.ShapeDtypeStruct((r, d), F32), jax.ShapeDtypeStruct((r, d), BF16)],
        compiler_params=_cparams(("arbitrary",)),
        name="ffn",
    )(x, g, g2, wg, wu, wd)


def _proj_attn_kernel(x_ref, g_ref, wqk_ref, wv_ref, widx_ref, seg_ref, segt_ref, gqk_ref,
                      q_ref, k_ref, v_ref, kb_ref, vb_ref, qi_ref, kiw_ref):
    h = _rms(x_ref[...], g_ref[...])
    hb = h.astype(BF16)
    qk = jnp.dot(hb, wqk_ref[...], preferred_element_type=F32)
    ss = jnp.dot((qk * qk).astype(BF16), seg_ref[...], preferred_element_type=F32)
    r = lax.rsqrt(ss * (1.0 / HEAD_DIM) + EPS)
    r_hi = r.astype(BF16)
    r_lo = (r - r_hi.astype(F32)).astype(BF16)
    rx = (jnp.dot(r_hi, segt_ref[...], preferred_element_type=F32)
          + jnp.dot(r_lo, segt_ref[...], preferred_element_type=F32))
    qkn = qk * rx * gqk_ref[...]
    nq = q_ref.shape[1]
    q_ref[...] = (qkn[:, :nq] * (HEAD_DIM ** -0.5)).astype(BF16)
    k = qkn[:, nq:]
    k_ref[...] = k
    kb_ref[...] = k.astype(BF16)
    v = jnp.dot(hb, wv_ref[...], preferred_element_type=F32)
    v_ref[...] = v
    vb_ref[...] = v.astype(BF16)
    idx = jnp.dot(h, widx_ref[...], preferred_element_type=F32, precision=HIGHEST)
    nqi = qi_ref.shape[1]
    qi_ref[...] = idx[:, :nqi]
    kiw_ref[...] = idx[:, nqi:]


def _proj_attn(x1, g, wqk, wv, widx, seg, segt, gqk, tm):
    r, d = x1.shape
    nqk, nv = wqk.shape[1], wv.shape[1]
    nq = N_HEADS * HEAD_DIM
    nqi = IDX_HEADS * IDX_DIM
    row = lambda i: (i, 0)
    outs = [(nq, BF16), (nqk - nq, F32), (nv, F32), (nqk - nq, BF16), (nv, BF16), (nqi, F32), (TILE, F32)]
    return pl.pallas_call(
        _proj_attn_kernel,
        grid=(r // tm,),
        in_specs=[pl.BlockSpec((tm, d), row), _const_spec((1, d)), _const_spec(wqk.shape),
                  _const_spec(wv.shape), _const_spec(widx.shape), _const_spec(seg.shape),
                  _const_spec(segt.shape), _const_spec(gqk.shape)],
        out_specs=[pl.BlockSpec((tm, n), row) for n, _ in outs],
        out_shape=[jax.ShapeDtypeStruct((r, n), dt) for n, dt in outs],
        compiler_params=_cparams(("arbitrary",)),
        name="proj_attn",
    )(x1, g, wqk, wv, widx, seg, segt, gqk)


def _proj_xbc_kernel(h_ref, wx_ref, wdt_ref, xbc_ref, dt_ref):
    h = h_ref[...]
    xbc_ref[...] = jnp.dot(h, wx_ref[...], preferred_element_type=F32)
    dt_ref[...] = jnp.dot(h, wdt_ref[...], preferred_element_type=F32)


def _proj_xbc(hb, wx, wdt, tm):
    r, d = hb.shape
    row = lambda i: (i, 0)
    return pl.pallas_call(
        _proj_xbc_kernel,
        grid=(r // tm,),
        in_specs=[pl.BlockSpec((tm, d), row), _const_spec(wx.shape), _const_spec(wdt.shape)],
        out_specs=[pl.BlockSpec((tm, wx.shape[1]), row), pl.BlockSpec((tm, wdt.shape[1]), row)],
        out_shape=[jax.ShapeDtypeStruct((r, wx.shape[1]), F32),
                   jax.ShapeDtypeStruct((r, wdt.shape[1]), F32)],
        compiler_params=_cparams(("arbitrary",)),
        name="proj_xbc",
    )(hb, wx, wdt)


def _proj_gate_kernel(h_ref, wz_ref, wg_ref, zs_ref, gt_ref):
    h = h_ref[...]
    zs_ref[...] = _silu(jnp.dot(h, wz_ref[...], preferred_element_type=F32)).astype(BF16)
    gt_ref[...] = jax.nn.sigmoid(jnp.dot(h, wg_ref[...], preferred_element_type=F32)).astype(BF16)


def _proj_gate(hb, wz, wg, tm):
    r, d = hb.shape
    row = lambda i: (i, 0)
    return pl.pallas_call(
        _proj_gate_kernel,
        grid=(r // tm,),
        in_specs=[pl.BlockSpec((tm, d), row), _const_spec(wz.shape), _const_spec(wg.shape)],
        out_specs=[pl.BlockSpec((tm, wz.shape[1]), row), pl.BlockSpec((tm, wg.shape[1]), row)],
        out_shape=[jax.ShapeDtypeStruct((r, wz.shape[1]), BF16),
                   jax.ShapeDtypeStruct((r, wg.shape[1]), BF16)],
        compiler_params=_cparams(("arbitrary",)),
        name="proj_gate",
    )(hb, wz, wg)


def _sortable(score):
    b = pltpu.bitcast(score, I32)
    return jnp.where(b < 0, -(b & 0x7FFFFFFF), b)


def _colsum(m):
    return jnp.sum(m.reshape(m.shape[0] // 8, 8, m.shape[1]), axis=0)


def _count(keys_ref, n_chunks, pred):
    def body(c, acc):
        return acc + _colsum(pred(keys_ref[c], c).astype(I32))
    part = lax.fori_loop(0, n_chunks, body, jnp.zeros((8, TILE), I32))
    return jnp.sum(part, axis=0, keepdims=True)


def _kth_largest(keys_ref, n_chunks, ksel):
    n_ge0 = _count(keys_ref, n_chunks, lambda k, c: k >= 0)
    t = jnp.where(n_ge0 >= ksel, jnp.zeros((1, TILE), I32), jnp.full((1, TILE), INT_MIN, I32))

    def bit_step(i, t):
        cand = t | jnp.left_shift(jnp.int32(1), 30 - i)
        n = _count(keys_ref, n_chunks, lambda k, c: k >= cand)
        return jnp.where(n >= ksel, cand, t)

    return lax.fori_loop(0, 31, bit_step, t)


def _tie_limit(keys_ref, n_chunks, t, ksel, n_bits):
    rows = lax.broadcasted_iota(I32, (TILE, TILE), 0)
    n_gt = _count(keys_ref, n_chunks, lambda k, c: k > t)
    n_eq = _count(keys_ref, n_chunks, lambda k, c: k == t)
    need = ksel - n_gt
    live = t != INT_MIN
    conflict = jnp.max(jnp.where(live & (n_eq > need), 1, 0)) > 0

    def search():
        def bit_step(i, p):
            cand = p | jnp.left_shift(jnp.int32(1), n_bits - 1 - i)
            n = _count(keys_ref, n_chunks, lambda k, c: (k == t) & ((rows + c * TILE) < cand))
            return jnp.where(n < need, cand, p)
        return lax.fori_loop(0, n_bits, bit_step, jnp.zeros((1, TILE), I32)) + 1

    lim = lax.cond(conflict, search, lambda: jnp.full((1, TILE), 1 << n_bits, I32))
    return jnp.where(live, lim, 0)


def _dsa_prompt_kernel(q_ref, qi_ref, kiwq_ref, kb_ref, vb_ref, kiw_ref, o_ref,
                       keys_ref, vt_ref, qt_ref, qit_ref, m_ref, l_ref, acc_ref, *, ksel, n_bits):
    b = pl.program_id(0)
    j = pl.program_id(1)
    n_chunks = j + 1
    nkv = N_KV_HEADS
    gw = Q_PER_KV * TILE

    @pl.when((b == 0) & (j == 0))
    def _():
        qt_ref[...] = jnp.zeros(qt_ref.shape, qt_ref.dtype)
        qit_ref[...] = jnp.zeros(qit_ref.shape, qit_ref.dtype)

    @pl.when(j == 0)
    def _():
        def tr(c, _):
            vt_ref[c] = vb_ref[pl.ds(pl.multiple_of(c * TILE, TILE), TILE), :].astype(F32).T.astype(BF16)
            return 0
        lax.fori_loop(0, vt_ref.shape[0], tr, 0)

    qt = q_ref[...].astype(F32).T.astype(BF16)
    for h in range(N_HEADS):
        g, hh = divmod(h, Q_PER_KV)
        qt_ref[g * HEAD_DIM:(g + 1) * HEAD_DIM, g * gw + hh * TILE: g * gw + (hh + 1) * TILE] = \
            qt[h * HEAD_DIM:(h + 1) * HEAD_DIM, :]
    qit = qi_ref[...].T
    for h in range(IDX_HEADS):
        qit_ref[0:IDX_DIM, h * TILE:(h + 1) * TILE] = qit[h * IDX_DIM:(h + 1) * IDX_DIM, :]
    wq = kiwq_ref[...].T[IDX_DIM:IDX_DIM + 8, :]
    wq = wq * (IDX_HEADS ** -0.5 * IDX_DIM ** -0.5)

    q_pos = j * TILE + lax.broadcasted_iota(I32, (TILE, TILE), 1)
    k_row = lax.broadcasted_iota(I32, (TILE, TILE), 0)

    def score_chunk(c, _):
        kiw = kiw_ref[pl.ds(pl.multiple_of(c * TILE, TILE), TILE), :]
        d = jnp.dot(kiw, qit_ref[...], preferred_element_type=F32, precision=HIGHEST)
        s = jnp.zeros((TILE, TILE), F32)
        for h in range(IDX_HEADS):
            s = s + wq[h:h + 1, :] * jnp.maximum(d[:, h * TILE:(h + 1) * TILE], 0.0)
        k_pos = k_row + c * TILE
        valid = (k_pos <= q_pos) & (k_pos >= PAD_FRONT)
        keys_ref[c] = jnp.where(valid, _sortable(s), INT_MIN)
        return 0
    lax.fori_loop(0, n_chunks, score_chunk, 0)

    t = _kth_largest(keys_ref, n_chunks, ksel)
    lim = _tie_limit(keys_ref, n_chunks, t, ksel, n_bits)

    m_ref[...] = jnp.full(m_ref.shape, NEG_BIG, F32)
    l_ref[...] = jnp.zeros(l_ref.shape, F32)
    acc_ref[...] = jnp.zeros(acc_ref.shape, F32)

    def attend_chunk(c, _):
        keys = keys_ref[c]
        sel = (keys > t) | ((keys == t) & ((k_row + c * TILE) < lim))
        bias = jnp.where(sel, 0.0, NEG_BIG)
        bias = jnp.concatenate([bias] * Q_PER_KV, axis=1)
        kc = kb_ref[pl.ds(pl.multiple_of(c * TILE, TILE), TILE), :]
        s_all = jnp.dot(kc, qt_ref[...], preferred_element_type=F32)
        vt = vt_ref[c]
        for g in range(nkv):
            s = s_all[:, g * gw:(g + 1) * gw] + bias
            m_old = m_ref[g:g + 1, :]
            m_new = jnp.maximum(m_old, jnp.max(s, axis=0, keepdims=True))
            alpha = jnp.exp(m_old - m_new)
            p = jnp.exp(s - m_new)
            l_ref[g:g + 1, :] = alpha * l_ref[g:g + 1, :] + jnp.sum(p, axis=0, keepdims=True)
            pv = jnp.dot(vt[g * HEAD_DIM:(g + 1) * HEAD_DIM, :], p.astype(BF16),
                         preferred_element_type=F32)
            acc_ref[g] = alpha * acc_ref[g] + pv
            m_ref[g:g + 1, :] = m_new
        return 0
    lax.fori_loop(0, n_chunks, attend_chunk, 0)

    pieces = []
    for g in range(nkv):
        o = acc_ref[g] / l_ref[g:g + 1, :]
        for hh in range(Q_PER_KV):
            pieces.append(o[:, hh * TILE:(hh + 1) * TILE])
    o_ref[...] = jnp.concatenate(pieces, axis=0).T.astype(BF16)


def _dsa_prompt(q, qi, kiw, kb, vb, n_batch, lp, ksel):
    nt = lp // TILE
    r_total = q.shape[0]
    qw = N_HEADS * HEAD_DIM
    kvw = N_KV_HEADS * HEAD_DIM
    gw = Q_PER_KV * TILE
    tile = lambda b, j: (b * nt + j, 0)
    seq = lambda b, j: (b, 0)
    n_bits = max(1, (lp - 1).bit_length())
    return pl.pallas_call(
        functools.partial(_dsa_prompt_kernel, ksel=ksel, n_bits=n_bits),
        grid=(n_batch, nt),
        in_specs=[pl.BlockSpec((TILE, qw), tile), pl.BlockSpec((TILE, IDX_HEADS * IDX_DIM), tile),
                  pl.BlockSpec((TILE, TILE), tile), pl.BlockSpec((lp, kvw), seq),
                  pl.BlockSpec((lp, kvw), seq), pl.BlockSpec((lp, TILE), seq)],
        out_specs=pl.BlockSpec((TILE, qw), tile),
        out_shape=jax.ShapeDtypeStruct((r_total, qw), BF16),
        scratch_shapes=[pltpu.VMEM((nt, TILE, TILE), I32),
                        pltpu.VMEM((nt, kvw, TILE), BF16),
                        pltpu.VMEM((kvw, N_KV_HEADS * gw), BF16),
                        pltpu.VMEM((TILE, IDX_HEADS * TILE), F32),
                        pltpu.VMEM((8, gw), F32), pltpu.VMEM((8, gw), F32),
                        pltpu.VMEM((N_KV_HEADS, HEAD_DIM, gw), F32)],
        compiler_params=_cparams(("arbitrary", "arbitrary")),
        name="dsa_prompt",
    )(q, qi, kiw, kb, vb, kiw)


def _expand_heads(v, e_ref):
    return jnp.dot(v, e_ref[...], preferred_element_type=F32, precision=HIGHEST)


def _softplus(x):
    return jnp.maximum(x, 0.0) + jnp.log1p(jnp.exp(-jnp.abs(x)))


def _ssd_prompt_kernel(xbc_ref, dtr_ref, zs_ref, cw_ref, cb_ref, dtb_ref, alog_ref, dsk_ref, gn_ref,
                       e_ref, y_ref, ssm_ref, conv_ref, xp_ref, ht_ref, *, n_heads):
    c = pl.program_id(1)
    nc = pl.num_programs(1)
    d_inner = n_heads * SSD_HEAD_DIM
    hpg = n_heads // SSD_GROUPS
    gwid = hpg * SSD_HEAD_DIM
    gn_w = d_inner // SSD_GROUPS

    @pl.when(c == 0)
    def _():
        xp_ref[0:8, :] = jnp.zeros((8, xp_ref.shape[1]), F32)
        ht_ref[...] = jnp.zeros(ht_ref.shape, F32)

    xp_ref[8:8 + TILE, :] = xbc_ref[...]
    u = cb_ref[...] + cw_ref[CONV_W - 1:CONV_W, :] * xp_ref[8:8 + TILE, :]
    for k in range(CONV_W - 1):
        off = 8 - (CONV_W - 1) + k
        u = u + cw_ref[k:k + 1, :] * xp_ref[off:off + TILE, :]
    tail = xp_ref[8 + TILE - (CONV_W - 1):8 + TILE, :]
    xp_ref[8 - (CONV_W - 1):8, :] = tail

    row = lax.broadcasted_iota(I32, (TILE, 1), 0)
    live = jnp.where((c > 0) | (row >= PAD_FRONT), 1.0, 0.0)
    act = _silu(u) * live
    xs = act[:, :d_inner]
    bm = act[:, d_inner:d_inner + SSD_GROUPS * D_STATE]
    cm = act[:, d_inner + SSD_GROUPS * D_STATE:]

    dt = _softplus(dtr_ref[...] + dtb_ref[...]) * live
    a = dt * (-jnp.exp(alog_ref[...]))
    ri = lax.broadcasted_iota(I32, (TILE, TILE), 0)
    ci = lax.broadcasted_iota(I32, (TILE, TILE), 1)
    tril = ri >= ci
    acs = jnp.dot(jnp.where(tril, 1.0, 0.0), a, preferred_element_type=F32, precision=HIGHEST)
    acs_t = acs.T
    dt_t = dt.T
    acs_last = acs[TILE - 1:TILE, :]
    w_in = _expand_heads(jnp.exp(acs_last - acs) * dt, e_ref)
    w_out = _expand_heads(jnp.exp(acs), e_ref)
    xw = (xs * w_in).astype(BF16)
    xsb = xs.astype(BF16)
    lane = lax.broadcasted_iota(I32, (1, gwid), 1)

    ys = []
    for g in range(SSD_GROUPS):
        bg = bm[:, g * D_STATE:(g + 1) * D_STATE]
        cg = cm[:, g * D_STATE:(g + 1) * D_STATE].astype(BF16)
        bgt = bg.T.astype(BF16)
        cb = jnp.dot(cg, bgt, preferred_element_type=F32)
        xw_g = xw[:, g * gwid:(g + 1) * gwid]
        xs_g = xsb[:, g * gwid:(g + 1) * gwid]
        h_prev = ht_ref[g]
        y_off = jnp.dot(cg, h_prev.astype(BF16), preferred_element_type=F32)
        y_g = y_off * w_out[:, g * gwid:(g + 1) * gwid]
        for e in range(hpg):
            he = g * hpg + e
            seg = acs[:, he:he + 1] - acs_t[he:he + 1, :]
            m = (cb * jnp.exp(jnp.where(tril, seg, -jnp.inf)) * dt_t[he:he + 1, :]).astype(BF16)
            only = (lane >= e * SSD_HEAD_DIM) & (lane < (e + 1) * SSD_HEAD_DIM)
            y_g = y_g + jnp.dot(m, jnp.where(only, xs_g, jnp.zeros_like(xs_g)),
                                preferred_element_type=F32)
        states = jnp.dot(bgt, xw_g, preferred_element_type=F32)
        ht_ref[g] = h_prev * w_out[TILE - 1:TILE, g * gwid:(g + 1) * gwid] + states
        ys.append(y_g)
    y = jnp.concatenate(ys, axis=1) + dsk_ref[...] * xs
    y = y * zs_ref[...].astype(F32)
    outs = []
    for g in range(SSD_GROUPS):
        yg = y[:, g * gn_w:(g + 1) * gn_w]
        outs.append(yg * lax.rsqrt(jnp.mean(yg * yg, axis=-1, keepdims=True) + EPS))
    y_ref[...] = (jnp.concatenate(outs, axis=1) * gn_ref[...]).astype(BF16)

    @pl.when(c == nc - 1)
    def _():
        for g in range(SSD_GROUPS):
            ssm_ref[0, g * gwid:(g + 1) * gwid, :] = ht_ref[g].T
        conv_ref[0] = tail


def _ssd_prompt(xbc, dtr, zs, cw, cb, dtb, alog, dsk, gn, e_mat, n_batch, lp, n_heads):
    nt = lp // TILE
    r_total = xbc.shape[0]
    d_inner = n_heads * SSD_HEAD_DIM
    cdim = xbc.shape[1]
    tile = lambda b, c: (b * nt + c, 0)
    fix = lambda b, c: (0, 0)
    return pl.pallas_call(
        functools.partial(_ssd_prompt_kernel, n_heads=n_heads),
        grid=(n_batch, nt),
        in_specs=[pl.BlockSpec((TILE, cdim), tile), pl.BlockSpec((TILE, TILE), tile),
                  pl.BlockSpec((TILE, d_inner), tile),
                  pl.BlockSpec(cw.shape, fix), pl.BlockSpec(cb.shape, fix), pl.BlockSpec(dtb.shape, fix),
                  pl.BlockSpec(alog.shape, fix), pl.BlockSpec(dsk.shape, fix), pl.BlockSpec(gn.shape, fix),
                  pl.BlockSpec(e_mat.shape, fix)],
        out_specs=[pl.BlockSpec((TILE, d_inner), tile),
                   pl.BlockSpec((1, d_inner, D_STATE), lambda b, c: (b, 0, 0)),
                   pl.BlockSpec((1, CONV_W - 1, cdim), lambda b, c: (b, 0, 0))],
        out_shape=[jax.ShapeDtypeStruct((r_total, d_inner), BF16),
                   jax.ShapeDtypeStruct((n_batch, d_inner, D_STATE), F32),
                   jax.ShapeDtypeStruct((n_batch, CONV_W - 1, cdim), F32)],
        scratch_shapes=[pltpu.VMEM((8 + TILE, cdim), F32),
                        pltpu.VMEM((SSD_GROUPS, D_STATE, d_inner // SSD_GROUPS), F32)],
        compiler_params=_cparams(("arbitrary", "arbitrary")),
        name="ssd_prompt",
    )(xbc, dtr, zs, cw, cb, dtb, alog, dsk, gn, e_mat)


def _idx_sample_kernel(pt_ref, qi_ref, kiw_ref, *rest, n_pages):
    pages, s_ref = rest[:n_pages], rest[n_pages]
    qi = qi_ref[0]
    kiw = kiw_ref[0]
    q8 = jnp.concatenate([qi[:, h * IDX_DIM:(h + 1) * IDX_DIM] for h in range(IDX_HEADS)]
                         + [jnp.zeros((8 - IDX_HEADS, IDX_DIM), F32)], axis=0)
    w_col = jnp.broadcast_to(kiw, (TILE, TILE)).T[IDX_DIM:IDX_DIM + 8, 0:1]
    w_col = w_col * (IDX_HEADS ** -0.5 * IDX_DIM ** -0.5)
    nt = (((1,), (1,)), ((), ()))
    for p in range(n_pages):
        d = lax.dot_general(q8, pages[p][0], nt, preferred_element_type=F32, precision=HIGHEST)
        s_ref[p, 0] = jnp.sum(w_col * jnp.maximum(d, 0.0), axis=0, keepdims=True)
    d_new = jnp.sum(q8 * kiw[:, :IDX_DIM], axis=1, keepdims=True)
    s_new = jnp.sum(w_col * jnp.maximum(d_new, 0.0), axis=0, keepdims=True)
    lane = lax.broadcasted_iota(I32, (1, TILE), 1)
    s_ref[n_pages, 0] = jnp.where(lane == 0, s_new, 0.0)


def _idx_sample(page_table, qi_s, kiw_s, cache_kidx):
    ns, n_pages = page_table.shape
    n_slots = n_pages + 1
    page = lambda p: pl.BlockSpec((1, TILE, IDX_DIM), lambda s, pt: (pt[s, p], 0, 0))
    one = lambda s, pt: (s, 0, 0)
    grid_spec = pltpu.PrefetchScalarGridSpec(
        num_scalar_prefetch=1, grid=(ns,),
        in_specs=[pl.BlockSpec((1, 1, qi_s.shape[2]), one), pl.BlockSpec((1, 1, TILE), one)]
        + [page(p) for p in range(n_pages)],
        out_specs=pl.BlockSpec((n_slots, 1, 1, TILE), lambda s, pt: (0, s, 0, 0)))
    return pl.pallas_call(
        functools.partial(_idx_sample_kernel, n_pages=n_pages),
        grid_spec=grid_spec,
        out_shape=jax.ShapeDtypeStruct((n_slots, ns, 1, TILE), F32),
        compiler_params=_cparams(("arbitrary",)),
        name="idx_sample",
    )(page_table, qi_s, kiw_s, *([cache_kidx] * n_pages))


def _select_sample_kernel(s_ref, bias_ref, keys_ref, *, n_pages, ksel, n_bits):
    n_slots = n_pages + 1
    rows = lax.broadcasted_iota(I32, (TILE, TILE), 0)
    for p in range(n_slots):
        key = _sortable(s_ref[p].T)
        if p == n_pages:
            key = jnp.where(rows == 0, key, INT_MIN)
        keys_ref[p] = key
    t = _kth_largest(keys_ref, n_slots, ksel)
    lim = _tie_limit(keys_ref, n_slots, t, ksel, n_bits)
    for p in range(n_slots):
        keys = keys_ref[p]
        sel = (keys > t) | ((keys == t) & ((rows + p * TILE) < lim))
        bias_ref[p] = jnp.where(sel, 0.0, NEG_BIG).T


def _select_sample(scores, ksel):
    n_slots, ns, _ = scores.shape
    n_pages = n_slots - 1
    n_bits = (n_slots * TILE - 1).bit_length()
    return pl.pallas_call(
        functools.partial(_select_sample_kernel, n_pages=n_pages, ksel=ksel, n_bits=n_bits),
        out_shape=jax.ShapeDtypeStruct((n_slots, ns, TILE), F32),
        scratch_shapes=[pltpu.VMEM((n_slots, TILE, TILE), I32)],
        compiler_params=pltpu.CompilerParams(vmem_limit_bytes=VMEM_LIMIT),
        name="select_sample",
    )(scores)


def _attn_sample_kernel(pt_ref, q_ref, kn_ref, vn_ref, bias_ref, *rest, n_pages):
    kpages, vpages, o_ref = rest[:n_pages], rest[n_pages:2 * n_pages], rest[2 * n_pages]
    kvw = N_KV_HEADS * HEAD_DIM
    q = q_ref[0].astype(F32)
    lane = lax.broadcasted_iota(I32, (N_HEADS, kvw), 1)
    head = lax.broadcasted_iota(I32, (N_HEADS, kvw), 0)
    own = jnp.right_shift(lane, 6) == jnp.right_shift(head, 2)
    qrows = []
    for h in range(N_HEADS):
        qh = q[:, h * HEAD_DIM:(h + 1) * HEAD_DIM]
        qrows.append(jnp.concatenate([qh] * N_KV_HEADS, axis=1))
    qf = jnp.where(own, jnp.concatenate(qrows, axis=0), 0.0)
    qw = qf.astype(BF16)
    nt = (((1,), (1,)), ((), ()))
    scores = []
    for p in range(n_pages):
        kp = kpages[p][0].astype(BF16)
        s = lax.dot_general(qw, kp, nt, preferred_element_type=F32)
        scores.append(s + bias_ref[p, 0])
    s_new = jnp.sum(qf * kn_ref[0], axis=1, keepdims=True)
    lane1 = lax.broadcasted_iota(I32, (1, TILE), 1)
    scores.append(jnp.where(lane1 == 0, s_new, NEG_BIG) + bias_ref[n_pages, 0])
    m = scores[0].max(axis=1, keepdims=True)
    for s in scores[1:]:
        m = jnp.maximum(m, s.max(axis=1, keepdims=True))
    den = jnp.zeros((N_HEADS, 1), F32)
    out = jnp.zeros((N_HEADS, kvw), F32)
    for p in range(n_pages):
        e = jnp.exp(scores[p] - m)
        den = den + e.sum(axis=1, keepdims=True)
        out = out + jnp.dot(e.astype(BF16), vpages[p][0].astype(BF16), preferred_element_type=F32)
    e = jnp.exp(scores[n_pages] - m)
    den = den + e.sum(axis=1, keepdims=True)
    out = out + e[:, 0:1] * vn_ref[0]
    out = out / den
    rolled = jnp.concatenate([pltpu.roll(out[:, i * TILE:(i + 1) * TILE], HEAD_DIM, 1)
                              for i in range(kvw // TILE)], axis=1)
    half = lax.broadcasted_iota(I32, (1, TILE), 1) < HEAD_DIM
    pieces = []
    for m2 in range(N_HEADS // 2):
        g = (2 * m2) // Q_PER_KV
        blk = (g // 2) * TILE
        a_src, b_src = (rolled, out) if g % 2 == 1 else (out, rolled)
        a = a_src[2 * m2:2 * m2 + 1, blk:blk + TILE]
        b = b_src[2 * m2 + 1:2 * m2 + 2, blk:blk + TILE]
        pieces.append(jnp.where(half, a, b))
    o_ref[0] = jnp.concatenate(pieces, axis=1)


def _attn_sample(page_table, q_s, k_s, v_s, bias, cache_k, cache_v):
    ns, n_pages = page_table.shape
    kvw = N_KV_HEADS * HEAD_DIM
    qw = N_HEADS * HEAD_DIM
    page = lambda p: pl.BlockSpec((1, TILE, kvw), lambda s, pt: (pt[s, p], 0, 0))
    one = lambda s, pt: (s, 0, 0)
    grid_spec = pltpu.PrefetchScalarGridSpec(
        num_scalar_prefetch=1, grid=(ns,),
        in_specs=[pl.BlockSpec((1, 1, qw), one), pl.BlockSpec((1, 1, kvw), one),
                  pl.BlockSpec((1, 1, kvw), one),
                  pl.BlockSpec((n_pages + 1, 1, 1, TILE), lambda s, pt: (0, s, 0, 0))]
        + [page(p) for p in range(n_pages)] * 2,
        out_specs=pl.BlockSpec((1, 1, qw), one))
    return pl.pallas_call(
        functools.partial(_attn_sample_kernel, n_pages=n_pages),
        grid_spec=grid_spec,
        out_shape=jax.ShapeDtypeStruct((ns, 1, qw), F32),
        compiler_params=_cparams(("arbitrary",)),
        name="attn_sample",
    )(page_table, q_s, k_s, v_s, bias, *([cache_k] * n_pages), *([cache_v] * n_pages))


def _ssd_sample_kernel(xbc_ref, dtr_ref, sc_ref, h0_ref, cw_ref, cb_ref, dtb_ref, alog_ref, dsk_ref,
                       e_ref, y_ref, h_ref, conv_ref, *, n_heads, sb):
    d_inner = n_heads * SSD_HEAD_DIM
    hpg = n_heads // SSD_GROUPS
    new = xbc_ref[...]
    cdim = new.shape[1]
    u = cb_ref[...] + cw_ref[CONV_W - 1:CONV_W, :] * new
    for k in range(CONV_W - 1):
        u = u + cw_ref[k:k + 1, :] * sc_ref[:, k * cdim:(k + 1) * cdim]
    for k in range(CONV_W - 2):
        conv_ref[:, k * cdim:(k + 1) * cdim] = sc_ref[:, (k + 1) * cdim:(k + 2) * cdim]
    conv_ref[:, (CONV_W - 2) * cdim:] = new
    act = _silu(u)
    xs = act[:, :d_inner]
    bm = act[:, d_inner:d_inner + SSD_GROUPS * D_STATE]
    cm = act[:, d_inner + SSD_GROUPS * D_STATE:]
    dt = _softplus(dtr_ref[...] + dtb_ref[...])
    decay = jnp.exp(dt * (-jnp.exp(alog_ref[...])))
    dec_x = _expand_heads(decay, e_ref)
    xdt = xs * _expand_heads(dt, e_ref)
    for i in range(sb):
        bcols, ccols = [], []
        for g in range(SSD_GROUPS):
            brow = bm[i:i + 1, g * D_STATE:(g + 1) * D_STATE]
            crow = cm[i:i + 1, g * D_STATE:(g + 1) * D_STATE]
            bcols.append(jnp.broadcast_to(brow, (TILE, D_STATE)).T)
            ccols.append(jnp.broadcast_to(crow, (TILE, D_STATE)).T)
        pieces = []
        for pr in range(n_heads // 2):
            g = (2 * pr) // hpg
            sl = slice(pr * TILE, (pr + 1) * TILE)
            ht = h0_ref[i, sl, :].T
            hn = ht * dec_x[i:i + 1, sl] + bcols[g] * xdt[i:i + 1, sl]
            pieces.append(jnp.sum(hn * ccols[g], axis=0, keepdims=True))
            h_ref[i, sl, :] = hn.T
        y_ref[i:i + 1, :] = jnp.concatenate(pieces, axis=1) + dsk_ref[...] * xs[i:i + 1, :]


def _ssd_sample(xbc_s, dtr_s, state_conv, state_ssm, cw, cb, dtb, alog, dsk, e_mat, n_heads, sb):
    ns, cdim = xbc_s.shape
    d_inner = n_heads * SSD_HEAD_DIM
    row = lambda i: (i, 0)
    row3 = lambda i: (i, 0, 0)
    fix = lambda i: (0, 0)
    return pl.pallas_call(
        functools.partial(_ssd_sample_kernel, n_heads=n_heads, sb=sb),
        grid=(ns // sb,),
        in_specs=[pl.BlockSpec((sb, cdim), row), pl.BlockSpec((sb, TILE), row),
                  pl.BlockSpec((sb, (CONV_W - 1) * cdim), row), pl.BlockSpec((sb, d_inner, D_STATE), row3),
                  pl.BlockSpec(cw.shape, fix), pl.BlockSpec(cb.shape, fix), pl.BlockSpec(dtb.shape, fix),
                  pl.BlockSpec(alog.shape, fix), pl.BlockSpec(dsk.shape, fix), pl.BlockSpec(e_mat.shape, fix)],
        out_specs=[pl.BlockSpec((sb, d_inner), row), pl.BlockSpec((sb, d_inner, D_STATE), row3),
                   pl.BlockSpec((sb, (CONV_W - 1) * cdim), row)],
        out_shape=[jax.ShapeDtypeStruct((ns, d_inner), F32),
                   jax.ShapeDtypeStruct((ns, d_inner, D_STATE), F32),
                   jax.ShapeDtypeStruct((ns, (CONV_W - 1) * cdim), F32)],
        compiler_params=_cparams(("arbitrary",)),
        name="ssd_sample",
    )(xbc_s, dtr_s, state_conv, state_ssm, cw, cb, dtb, alog, dsk, e_mat)


def _fill_sample_kernel(attn_any, ssd_any, a_ref, y_ref, zs_ref, gn_ref, attn_ref, ssd_ref):
    del attn_any, ssd_any
    attn_ref[...] = a_ref[...].astype(BF16)
    y = y_ref[...] * zs_ref[...].astype(F32)
    gn_w = y.shape[1] // SSD_GROUPS
    outs = []
    for g in range(SSD_GROUPS):
        yg = y[:, g * gn_w:(g + 1) * gn_w]
        outs.append(yg * lax.rsqrt(jnp.mean(yg * yg, axis=-1, keepdims=True) + EPS))
    ssd_ref[...] = (jnp.concatenate(outs, axis=1) * gn_ref[...]).astype(BF16)


def _fill_sample(attn, ssd, attn_s, y_s, zs, gn, blk):
    ns = attn_s.shape[0]
    last = lambda i: (blk, 0)
    fix = lambda i: (0, 0)
    return pl.pallas_call(
        _fill_sample_kernel,
        grid=(1,),
        in_specs=[pl.BlockSpec(memory_space=pl.ANY), pl.BlockSpec(memory_space=pl.ANY),
                  pl.BlockSpec(attn_s.shape, fix), pl.BlockSpec(y_s.shape, fix),
                  pl.BlockSpec((ns, zs.shape[1]), last), pl.BlockSpec(gn.shape, fix)],
        out_specs=[pl.BlockSpec((ns, attn.shape[1]), last), pl.BlockSpec((ns, ssd.shape[1]), last)],
        out_shape=[jax.ShapeDtypeStruct(attn.shape, attn.dtype), jax.ShapeDtypeStruct(ssd.shape, ssd.dtype)],
        input_output_aliases={0: 0, 1: 1},
        compiler_params=_cparams(("arbitrary",)),
        name="fill_sample",
    )(attn, ssd, attn_s, y_s, zs, gn)


def _mix_kernel(x_ref, a_ref, s_ref, gt_ref, wao_ref, wso_ref, wo_ref, o_ref):
    d = x_ref.shape[1]
    gt = gt_ref[...].astype(F32)
    ao = jnp.dot(a_ref[...], wao_ref[...], preferred_element_type=F32)
    so = jnp.dot(s_ref[...], wso_ref[...], preferred_element_type=F32)
    mixed = (gt[:, :d] * ao + gt[:, d:] * so).astype(BF16)
    o_ref[...] = x_ref[...] + jnp.dot(mixed, wo_ref[...], preferred_element_type=F32)


def _mix(x1, attn, ssd, gates, wao, wso, wo, tm):
    r, d = x1.shape
    row = lambda i: (i, 0)
    return pl.pallas_call(
        _mix_kernel,
        grid=(r // tm,),
        in_specs=[pl.BlockSpec((tm, d), row), pl.BlockSpec((tm, attn.shape[1]), row),
                  pl.BlockSpec((tm, ssd.shape[1]), row), pl.BlockSpec((tm, gates.shape[1]), row),
                  _const_spec(wao.shape), _const_spec(wso.shape), _const_spec(wo.shape)],
        out_specs=pl.BlockSpec((tm, d), row),
        out_shape=jax.ShapeDtypeStruct((r, d), F32),
        compiler_params=_cparams(("arbitrary",)),
        name="mix",
    )(x1, attn, ssd, gates, wao, wso, wo)


def kernel(x_prompt, x_sample, cache_k, cache_v, cache_kidx, state_ssm, state_conv, page_table, meta_tokens,
           g_ffn1, w_ffn1_gate, w_ffn1_up, w_ffn1_down, g_mix, w_in, g_q, g_k, conv_w, conv_b, dt_bias, a_log,
           d_skip, g_ssd_norm, w_attn_out, w_ssd_out, w_o, g_ffn2, w_ffn2_gate, w_ffn2_up, w_ffn2_down):
    depth = w_in.shape[0]
    assert depth == 1, "single-layer step"
    n_batch, seq, d_model = x_prompt.shape
    ns = x_sample.shape[0]
    assert x_sample.shape[1] == 1 and seq % TILE == 0 and ns == TILE
    page = cache_k.shape[2]
    assert page == TILE
    l_seq = seq + N_META
    lp = seq + TILE
    r_prompt = n_batch * lp
    r_total = r_prompt + ns
    n_heads = state_ssm.shape[2]
    d_inner = n_heads * SSD_HEAD_DIM
    assert n_heads <= TILE and n_heads % (2 * SSD_GROUPS) == 0
    cdim = d_inner + 2 * SSD_GROUPS * D_STATE
    nq, nkv = N_HEADS * HEAD_DIM, N_KV_HEADS * HEAD_DIM
    nqi = IDX_HEADS * IDX_DIM
    tm = _row_tile(r_total // TILE, 8)

    w = w_in[0]
    offs = [0]
    for n in (nq, nkv, nkv, nqi, IDX_DIM, IDX_HEADS, d_inner, cdim, n_heads, d_model, d_model):
        offs.append(offs[-1] + n)
    col = lambda i: w[:, offs[i]:offs[i + 1]]
    wqk = jnp.concatenate([col(0), col(1)], axis=1).astype(BF16)
    wv = col(2).astype(BF16)
    widx = jnp.concatenate([col(3), col(4), col(5), jnp.zeros((d_model, TILE - IDX_DIM - IDX_HEADS), F32)], axis=1)
    wz = col(6).astype(BF16)
    wx = col(7).astype(BF16)
    wdt = jnp.concatenate([col(8), jnp.zeros((d_model, TILE - n_heads), F32)], axis=1).astype(BF16)
    wgate = jnp.concatenate([col(9), col(10)], axis=1).astype(BF16)
    n_seg = (nq + nkv) // HEAD_DIM
    seg = (jnp.arange(nq + nkv)[:, None] // HEAD_DIM == jnp.arange(TILE)[None, :]).astype(BF16)
    segt = seg.T
    del n_seg
    gqk = jnp.concatenate([jnp.tile(g_q[0], N_HEADS), jnp.tile(g_k[0], N_KV_HEADS)])[None, :]
    row1 = lambda v: v.reshape(1, -1)
    pad_heads = lambda v, fill: jnp.concatenate([v, jnp.full((TILE - n_heads,), fill, F32)])[None, :]
    dtb = pad_heads(dt_bias[0], -1e4)
    alog = pad_heads(a_log[0], 0.0)
    dsk = jnp.repeat(d_skip[0], SSD_HEAD_DIM)[None, :]
    e_mat = (jnp.arange(TILE)[:, None] == jnp.arange(d_inner)[None, :] // SSD_HEAD_DIM).astype(F32)
    gn = row1(g_ssd_norm[0])

    meta = jnp.broadcast_to(meta_tokens[None].astype(F32), (n_batch, N_META, d_model))
    xp = jnp.concatenate([jnp.zeros((n_batch, PAD_FRONT, d_model), F32), meta, x_prompt], axis=1)
    x_all = jnp.concatenate([xp.reshape(r_prompt, d_model), x_sample.reshape(ns, d_model)], axis=0)

    x1, hb = _ffn(x_all, row1(g_ffn1[0]), row1(g_mix[0]), w_ffn1_gate[0].astype(BF16),
                  w_ffn1_up[0].astype(BF16), w_ffn1_down[0].astype(BF16), tm)
    q, k, v, kb, vb, qi, kiw = _proj_attn(x1, row1(g_mix[0]), wqk, wv, widx, seg, segt, gqk, tm)
    xbc, dtr = _proj_xbc(hb, wx, wdt, tm)
    zs, gates = _proj_gate(hb, wz, wgate, tm)

    ksel_p = min(TOP_K_MAX, l_seq // 4)
    attn = _dsa_prompt(q, qi, kiw, kb, vb, n_batch, lp, ksel_p)
    ssd, ssm_p, conv_p = _ssd_prompt(xbc, dtr, zs, conv_w[0], row1(conv_b[0]), dtb, alog, dsk, gn, e_mat,
                                     n_batch, lp, n_heads)

    n_pages = page_table.shape[1]
    ksel_s = min(TOP_K_MAX, (n_pages * page + 1) // 4)
    smp = lambda a: a[r_prompt:]
    scores = _idx_sample(page_table, smp(qi).reshape(ns, 1, nqi), smp(kiw).reshape(ns, 1, TILE), cache_kidx[0])
    bias = _select_sample(scores.reshape(n_pages + 1, ns, TILE), ksel_s).reshape(n_pages + 1, ns, 1, TILE)
    ck = cache_k[0].reshape(cache_k.shape[1], page, nkv)
    cv = cache_v[0].reshape(cache_v.shape[1], page, nkv)
    attn_s = _attn_sample(page_table, smp(q).reshape(ns, 1, nq), smp(k).reshape(ns, 1, nkv),
                          smp(v).reshape(ns, 1, nkv), bias, ck, cv)
    y_s, ssm_s, conv_s = _ssd_sample(smp(xbc), smp(dtr), state_conv[0].reshape(ns, -1), state_ssm[0].reshape(ns, d_inner, D_STATE),
                                     conv_w[0], row1(conv_b[0]), dtb, alog, dsk, e_mat, n_heads, 8)
    attn, ssd = _fill_sample(attn, ssd, attn_s.reshape(ns, nq), y_s, zs, gn, r_prompt // ns)

    x2 = _mix(x1, attn, ssd, gates, w_attn_out[0].astype(BF16), w_ssd_out[0].astype(BF16),
              w_o[0].astype(BF16), tm)
    y, _ = _ffn(x2, row1(g_ffn2[0]), row1(g_ffn2[0]), w_ffn2_gate[0].astype(BF16),
                w_ffn2_up[0].astype(BF16), w_ffn2_down[0].astype(BF16), tm)

    pr = lambda a: a[:r_prompt].reshape(n_batch, lp, -1)[:, PAD_FRONT:]
    y_prompt = y[:r_prompt].reshape(n_batch, lp, d_model)[:, TILE:]
    y_sample = y[r_prompt:].reshape(ns, 1, d_model)
    k_prompt = pr(k).reshape(1, n_batch, l_seq, N_KV_HEADS, HEAD_DIM)
    v_prompt = pr(v).reshape(1, n_batch, l_seq, N_KV_HEADS, HEAD_DIM)
    kidx_prompt = pr(kiw)[:, :, :IDX_DIM].reshape(1, n_batch, l_seq, IDX_DIM)
    ssm_prompt = ssm_p.reshape(1, n_batch, n_heads, SSD_HEAD_DIM, D_STATE)
    conv_prompt = conv_p.reshape(1, n_batch, CONV_W - 1, cdim)
    k_sample = smp(k).reshape(1, ns, 1, N_KV_HEADS, HEAD_DIM)
    v_sample = smp(v).reshape(1, ns, 1, N_KV_HEADS, HEAD_DIM)
    kidx_sample = smp(kiw)[:, :IDX_DIM].reshape(1, ns, 1, IDX_DIM)
    ssm_sample = ssm_s.reshape(1, ns, n_heads, SSD_HEAD_DIM, D_STATE)
    conv_sample = conv_s.reshape(1, ns, CONV_W - 1, cdim)
    return (y_prompt, y_sample, k_prompt, v_prompt, kidx_prompt, ssm_prompt, conv_prompt,
            k_sample, v_sample, kidx_sample, ssm_sample, conv_sample)
```

```python
import functools

import jax
import jax.numpy as jnp
from jax import lax
from jax.experimental import pallas as pl
from jax.experimental.pallas import tpu as pltpu

F32 = jnp.float32
BF16 = jnp.bfloat16
I32 = jnp.int32
HIGHEST = lax.Precision.HIGHEST

EPS = 1e-6
N_META = 16
N_HEADS = 16
N_KV_HEADS = 4
HEAD_DIM = 64
Q_PER_KV = N_HEADS // N_KV_HEADS
IDX_HEADS = 4
IDX_DIM = 64
TOP_K_MAX = 256
SSD_HEAD_DIM = 64
SSD_GROUPS = 4
D_STATE = 128
CONV_W = 4
TILE = 128
PAD_FRONT = TILE - N_META
INT_MIN = -(2 ** 31)
NEG_BIG = -(2.0 ** 100)
Q_SCALE = HEAD_DIM ** -0.5 * 1.4426950408889634
VMEM_LIMIT = 56 * 1024 * 1024


def _cparams(sem):
    return pltpu.CompilerParams(dimension_semantics=sem, vmem_limit_bytes=VMEM_LIMIT)


def _const_spec(shape):
    nd = len(shape)
    return pl.BlockSpec(shape, lambda *_: (0,) * nd, pipeline_mode=pl.Buffered(1))


def _rms(x, g):
    return x * lax.rsqrt(jnp.mean(x * x, axis=-1, keepdims=True) + EPS) * g


def _silu(x):
    return x * jax.nn.sigmoid(x)


def _row_tile(n_tiles, cap):
    for k in range(cap, 0, -1):
        if n_tiles % k == 0:
            return k * TILE
    return TILE


def _ffn_kernel(x_ref, g_ref, g2_ref, wg_ref, wu_ref, wd_ref, o_ref, h2_ref, *, ck):
    x = x_ref[...]
    h = _rms(x, g_ref[...]).astype(BF16)
    acc = jnp.zeros(x.shape, F32)
    for c in range(wg_ref.shape[1] // ck):
        sl = slice(c * ck, (c + 1) * ck)
        gate = jnp.dot(h, wg_ref[:, sl], preferred_element_type=F32)
        up = jnp.dot(h, wu_ref[:, sl], preferred_element_type=F32)
        act = (_silu(gate) * up).astype(BF16)
        acc = acc + jnp.dot(act, wd_ref[sl, :], preferred_element_type=F32)
    y = x + 0.5 * acc
    o_ref[...] = y
    h2_ref[...] = _rms(y, g2_ref[...]).astype(BF16)


def _ffn(x, g, g2, wg, wu, wd, tm):
    r, d = x.shape
    dff = wg.shape[1]
    row = lambda i: (i, 0)
    return pl.pallas_call(
        functools.partial(_ffn_kernel, ck=256),
        grid=(r // tm,),
        in_specs=[pl.BlockSpec((tm, d), row), _const_spec((1, d)), _const_spec((1, d)),
                  _const_spec((d, dff)), _const_spec((d, dff)), _const_spec((dff, d))],
        out_specs=[pl.BlockSpec((tm, d), row), pl.BlockSpec((tm, d), row)],
        out_shape=[jax.ShapeDtypeStruct((r, d), F32), jax.ShapeDtypeStruct((r, d), BF16)],
        compiler_params=_cparams(("arbitrary",)),
        name="ffn",
    )(x, g, g2, wg, wu, wd)


def _proj_attn_kernel(x_ref, g_ref, wqk_ref, wv_ref, widx_ref, seg_ref, segt_ref, gqk_ref,
                      q_ref, k_ref, v_ref, kb_ref, vb_ref, qi_ref, kiw_ref):
    h = _rms(x_ref[...], g_ref[...])
    hb = h.astype(BF16)
    qk = jnp.dot(hb, wqk_ref[...], preferred_element_type=F32)
    ss = jnp.dot((qk * qk).astype(BF16), seg_ref[...], preferred_element_type=F32)
    r = lax.rsqrt(ss * (1.0 / HEAD_DIM) + EPS)
    r_hi = r.astype(BF16)
    r_lo = (r - r_hi.astype(F32)).astype(BF16)
    rx = (jnp.dot(r_hi, segt_ref[...], preferred_element_type=F32)
          + jnp.dot(r_lo, segt_ref[...], preferred_element_type=F32))
    qkn = qk * rx * gqk_ref[...]
    nq = q_ref.shape[1]
    q_ref[...] = (qkn[:, :nq] * Q_SCALE).astype(BF16)
    k = qkn[:, nq:]
    k_ref[...] = k
    kb_ref[...] = k.astype(BF16)
    v = jnp.dot(hb, wv_ref[...], preferred_element_type=F32)
    v_ref[...] = v
    vb_ref[...] = v.astype(BF16)
    idx = jnp.dot(h, widx_ref[...], preferred_element_type=F32, precision=HIGHEST)
    nqi = qi_ref.shape[1]
    qi_ref[...] = idx[:, :nqi]
    kiw_ref[...] = idx[:, nqi:]


def _proj_attn(x1, g, wqk, wv, widx, seg, segt, gqk, tm):
    r, d = x1.shape
    nqk, nv = wqk.shape[1], wv.shape[1]
    nq = N_HEADS * HEAD_DIM
    nqi = IDX_HEADS * IDX_DIM
    row = lambda i: (i, 0)
    outs = [(nq, BF16), (nqk - nq, F32), (nv, F32), (nqk - nq, BF16), (nv, BF16), (nqi, F32), (TILE, F32)]
    return pl.pallas_call(
        _proj_attn_kernel,
        grid=(r // tm,),
        in_specs=[pl.BlockSpec((tm, d), row), _const_spec((1, d)), _const_spec(wqk.shape),
                  _const_spec(wv.shape), _const_spec(widx.shape), _const_spec(seg.shape),
                  _const_spec(segt.shape), _const_spec(gqk.shape)],
        out_specs=[pl.BlockSpec((tm, n), row) for n, _ in outs],
        out_shape=[jax.ShapeDtypeStruct((r, n), dt) for n, dt in outs],
        compiler_params=_cparams(("arbitrary",)),
        name="proj_attn",
    )(x1, g, wqk, wv, widx, seg, segt, gqk)


def _proj_xbc_kernel(h_ref, wx_ref, wdt_ref, xbc_ref, dt_ref):
    h = h_ref[...]
    xbc_ref[...] = jnp.dot(h, wx_ref[...], preferred_element_type=F32)
    dt_ref[...] = jnp.dot(h, wdt_ref[...], preferred_element_type=F32)


def _proj_xbc(hb, wx, wdt, tm):
    r, d = hb.shape
    row = lambda i: (i, 0)
    return pl.pallas_call(
        _proj_xbc_kernel,
        grid=(r // tm,),
        in_specs=[pl.BlockSpec((tm, d), row), _const_spec(wx.shape), _const_spec(wdt.shape)],
        out_specs=[pl.BlockSpec((tm, wx.shape[1]), row), pl.BlockSpec((tm, wdt.shape[1]), row)],
        out_shape=[jax.ShapeDtypeStruct((r, wx.shape[1]), F32),
                   jax.ShapeDtypeStruct((r, wdt.shape[1]), F32)],
        compiler_params=_cparams(("arbitrary",)),
        name="proj_xbc",
    )(hb, wx, wdt)


def _proj_gate_kernel(h_ref, wz_ref, wg_ref, zs_ref, gt_ref):
    h = h_ref[...]
    zs_ref[...] = _silu(jnp.dot(h, wz_ref[...], preferred_element_type=F32)).astype(BF16)
    gt_ref[...] = jax.nn.sigmoid(jnp.dot(h, wg_ref[...], preferred_element_type=F32)).astype(BF16)


def _proj_gate(hb, wz, wg, tm):
    r, d = hb.shape
    row = lambda i: (i, 0)
    return pl.pallas_call(
        _proj_gate_kernel,
        grid=(r // tm,),
        in_specs=[pl.BlockSpec((tm, d), row), _const_spec(wz.shape), _const_spec(wg.shape)],
        out_specs=[pl.BlockSpec((tm, wz.shape[1]), row), pl.BlockSpec((tm, wg.shape[1]), row)],
        out_shape=[jax.ShapeDtypeStruct((r, wz.shape[1]), BF16),
                   jax.ShapeDtypeStruct((r, wg.shape[1]), BF16)],
        compiler_params=_cparams(("arbitrary",)),
        name="proj_gate",
    )(hb, wz, wg)


def _sortable(score):
    b = pltpu.bitcast(score, I32)
    return jnp.where(b < 0, -(b & 0x7FFFFFFF), b)


KEY_GROUP = 4


def _tree_sum(x):
    while x.shape[0] > 1:
        h = x.shape[0] // 2
        x = x[:h] + x[h:]
    return x[0]


def _count(keys_ref, n_groups, pred):
    def body(gi, acc):
        k = keys_ref[pl.ds(gi * KEY_GROUP, KEY_GROUP)]
        hit = jnp.where(pred(k, gi * (KEY_GROUP * TILE)), 1, 0)
        return acc + _tree_sum(hit.reshape(KEY_GROUP * TILE // 8, 8, TILE))
    part = lax.fori_loop(0, n_groups, body, jnp.zeros((8, TILE), I32))
    return jnp.sum(part, axis=0, keepdims=True)


def _kth_largest(keys_ref, n_groups, ksel):
    n_ge0 = _count(keys_ref, n_groups, lambda k, r0: k >= 0)
    t = jnp.where(n_ge0 >= ksel, jnp.zeros((1, TILE), I32), jnp.full((1, TILE), INT_MIN, I32))

    def bit_step(i, t):
        cand = t | jnp.left_shift(jnp.int32(1), 30 - i)
        n = _count(keys_ref, n_groups, lambda k, r0: k >= cand)
        return jnp.where(n >= ksel, cand, t)

    return lax.fori_loop(0, 31, bit_step, t)


def _tie_limit(keys_ref, n_groups, t, ksel, n_bits):
    shape = (KEY_GROUP, TILE, TILE)
    rows = lax.broadcasted_iota(I32, shape, 0) * TILE + lax.broadcasted_iota(I32, shape, 1)
    n_gt = _count(keys_ref, n_groups, lambda k, r0: k > t)
    n_eq = _count(keys_ref, n_groups, lambda k, r0: k == t)
    need = ksel - n_gt
    live = t != INT_MIN
    conflict = jnp.max(jnp.where(live & (n_eq > need), 1, 0)) > 0

    def search():
        def bit_step(i, p):
            cand = p | jnp.left_shift(jnp.int32(1), n_bits - 1 - i)
            n = _count(keys_ref, n_groups, lambda k, r0: (k == t) & ((rows + r0) < cand))
            return jnp.where(n < need, cand, p)
        return lax.fori_loop(0, n_bits, bit_step, jnp.zeros((1, TILE), I32)) + 1

    lim = lax.cond(conflict, search, lambda: jnp.full((1, TILE), 1 << n_bits, I32))
    return jnp.where(live, lim, 0)


def _split_bf16(x):
    hi = x.astype(BF16)
    return hi, (x - hi.astype(F32)).astype(BF16)


def _dot3(a_hi, a_lo, b_hi, b_lo):
    return (jnp.dot(a_hi, b_hi, preferred_element_type=F32)
            + jnp.dot(a_hi, b_lo, preferred_element_type=F32)
            + jnp.dot(a_lo, b_hi, preferred_element_type=F32))


VROWS = HEAD_DIM + 16


def _dsa_prompt_kernel(q_ref, qi_ref, kiwq_ref, kb_ref, vb_ref, kiw_ref, o_ref,
                       keys_ref, kk_ref, vt_ref, kihl_ref, rhs_ref, qa_ref, qb_ref, m_ref, acc_ref,
                       *, ksel, n_bits):
    b = pl.program_id(0)
    j = pl.program_id(1)
    nt = vb_ref.shape[0] // TILE
    n_groups = lax.shift_right_logical(j + KEY_GROUP, KEY_GROUP.bit_length() - 1)
    n_pairs = lax.shift_right_logical(j + 2, 1)
    nkv = N_KV_HEADS
    gw = Q_PER_KV * TILE
    eye = (lax.broadcasted_iota(I32, (TILE, TILE), 0) == lax.broadcasted_iota(I32, (TILE, TILE), 1))

    @pl.when((b == 0) & (j == 0))
    def _():
        rhs_ref[...] = jnp.zeros(rhs_ref.shape, rhs_ref.dtype)
        ident = jnp.where(eye, 1.0, 0.0).astype(BF16)
        for g in range(nkv):
            for hh in range(Q_PER_KV):
                rhs_ref[g, 0:TILE, hh * TILE:(hh + 1) * TILE] = ident
        qa_ref[...] = jnp.zeros(qa_ref.shape, qa_ref.dtype)
        qb_ref[...] = jnp.zeros(qb_ref.shape, qb_ref.dtype)
        kk_ref[...] = jnp.zeros(kk_ref.shape, kk_ref.dtype)
        kihl_ref[...] = jnp.zeros(kihl_ref.shape, kihl_ref.dtype)
        ones_row = jnp.where(lax.broadcasted_iota(I32, (VROWS - HEAD_DIM, TILE), 0) == 0, 1.0, 0.0).astype(BF16)
        for c in range(vt_ref.shape[0]):
            vt_ref[c] = jnp.zeros(vt_ref.shape[1:], vt_ref.dtype)
            for g in range(nkv):
                vt_ref[c, g * VROWS + HEAD_DIM:(g + 1) * VROWS, :] = ones_row

    @pl.when(j == 0)
    def _():
        def prep(c, _):
            rows = pl.ds(pl.multiple_of(c * TILE, TILE), TILE)
            kk_ref[c] = kb_ref[rows, :]
            vt = vb_ref[rows, :].astype(F32).T.astype(BF16)
            for g in range(nkv):
                vt_ref[c, g * VROWS:g * VROWS + HEAD_DIM, :] = vt[g * HEAD_DIM:(g + 1) * HEAD_DIM, :]
            hi, lo = _split_bf16(kiw_ref[rows, :])
            kihl_ref[c] = jnp.concatenate([hi, lo], axis=1)
            return 0
        lax.fori_loop(0, nt, prep, 0)

    qt = q_ref[...].astype(F32).T.astype(BF16)
    for h in range(N_HEADS):
        g, hh = divmod(h, Q_PER_KV)
        r0 = TILE + (g % 2) * HEAD_DIM
        rhs_ref[g, r0:r0 + HEAD_DIM, hh * TILE:(hh + 1) * TILE] = qt[h * HEAD_DIM:(h + 1) * HEAD_DIM, :]
    qit = qi_ref[...].T
    for h in range(IDX_HEADS):
        hi, lo = _split_bf16(qit[h * IDX_DIM:(h + 1) * IDX_DIM, :])
        qa_ref[0:IDX_DIM, h * TILE:(h + 1) * TILE] = hi
        qa_ref[TILE:TILE + IDX_DIM, h * TILE:(h + 1) * TILE] = hi
        qb_ref[0:IDX_DIM, h * TILE:(h + 1) * TILE] = lo
    wq = kiwq_ref[...].T[IDX_DIM:IDX_DIM + 8, :]
    wq = wq * (IDX_HEADS ** -0.5 * IDX_DIM ** -0.5)

    grp_rows = KEY_GROUP * TILE
    q_pos = j * TILE + lax.broadcasted_iota(I32, (grp_rows, TILE), 1)
    k_row = lax.broadcasted_iota(I32, (grp_rows, TILE), 0)

    def score_group(gi, _):
        khl = kihl_ref[pl.ds(gi * KEY_GROUP, KEY_GROUP)].reshape(grp_rows, 2 * TILE)
        d = (jnp.dot(khl, qa_ref[...], preferred_element_type=F32)
             + jnp.dot(khl, qb_ref[...], preferred_element_type=F32))
        s = jnp.zeros((grp_rows, TILE), F32)
        for h in range(IDX_HEADS):
            s = s + wq[h:h + 1, :] * jnp.maximum(d[:, h * TILE:(h + 1) * TILE], 0.0)
        k_pos = k_row + gi * grp_rows
        valid = (k_pos <= q_pos) & (k_pos >= PAD_FRONT)
        keys = jnp.where(valid, _sortable(s), INT_MIN)
        keys_ref[pl.ds(gi * KEY_GROUP, KEY_GROUP)] = keys.reshape(KEY_GROUP, TILE, TILE)
        return 0
    lax.fori_loop(0, n_groups, score_group, 0)

    t = _kth_largest(keys_ref, n_groups, ksel)
    lim = _tie_limit(keys_ref, n_groups, t, ksel, n_bits)

    m_ref[...] = jnp.full(m_ref.shape, NEG_BIG, F32)
    acc_ref[...] = jnp.zeros(acc_ref.shape, F32)

    pair_shape = (2, TILE, TILE)
    pair_row = lax.broadcasted_iota(I32, pair_shape, 0) * TILE + lax.broadcasted_iota(I32, pair_shape, 1)

    def attend_pair(i, _):
        keys = keys_ref[pl.ds(2 * i, 2)]
        sel = (keys > t) | ((keys == t) & ((pair_row + i * (2 * TILE)) < lim))
        bias = jnp.where(sel, 0.0, NEG_BIG).astype(BF16).reshape(2 * TILE, TILE)
        kc = kk_ref[pl.ds(2 * i, 2)].reshape(2 * TILE, nkv * HEAD_DIM)
        vt = jnp.concatenate([vt_ref[2 * i], vt_ref[2 * i + 1]], axis=1)
        m_all = m_ref[...]
        s = []
        for g in range(nkv):
            half = (g // 2) * TILE
            lhs = jnp.concatenate([bias, kc[:, half:half + TILE]], axis=1)
            s.append(jnp.dot(lhs, rhs_ref[g], preferred_element_type=F32))
        m_new = [jnp.maximum(m_all[g:g + 1, :], jnp.max(s[g], axis=0, keepdims=True)) for g in range(nkv)]
        for g in range(nkv):
            alpha = jnp.exp2(m_all[g:g + 1, :] - m_new[g])
            p = jnp.exp2(s[g] - m_new[g]).astype(BF16)
            pv = jnp.dot(vt[g * VROWS:(g + 1) * VROWS, :], p, preferred_element_type=F32)
            acc_ref[g] = alpha * acc_ref[g] + pv
        m_ref[...] = jnp.concatenate(m_new + [m_all[nkv:, :]], axis=0)
        return 0
    lax.fori_loop(0, n_pairs, attend_pair, 0)

    pieces = []
    for g in range(nkv):
        acc = acc_ref[g]
        o = acc[:HEAD_DIM] / acc[HEAD_DIM:HEAD_DIM + 1]
        for hh in range(Q_PER_KV):
            pieces.append(o[:, hh * TILE:(hh + 1) * TILE])
    o_ref[...] = jnp.concatenate(pieces, axis=0).T.astype(BF16)


def _dsa_prompt(q, qi, kiw, kb, vb, n_batch, lp, ksel):
    nt = lp // TILE
    nt_pad = -(-nt // KEY_GROUP) * KEY_GROUP
    r_total = q.shape[0]
    qw = N_HEADS * HEAD_DIM
    kvw = N_KV_HEADS * HEAD_DIM
    gw = Q_PER_KV * TILE
    tile = lambda b, j: (b * nt + j, 0)
    seq = lambda b, j: (b, 0)
    n_bits = max(1, (lp - 1).bit_length())
    return pl.pallas_call(
        functools.partial(_dsa_prompt_kernel, ksel=ksel, n_bits=n_bits),
        grid=(n_batch, nt),
        in_specs=[pl.BlockSpec((TILE, qw), tile), pl.BlockSpec((TILE, IDX_HEADS * IDX_DIM), tile),
                  pl.BlockSpec((TILE, TILE), tile), pl.BlockSpec((lp, kvw), seq),
                  pl.BlockSpec((lp, kvw), seq), pl.BlockSpec((lp, TILE), seq)],
        out_specs=pl.BlockSpec((TILE, qw), tile),
        out_shape=jax.ShapeDtypeStruct((r_total, qw), BF16),
        scratch_shapes=[pltpu.VMEM((nt_pad, TILE, TILE), I32),
                        pltpu.VMEM((nt_pad, TILE, kvw), BF16),
                        pltpu.VMEM((nt_pad, N_KV_HEADS * VROWS, TILE), BF16),
                        pltpu.VMEM((nt_pad, TILE, 2 * TILE), BF16),
                        pltpu.VMEM((N_KV_HEADS, 2 * TILE, gw), BF16),
                        pltpu.VMEM((2 * TILE, IDX_HEADS * TILE), BF16),
                        pltpu.VMEM((2 * TILE, IDX_HEADS * TILE), BF16),
                        pltpu.VMEM((8, gw), F32),
                        pltpu.VMEM((N_KV_HEADS, VROWS, gw), F32)],
        compiler_params=_cparams(("arbitrary", "arbitrary")),
        name="dsa_prompt",
    )(q, qi, kiw, kb, vb, kiw)


def _expand_heads(v, e_ref):
    return jnp.dot(v, e_ref[...], preferred_element_type=F32, precision=HIGHEST)


def _softplus(x):
    return jnp.maximum(x, 0.0) + jnp.log1p(jnp.exp(-jnp.abs(x)))


def _ssd_prompt_kernel(xbc_ref, dtr_ref, zs_ref, cw_ref, cb_ref, dtb_ref, alog_ref, dsk_ref, gn_ref,
                       e_ref, y_ref, ssm_ref, conv_ref, xp_ref, ht_ref, *, n_heads):
    c = pl.program_id(1)
    nc = pl.num_programs(1)
    d_inner = n_heads * SSD_HEAD_DIM
    hpg = n_heads // SSD_GROUPS
    gwid = hpg * SSD_HEAD_DIM
    gn_w = d_inner // SSD_GROUPS

    @pl.when(c == 0)
    def _():
        xp_ref[0:8, :] = jnp.zeros((8, xp_ref.shape[1]), F32)
        ht_ref[...] = jnp.zeros(ht_ref.shape, F32)

    xp_ref[8:8 + TILE, :] = xbc_ref[...]
    u = cb_ref[...] + cw_ref[CONV_W - 1:CONV_W, :] * xp_ref[8:8 + TILE, :]
    for k in range(CONV_W - 1):
        off = 8 - (CONV_W - 1) + k
        u = u + cw_ref[k:k + 1, :] * xp_ref[off:off + TILE, :]
    tail = xp_ref[8 + TILE - (CONV_W - 1):8 + TILE, :]
    xp_ref[8 - (CONV_W - 1):8, :] = tail

    row = lax.broadcasted_iota(I32, (TILE, 1), 0)
    live = jnp.where((c > 0) | (row >= PAD_FRONT), 1.0, 0.0)
    act = _silu(u) * live
    xs = act[:, :d_inner]
    bm = act[:, d_inner:d_inner + SSD_GROUPS * D_STATE]
    cm = act[:, d_inner + SSD_GROUPS * D_STATE:]

    dt = _softplus(dtr_ref[...] + dtb_ref[...]) * live
    a = dt * (-jnp.exp(alog_ref[...]))
    ri = lax.broadcasted_iota(I32, (TILE, TILE), 0)
    ci = lax.broadcasted_iota(I32, (TILE, TILE), 1)
    tril = ri >= ci
    acs = jnp.dot(jnp.where(tril, 1.0, 0.0), a, preferred_element_type=F32, precision=HIGHEST)
    acs_t = acs.T
    dt_t = dt.T
    acs_last = acs[TILE - 1:TILE, :]
    w_in = _expand_heads(jnp.exp(acs_last - acs) * dt, e_ref)
    w_out = _expand_heads(jnp.exp(acs), e_ref)
    xw = (xs * w_in).astype(BF16)
    xsb = xs.astype(BF16)
    lane = lax.broadcasted_iota(I32, (1, gwid), 1)

    ys = []
    for g in range(SSD_GROUPS):
        bg = bm[:, g * D_STATE:(g + 1) * D_STATE]
        cg = cm[:, g * D_STATE:(g + 1) * D_STATE].astype(BF16)
        bgt = bg.T.astype(BF16)
        cb = jnp.dot(cg, bgt, preferred_element_type=F32)
        xw_g = xw[:, g * gwid:(g + 1) * gwid]
        xs_g = xsb[:, g * gwid:(g + 1) * gwid]
        h_prev = ht_ref[g]
        y_off = jnp.dot(cg, h_prev.astype(BF16), preferred_element_type=F32)
        y_g = y_off * w_out[:, g * gwid:(g + 1) * gwid]
        for e in range(hpg):
            he = g * hpg + e
            seg = acs[:, he:he + 1] - acs_t[he:he + 1, :]
            m = (cb * jnp.exp(jnp.where(tril, seg, -jnp.inf)) * dt_t[he:he + 1, :]).astype(BF16)
            only = (lane >= e * SSD_HEAD_DIM) & (lane < (e + 1) * SSD_HEAD_DIM)
            y_g = y_g + jnp.dot(m, jnp.where(only, xs_g, jnp.zeros_like(xs_g)),
                                preferred_element_type=F32)
        states = jnp.dot(bgt, xw_g, preferred_element_type=F32)
        ht_ref[g] = h_prev * w_out[TILE - 1:TILE, g * gwid:(g + 1) * gwid] + states
        ys.append(y_g)
    y = jnp.concatenate(ys, axis=1) + dsk_ref[...] * xs
    y = y * zs_ref[...].astype(F32)
    outs = []
    for g in range(SSD_GROUPS):
        yg = y[:, g * gn_w:(g + 1) * gn_w]
        outs.append(yg * lax.rsqrt(jnp.mean(yg * yg, axis=-1, keepdims=True) + EPS))
    y_ref[...] = (jnp.concatenate(outs, axis=1) * gn_ref[...]).astype(BF16)

    @pl.when(c == nc - 1)
    def _():
        for g in range(SSD_GROUPS):
            ssm_ref[0, g * gwid:(g + 1) * gwid, :] = ht_ref[g].T
        conv_ref[0] = tail


def _ssd_prompt(xbc, dtr, zs, cw, cb, dtb, alog, dsk, gn, e_mat, n_batch, lp, n_heads):
    nt = lp // TILE
    r_total = xbc.shape[0]
    d_inner = n_heads * SSD_HEAD_DIM
    cdim = xbc.shape[1]
    tile = lambda b, c: (b * nt + c, 0)
    fix = lambda b, c: (0, 0)
    return pl.pallas_call(
        functools.partial(_ssd_prompt_kernel, n_heads=n_heads),
        grid=(n_batch, nt),
        in_specs=[pl.BlockSpec((TILE, cdim), tile), pl.BlockSpec((TILE, TILE), tile),
                  pl.BlockSpec((TILE, d_inner), tile),
                  pl.BlockSpec(cw.shape, fix), pl.BlockSpec(cb.shape, fix), pl.BlockSpec(dtb.shape, fix),
                  pl.BlockSpec(alog.shape, fix), pl.BlockSpec(dsk.shape, fix), pl.BlockSpec(gn.shape, fix),
                  pl.BlockSpec(e_mat.shape, fix)],
        out_specs=[pl.BlockSpec((TILE, d_inner), tile),
                   pl.BlockSpec((1, d_inner, D_STATE), lambda b, c: (b, 0, 0)),
                   pl.BlockSpec((1, CONV_W - 1, cdim), lambda b, c: (b, 0, 0))],
        out_shape=[jax.ShapeDtypeStruct((r_total, d_inner), BF16),
                   jax.ShapeDtypeStruct((n_batch, d_inner, D_STATE), F32),
                   jax.ShapeDtypeStruct((n_batch, CONV_W - 1, cdim), F32)],
        scratch_shapes=[pltpu.VMEM((8 + TILE, cdim), F32),
                        pltpu.VMEM((SSD_GROUPS, D_STATE, d_inner // SSD_GROUPS), F32)],
        compiler_params=_cparams(("arbitrary", "arbitrary")),
        name="ssd_prompt",
    )(xbc, dtr, zs, cw, cb, dtb, alog, dsk, gn, e_mat)


def _idx_sample_kernel(pt_ref, qi_ref, kiw_ref, *rest, n_pages):
    pages, s_ref = rest[:n_pages], rest[n_pages]
    qi = qi_ref[0]
    kiw = kiw_ref[0]
    q8 = jnp.concatenate([qi[:, h * IDX_DIM:(h + 1) * IDX_DIM] for h in range(IDX_HEADS)]
                         + [jnp.zeros((8 - IDX_HEADS, IDX_DIM), F32)], axis=0)
    w_col = jnp.broadcast_to(kiw, (TILE, TILE)).T[IDX_DIM:IDX_DIM + 8, 0:1]
    w_col = w_col * (IDX_HEADS ** -0.5 * IDX_DIM ** -0.5)
    q_hi, q_lo = _split_bf16(q8)
    for p in range(n_pages):
        k_hi, k_lo = _split_bf16(pages[p][0])
        d = _dot3(q_hi, q_lo, k_hi, k_lo)
        s_ref[p, 0] = jnp.sum(w_col * jnp.maximum(d, 0.0), axis=0, keepdims=True)
    d_new = jnp.sum(q8 * kiw[:, :IDX_DIM], axis=1, keepdims=True)
    s_new = jnp.sum(w_col * jnp.maximum(d_new, 0.0), axis=0, keepdims=True)
    lane = lax.broadcasted_iota(I32, (1, TILE), 1)
    s_ref[n_pages, 0] = jnp.where(lane == 0, s_new, 0.0)


def _idx_sample(page_table, qi_s, kiw_s, cache_kidx):
    ns, n_pages = page_table.shape
    n_slots = n_pages + 1
    page = lambda p: pl.BlockSpec((1, IDX_DIM, TILE), lambda s, pt: (pt[s, p], 0, 0))
    one = lambda s, pt: (s, 0, 0)
    grid_spec = pltpu.PrefetchScalarGridSpec(
        num_scalar_prefetch=1, grid=(ns,),
        in_specs=[pl.BlockSpec((1, 1, qi_s.shape[2]), one), pl.BlockSpec((1, 1, TILE), one)]
        + [page(p) for p in range(n_pages)],
        out_specs=pl.BlockSpec((n_slots, 1, 1, TILE), lambda s, pt: (0, s, 0, 0)))
    return pl.pallas_call(
        functools.partial(_idx_sample_kernel, n_pages=n_pages),
        grid_spec=grid_spec,
        out_shape=jax.ShapeDtypeStruct((n_slots, ns, 1, TILE), F32),
        compiler_params=_cparams(("arbitrary",)),
        name="idx_sample",
    )(page_table, qi_s, kiw_s, *([cache_kidx] * n_pages))


def _select_sample_kernel(s_ref, bias_ref, keys_ref, *, n_pages, ksel, n_bits):
    n_slots = n_pages + 1
    rows = lax.broadcasted_iota(I32, (TILE, TILE), 0)
    for p in range(n_slots):
        key = _sortable(s_ref[p].T)
        if p == n_pages:
            key = jnp.where(rows == 0, key, INT_MIN)
        keys_ref[p] = key
    for p in range(n_slots, keys_ref.shape[0]):
        keys_ref[p] = jnp.full((TILE, TILE), INT_MIN, I32)
    n_groups = keys_ref.shape[0] // KEY_GROUP
    t = _kth_largest(keys_ref, n_groups, ksel)
    lim = _tie_limit(keys_ref, n_groups, t, ksel, n_bits)
    for p in range(n_slots):
        keys = keys_ref[p]
        sel = (keys > t) | ((keys == t) & ((rows + p * TILE) < lim))
        bias_ref[p] = jnp.where(sel, 0.0, NEG_BIG).T


def _select_sample(scores, ksel):
    n_slots, ns, _ = scores.shape
    n_pages = n_slots - 1
    n_bits = (n_slots * TILE - 1).bit_length()
    return pl.pallas_call(
        functools.partial(_select_sample_kernel, n_pages=n_pages, ksel=ksel, n_bits=n_bits),
        out_shape=jax.ShapeDtypeStruct((n_slots, ns, TILE), F32),
        scratch_shapes=[pltpu.VMEM((-(-n_slots // KEY_GROUP) * KEY_GROUP, TILE, TILE), I32)],
        compiler_params=pltpu.CompilerParams(vmem_limit_bytes=VMEM_LIMIT),
        name="select_sample",
    )(scores)


def _attn_sample_kernel(pt_ref, q_ref, kn_ref, vn_ref, bias_ref, *rest, n_pages):
    kpages, vpages, o_ref = rest[:n_pages], rest[n_pages:2 * n_pages], rest[2 * n_pages]
    kvw = N_KV_HEADS * HEAD_DIM
    q = q_ref[0].astype(F32)
    lane = lax.broadcasted_iota(I32, (N_HEADS, kvw), 1)
    head = lax.broadcasted_iota(I32, (N_HEADS, kvw), 0)
    own = jnp.right_shift(lane, 6) == jnp.right_shift(head, 2)
    qrows = []
    for h in range(N_HEADS):
        qh = q[:, h * HEAD_DIM:(h + 1) * HEAD_DIM]
        qrows.append(jnp.concatenate([qh] * N_KV_HEADS, axis=1))
    qf = jnp.where(own, jnp.concatenate(qrows, axis=0), 0.0)
    qw = qf.astype(BF16)
    nt = (((1,), (1,)), ((), ()))
    scores = []
    for p in range(n_pages):
        kp = kpages[p][0].astype(BF16)
        s = jnp.dot(qw, kp, preferred_element_type=F32)
        scores.append(s + bias_ref[p, 0])
    s_new = jnp.sum(qf * kn_ref[0], axis=1, keepdims=True)
    lane1 = lax.broadcasted_iota(I32, (1, TILE), 1)
    scores.append(jnp.where(lane1 == 0, s_new, NEG_BIG) + bias_ref[n_pages, 0])
    m = scores[0].max(axis=1, keepdims=True)
    for s in scores[1:]:
        m = jnp.maximum(m, s.max(axis=1, keepdims=True))
    den = jnp.zeros((N_HEADS, 1), F32)
    out = jnp.zeros((N_HEADS, kvw), F32)
    for p in range(n_pages):
        e = jnp.exp2(scores[p] - m)
        den = den + e.sum(axis=1, keepdims=True)
        out = out + lax.dot_general(e.astype(BF16), vpages[p][0].astype(BF16), nt,
                                    preferred_element_type=F32)
    e = jnp.exp2(scores[n_pages] - m)
    den = den + e.sum(axis=1, keepdims=True)
    out = out + e[:, 0:1] * vn_ref[0]
    out = out / den
    rolled = jnp.concatenate([pltpu.roll(out[:, i * TILE:(i + 1) * TILE], HEAD_DIM, 1)
                              for i in range(kvw // TILE)], axis=1)
    half = lax.broadcasted_iota(I32, (1, TILE), 1) < HEAD_DIM
    pieces = []
    for m2 in range(N_HEADS // 2):
        g = (2 * m2) // Q_PER_KV
        blk = (g // 2) * TILE
        a_src, b_src = (rolled, out) if g % 2 == 1 else (out, rolled)
        a = a_src[2 * m2:2 * m2 + 1, blk:blk + TILE]
        b = b_src[2 * m2 + 1:2 * m2 + 2, blk:blk + TILE]
        pieces.append(jnp.where(half, a, b))
    o_ref[0] = jnp.concatenate(pieces, axis=1)


def _attn_sample(page_table, q_s, k_s, v_s, bias, cache_k, cache_v):
    ns, n_pages = page_table.shape
    kvw = N_KV_HEADS * HEAD_DIM
    qw = N_HEADS * HEAD_DIM
    page = lambda p: pl.BlockSpec((1, kvw, TILE), lambda s, pt: (pt[s, p], 0, 0))
    one = lambda s, pt: (s, 0, 0)
    grid_spec = pltpu.PrefetchScalarGridSpec(
        num_scalar_prefetch=1, grid=(ns,),
        in_specs=[pl.BlockSpec((1, 1, qw), one), pl.BlockSpec((1, 1, kvw), one),
                  pl.BlockSpec((1, 1, kvw), one),
                  pl.BlockSpec((n_pages + 1, 1, 1, TILE), lambda s, pt: (0, s, 0, 0))]
        + [page(p) for p in range(n_pages)] * 2,
        out_specs=pl.BlockSpec((1, 1, qw), one))
    return pl.pallas_call(
        functools.partial(_attn_sample_kernel, n_pages=n_pages),
        grid_spec=grid_spec,
        out_shape=jax.ShapeDtypeStruct((ns, 1, qw), F32),
        compiler_params=_cparams(("arbitrary",)),
        name="attn_sample",
    )(page_table, q_s, k_s, v_s, bias, *([cache_k] * n_pages), *([cache_v] * n_pages))


def _ssd_sample_kernel(xbc_ref, dtr_ref, sc_ref, h0_ref, cw_ref, cb_ref, dtb_ref, alog_ref, dsk_ref,
                       e_ref, y_ref, h_ref, conv_ref, *, n_heads, sb):
    d_inner = n_heads * SSD_HEAD_DIM
    hpg = n_heads // SSD_GROUPS
    new = xbc_ref[...]
    cdim = new.shape[1]
    u = cb_ref[...] + cw_ref[CONV_W - 1:CONV_W, :] * new
    for k in range(CONV_W - 1):
        u = u + cw_ref[k:k + 1, :] * sc_ref[:, k * cdim:(k + 1) * cdim]
    for k in range(CONV_W - 2):
        conv_ref[:, k * cdim:(k + 1) * cdim] = sc_ref[:, (k + 1) * cdim:(k + 2) * cdim]
    conv_ref[:, (CONV_W - 2) * cdim:] = new
    act = _silu(u)
    xs = act[:, :d_inner]
    bm = act[:, d_inner:d_inner + SSD_GROUPS * D_STATE]
    cm = act[:, d_inner + SSD_GROUPS * D_STATE:]
    dt = _softplus(dtr_ref[...] + dtb_ref[...])
    decay = jnp.exp(dt * (-jnp.exp(alog_ref[...])))
    dec_x = _expand_heads(decay, e_ref)
    xdt = xs * _expand_heads(dt, e_ref)
    for i in range(sb):
        bcols, ccols = [], []
        for g in range(SSD_GROUPS):
            brow = bm[i:i + 1, g * D_STATE:(g + 1) * D_STATE]
            crow = cm[i:i + 1, g * D_STATE:(g + 1) * D_STATE]
            bcols.append(jnp.broadcast_to(brow, (TILE, D_STATE)).T)
            ccols.append(jnp.broadcast_to(crow, (TILE, D_STATE)).T)
        pieces = []
        for pr in range(n_heads // 2):
            g = (2 * pr) // hpg
            sl = slice(pr * TILE, (pr + 1) * TILE)
            ht = h0_ref[i, sl, :].T
            hn = ht * dec_x[i:i + 1, sl] + bcols[g] * xdt[i:i + 1, sl]
            pieces.append(jnp.sum(hn * ccols[g], axis=0, keepdims=True))
            h_ref[i, sl, :] = hn.T
        y_ref[i:i + 1, :] = jnp.concatenate(pieces, axis=1) + dsk_ref[...] * xs[i:i + 1, :]


def _ssd_sample(xbc_s, dtr_s, state_conv, state_ssm, cw, cb, dtb, alog, dsk, e_mat, n_heads, sb):
    ns, cdim = xbc_s.shape
    d_inner = n_heads * SSD_HEAD_DIM
    row = lambda i: (i, 0)
    row3 = lambda i: (i, 0, 0)
    fix = lambda i: (0, 0)
    return pl.pallas_call(
        functools.partial(_ssd_sample_kernel, n_heads=n_heads, sb=sb),
        grid=(ns // sb,),
        in_specs=[pl.BlockSpec((sb, cdim), row), pl.BlockSpec((sb, TILE), row),
                  pl.BlockSpec((sb, (CONV_W - 1) * cdim), row), pl.BlockSpec((sb, d_inner, D_STATE), row3),
                  pl.BlockSpec(cw.shape, fix), pl.BlockSpec(cb.shape, fix), pl.BlockSpec(dtb.shape, fix),
                  pl.BlockSpec(alog.shape, fix), pl.BlockSpec(dsk.shape, fix), pl.BlockSpec(e_mat.shape, fix)],
        out_specs=[pl.BlockSpec((sb, d_inner), row), pl.BlockSpec((sb, d_inner, D_STATE), row3),
                   pl.BlockSpec((sb, (CONV_W - 1) * cdim), row)],
        out_shape=[jax.ShapeDtypeStruct((ns, d_inner), F32),
                   jax.ShapeDtypeStruct((ns, d_inner, D_STATE), F32),
                   jax.ShapeDtypeStruct((ns, (CONV_W - 1) * cdim), F32)],
        compiler_params=_cparams(("arbitrary",)),
        name="ssd_sample",
    )(xbc_s, dtr_s, state_conv, state_ssm, cw, cb, dtb, alog, dsk, e_mat)


def _fill_sample_kernel(attn_any, ssd_any, a_ref, y_ref, zs_ref, gn_ref, attn_ref, ssd_ref):
    del attn_any, ssd_any
    attn_ref[...] = a_ref[...].astype(BF16)
    y = y_ref[...] * zs_ref[...].astype(F32)
    gn_w = y.shape[1] // SSD_GROUPS
    outs = []
    for g in range(SSD_GROUPS):
        yg = y[:, g * gn_w:(g + 1) * gn_w]
        outs.append(yg * lax.rsqrt(jnp.mean(yg * yg, axis=-1, keepdims=True) + EPS))
    ssd_ref[...] = (jnp.concatenate(outs, axis=1) * gn_ref[...]).astype(BF16)


def _fill_sample(attn, ssd, attn_s, y_s, zs, gn, blk):
    ns = attn_s.shape[0]
    last = lambda i: (blk, 0)
    fix = lambda i: (0, 0)
    return pl.pallas_call(
        _fill_sample_kernel,
        grid=(1,),
        in_specs=[pl.BlockSpec(memory_space=pl.ANY), pl.BlockSpec(memory_space=pl.ANY),
                  pl.BlockSpec(attn_s.shape, fix), pl.BlockSpec(y_s.shape, fix),
                  pl.BlockSpec((ns, zs.shape[1]), last), pl.BlockSpec(gn.shape, fix)],
        out_specs=[pl.BlockSpec((ns, attn.shape[1]), last), pl.BlockSpec((ns, ssd.shape[1]), last)],
        out_shape=[jax.ShapeDtypeStruct(attn.shape, attn.dtype), jax.ShapeDtypeStruct(ssd.shape, ssd.dtype)],
        input_output_aliases={0: 0, 1: 1},
        compiler_params=_cparams(("arbitrary",)),
        name="fill_sample",
    )(attn, ssd, attn_s, y_s, zs, gn)


def _mix_kernel(x_ref, a_ref, s_ref, gt_ref, wao_ref, wso_ref, wo_ref, o_ref):
    d = x_ref.shape[1]
    gt = gt_ref[...].astype(F32)
    ao = jnp.dot(a_ref[...], wao_ref[...], preferred_element_type=F32)
    so = jnp.dot(s_ref[...], wso_ref[...], preferred_element_type=F32)
    mixed = (gt[:, :d] * ao + gt[:, d:] * so).astype(BF16)
    o_ref[...] = x_ref[...] + jnp.dot(mixed, wo_ref[...], preferred_element_type=F32)


def _mix(x1, attn, ssd, gates, wao, wso, wo, tm):
    r, d = x1.shape
    row = lambda i: (i, 0)
    return pl.pallas_call(
        _mix_kernel,
        grid=(r // tm,),
        in_specs=[pl.BlockSpec((tm, d), row), pl.BlockSpec((tm, attn.shape[1]), row),
                  pl.BlockSpec((tm, ssd.shape[1]), row), pl.BlockSpec((tm, gates.shape[1]), row),
                  _const_spec(wao.shape), _const_spec(wso.shape), _const_spec(wo.shape)],
        out_specs=pl.BlockSpec((tm, d), row),
        out_shape=jax.ShapeDtypeStruct((r, d), F32),
        compiler_params=_cparams(("arbitrary",)),
        name="mix",
    )(x1, attn, ssd, gates, wao, wso, wo)


def kernel(x_prompt, x_sample, cache_k, cache_v, cache_kidx, state_ssm, state_conv, page_table, meta_tokens,
           g_ffn1, w_ffn1_gate, w_ffn1_up, w_ffn1_down, g_mix, w_in, g_q, g_k, conv_w, conv_b, dt_bias, a_log,
           d_skip, g_ssd_norm, w_attn_out, w_ssd_out, w_o, g_ffn2, w_ffn2_gate, w_ffn2_up, w_ffn2_down):
    depth = w_in.shape[0]
    assert depth == 1, "single-layer step"
    n_batch, seq, d_model = x_prompt.shape
    ns = x_sample.shape[0]
    assert x_sample.shape[1] == 1 and seq % TILE == 0 and ns == TILE
    page = cache_k.shape[2]
    assert page == TILE
    l_seq = seq + N_META
    lp = seq + TILE
    r_prompt = n_batch * lp
    r_total = r_prompt + ns
    n_heads = state_ssm.shape[2]
    d_inner = n_heads * SSD_HEAD_DIM
    assert n_heads <= TILE and n_heads % (2 * SSD_GROUPS) == 0
    cdim = d_inner + 2 * SSD_GROUPS * D_STATE
    nq, nkv = N_HEADS * HEAD_DIM, N_KV_HEADS * HEAD_DIM
    nqi = IDX_HEADS * IDX_DIM
    tm = _row_tile(r_total // TILE, 8)

    w = w_in[0]
    offs = [0]
    for n in (nq, nkv, nkv, nqi, IDX_DIM, IDX_HEADS, d_inner, cdim, n_heads, d_model, d_model):
        offs.append(offs[-1] + n)
    col = lambda i: w[:, offs[i]:offs[i + 1]]
    wqk = jnp.concatenate([col(0), col(1)], axis=1).astype(BF16)
    wv = col(2).astype(BF16)
    widx = jnp.concatenate([col(3), col(4), col(5), jnp.zeros((d_model, TILE - IDX_DIM - IDX_HEADS), F32)], axis=1)
    wz = col(6).astype(BF16)
    wx = col(7).astype(BF16)
    wdt = jnp.concatenate([col(8), jnp.zeros((d_model, TILE - n_heads), F32)], axis=1).astype(BF16)
    wgate = jnp.concatenate([col(9), col(10)], axis=1).astype(BF16)
    n_seg = (nq + nkv) // HEAD_DIM
    seg = (jnp.arange(nq + nkv)[:, None] // HEAD_DIM == jnp.arange(TILE)[None, :]).astype(BF16)
    segt = seg.T
    del n_seg
    gqk = jnp.concatenate([jnp.tile(g_q[0], N_HEADS), jnp.tile(g_k[0], N_KV_HEADS)])[None, :]
    row1 = lambda v: v.reshape(1, -1)
    pad_heads = lambda v, fill: jnp.concatenate([v, jnp.full((TILE - n_heads,), fill, F32)])[None, :]
    dtb = pad_heads(dt_bias[0], -1e4)
    alog = pad_heads(a_log[0], 0.0)
    dsk = jnp.repeat(d_skip[0], SSD_HEAD_DIM)[None, :]
    e_mat = (jnp.arange(TILE)[:, None] == jnp.arange(d_inner)[None, :] // SSD_HEAD_DIM).astype(F32)
    gn = row1(g_ssd_norm[0])

    meta = jnp.broadcast_to(meta_tokens[None].astype(F32), (n_batch, N_META, d_model))
    xp = jnp.concatenate([jnp.zeros((n_batch, PAD_FRONT, d_model), F32), meta, x_prompt], axis=1)
    x_all = jnp.concatenate([xp.reshape(r_prompt, d_model), x_sample.reshape(ns, d_model)], axis=0)

    x1, hb = _ffn(x_all, row1(g_ffn1[0]), row1(g_mix[0]), w_ffn1_gate[0].astype(BF16),
                  w_ffn1_up[0].astype(BF16), w_ffn1_down[0].astype(BF16), tm)
    q, k, v, kb, vb, qi, kiw = _proj_attn(x1, row1(g_mix[0]), wqk, wv, widx, seg, segt, gqk, tm)
    xbc, dtr = _proj_xbc(hb, wx, wdt, tm)
    zs, gates = _proj_gate(hb, wz, wgate, tm)

    ksel_p = min(TOP_K_MAX, l_seq // 4)
    attn = _dsa_prompt(q, qi, kiw, kb, vb, n_batch, lp, ksel_p)
    ssd, ssm_p, conv_p = _ssd_prompt(xbc, dtr, zs, conv_w[0], row1(conv_b[0]), dtb, alog, dsk, gn, e_mat,
                                     n_batch, lp, n_heads)

    n_pages = page_table.shape[1]
    ksel_s = min(TOP_K_MAX, (n_pages * page + 1) // 4)
    smp = lambda a: a[r_prompt:]
    kidx_t = jnp.transpose(cache_kidx[0], (0, 2, 1))
    scores = _idx_sample(page_table, smp(qi).reshape(ns, 1, nqi), smp(kiw).reshape(ns, 1, TILE), kidx_t)
    bias = _select_sample(scores.reshape(n_pages + 1, ns, TILE), ksel_s).reshape(n_pages + 1, ns, 1, TILE)
    ck = jnp.transpose(cache_k[0], (0, 2, 3, 1)).reshape(cache_k.shape[1], nkv, page)
    cv = jnp.transpose(cache_v[0], (0, 2, 3, 1)).reshape(cache_v.shape[1], nkv, page)
    attn_s = _attn_sample(page_table, smp(q).reshape(ns, 1, nq), smp(k).reshape(ns, 1, nkv),
                          smp(v).reshape(ns, 1, nkv), bias, ck, cv)
    y_s, ssm_s, conv_s = _ssd_sample(smp(xbc), smp(dtr), state_conv[0].reshape(ns, -1), state_ssm[0].reshape(ns, d_inner, D_STATE),
                                     conv_w[0], row1(conv_b[0]), dtb, alog, dsk, e_mat, n_heads, 8)
    attn, ssd = _fill_sample(attn, ssd, attn_s.reshape(ns, nq), y_s, zs, gn, r_prompt // ns)

    x2 = _mix(x1, attn, ssd, gates, w_attn_out[0].astype(BF16), w_ssd_out[0].astype(BF16),
              w_o[0].astype(BF16), tm)
    y, _ = _ffn(x2, row1(g_ffn2[0]), row1(g_ffn2[0]), w_ffn2_gate[0].astype(BF16),
                w_ffn2_up[0].astype(BF16), w_ffn2_down[0].astype(BF16), tm)

    pr = lambda a: a[:r_prompt].reshape(n_batch, lp, -1)[:, PAD_FRONT:]
    y_prompt = y[:r_prompt].reshape(n_batch, lp, d_model)[:, TILE:]
    y_sample = y[r_prompt:].reshape(ns, 1, d_model)
    k_prompt = pr(k).reshape(1, n_batch, l_seq, N_KV_HEADS, HEAD_DIM)
    v_prompt = pr(v).reshape(1, n_batch, l_seq, N_KV_HEADS, HEAD_DIM)
    kidx_prompt = pr(kiw)[:, :, :IDX_DIM].reshape(1, n_batch, l_seq, IDX_DIM)
    ssm_prompt = ssm_p.reshape(1, n_batch, n_heads, SSD_HEAD_DIM, D_STATE)
    conv_prompt = conv_p.reshape(1, n_batch, CONV_W - 1, cdim)
    k_sample = smp(k).reshape(1, ns, 1, N_KV_HEADS, HEAD_DIM)
    v_sample = smp(v).reshape(1, ns, 1, N_KV_HEADS, HEAD_DIM)
    kidx_sample = smp(kiw)[:, :IDX_DIM].reshape(1, ns, 1, IDX_DIM)
    ssm_sample = ssm_s.reshape(1, ns, n_heads, SSD_HEAD_DIM, D_STATE)
    conv_sample = conv_s.reshape(1, ns, CONV_W - 1, cdim)
    return (y_prompt, y_sample, k_prompt, v_prompt, kidx_prompt, ssm_prompt, conv_prompt,
            k_sample, v_sample, kidx_sample, ssm_sample, conv_sample)
```

```python
import functools

import jax
import jax.numpy as jnp
from jax import lax
from jax.experimental import pallas as pl
from jax.experimental.pallas import tpu as pltpu

F32 = jnp.float32
BF16 = jnp.bfloat16
I32 = jnp.int32
HIGHEST = lax.Precision.HIGHEST

EPS = 1e-6
N_META = 16
N_HEADS = 16
N_KV_HEADS = 4
HEAD_DIM = 64
Q_PER_KV = N_HEADS // N_KV_HEADS
IDX_HEADS = 4
IDX_DIM = 64
TOP_K_MAX = 256
SSD_HEAD_DIM = 64
SSD_GROUPS = 4
D_STATE = 128
CONV_W = 4
TILE = 128
PAD_FRONT = TILE - N_META
INT_MIN = -(2 ** 31)
NEG_BIG = -(2.0 ** 100)
Q_SCALE = HEAD_DIM ** -0.5 * 1.4426950408889634
VMEM_LIMIT = 56 * 1024 * 1024


def _cparams(sem, flags=None):
    return pltpu.CompilerParams(dimension_semantics=sem, vmem_limit_bytes=VMEM_LIMIT, flags=flags)


def _const_spec(shape):
    nd = len(shape)
    return pl.BlockSpec(shape, lambda *_: (0,) * nd, pipeline_mode=pl.Buffered(1))


def _rms(x, g):
    return x * lax.rsqrt(jnp.mean(x * x, axis=-1, keepdims=True) + EPS) * g


def _silu(x):
    return x * jax.nn.sigmoid(x)


def _row_tile(n_tiles, cap):
    for k in range(cap, 0, -1):
        if n_tiles % k == 0:
            return k * TILE
    return TILE


def _ffn_kernel(x_ref, g_ref, g2_ref, wg_ref, wu_ref, wd_ref, o_ref, h2_ref=None, *, ck):
    x = x_ref[...]
    h = _rms(x, g_ref[...]).astype(BF16)
    acc = jnp.zeros(x.shape, F32)
    for c in range(wg_ref.shape[1] // ck):
        sl = slice(c * ck, (c + 1) * ck)
        gate = jnp.dot(h, wg_ref[:, sl], preferred_element_type=F32)
        up = jnp.dot(h, wu_ref[:, sl], preferred_element_type=F32)
        act = (_silu(gate) * up).astype(BF16)
        acc = acc + jnp.dot(act, wd_ref[sl, :], preferred_element_type=F32)
    y = x + 0.5 * acc
    o_ref[...] = y
    if h2_ref is not None:
        h2_ref[...] = _rms(y, g2_ref[...]).astype(BF16)


def _ffn(x, g, g2, wg, wu, wd, tm, rows=None, with_norm=True):
    d = x.shape[1]
    r0, r = (0, x.shape[0]) if rows is None else rows
    dff = wg.shape[1]
    blk0 = r0 // tm
    out_specs = [pl.BlockSpec((tm, d), lambda i: (i, 0))]
    out_shape = [jax.ShapeDtypeStruct((r, d), F32)]
    if with_norm:
        out_specs.append(pl.BlockSpec((tm, d), lambda i: (i, 0)))
        out_shape.append(jax.ShapeDtypeStruct((r, d), BF16))
    return pl.pallas_call(
        functools.partial(_ffn_kernel, ck=256),
        grid=(r // tm,),
        in_specs=[pl.BlockSpec((tm, d), lambda i: (blk0 + i, 0)), _const_spec((1, d)), _const_spec((1, d)),
                  _const_spec((d, dff)), _const_spec((d, dff)), _const_spec((dff, d))],
        out_specs=out_specs,
        out_shape=out_shape,
        compiler_params=_cparams(("arbitrary",)),
        name="ffn",
    )(x, g, g2, wg, wu, wd)


def _proj_attn_kernel(x_ref, g_ref, wqk_ref, wv_ref, wih_ref, wil_ref, seg_ref, segt_ref, gqk_ref,
                      q_ref, k_ref, v_ref, kb_ref, vb_ref, qi_ref, kiw_ref):
    h = _rms(x_ref[...], g_ref[...])
    hb, h_lo = _split_bf16(h)
    qk = jnp.dot(hb, wqk_ref[...], preferred_element_type=F32)
    ss = jnp.dot((qk * qk).astype(BF16), seg_ref[...], preferred_element_type=F32)
    r = lax.rsqrt(ss * (1.0 / HEAD_DIM) + EPS)
    r_hi = r.astype(BF16)
    r_lo = (r - r_hi.astype(F32)).astype(BF16)
    rx = (jnp.dot(r_hi, segt_ref[...], preferred_element_type=F32)
          + jnp.dot(r_lo, segt_ref[...], preferred_element_type=F32))
    qkn = qk * rx * gqk_ref[...]
    nq = q_ref.shape[1]
    q_ref[...] = (qkn[:, :nq] * Q_SCALE).astype(BF16)
    k = qkn[:, nq:]
    k_ref[...] = k
    kb_ref[...] = k.astype(BF16)
    v = jnp.dot(hb, wv_ref[...], preferred_element_type=F32)
    v_ref[...] = v
    vb_ref[...] = v.astype(BF16)
    idx = _dot3(hb, h_lo, wih_ref[...], wil_ref[...])
    nqi = qi_ref.shape[1]
    qi_ref[...] = idx[:, :nqi]
    kiw_ref[...] = idx[:, nqi:]


def _proj_attn(x1, g, wqk, wv, wih, wil, seg, segt, gqk, tm):
    r, d = x1.shape
    nqk, nv = wqk.shape[1], wv.shape[1]
    nq = N_HEADS * HEAD_DIM
    nqi = IDX_HEADS * IDX_DIM
    row = lambda i: (i, 0)
    outs = [(nq, BF16), (nqk - nq, F32), (nv, F32), (nqk - nq, BF16), (nv, BF16), (nqi, F32), (TILE, F32)]
    return pl.pallas_call(
        _proj_attn_kernel,
        grid=(r // tm,),
        in_specs=[pl.BlockSpec((tm, d), row), _const_spec((1, d)), _const_spec(wqk.shape),
                  _const_spec(wv.shape), _const_spec(wih.shape), _const_spec(wil.shape), _const_spec(seg.shape),
                  _const_spec(segt.shape), _const_spec(gqk.shape)],
        out_specs=[pl.BlockSpec((tm, n), row) for n, _ in outs],
        out_shape=[jax.ShapeDtypeStruct((r, n), dt) for n, dt in outs],
        compiler_params=_cparams(("arbitrary",)),
        name="proj_attn",
    )(x1, g, wqk, wv, wih, wil, seg, segt, gqk)


def _proj_xbc_kernel(h_ref, wx_ref, wdt_ref, xbc_ref, dt_ref):
    h = h_ref[...]
    xbc_ref[...] = jnp.dot(h, wx_ref[...], preferred_element_type=F32)
    dt_ref[...] = jnp.dot(h, wdt_ref[...], preferred_element_type=F32)


def _proj_xbc(hb, wx, wdt, tm):
    r, d = hb.shape
    row = lambda i: (i, 0)
    return pl.pallas_call(
        _proj_xbc_kernel,
        grid=(r // tm,),
        in_specs=[pl.BlockSpec((tm, d), row), _const_spec(wx.shape), _const_spec(wdt.shape)],
        out_specs=[pl.BlockSpec((tm, wx.shape[1]), row), pl.BlockSpec((tm, wdt.shape[1]), row)],
        out_shape=[jax.ShapeDtypeStruct((r, wx.shape[1]), F32),
                   jax.ShapeDtypeStruct((r, wdt.shape[1]), F32)],
        compiler_params=_cparams(("arbitrary",)),
        name="proj_xbc",
    )(hb, wx, wdt)


def _proj_gate_kernel(h_ref, wz_ref, wg_ref, zs_ref, gt_ref):
    h = h_ref[...]
    zs_ref[...] = _silu(jnp.dot(h, wz_ref[...], preferred_element_type=F32)).astype(BF16)
    gt_ref[...] = jax.nn.sigmoid(jnp.dot(h, wg_ref[...], preferred_element_type=F32)).astype(BF16)


def _proj_gate(hb, wz, wg, tm):
    r, d = hb.shape
    row = lambda i: (i, 0)
    return pl.pallas_call(
        _proj_gate_kernel,
        grid=(r // tm,),
        in_specs=[pl.BlockSpec((tm, d), row), _const_spec(wz.shape), _const_spec(wg.shape)],
        out_specs=[pl.BlockSpec((tm, wz.shape[1]), row), pl.BlockSpec((tm, wg.shape[1]), row)],
        out_shape=[jax.ShapeDtypeStruct((r, wz.shape[1]), BF16),
                   jax.ShapeDtypeStruct((r, wg.shape[1]), BF16)],
        compiler_params=_cparams(("arbitrary",)),
        name="proj_gate",
    )(hb, wz, wg)


def _sortable(score):
    b = pltpu.bitcast(score, I32)
    return jnp.where(b < 0, -(b & 0x7FFFFFFF), b)


KEY_GROUP = 4


def _tree_sum(x):
    while x.shape[0] > 1:
        h = x.shape[0] // 2
        x = x[:h] + x[h:]
    return x[0]


def _count(keys_ref, n_groups, pred):
    def body(gi, acc):
        k = keys_ref[pl.ds(gi * KEY_GROUP, KEY_GROUP)]
        hit = jnp.where(pred(k, gi * (KEY_GROUP * TILE)), 1, 0)
        return acc + _tree_sum(hit.reshape(KEY_GROUP * TILE // 8, 8, TILE))
    part = lax.fori_loop(0, n_groups, body, jnp.zeros((8, TILE), I32))
    return jnp.sum(part, axis=0, keepdims=True)


def _kth_largest(keys_ref, n_groups, ksel):
    n_ge0 = _count(keys_ref, n_groups, lambda k, r0: k >= 0)
    t = jnp.where(n_ge0 >= ksel, jnp.zeros((1, TILE), I32), jnp.full((1, TILE), INT_MIN, I32))

    def bit_step(i, t):
        cand = t | jnp.left_shift(jnp.int32(1), 30 - i)
        n = _count(keys_ref, n_groups, lambda k, r0: k >= cand)
        return jnp.where(n >= ksel, cand, t)

    return lax.fori_loop(0, 31, bit_step, t)


I16 = jnp.int16
I16_MIN = -(2 ** 15)


def _count16(ref, n_groups, pred):
    def body(gi, acc):
        v = ref[pl.ds(gi * KEY_GROUP, KEY_GROUP)]
        hit = jnp.where(pred(v), jnp.int16(1), jnp.int16(0))
        return acc + _tree_sum(hit.reshape(KEY_GROUP * TILE // 16, 16, TILE))
    part = lax.fori_loop(0, n_groups, body, jnp.zeros((16, TILE), I16))
    return jnp.sum(part.astype(I32), axis=0, keepdims=True)


def _kth_largest16(ref, n_groups, need):
    n_ge0 = _count16(ref, n_groups, lambda v: v >= 0)
    t = jnp.where(n_ge0 >= need, jnp.zeros((1, TILE), I32), jnp.full((1, TILE), I16_MIN, I32))

    def bit_step(i, t):
        cand = t | jnp.left_shift(jnp.int32(1), 14 - i)
        c16 = cand.astype(I16)
        n = _count16(ref, n_groups, lambda v: v >= c16)
        return jnp.where(n >= need, cand, t)

    return lax.fori_loop(0, 15, bit_step, t)


def _kth_largest_split(hi_ref, lo_ref, lo2_ref, n_groups, ksel):
    t_hi = _kth_largest16(hi_ref, n_groups, ksel)
    t_hi16 = t_hi.astype(I16)
    need = ksel - _count16(hi_ref, n_groups, lambda v: v > t_hi16)

    def park(gi, _):
        sl = pl.ds(gi * KEY_GROUP, KEY_GROUP)
        lo2_ref[sl] = jnp.where(hi_ref[sl] == t_hi16, lo_ref[sl], jnp.int16(I16_MIN))
        return 0
    lax.fori_loop(0, n_groups, park, 0)
    t_lo = _kth_largest16(lo2_ref, n_groups, need)
    t = jnp.left_shift(t_hi, 16) | (t_lo - I16_MIN)
    return jnp.where(t_hi == I16_MIN, INT_MIN, t)


def _tie_limit(keys_ref, n_groups, t, ksel, n_bits):
    shape = (KEY_GROUP, TILE, TILE)
    rows = lax.broadcasted_iota(I32, shape, 0) * TILE + lax.broadcasted_iota(I32, shape, 1)
    n_gt = _count(keys_ref, n_groups, lambda k, r0: k > t)
    n_eq = _count(keys_ref, n_groups, lambda k, r0: k == t)
    need = ksel - n_gt
    live = t != INT_MIN
    conflict = jnp.max(jnp.where(live & (n_eq > need), 1, 0)) > 0

    def search():
        def bit_step(i, p):
            cand = p | jnp.left_shift(jnp.int32(1), n_bits - 1 - i)
            n = _count(keys_ref, n_groups, lambda k, r0: (k == t) & ((rows + r0) < cand))
            return jnp.where(n < need, cand, p)
        return lax.fori_loop(0, n_bits, bit_step, jnp.zeros((1, TILE), I32)) + 1

    lim = lax.cond(conflict, search, lambda: jnp.full((1, TILE), 1 << n_bits, I32))
    return jnp.where(live, lim, 0)


def _split_bf16(x):
    hi = x.astype(BF16)
    return hi, (x - hi.astype(F32)).astype(BF16)


def _dot3(a_hi, a_lo, b_hi, b_lo):
    return (jnp.dot(a_hi, b_hi, preferred_element_type=F32)
            + jnp.dot(a_hi, b_lo, preferred_element_type=F32)
            + jnp.dot(a_lo, b_hi, preferred_element_type=F32))


VROWS = HEAD_DIM + 16


def _dsa_prompt_kernel(q_ref, qi_ref, kiwq_ref, kb_ref, vb_ref, kiw_ref, kbm_ref, vbm_ref, kiwm_ref, o_ref,
                       keys_ref, hi_ref, lo_ref, lo2_ref, kk_ref, vt_ref, kihl_ref, rhs_ref, qa_ref, qb_ref,
                       s0_ref, s1_ref, m_ref, acc_ref, *, ksel, n_bits):
    b = pl.program_id(0)
    j = pl.program_id(1)
    nt = vb_ref.shape[0] // TILE + 1
    n_groups = lax.shift_right_logical(j + KEY_GROUP, KEY_GROUP.bit_length() - 1)
    n_pairs = lax.shift_right_logical(j + 2, 1)
    nkv = N_KV_HEADS
    gw = Q_PER_KV * TILE
    eye = (lax.broadcasted_iota(I32, (TILE, TILE), 0) == lax.broadcasted_iota(I32, (TILE, TILE), 1))

    @pl.when((b == 0) & (j == 0))
    def _():
        rhs_ref[...] = jnp.zeros(rhs_ref.shape, rhs_ref.dtype)
        ident = jnp.where(eye, 1.0, 0.0).astype(BF16)
        for g in range(nkv):
            for hh in range(Q_PER_KV):
                rhs_ref[g, 0:TILE, hh * TILE:(hh + 1) * TILE] = ident
        qa_ref[...] = jnp.zeros(qa_ref.shape, qa_ref.dtype)
        qb_ref[...] = jnp.zeros(qb_ref.shape, qb_ref.dtype)
        kk_ref[...] = jnp.zeros(kk_ref.shape, kk_ref.dtype)
        kihl_ref[...] = jnp.zeros(kihl_ref.shape, kihl_ref.dtype)
        ones_row = jnp.where(lax.broadcasted_iota(I32, (VROWS - HEAD_DIM, TILE), 0) == 0, 1.0, 0.0).astype(BF16)
        for c in range(vt_ref.shape[0]):
            vt_ref[c] = jnp.zeros(vt_ref.shape[1:], vt_ref.dtype)
            for g in range(nkv):
                vt_ref[c, g * VROWS + HEAD_DIM:(g + 1) * VROWS, :] = ones_row

    @pl.when(j == 0)
    def _():
        def stage(c, kb, vb, kiw):
            kk_ref[c] = kb
            vt = vb.astype(F32).T.astype(BF16)
            for g in range(nkv):
                vt_ref[c, g * VROWS:g * VROWS + HEAD_DIM, :] = vt[g * HEAD_DIM:(g + 1) * HEAD_DIM, :]
            hi, lo = _split_bf16(kiw)
            kihl_ref[c] = jnp.concatenate([hi, lo], axis=1)

        stage(0, kbm_ref[...], vbm_ref[...], kiwm_ref[...])

        def prep(c, _):
            rows = pl.ds(pl.multiple_of((c - 1) * TILE, TILE), TILE)
            stage(c, kb_ref[rows, :], vb_ref[rows, :], kiw_ref[rows, :])
            return 0
        lax.fori_loop(1, nt, prep, 0)

    qt = q_ref[...].astype(F32).T.astype(BF16)
    for h in range(N_HEADS):
        g, hh = divmod(h, Q_PER_KV)
        r0 = TILE + (g % 2) * HEAD_DIM
        rhs_ref[g, r0:r0 + HEAD_DIM, hh * TILE:(hh + 1) * TILE] = qt[h * HEAD_DIM:(h + 1) * HEAD_DIM, :]
    qit = qi_ref[...].T
    for h in range(IDX_HEADS):
        hi, lo = _split_bf16(qit[h * IDX_DIM:(h + 1) * IDX_DIM, :])
        qa_ref[0:IDX_DIM, h * TILE:(h + 1) * TILE] = hi
        qa_ref[TILE:TILE + IDX_DIM, h * TILE:(h + 1) * TILE] = hi
        qb_ref[0:IDX_DIM, h * TILE:(h + 1) * TILE] = lo
    wq = kiwq_ref[...].T[IDX_DIM:IDX_DIM + 8, :]
    wq = wq * (IDX_HEADS ** -0.5 * IDX_DIM ** -0.5)

    grp_rows = KEY_GROUP * TILE
    q_pos = j * TILE + lax.broadcasted_iota(I32, (grp_rows, TILE), 1)
    k_row = lax.broadcasted_iota(I32, (grp_rows, TILE), 0)

    def score_group(gi, _):
        khl = kihl_ref[pl.ds(gi * KEY_GROUP, KEY_GROUP)].reshape(grp_rows, 2 * TILE)
        d = (jnp.dot(khl, qa_ref[...], preferred_element_type=F32)
             + jnp.dot(khl, qb_ref[...], preferred_element_type=F32))
        s = jnp.zeros((grp_rows, TILE), F32)
        for h in range(IDX_HEADS):
            s = s + wq[h:h + 1, :] * jnp.maximum(d[:, h * TILE:(h + 1) * TILE], 0.0)
        k_pos = k_row + gi * grp_rows
        valid = (k_pos <= q_pos) & (k_pos >= PAD_FRONT)
        keys = jnp.where(valid, _sortable(s), INT_MIN)
        sl = pl.ds(gi * KEY_GROUP, KEY_GROUP)
        keys_ref[sl] = keys.reshape(KEY_GROUP, TILE, TILE)
        hi_ref[sl] = jnp.right_shift(keys, 16).astype(I16).reshape(KEY_GROUP, TILE, TILE)
        lo_ref[sl] = ((keys & 0xFFFF) + I16_MIN).astype(I16).reshape(KEY_GROUP, TILE, TILE)
        return 0
    lax.fori_loop(0, n_groups, score_group, 0)

    t = _kth_largest_split(hi_ref, lo_ref, lo2_ref, n_groups, ksel)
    lim = _tie_limit(keys_ref, n_groups, t, ksel, n_bits)

    m_ref[...] = jnp.full(m_ref.shape, NEG_BIG, F32)
    acc_ref[...] = jnp.zeros(acc_ref.shape, F32)

    pair_shape = (2, TILE, TILE)
    pair_row = lax.broadcasted_iota(I32, pair_shape, 0) * TILE + lax.broadcasted_iota(I32, pair_shape, 1)

    def masked_scores(i, s_ref):
        keys = keys_ref[pl.ds(2 * i, 2)]
        sel = (keys > t) | ((keys == t) & ((pair_row + i * (2 * TILE)) < lim))
        bias = jnp.where(sel, 0.0, NEG_BIG).astype(BF16).reshape(2 * TILE, TILE)
        kc = kk_ref[pl.ds(2 * i, 2)].reshape(2 * TILE, nkv * HEAD_DIM)
        for g in range(nkv):
            half = (g // 2) * TILE
            lhs = jnp.concatenate([bias, kc[:, half:half + TILE]], axis=1)
            s_ref[g] = jnp.dot(lhs, rhs_ref[g], preferred_element_type=F32).astype(BF16)

    def softmax_pv(i, s_ref):
        vt = jnp.concatenate([vt_ref[2 * i], vt_ref[2 * i + 1]], axis=1)
        m_all = m_ref[...]
        m_new = []
        for g in range(nkv):
            s = s_ref[g]
            m_g = jnp.maximum(m_all[g:g + 1, :], jnp.max(s, axis=0, keepdims=True).astype(F32))
            alpha = jnp.exp2(m_all[g:g + 1, :] - m_g)
            p = jnp.exp2(s - m_g.astype(BF16))
            pv = jnp.dot(vt[g * VROWS:(g + 1) * VROWS, :], p, preferred_element_type=F32)
            acc_ref[g] = alpha * acc_ref[g] + pv
            m_new.append(m_g)
        m_ref[...] = jnp.concatenate(m_new + [m_all[nkv:, :]], axis=0)

    masked_scores(0, s0_ref)

    def attend_group(gi, _):
        masked_scores(2 * gi + 1, s1_ref)
        softmax_pv(2 * gi, s0_ref)
        masked_scores(2 * gi + 2, s0_ref)
        softmax_pv(2 * gi + 1, s1_ref)
        return 0
    lax.fori_loop(0, n_groups, attend_group, 0)

    pieces = []
    for g in range(nkv):
        acc = acc_ref[g]
        o = acc[:HEAD_DIM] / acc[HEAD_DIM:HEAD_DIM + 1]
        for hh in range(Q_PER_KV):
            pieces.append(o[:, hh * TILE:(hh + 1) * TILE])
    o_ref[...] = jnp.concatenate(pieces, axis=0).T.astype(BF16)


def _seq_tile_map(seq, meta_blk, per_seq=False):
    n_tok = seq // TILE
    if per_seq:
        return lambda b, j: (jnp.where(j == 0, meta_blk + 1 + b, b * n_tok + j - 1), 0)
    return lambda b, j: (jnp.where(j == 0, meta_blk, b * n_tok + j - 1), 0)


def _dsa_prompt(q, qi, kiw, kb, vb, n_batch, seq, meta_blk, ksel):
    lp = seq + TILE
    nt = lp // TILE
    nt_pad = -(-nt // KEY_GROUP) * KEY_GROUP + KEY_GROUP
    r_total = q.shape[0]
    qw = N_HEADS * HEAD_DIM
    kvw = N_KV_HEADS * HEAD_DIM
    gw = Q_PER_KV * TILE
    tile = _seq_tile_map(seq, meta_blk)
    toks = lambda b, j: (b, 0)
    meta = lambda b, j: (meta_blk, 0)
    n_bits = max(1, (lp - 1).bit_length())
    return pl.pallas_call(
        functools.partial(_dsa_prompt_kernel, ksel=ksel, n_bits=n_bits),
        grid=(n_batch, nt),
        in_specs=[pl.BlockSpec((TILE, qw), tile), pl.BlockSpec((TILE, IDX_HEADS * IDX_DIM), tile),
                  pl.BlockSpec((TILE, TILE), tile), pl.BlockSpec((seq, kvw), toks),
                  pl.BlockSpec((seq, kvw), toks), pl.BlockSpec((seq, TILE), toks),
                  pl.BlockSpec((TILE, kvw), meta), pl.BlockSpec((TILE, kvw), meta),
                  pl.BlockSpec((TILE, TILE), meta)],
        out_specs=pl.BlockSpec((TILE, qw), _seq_tile_map(seq, meta_blk, per_seq=True)),
        out_shape=jax.ShapeDtypeStruct((r_total + n_batch * TILE, qw), BF16),
        scratch_shapes=[pltpu.VMEM((nt_pad, TILE, TILE), I32),
                        pltpu.VMEM((nt_pad, TILE, TILE), I16),
                        pltpu.VMEM((nt_pad, TILE, TILE), I16),
                        pltpu.VMEM((nt_pad, TILE, TILE), I16),
                        pltpu.VMEM((nt_pad, TILE, kvw), BF16),
                        pltpu.VMEM((nt_pad, N_KV_HEADS * VROWS, TILE), BF16),
                        pltpu.VMEM((nt_pad, TILE, 2 * TILE), BF16),
                        pltpu.VMEM((N_KV_HEADS, 2 * TILE, gw), BF16),
                        pltpu.VMEM((2 * TILE, IDX_HEADS * TILE), BF16),
                        pltpu.VMEM((2 * TILE, IDX_HEADS * TILE), BF16),
                        pltpu.VMEM((N_KV_HEADS, 2 * TILE, gw), BF16),
                        pltpu.VMEM((N_KV_HEADS, 2 * TILE, gw), BF16),
                        pltpu.VMEM((8, gw), F32),
                        pltpu.VMEM((N_KV_HEADS, VROWS, gw), F32)],
        compiler_params=_cparams(("arbitrary", "arbitrary")),
        name="dsa_prompt",
    )(q, qi, kiw, kb, vb, kiw, kb, vb, kiw)


def _expand_heads(v, e_ref):
    return jnp.dot(v, e_ref[...], preferred_element_type=F32, precision=HIGHEST)


def _softplus(x):
    return jnp.maximum(x, 0.0) + jnp.log1p(jnp.exp(-jnp.abs(x)))


def _ssd_prompt_kernel(xbc_ref, dtr_ref, zs_ref, cw_ref, cb_ref, dtb_ref, alog_ref, dsk_ref, gn_ref,
                       e_ref, y_ref, ssm_ref, conv_ref, xp_ref, ht_ref, *, n_heads):
    c = pl.program_id(1)
    nc = pl.num_programs(1)
    d_inner = n_heads * SSD_HEAD_DIM
    hpg = n_heads // SSD_GROUPS
    gwid = hpg * SSD_HEAD_DIM
    gn_w = d_inner // SSD_GROUPS

    @pl.when(c == 0)
    def _():
        xp_ref[0:8, :] = jnp.zeros((8, xp_ref.shape[1]), F32)
        ht_ref[...] = jnp.zeros(ht_ref.shape, F32)

    xp_ref[8:8 + TILE, :] = xbc_ref[...]
    u = cb_ref[...] + cw_ref[CONV_W - 1:CONV_W, :] * xp_ref[8:8 + TILE, :]
    for k in range(CONV_W - 1):
        off = 8 - (CONV_W - 1) + k
        u = u + cw_ref[k:k + 1, :] * xp_ref[off:off + TILE, :]
    tail = xp_ref[8 + TILE - (CONV_W - 1):8 + TILE, :]
    xp_ref[8 - (CONV_W - 1):8, :] = tail

    row = lax.broadcasted_iota(I32, (TILE, 1), 0)
    live = jnp.where((c > 0) | (row >= PAD_FRONT), 1.0, 0.0)
    act = _silu(u) * live
    xs = act[:, :d_inner]
    bm = act[:, d_inner:d_inner + SSD_GROUPS * D_STATE]
    cm = act[:, d_inner + SSD_GROUPS * D_STATE:]

    dt = _softplus(dtr_ref[...] + dtb_ref[...]) * live
    a = dt * (-jnp.exp(alog_ref[...]))
    ri = lax.broadcasted_iota(I32, (TILE, TILE), 0)
    ci = lax.broadcasted_iota(I32, (TILE, TILE), 1)
    tril = ri >= ci
    acs = jnp.dot(jnp.where(tril, 1.0, 0.0), a, preferred_element_type=F32, precision=HIGHEST)
    acs_t = acs.T
    dt_t = dt.T
    acs_last = acs[TILE - 1:TILE, :]
    w_in = _expand_heads(jnp.exp(acs_last - acs) * dt, e_ref)
    w_out = _expand_heads(jnp.exp(acs), e_ref)
    xw = (xs * w_in).astype(BF16)
    xsb = xs.astype(BF16)
    lane = lax.broadcasted_iota(I32, (1, gwid), 1)

    ys = []
    for g in range(SSD_GROUPS):
        bg = bm[:, g * D_STATE:(g + 1) * D_STATE]
        cg = cm[:, g * D_STATE:(g + 1) * D_STATE].astype(BF16)
        bgt = bg.T.astype(BF16)
        cb = jnp.dot(cg, bgt, preferred_element_type=F32)
        xw_g = xw[:, g * gwid:(g + 1) * gwid]
        xs_g = xsb[:, g * gwid:(g + 1) * gwid]
        h_prev = ht_ref[g]
        y_off = jnp.dot(cg, h_prev.astype(BF16), preferred_element_type=F32)
        y_g = y_off * w_out[:, g * gwid:(g + 1) * gwid]
        for e0 in range(0, hpg, 2):
            ms, xm = [], []
            for e in (e0, e0 + 1):
                he = g * hpg + e
                seg = acs[:, he:he + 1] - acs_t[he:he + 1, :]
                ms.append((cb * jnp.exp(jnp.where(tril, seg, -jnp.inf)) * dt_t[he:he + 1, :]).astype(BF16))
                only = (lane >= e * SSD_HEAD_DIM) & (lane < (e + 1) * SSD_HEAD_DIM)
                xm.append(jnp.where(only, xs_g, jnp.zeros_like(xs_g)))
            y_g = y_g + jnp.dot(jnp.concatenate(ms, axis=1), jnp.concatenate(xm, axis=0),
                                preferred_element_type=F32)
        states = jnp.dot(bgt, xw_g, preferred_element_type=F32)
        ht_ref[g] = h_prev * w_out[TILE - 1:TILE, g * gwid:(g + 1) * gwid] + states
        ys.append(y_g)
    y = jnp.concatenate(ys, axis=1) + dsk_ref[...] * xs
    y = y * zs_ref[...].astype(F32)
    outs = []
    for g in range(SSD_GROUPS):
        yg = y[:, g * gn_w:(g + 1) * gn_w]
        outs.append(yg * lax.rsqrt(jnp.mean(yg * yg, axis=-1, keepdims=True) + EPS))
    y_ref[...] = (jnp.concatenate(outs, axis=1) * gn_ref[...]).astype(BF16)

    @pl.when(c == nc - 1)
    def _():
        for g in range(SSD_GROUPS):
            ssm_ref[0, g * gwid:(g + 1) * gwid, :] = ht_ref[g].T
        conv_ref[0] = tail


def _ssd_prompt(xbc, dtr, zs, cw, cb, dtb, alog, dsk, gn, e_mat, n_batch, seq, meta_blk, n_heads):
    nt = seq // TILE + 1
    r_total = xbc.shape[0]
    d_inner = n_heads * SSD_HEAD_DIM
    cdim = xbc.shape[1]
    tile = _seq_tile_map(seq, meta_blk)
    fix = lambda b, c: (0, 0)
    return pl.pallas_call(
        functools.partial(_ssd_prompt_kernel, n_heads=n_heads),
        grid=(n_batch, nt),
        in_specs=[pl.BlockSpec((TILE, cdim), tile), pl.BlockSpec((TILE, TILE), tile),
                  pl.BlockSpec((TILE, d_inner), tile),
                  pl.BlockSpec(cw.shape, fix), pl.BlockSpec(cb.shape, fix), pl.BlockSpec(dtb.shape, fix),
                  pl.BlockSpec(alog.shape, fix), pl.BlockSpec(dsk.shape, fix), pl.BlockSpec(gn.shape, fix),
                  pl.BlockSpec(e_mat.shape, fix)],
        out_specs=[pl.BlockSpec((TILE, d_inner), _seq_tile_map(seq, meta_blk, per_seq=True)),
                   pl.BlockSpec((1, d_inner, D_STATE), lambda b, c: (b, 0, 0)),
                   pl.BlockSpec((1, CONV_W - 1, cdim), lambda b, c: (b, 0, 0))],
        out_shape=[jax.ShapeDtypeStruct((r_total + n_batch * TILE, d_inner), BF16),
                   jax.ShapeDtypeStruct((n_batch, d_inner, D_STATE), F32),
                   jax.ShapeDtypeStruct((n_batch, CONV_W - 1, cdim), F32)],
        scratch_shapes=[pltpu.VMEM((8 + TILE, cdim), F32),
                        pltpu.VMEM((SSD_GROUPS, D_STATE, d_inner // SSD_GROUPS), F32)],
        compiler_params=_cparams(("arbitrary", "arbitrary")),
        name="ssd_prompt",
    )(xbc, dtr, zs, cw, cb, dtb, alog, dsk, gn, e_mat)


def _idx_sample_kernel(pt_ref, qi_ref, kiw_ref, *rest, n_pages):
    pages, s_ref = rest[:n_pages], rest[n_pages]
    qi = qi_ref[0]
    kiw = kiw_ref[0]
    q8 = jnp.concatenate([qi[:, h * IDX_DIM:(h + 1) * IDX_DIM] for h in range(IDX_HEADS)]
                         + [jnp.zeros((8 - IDX_HEADS, IDX_DIM), F32)], axis=0)
    w_col = jnp.broadcast_to(kiw, (TILE, TILE)).T[IDX_DIM:IDX_DIM + 8, 0:1]
    w_col = w_col * (IDX_HEADS ** -0.5 * IDX_DIM ** -0.5)
    q_hi, q_lo = _split_bf16(q8)
    for p in range(n_pages):
        k_hi, k_lo = _split_bf16(pages[p][0])
        d = _dot3(q_hi, q_lo, k_hi, k_lo)
        s_ref[p, 0] = jnp.sum(w_col * jnp.maximum(d, 0.0), axis=0, keepdims=True)
    d_new = jnp.sum(q8 * kiw[:, :IDX_DIM], axis=1, keepdims=True)
    s_new = jnp.sum(w_col * jnp.maximum(d_new, 0.0), axis=0, keepdims=True)
    lane = lax.broadcasted_iota(I32, (1, TILE), 1)
    s_ref[n_pages, 0] = jnp.where(lane == 0, s_new, 0.0)


def _idx_sample(page_table, qi_s, kiw_s, cache_kidx):
    ns, n_pages = page_table.shape
    n_slots = n_pages + 1
    page = lambda p: pl.BlockSpec((1, IDX_DIM, TILE), lambda s, pt: (pt[s, p], 0, 0))
    one = lambda s, pt: (s, 0, 0)
    grid_spec = pltpu.PrefetchScalarGridSpec(
        num_scalar_prefetch=1, grid=(ns,),
        in_specs=[pl.BlockSpec((1, 1, qi_s.shape[2]), one), pl.BlockSpec((1, 1, TILE), one)]
        + [page(p) for p in range(n_pages)],
        out_specs=pl.BlockSpec((n_slots, 1, 1, TILE), lambda s, pt: (0, s, 0, 0)))
    return pl.pallas_call(
        functools.partial(_idx_sample_kernel, n_pages=n_pages),
        grid_spec=grid_spec,
        out_shape=jax.ShapeDtypeStruct((n_slots, ns, 1, TILE), F32),
        compiler_params=_cparams(("arbitrary",)),
        name="idx_sample",
    )(page_table, qi_s, kiw_s, *([cache_kidx] * n_pages))


def _select_sample_kernel(s_ref, bias_ref, keys_ref, *, n_pages, ksel, n_bits):
    n_slots = n_pages + 1
    rows = lax.broadcasted_iota(I32, (TILE, TILE), 0)
    for p in range(n_slots):
        key = _sortable(s_ref[p].T)
        if p == n_pages:
            key = jnp.where(rows == 0, key, INT_MIN)
        keys_ref[p] = key
    for p in range(n_slots, keys_ref.shape[0]):
        keys_ref[p] = jnp.full((TILE, TILE), INT_MIN, I32)
    n_groups = keys_ref.shape[0] // KEY_GROUP
    t = _kth_largest(keys_ref, n_groups, ksel)
    lim = _tie_limit(keys_ref, n_groups, t, ksel, n_bits)
    for p in range(n_slots):
        keys = keys_ref[p]
        sel = (keys > t) | ((keys == t) & ((rows + p * TILE) < lim))
        bias_ref[p] = jnp.where(sel, 0.0, NEG_BIG).T


def _select_sample(scores, ksel):
    n_slots, ns, _ = scores.shape
    n_pages = n_slots - 1
    n_bits = (n_slots * TILE - 1).bit_length()
    return pl.pallas_call(
        functools.partial(_select_sample_kernel, n_pages=n_pages, ksel=ksel, n_bits=n_bits),
        out_shape=jax.ShapeDtypeStruct((n_slots, ns, TILE), F32),
        scratch_shapes=[pltpu.VMEM((-(-n_slots // KEY_GROUP) * KEY_GROUP, TILE, TILE), I32)],
        compiler_params=pltpu.CompilerParams(vmem_limit_bytes=VMEM_LIMIT),
        name="select_sample",
    )(scores)


def _attn_sample_kernel(pt_ref, q_ref, kn_ref, vn_ref, bias_ref, *rest, n_pages):
    kpages, vpages, o_ref = rest[:n_pages], rest[n_pages:2 * n_pages], rest[2 * n_pages]
    kvw = N_KV_HEADS * HEAD_DIM
    q = q_ref[0].astype(F32)
    lane = lax.broadcasted_iota(I32, (N_HEADS, kvw), 1)
    head = lax.broadcasted_iota(I32, (N_HEADS, kvw), 0)
    own = jnp.right_shift(lane, 6) == jnp.right_shift(head, 2)
    qrows = []
    for h in range(N_HEADS):
        qh = q[:, h * HEAD_DIM:(h + 1) * HEAD_DIM]
        qrows.append(jnp.concatenate([qh] * N_KV_HEADS, axis=1))
    qf = jnp.where(own, jnp.concatenate(qrows, axis=0), 0.0)
    qw = qf.astype(BF16)
    nt = (((1,), (1,)), ((), ()))
    scores = []
    for p in range(n_pages):
        kp = kpages[p][0].astype(BF16)
        s = jnp.dot(qw, kp, preferred_element_type=F32)
        scores.append(s + bias_ref[p, 0])
    s_new = jnp.sum(qf * kn_ref[0], axis=1, keepdims=True)
    lane1 = lax.broadcasted_iota(I32, (1, TILE), 1)
    scores.append(jnp.where(lane1 == 0, s_new, NEG_BIG) + bias_ref[n_pages, 0])
    m = scores[0].max(axis=1, keepdims=True)
    for s in scores[1:]:
        m = jnp.maximum(m, s.max(axis=1, keepdims=True))
    den = jnp.zeros((N_HEADS, 1), F32)
    out = jnp.zeros((N_HEADS, kvw), F32)
    for p in range(n_pages):
        e = jnp.exp2(scores[p] - m)
        den = den + e.sum(axis=1, keepdims=True)
        out = out + lax.dot_general(e.astype(BF16), vpages[p][0].astype(BF16), nt,
                                    preferred_element_type=F32)
    e = jnp.exp2(scores[n_pages] - m)
    den = den + e.sum(axis=1, keepdims=True)
    out = out + e[:, 0:1] * vn_ref[0]
    out = out / den
    rolled = jnp.concatenate([pltpu.roll(out[:, i * TILE:(i + 1) * TILE], HEAD_DIM, 1)
                              for i in range(kvw // TILE)], axis=1)
    half = lax.broadcasted_iota(I32, (1, TILE), 1) < HEAD_DIM
    pieces = []
    for m2 in range(N_HEADS // 2):
        g = (2 * m2) // Q_PER_KV
        blk = (g // 2) * TILE
        a_src, b_src = (rolled, out) if g % 2 == 1 else (out, rolled)
        a = a_src[2 * m2:2 * m2 + 1, blk:blk + TILE]
        b = b_src[2 * m2 + 1:2 * m2 + 2, blk:blk + TILE]
        pieces.append(jnp.where(half, a, b))
    o_ref[0] = jnp.concatenate(pieces, axis=1)


def _attn_sample(page_table, q_s, k_s, v_s, bias, cache_k, cache_v):
    ns, n_pages = page_table.shape
    kvw = N_KV_HEADS * HEAD_DIM
    qw = N_HEADS * HEAD_DIM
    page = lambda p: pl.BlockSpec((1, kvw, TILE), lambda s, pt: (pt[s, p], 0, 0))
    one = lambda s, pt: (s, 0, 0)
    grid_spec = pltpu.PrefetchScalarGridSpec(
        num_scalar_prefetch=1, grid=(ns,),
        in_specs=[pl.BlockSpec((1, 1, qw), one), pl.BlockSpec((1, 1, kvw), one),
                  pl.BlockSpec((1, 1, kvw), one),
                  pl.BlockSpec((n_pages + 1, 1, 1, TILE), lambda s, pt: (0, s, 0, 0))]
        + [page(p) for p in range(n_pages)] * 2,
        out_specs=pl.BlockSpec((1, 1, qw), one))
    return pl.pallas_call(
        functools.partial(_attn_sample_kernel, n_pages=n_pages),
        grid_spec=grid_spec,
        out_shape=jax.ShapeDtypeStruct((ns, 1, qw), F32),
        compiler_params=_cparams(("arbitrary",)),
        name="attn_sample",
    )(page_table, q_s, k_s, v_s, bias, *([cache_k] * n_pages), *([cache_v] * n_pages))


def _ssd_sample_kernel(xbc_ref, dtr_ref, sc_ref, h0_ref, cw_ref, cb_ref, dtb_ref, alog_ref, dsk_ref,
                       e_ref, y_ref, h_ref, conv_ref, *, n_heads, sb):
    d_inner = n_heads * SSD_HEAD_DIM
    hpg = n_heads // SSD_GROUPS
    new = xbc_ref[...]
    cdim = new.shape[1]
    u = cb_ref[...] + cw_ref[CONV_W - 1:CONV_W, :] * new
    for k in range(CONV_W - 1):
        u = u + cw_ref[k:k + 1, :] * sc_ref[:, k * cdim:(k + 1) * cdim]
    for k in range(CONV_W - 2):
        conv_ref[:, k * cdim:(k + 1) * cdim] = sc_ref[:, (k + 1) * cdim:(k + 2) * cdim]
    conv_ref[:, (CONV_W - 2) * cdim:] = new
    act = _silu(u)
    xs = act[:, :d_inner]
    bm = act[:, d_inner:d_inner + SSD_GROUPS * D_STATE]
    cm = act[:, d_inner + SSD_GROUPS * D_STATE:]
    dt = _softplus(dtr_ref[...] + dtb_ref[...])
    decay = jnp.exp(dt * (-jnp.exp(alog_ref[...])))
    dec_x = _expand_heads(decay, e_ref)
    xdt = xs * _expand_heads(dt, e_ref)
    for i in range(sb):
        bcols, ccols = [], []
        for g in range(SSD_GROUPS):
            brow = bm[i:i + 1, g * D_STATE:(g + 1) * D_STATE]
            crow = cm[i:i + 1, g * D_STATE:(g + 1) * D_STATE]
            bcols.append(jnp.broadcast_to(brow, (TILE, D_STATE)).T)
            ccols.append(jnp.broadcast_to(crow, (TILE, D_STATE)).T)
        pieces = []
        for pr in range(n_heads // 2):
            g = (2 * pr) // hpg
            sl = slice(pr * TILE, (pr + 1) * TILE)
            ht = h0_ref[i, sl, :].T
            hn = ht * dec_x[i:i + 1, sl] + bcols[g] * xdt[i:i + 1, sl]
            pieces.append(jnp.sum(hn * ccols[g], axis=0, keepdims=True))
            h_ref[i, sl, :] = hn.T
        y_ref[i:i + 1, :] = jnp.concatenate(pieces, axis=1) + dsk_ref[...] * xs[i:i + 1, :]


def _ssd_sample(xbc_s, dtr_s, state_conv, state_ssm, cw, cb, dtb, alog, dsk, e_mat, n_heads, sb):
    ns, cdim = xbc_s.shape
    d_inner = n_heads * SSD_HEAD_DIM
    row = lambda i: (i, 0)
    row3 = lambda i: (i, 0, 0)
    fix = lambda i: (0, 0)
    return pl.pallas_call(
        functools.partial(_ssd_sample_kernel, n_heads=n_heads, sb=sb),
        grid=(ns // sb,),
        in_specs=[pl.BlockSpec((sb, cdim), row), pl.BlockSpec((sb, TILE), row),
                  pl.BlockSpec((sb, (CONV_W - 1) * cdim), row), pl.BlockSpec((sb, d_inner, D_STATE), row3),
                  pl.BlockSpec(cw.shape, fix), pl.BlockSpec(cb.shape, fix), pl.BlockSpec(dtb.shape, fix),
                  pl.BlockSpec(alog.shape, fix), pl.BlockSpec(dsk.shape, fix), pl.BlockSpec(e_mat.shape, fix)],
        out_specs=[pl.BlockSpec((sb, d_inner), row), pl.BlockSpec((sb, d_inner, D_STATE), row3),
                   pl.BlockSpec((sb, (CONV_W - 1) * cdim), row)],
        out_shape=[jax.ShapeDtypeStruct((ns, d_inner), F32),
                   jax.ShapeDtypeStruct((ns, d_inner, D_STATE), F32),
                   jax.ShapeDtypeStruct((ns, (CONV_W - 1) * cdim), F32)],
        compiler_params=_cparams(("arbitrary",)),
        name="ssd_sample",
    )(xbc_s, dtr_s, state_conv, state_ssm, cw, cb, dtb, alog, dsk, e_mat)


def _fill_sample_kernel(attn_any, ssd_any, a_ref, y_ref, zs_ref, gn_ref, attn_ref, ssd_ref):
    del attn_any, ssd_any
    attn_ref[...] = a_ref[...].astype(BF16)
    y = y_ref[...] * zs_ref[...].astype(F32)
    gn_w = y.shape[1] // SSD_GROUPS
    outs = []
    for g in range(SSD_GROUPS):
        yg = y[:, g * gn_w:(g + 1) * gn_w]
        outs.append(yg * lax.rsqrt(jnp.mean(yg * yg, axis=-1, keepdims=True) + EPS))
    ssd_ref[...] = (jnp.concatenate(outs, axis=1) * gn_ref[...]).astype(BF16)


def _fill_sample(attn, ssd, attn_s, y_s, zs, gn, blk):
    ns = attn_s.shape[0]
    last = lambda i: (blk, 0)
    fix = lambda i: (0, 0)
    return pl.pallas_call(
        _fill_sample_kernel,
        grid=(1,),
        in_specs=[pl.BlockSpec(memory_space=pl.ANY), pl.BlockSpec(memory_space=pl.ANY),
                  pl.BlockSpec(attn_s.shape, fix), pl.BlockSpec(y_s.shape, fix),
                  pl.BlockSpec((ns, zs.shape[1]), last), pl.BlockSpec(gn.shape, fix)],
        out_specs=[pl.BlockSpec((ns, attn.shape[1]), last), pl.BlockSpec((ns, ssd.shape[1]), last)],
        out_shape=[jax.ShapeDtypeStruct(attn.shape, attn.dtype), jax.ShapeDtypeStruct(ssd.shape, ssd.dtype)],
        input_output_aliases={0: 0, 1: 1},
        compiler_params=_cparams(("arbitrary",)),
        name="fill_sample",
    )(attn, ssd, attn_s, y_s, zs, gn)


def _mix_kernel(x_ref, a_ref, s_ref, gt_ref, wao_ref, wso_ref, wo_ref, o_ref):
    d = x_ref.shape[1]
    gt = gt_ref[...].astype(F32)
    ao = jnp.dot(a_ref[...], wao_ref[...], preferred_element_type=F32)
    so = jnp.dot(s_ref[...], wso_ref[...], preferred_element_type=F32)
    mixed = (gt[:, :d] * ao + gt[:, d:] * so).astype(BF16)
    o_ref[...] = x_ref[...] + jnp.dot(mixed, wo_ref[...], preferred_element_type=F32)


def _mix(x1, attn, ssd, gates, wao, wso, wo, tm, rows):
    d = x1.shape[1]
    r0, r = rows
    blk0 = r0 // tm
    row = lambda i: (blk0 + i, 0)
    return pl.pallas_call(
        _mix_kernel,
        grid=(r // tm,),
        in_specs=[pl.BlockSpec((tm, d), row), pl.BlockSpec((tm, attn.shape[1]), row),
                  pl.BlockSpec((tm, ssd.shape[1]), row), pl.BlockSpec((tm, gates.shape[1]), row),
                  _const_spec(wao.shape), _const_spec(wso.shape), _const_spec(wo.shape)],
        out_specs=pl.BlockSpec((tm, d), lambda i: (i, 0)),
        out_shape=jax.ShapeDtypeStruct((r, d), F32),
        compiler_params=_cparams(("arbitrary",)),
        name="mix",
    )(x1, attn, ssd, gates, wao, wso, wo)


def kernel(x_prompt, x_sample, cache_k, cache_v, cache_kidx, state_ssm, state_conv, page_table, meta_tokens,
           g_ffn1, w_ffn1_gate, w_ffn1_up, w_ffn1_down, g_mix, w_in, g_q, g_k, conv_w, conv_b, dt_bias, a_log,
           d_skip, g_ssd_norm, w_attn_out, w_ssd_out, w_o, g_ffn2, w_ffn2_gate, w_ffn2_up, w_ffn2_down):
    depth = w_in.shape[0]
    assert depth == 1, "single-layer step"
    n_batch, seq, d_model = x_prompt.shape
    ns = x_sample.shape[0]
    assert x_sample.shape[1] == 1 and seq % TILE == 0 and ns == TILE
    page = cache_k.shape[2]
    assert page == TILE
    l_seq = seq + N_META
    r_tok = n_batch * seq
    r_total = r_tok + ns + TILE
    meta_blk = (r_tok + ns) // TILE
    n_heads = state_ssm.shape[2]
    d_inner = n_heads * SSD_HEAD_DIM
    assert n_heads <= TILE and n_heads % (2 * SSD_GROUPS) == 0
    cdim = d_inner + 2 * SSD_GROUPS * D_STATE
    nq, nkv = N_HEADS * HEAD_DIM, N_KV_HEADS * HEAD_DIM
    nqi = IDX_HEADS * IDX_DIM
    tm = _row_tile(r_total // TILE, 6)
    tm_tok = _row_tile(r_tok // TILE, 8)

    w = w_in[0]
    offs = [0]
    for n in (nq, nkv, nkv, nqi, IDX_DIM, IDX_HEADS, d_inner, cdim, n_heads, d_model, d_model):
        offs.append(offs[-1] + n)
    col = lambda i: w[:, offs[i]:offs[i + 1]]
    wqk = jnp.concatenate([col(0), col(1)], axis=1).astype(BF16)
    wv = col(2).astype(BF16)
    widx = jnp.concatenate([col(3), col(4), col(5), jnp.zeros((d_model, TILE - IDX_DIM - IDX_HEADS), F32)], axis=1)
    wz = col(6).astype(BF16)
    wx = col(7).astype(BF16)
    wdt = jnp.concatenate([col(8), jnp.zeros((d_model, TILE - n_heads), F32)], axis=1).astype(BF16)
    wgate = jnp.concatenate([col(9), col(10)], axis=1).astype(BF16)
    n_seg = (nq + nkv) // HEAD_DIM
    seg = (jnp.arange(nq + nkv)[:, None] // HEAD_DIM == jnp.arange(TILE)[None, :]).astype(BF16)
    segt = seg.T
    del n_seg
    gqk = jnp.concatenate([jnp.tile(g_q[0], N_HEADS), jnp.tile(g_k[0], N_KV_HEADS)])[None, :]
    row1 = lambda v: v.reshape(1, -1)
    pad_heads = lambda v, fill: jnp.concatenate([v, jnp.full((TILE - n_heads,), fill, F32)])[None, :]
    dtb = pad_heads(dt_bias[0], -1e4)
    alog = pad_heads(a_log[0], 0.0)
    dsk = jnp.repeat(d_skip[0], SSD_HEAD_DIM)[None, :]
    e_mat = (jnp.arange(TILE)[:, None] == jnp.arange(d_inner)[None, :] // SSD_HEAD_DIM).astype(F32)
    gn = row1(g_ssd_norm[0])

    x_all = jnp.concatenate([x_prompt.reshape(r_tok, d_model), x_sample.reshape(ns, d_model),
                             jnp.zeros((PAD_FRONT, d_model), F32), meta_tokens.astype(F32)], axis=0)

    x1, hb = _ffn(x_all, row1(g_ffn1[0]), row1(g_mix[0]), w_ffn1_gate[0].astype(BF16),
                  w_ffn1_up[0].astype(BF16), w_ffn1_down[0].astype(BF16), tm)
    wih = widx.astype(BF16)
    wil = (widx - wih.astype(F32)).astype(BF16)
    q, k, v, kb, vb, qi, kiw = _proj_attn(x1, row1(g_mix[0]), wqk, wv, wih, wil, seg, segt, gqk, tm)
    xbc, dtr = _proj_xbc(hb, wx, wdt, tm)
    zs, gates = _proj_gate(hb, wz, wgate, tm)

    ksel_p = min(TOP_K_MAX, l_seq // 4)
    attn = _dsa_prompt(q, qi, kiw, kb, vb, n_batch, seq, meta_blk, ksel_p)
    ssd, ssm_p, conv_p = _ssd_prompt(xbc, dtr, zs, conv_w[0], row1(conv_b[0]), dtb, alog, dsk, gn, e_mat,
                                     n_batch, seq, meta_blk, n_heads)

    n_pages = page_table.shape[1]
    ksel_s = min(TOP_K_MAX, (n_pages * page + 1) // 4)
    smp = lambda a: a[r_tok:r_tok + ns]
    kidx_t = jnp.transpose(cache_kidx[0], (0, 2, 1))
    scores = _idx_sample(page_table, smp(qi).reshape(ns, 1, nqi), smp(kiw).reshape(ns, 1, TILE), kidx_t)
    bias = _select_sample(scores.reshape(n_pages + 1, ns, TILE), ksel_s).reshape(n_pages + 1, ns, 1, TILE)
    ck = jnp.transpose(cache_k[0], (0, 2, 3, 1)).reshape(cache_k.shape[1], nkv, page)
    cv = jnp.transpose(cache_v[0], (0, 2, 3, 1)).reshape(cache_v.shape[1], nkv, page)
    attn_s = _attn_sample(page_table, smp(q).reshape(ns, 1, nq), smp(k).reshape(ns, 1, nkv),
                          smp(v).reshape(ns, 1, nkv), bias, ck, cv)
    y_s, ssm_s, conv_s = _ssd_sample(smp(xbc), smp(dtr), state_conv[0].reshape(ns, -1), state_ssm[0].reshape(ns, d_inner, D_STATE),
                                     conv_w[0], row1(conv_b[0]), dtb, alog, dsk, e_mat, n_heads, 8)
    attn, ssd = _fill_sample(attn, ssd, attn_s.reshape(ns, nq), y_s, zs, gn, r_tok // ns)

    wao, wso, wo = w_attn_out[0].astype(BF16), w_ssd_out[0].astype(BF16), w_o[0].astype(BF16)
    ffn2 = (row1(g_ffn2[0]), row1(g_ffn2[0]), w_ffn2_gate[0].astype(BF16), w_ffn2_up[0].astype(BF16),
            w_ffn2_down[0].astype(BF16))
    x2 = _mix(x1, attn, ssd, gates, wao, wso, wo, tm_tok, (0, r_tok))
    y_prompt = _ffn(x2, *ffn2, tm_tok, with_norm=False)[0].reshape(n_batch, seq, d_model)
    x2_s = _mix(x1, attn, ssd, gates, wao, wso, wo, ns, (r_tok, ns))
    y_sample = _ffn(x2_s, *ffn2, ns, with_norm=False)[0].reshape(ns, 1, d_model)

    meta_rows = lambda a: jnp.broadcast_to(a[r_total - N_META:][None], (n_batch, N_META, a.shape[1]))
    pr = lambda a: jnp.concatenate([meta_rows(a), a[:r_tok].reshape(n_batch, seq, -1)], axis=1)
    k_prompt = pr(k).reshape(1, n_batch, l_seq, N_KV_HEADS, HEAD_DIM)
    v_prompt = pr(v).reshape(1, n_batch, l_seq, N_KV_HEADS, HEAD_DIM)
    kidx_prompt = pr(kiw)[:, :, :IDX_DIM].reshape(1, n_batch, l_seq, IDX_DIM)
    ssm_prompt = ssm_p.reshape(1, n_batch, n_heads, SSD_HEAD_DIM, D_STATE)
    conv_prompt = conv_p.reshape(1, n_batch, CONV_W - 1, cdim)
    k_sample = smp(k).reshape(1, ns, 1, N_KV_HEADS, HEAD_DIM)
    v_sample = smp(v).reshape(1, ns, 1, N_KV_HEADS, HEAD_DIM)
    kidx_sample = smp(kiw)[:, :IDX_DIM].reshape(1, ns, 1, IDX_DIM)
    ssm_sample = ssm_s.reshape(1, ns, n_heads, SSD_HEAD_DIM, D_STATE)
    conv_sample = conv_s.reshape(1, ns, CONV_W - 1, cdim)
    return (y_prompt, y_sample, k_prompt, v_prompt, kidx_prompt, ssm_prompt, conv_prompt,
            k_sample, v_sample, kidx_sample, ssm_sample, conv_sample)
```

```python
import functools

import jax
import jax.numpy as jnp
from jax import lax
from jax.experimental import pallas as pl
from jax.experimental.pallas import tpu as pltpu

F32 = jnp.float32
BF16 = jnp.bfloat16
I32 = jnp.int32
HIGHEST = lax.Precision.HIGHEST

EPS = 1e-6
N_META = 16
N_HEADS = 16
N_KV_HEADS = 4
HEAD_DIM = 64
Q_PER_KV = N_HEADS // N_KV_HEADS
IDX_HEADS = 4
IDX_DIM = 64
TOP_K_MAX = 256
SSD_HEAD_DIM = 64
SSD_GROUPS = 4
D_STATE = 128
CONV_W = 4
TILE = 128
PAD_FRONT = TILE - N_META
INT_MIN = -(2 ** 31)
NEG_BIG = -(2.0 ** 100)
Q_SCALE = HEAD_DIM ** -0.5 * 1.4426950408889634
VMEM_LIMIT = 56 * 1024 * 1024


def _cparams(sem, flags=None):
    return pltpu.CompilerParams(dimension_semantics=sem, vmem_limit_bytes=VMEM_LIMIT, flags=flags)


def _const_spec(shape):
    nd = len(shape)
    return pl.BlockSpec(shape, lambda *_: (0,) * nd, pipeline_mode=pl.Buffered(1))


def _rms(x, g):
    return x * lax.rsqrt(jnp.mean(x * x, axis=-1, keepdims=True) + EPS) * g


def _silu(x):
    return x * jax.nn.sigmoid(x)


def _row_tile(n_tiles, cap):
    for k in range(cap, 0, -1):
        if n_tiles % k == 0:
            return k * TILE
    return TILE


def _ffn_kernel(x_ref, g_ref, g2_ref, wg_ref, wu_ref, wd_ref, o_ref, h2_ref=None, *, ck):
    x = x_ref[...]
    h = _rms(x, g_ref[...]).astype(BF16)
    acc = jnp.zeros(x.shape, F32)
    for c in range(wg_ref.shape[1] // ck):
        sl = slice(c * ck, (c + 1) * ck)
        gate = jnp.dot(h, wg_ref[:, sl], preferred_element_type=F32)
        up = jnp.dot(h, wu_ref[:, sl], preferred_element_type=F32)
        act = (_silu(gate) * up).astype(BF16)
        acc = acc + jnp.dot(act, wd_ref[sl, :], preferred_element_type=F32)
    y = x + 0.5 * acc
    o_ref[...] = y
    if h2_ref is not None:
        h2_ref[...] = _rms(y, g2_ref[...]).astype(BF16)


def _ffn(x, g, g2, wg, wu, wd, tm, rows=None, with_norm=True):
    d = x.shape[1]
    r0, r = (0, x.shape[0]) if rows is None else rows
    dff = wg.shape[1]
    blk0 = r0 // tm
    out_specs = [pl.BlockSpec((tm, d), lambda i: (i, 0))]
    out_shape = [jax.ShapeDtypeStruct((r, d), F32)]
    if with_norm:
        out_specs.append(pl.BlockSpec((tm, d), lambda i: (i, 0)))
        out_shape.append(jax.ShapeDtypeStruct((r, d), BF16))
    return pl.pallas_call(
        functools.partial(_ffn_kernel, ck=256),
        grid=(r // tm,),
        in_specs=[pl.BlockSpec((tm, d), lambda i: (blk0 + i, 0)), _const_spec((1, d)), _const_spec((1, d)),
                  _const_spec((d, dff)), _const_spec((d, dff)), _const_spec((dff, d))],
        out_specs=out_specs,
        out_shape=out_shape,
        compiler_params=_cparams(("arbitrary",)),
        name="ffn",
    )(x, g, g2, wg, wu, wd)


def _proj_attn_kernel(x_ref, g_ref, wqk_ref, wv_ref, wih_ref, wil_ref, seg_ref, segt_ref, gqk_ref,
                      q_ref, k_ref, v_ref, kb_ref, vb_ref, qi_ref, kiw_ref):
    h = _rms(x_ref[...], g_ref[...])
    hb, h_lo = _split_bf16(h)
    qk = jnp.dot(hb, wqk_ref[...], preferred_element_type=F32)
    ss = jnp.dot((qk * qk).astype(BF16), seg_ref[...], preferred_element_type=F32)
    r = lax.rsqrt(ss * (1.0 / HEAD_DIM) + EPS)
    r_hi = r.astype(BF16)
    r_lo = (r - r_hi.astype(F32)).astype(BF16)
    rx = (jnp.dot(r_hi, segt_ref[...], preferred_element_type=F32)
          + jnp.dot(r_lo, segt_ref[...], preferred_element_type=F32))
    qkn = qk * rx * gqk_ref[...]
    nq = q_ref.shape[1]
    q_ref[...] = (qkn[:, :nq] * Q_SCALE).astype(BF16)
    k = qkn[:, nq:]
    k_ref[...] = k
    kb_ref[...] = k.astype(BF16)
    v = jnp.dot(hb, wv_ref[...], preferred_element_type=F32)
    v_ref[...] = v
    vb_ref[...] = v.astype(BF16)
    idx = _dot3(hb, h_lo, wih_ref[...], wil_ref[...])
    nqi = qi_ref.shape[1]
    qi_ref[...] = idx[:, :nqi]
    kiw_ref[...] = idx[:, nqi:]


def _proj_attn(x1, g, wqk, wv, wih, wil, seg, segt, gqk, tm):
    r, d = x1.shape
    nqk, nv = wqk.shape[1], wv.shape[1]
    nq = N_HEADS * HEAD_DIM
    nqi = IDX_HEADS * IDX_DIM
    row = lambda i: (i, 0)
    outs = [(nq, BF16), (nqk - nq, F32), (nv, F32), (nqk - nq, BF16), (nv, BF16), (nqi, F32), (TILE, F32)]
    return pl.pallas_call(
        _proj_attn_kernel,
        grid=(r // tm,),
        in_specs=[pl.BlockSpec((tm, d), row), _const_spec((1, d)), _const_spec(wqk.shape),
                  _const_spec(wv.shape), _const_spec(wih.shape), _const_spec(wil.shape), _const_spec(seg.shape),
                  _const_spec(segt.shape), _const_spec(gqk.shape)],
        out_specs=[pl.BlockSpec((tm, n), row) for n, _ in outs],
        out_shape=[jax.ShapeDtypeStruct((r, n), dt) for n, dt in outs],
        compiler_params=_cparams(("arbitrary",)),
        name="proj_attn",
    )(x1, g, wqk, wv, wih, wil, seg, segt, gqk)


def _proj_xbc_kernel(h_ref, wx_ref, wdt_ref, xbc_ref, dt_ref):
    h = h_ref[...]
    xbc_ref[...] = jnp.dot(h, wx_ref[...], preferred_element_type=F32)
    dt_ref[...] = jnp.dot(h, wdt_ref[...], preferred_element_type=F32)


def _proj_xbc(hb, wx, wdt, tm):
    r, d = hb.shape
    row = lambda i: (i, 0)
    return pl.pallas_call(
        _proj_xbc_kernel,
        grid=(r // tm,),
        in_specs=[pl.BlockSpec((tm, d), row), _const_spec(wx.shape), _const_spec(wdt.shape)],
        out_specs=[pl.BlockSpec((tm, wx.shape[1]), row), pl.BlockSpec((tm, wdt.shape[1]), row)],
        out_shape=[jax.ShapeDtypeStruct((r, wx.shape[1]), F32),
                   jax.ShapeDtypeStruct((r, wdt.shape[1]), F32)],
        compiler_params=_cparams(("arbitrary",)),
        name="proj_xbc",
    )(hb, wx, wdt)


def _proj_gate_kernel(h_ref, wz_ref, wg_ref, zs_ref, gt_ref):
    h = h_ref[...]
    zs_ref[...] = _silu(jnp.dot(h, wz_ref[...], preferred_element_type=F32)).astype(BF16)
    gt_ref[...] = jax.nn.sigmoid(jnp.dot(h, wg_ref[...], preferred_element_type=F32)).astype(BF16)


def _proj_gate(hb, wz, wg, tm):
    r, d = hb.shape
    row = lambda i: (i, 0)
    return pl.pallas_call(
        _proj_gate_kernel,
        grid=(r // tm,),
        in_specs=[pl.BlockSpec((tm, d), row), _const_spec(wz.shape), _const_spec(wg.shape)],
        out_specs=[pl.BlockSpec((tm, wz.shape[1]), row), pl.BlockSpec((tm, wg.shape[1]), row)],
        out_shape=[jax.ShapeDtypeStruct((r, wz.shape[1]), BF16),
                   jax.ShapeDtypeStruct((r, wg.shape[1]), BF16)],
        compiler_params=_cparams(("arbitrary",)),
        name="proj_gate",
    )(hb, wz, wg)


KEY_GROUP = 4


def _key_value(key):
    mag = jnp.where(key < 0, jnp.where(key == INT_MIN, 0x7F800000, -key), key)
    v = pltpu.bitcast(mag, F32)
    return jnp.where(key < 0, -v, v)


def _tree_sum(x):
    while x.shape[0] > 1:
        h = x.shape[0] // 2
        x = x[:h] + x[h:]
    return x[0]


def _count(sc_ref, n_groups, pred):
    def body(gi, acc):
        s = sc_ref[pl.ds(gi * KEY_GROUP, KEY_GROUP)]
        hit = jnp.where(pred(s, gi * (KEY_GROUP * TILE)), 1.0, 0.0)
        return acc + _tree_sum(hit.reshape(KEY_GROUP * TILE // 8, 8, TILE))
    part = lax.fori_loop(0, n_groups, body, jnp.zeros((8, TILE), F32))
    return jnp.sum(part, axis=0, keepdims=True)


def _any(flag):
    return jnp.max(jnp.where(flag, 1, 0)) > 0


def _kth_bitwise(sc_ref, n_groups, ksel):
    n_ge0 = _count(sc_ref, n_groups, lambda s, r0: s >= 0.0)
    t = jnp.where(n_ge0 >= ksel, jnp.zeros((1, TILE), I32), jnp.full((1, TILE), INT_MIN, I32))

    def bit_step(i, t):
        cand = t | jnp.left_shift(jnp.int32(1), 30 - i)
        cv = _key_value(cand)
        n = _count(sc_ref, n_groups, lambda s, r0: s >= cv)
        return jnp.where(n >= ksel, cand, t)

    return lax.fori_loop(0, 31, bit_step, t)


def _tie_search(sc_ref, n_groups, tv, need, n_bits):
    shape = (KEY_GROUP, TILE, TILE)
    rows = (lax.broadcasted_iota(I32, shape, 0) * TILE + lax.broadcasted_iota(I32, shape, 1)).astype(F32)

    def bit_step(i, p):
        cand = p | jnp.left_shift(jnp.int32(1), n_bits - 1 - i)
        cf = cand.astype(F32)
        n = _count(sc_ref, n_groups, lambda s, r0: (s == tv) & ((rows + r0.astype(F32)) < cf))
        return jnp.where(n < need, cand, p)
    return lax.fori_loop(0, n_bits, bit_step, jnp.zeros((1, TILE), I32)) + 1


def _select_topk(sc_ref, n_groups, ksel, n_bits):
    no_limit = jnp.full((1, TILE), float(1 << n_bits), F32)
    t = _kth_bitwise(sc_ref, n_groups, ksel)
    tv = _key_value(t)
    n_gt = _count(sc_ref, n_groups, lambda s, r0: s > tv)
    n_eq = _count(sc_ref, n_groups, lambda s, r0: s == tv)
    need = ksel - n_gt
    conflict = (t != INT_MIN) & (n_eq > need)
    lim = lax.cond(_any(conflict), lambda: _tie_search(sc_ref, n_groups, tv, need, n_bits).astype(F32),
                   lambda: no_limit)
    return tv, jnp.where(conflict, lim, no_limit)


def _split_bf16(x):
    hi = x.astype(BF16)
    return hi, (x - hi.astype(F32)).astype(BF16)


def _dot3(a_hi, a_lo, b_hi, b_lo):
    return (jnp.dot(a_hi, b_hi, preferred_element_type=F32)
            + jnp.dot(a_hi, b_lo, preferred_element_type=F32)
            + jnp.dot(a_lo, b_hi, preferred_element_type=F32))


VROWS = HEAD_DIM + 16


def _dsa_prompt_kernel(q_ref, qi_ref, kiwq_ref, kb_ref, vb_ref, kiw_ref, kbm_ref, vbm_ref, kiwm_ref, o_ref,
                       sc_ref, kk_ref, vt_ref, kihl_ref, rhs_ref, qa_ref, qb_ref,
                       m_ref, acc_ref, *, ksel, n_bits):
    b = pl.program_id(0)
    j = pl.program_id(1)
    nt = vb_ref.shape[0] // TILE + 1
    n_groups = lax.shift_right_logical(j + KEY_GROUP, KEY_GROUP.bit_length() - 1)
    n_pairs = lax.shift_right_logical(j + 2, 1)
    nkv = N_KV_HEADS
    gw = Q_PER_KV * TILE
    eye = (lax.broadcasted_iota(I32, (TILE, TILE), 0) == lax.broadcasted_iota(I32, (TILE, TILE), 1))

    @pl.when((b == 0) & (j == 0))
    def _():
        rhs_ref[...] = jnp.zeros(rhs_ref.shape, rhs_ref.dtype)
        ident = jnp.where(eye, 1.0, 0.0).astype(BF16)
        for g in range(nkv):
            for hh in range(Q_PER_KV):
                rhs_ref[g, 0:TILE, hh * TILE:(hh + 1) * TILE] = ident
        qa_ref[...] = jnp.zeros(qa_ref.shape, qa_ref.dtype)
        qb_ref[...] = jnp.zeros(qb_ref.shape, qb_ref.dtype)
        kk_ref[...] = jnp.zeros(kk_ref.shape, kk_ref.dtype)
        kihl_ref[...] = jnp.zeros(kihl_ref.shape, kihl_ref.dtype)
        ones_row = jnp.where(lax.broadcasted_iota(I32, (VROWS - HEAD_DIM, TILE), 0) == 0, 1.0, 0.0).astype(BF16)
        for c in range(vt_ref.shape[0]):
            vt_ref[c] = jnp.zeros(vt_ref.shape[1:], vt_ref.dtype)
            for g in range(nkv):
                vt_ref[c, g * VROWS + HEAD_DIM:(g + 1) * VROWS, :] = ones_row

    @pl.when(j == 0)
    def _():
        def stage(c, kb, vb, kiw):
            kk_ref[c] = kb
            vt = vb.astype(F32).T.astype(BF16)
            for g in range(nkv):
                vt_ref[c, g * VROWS:g * VROWS + HEAD_DIM, :] = vt[g * HEAD_DIM:(g + 1) * HEAD_DIM, :]
            hi, lo = _split_bf16(kiw)
            kihl_ref[c] = jnp.concatenate([hi, lo], axis=1)

        stage(0, kbm_ref[...], vbm_ref[...], kiwm_ref[...])

        def prep(c, _):
            rows = pl.ds(pl.multiple_of((c - 1) * TILE, TILE), TILE)
            stage(c, kb_ref[rows, :], vb_ref[rows, :], kiw_ref[rows, :])
            return 0
        lax.fori_loop(1, nt, prep, 0)

    qt = q_ref[...].astype(F32).T.astype(BF16)
    for h in range(N_HEADS):
        g, hh = divmod(h, Q_PER_KV)
        r0 = TILE + (g % 2) * HEAD_DIM
        rhs_ref[g, r0:r0 + HEAD_DIM, hh * TILE:(hh + 1) * TILE] = qt[h * HEAD_DIM:(h + 1) * HEAD_DIM, :]
    qit = qi_ref[...].T
    for h in range(IDX_HEADS):
        hi, lo = _split_bf16(qit[h * IDX_DIM:(h + 1) * IDX_DIM, :])
        qa_ref[0:IDX_DIM, h * TILE:(h + 1) * TILE] = hi
        qa_ref[TILE:TILE + IDX_DIM, h * TILE:(h + 1) * TILE] = hi
        qb_ref[0:IDX_DIM, h * TILE:(h + 1) * TILE] = lo
    wq = kiwq_ref[...].T[IDX_DIM:IDX_DIM + 8, :]
    wq = wq * (IDX_HEADS ** -0.5 * IDX_DIM ** -0.5)

    grp_rows = KEY_GROUP * TILE
    q_pos = j * TILE + lax.broadcasted_iota(I32, (grp_rows, TILE), 1)
    k_row = lax.broadcasted_iota(I32, (grp_rows, TILE), 0)

    def score_group(gi, _):
        khl = kihl_ref[pl.ds(gi * KEY_GROUP, KEY_GROUP)].reshape(grp_rows, 2 * TILE)
        d = (jnp.dot(khl, qa_ref[...], preferred_element_type=F32)
             + jnp.dot(khl, qb_ref[...], preferred_element_type=F32))
        s = jnp.zeros((grp_rows, TILE), F32)
        for h in range(IDX_HEADS):
            s = s + wq[h:h + 1, :] * jnp.maximum(d[:, h * TILE:(h + 1) * TILE], 0.0)
        k_pos = k_row + gi * grp_rows
        valid = (k_pos <= q_pos) & (k_pos >= PAD_FRONT)
        sc_ref[pl.ds(gi * KEY_GROUP, KEY_GROUP)] = jnp.where(valid, s, jnp.nan).reshape(KEY_GROUP, TILE, TILE)
        return 0
    lax.fori_loop(0, n_groups, score_group, 0)

    tv, lim = _select_topk(sc_ref, n_groups, ksel, n_bits)

    m_ref[...] = jnp.full(m_ref.shape, NEG_BIG, F32)
    acc_ref[...] = jnp.zeros(acc_ref.shape, F32)

    pair_shape = (2, TILE, TILE)
    pair_row = (lax.broadcasted_iota(I32, pair_shape, 0) * TILE
                + lax.broadcasted_iota(I32, pair_shape, 1)).astype(F32)

    def attend_pair(i, _):
        sc = sc_ref[pl.ds(2 * i, 2)]
        sel = (sc > tv) | ((sc == tv) & ((pair_row + (i * (2 * TILE)).astype(F32)) < lim))
        bias = jnp.where(sel, 0.0, NEG_BIG).astype(BF16).reshape(2 * TILE, TILE)
        kc = kk_ref[pl.ds(2 * i, 2)].reshape(2 * TILE, nkv * HEAD_DIM)
        vt = jnp.concatenate([vt_ref[2 * i], vt_ref[2 * i + 1]], axis=1)
        m_all = m_ref[...]
        s = []
        for g in range(nkv):
            half = (g // 2) * TILE
            lhs = jnp.concatenate([bias, kc[:, half:half + TILE]], axis=1)
            s.append(jnp.dot(lhs, rhs_ref[g], preferred_element_type=F32))
        m_new = []
        for g in range(nkv):
            m_g = jnp.maximum(m_all[g:g + 1, :], jnp.max(s[g], axis=0, keepdims=True))
            alpha = jnp.exp2(m_all[g:g + 1, :] - m_g)
            p = jnp.exp2(s[g] - m_g).astype(BF16)
            pv = jnp.dot(vt[g * VROWS:(g + 1) * VROWS, :], p, preferred_element_type=F32)
            acc_ref[g] = alpha * acc_ref[g] + pv
            m_new.append(m_g)
        m_ref[...] = jnp.concatenate(m_new + [m_all[nkv:, :]], axis=0)
        return 0
    lax.fori_loop(0, n_pairs, attend_pair, 0)

    pieces = []
    for g in range(nkv):
        acc = acc_ref[g]
        o = acc[:HEAD_DIM] / acc[HEAD_DIM:HEAD_DIM + 1]
        for hh in range(Q_PER_KV):
            pieces.append(o[:, hh * TILE:(hh + 1) * TILE])
    o_ref[...] = jnp.concatenate(pieces, axis=0).T.astype(BF16)


def _seq_tile_map(seq, meta_blk, per_seq=False):
    n_tok = seq // TILE
    if per_seq:
        return lambda b, j: (jnp.where(j == 0, meta_blk + 1 + b, b * n_tok + j - 1), 0)
    return lambda b, j: (jnp.where(j == 0, meta_blk, b * n_tok + j - 1), 0)


def _dsa_prompt(q, qi, kiw, kb, vb, n_batch, seq, meta_blk, ksel):
    lp = seq + TILE
    nt = lp // TILE
    nt_pad = -(-nt // KEY_GROUP) * KEY_GROUP
    r_total = q.shape[0]
    qw = N_HEADS * HEAD_DIM
    kvw = N_KV_HEADS * HEAD_DIM
    gw = Q_PER_KV * TILE
    tile = _seq_tile_map(seq, meta_blk)
    toks = lambda b, j: (b, 0)
    meta = lambda b, j: (meta_blk, 0)
    n_bits = max(1, (lp - 1).bit_length())
    return pl.pallas_call(
        functools.partial(_dsa_prompt_kernel, ksel=ksel, n_bits=n_bits),
        grid=(n_batch, nt),
        in_specs=[pl.BlockSpec((TILE, qw), tile), pl.BlockSpec((TILE, IDX_HEADS * IDX_DIM), tile),
                  pl.BlockSpec((TILE, TILE), tile), pl.BlockSpec((seq, kvw), toks),
                  pl.BlockSpec((seq, kvw), toks), pl.BlockSpec((seq, TILE), toks),
                  pl.BlockSpec((TILE, kvw), meta), pl.BlockSpec((TILE, kvw), meta),
                  pl.BlockSpec((TILE, TILE), meta)],
        out_specs=pl.BlockSpec((TILE, qw), _seq_tile_map(seq, meta_blk, per_seq=True)),
        out_shape=jax.ShapeDtypeStruct((r_total + n_batch * TILE, qw), BF16),
        scratch_shapes=[pltpu.VMEM((nt_pad, TILE, TILE), F32),
                        pltpu.VMEM((nt_pad, TILE, kvw), BF16),
                        pltpu.VMEM((nt_pad, N_KV_HEADS * VROWS, TILE), BF16),
                        pltpu.VMEM((nt_pad, TILE, 2 * TILE), BF16),
                        pltpu.VMEM((N_KV_HEADS, 2 * TILE, gw), BF16),
                        pltpu.VMEM((2 * TILE, IDX_HEADS * TILE), BF16),
                        pltpu.VMEM((2 * TILE, IDX_HEADS * TILE), BF16),
                        pltpu.VMEM((8, gw), F32),
                        pltpu.VMEM((N_KV_HEADS, VROWS, gw), F32)],
        compiler_params=_cparams(("arbitrary", "arbitrary")),
        name="dsa_prompt",
    )(q, qi, kiw, kb, vb, kiw, kb, vb, kiw)


def _expand_heads(v, e_ref):
    return jnp.dot(v, e_ref[...], preferred_element_type=F32, precision=HIGHEST)


def _softplus(x):
    return jnp.maximum(x, 0.0) + jnp.log1p(jnp.exp(-jnp.abs(x)))


def _ssd_prompt_kernel(xbc_ref, dtr_ref, zs_ref, cw_ref, cb_ref, dtb_ref, alog_ref, dsk_ref, gn_ref,
                       e_ref, y_ref, ssm_ref, conv_ref, xp_ref, ht_ref, *, n_heads):
    c = pl.program_id(1)
    nc = pl.num_programs(1)
    d_inner = n_heads * SSD_HEAD_DIM
    hpg = n_heads // SSD_GROUPS
    gwid = hpg * SSD_HEAD_DIM
    gn_w = d_inner // SSD_GROUPS

    @pl.when(c == 0)
    def _():
        xp_ref[0:8, :] = jnp.zeros((8, xp_ref.shape[1]), F32)
        ht_ref[...] = jnp.zeros(ht_ref.shape, F32)

    xp_ref[8:8 + TILE, :] = xbc_ref[...]
    u = cb_ref[...] + cw_ref[CONV_W - 1:CONV_W, :] * xp_ref[8:8 + TILE, :]
    for k in range(CONV_W - 1):
        off = 8 - (CONV_W - 1) + k
        u = u + cw_ref[k:k + 1, :] * xp_ref[off:off + TILE, :]
    tail = xp_ref[8 + TILE - (CONV_W - 1):8 + TILE, :]
    xp_ref[8 - (CONV_W - 1):8, :] = tail

    row = lax.broadcasted_iota(I32, (TILE, 1), 0)
    live = jnp.where((c > 0) | (row >= PAD_FRONT), 1.0, 0.0)
    act = _silu(u) * live
    xs = act[:, :d_inner]
    bm = act[:, d_inner:d_inner + SSD_GROUPS * D_STATE]
    cm = act[:, d_inner + SSD_GROUPS * D_STATE:]

    dt = _softplus(dtr_ref[...] + dtb_ref[...]) * live
    a = dt * (-jnp.exp(alog_ref[...]))
    ri = lax.broadcasted_iota(I32, (TILE, TILE), 0)
    ci = lax.broadcasted_iota(I32, (TILE, TILE), 1)
    tril = ri >= ci
    acs = jnp.dot(jnp.where(tril, 1.0, 0.0), a, preferred_element_type=F32, precision=HIGHEST)
    acs_t = acs.T
    dt_t = dt.T
    acs_last = acs[TILE - 1:TILE, :]
    w_in = _expand_heads(jnp.exp(acs_last - acs) * dt, e_ref)
    w_out = _expand_heads(jnp.exp(acs), e_ref)
    xw = (xs * w_in).astype(BF16)
    xsb = xs.astype(BF16)
    lane = lax.broadcasted_iota(I32, (1, gwid), 1)

    ys = []
    for g in range(SSD_GROUPS):
        bg = bm[:, g * D_STATE:(g + 1) * D_STATE]
        cg = cm[:, g * D_STATE:(g + 1) * D_STATE].astype(BF16)
        bgt = bg.T.astype(BF16)
        cb = jnp.dot(cg, bgt, preferred_element_type=F32)
        xw_g = xw[:, g * gwid:(g + 1) * gwid]
        xs_g = xsb[:, g * gwid:(g + 1) * gwid]
        h_prev = ht_ref[g]
        y_off = jnp.dot(cg, h_prev.astype(BF16), preferred_element_type=F32)
        y_g = y_off * w_out[:, g * gwid:(g + 1) * gwid]
        for e0 in range(0, hpg, 2):
            ms, xm = [], []
            for e in (e0, e0 + 1):
                he = g * hpg + e
                seg = acs[:, he:he + 1] - acs_t[he:he + 1, :]
                ms.append((cb * jnp.exp(jnp.where(tril, seg, -jnp.inf)) * dt_t[he:he + 1, :]).astype(BF16))
                only = (lane >= e * SSD_HEAD_DIM) & (lane < (e + 1) * SSD_HEAD_DIM)
                xm.append(jnp.where(only, xs_g, jnp.zeros_like(xs_g)))
            y_g = y_g + jnp.dot(jnp.concatenate(ms, axis=1), jnp.concatenate(xm, axis=0),
                                preferred_element_type=F32)
        states = jnp.dot(bgt, xw_g, preferred_element_type=F32)
        ht_ref[g] = h_prev * w_out[TILE - 1:TILE, g * gwid:(g + 1) * gwid] + states
        ys.append(y_g)
    y = jnp.concatenate(ys, axis=1) + dsk_ref[...] * xs
    y = y * zs_ref[...].astype(F32)
    outs = []
    for g in range(SSD_GROUPS):
        yg = y[:, g * gn_w:(g + 1) * gn_w]
        outs.append(yg * lax.rsqrt(jnp.mean(yg * yg, axis=-1, keepdims=True) + EPS))
    y_ref[...] = (jnp.concatenate(outs, axis=1) * gn_ref[...]).astype(BF16)

    @pl.when(c == nc - 1)
    def _():
        for g in range(SSD_GROUPS):
            ssm_ref[0, g * gwid:(g + 1) * gwid, :] = ht_ref[g].T
        conv_ref[0] = tail


def _ssd_prompt(xbc, dtr, zs, cw, cb, dtb, alog, dsk, gn, e_mat, n_batch, seq, meta_blk, n_heads):
    nt = seq // TILE + 1
    r_total = xbc.shape[0]
    d_inner = n_heads * SSD_HEAD_DIM
    cdim = xbc.shape[1]
    tile = _seq_tile_map(seq, meta_blk)
    fix = lambda b, c: (0, 0)
    return pl.pallas_call(
        functools.partial(_ssd_prompt_kernel, n_heads=n_heads),
        grid=(n_batch, nt),
        in_specs=[pl.BlockSpec((TILE, cdim), tile), pl.BlockSpec((TILE, TILE), tile),
                  pl.BlockSpec((TILE, d_inner), tile),
                  pl.BlockSpec(cw.shape, fix), pl.BlockSpec(cb.shape, fix), pl.BlockSpec(dtb.shape, fix),
                  pl.BlockSpec(alog.shape, fix), pl.BlockSpec(dsk.shape, fix), pl.BlockSpec(gn.shape, fix),
                  pl.BlockSpec(e_mat.shape, fix)],
        out_specs=[pl.BlockSpec((TILE, d_inner), _seq_tile_map(seq, meta_blk, per_seq=True)),
                   pl.BlockSpec((1, d_inner, D_STATE), lambda b, c: (b, 0, 0)),
                   pl.BlockSpec((1, CONV_W - 1, cdim), lambda b, c: (b, 0, 0))],
        out_shape=[jax.ShapeDtypeStruct((r_total + n_batch * TILE, d_inner), BF16),
                   jax.ShapeDtypeStruct((n_batch, d_inner, D_STATE), F32),
                   jax.ShapeDtypeStruct((n_batch, CONV_W - 1, cdim), F32)],
        scratch_shapes=[pltpu.VMEM((8 + TILE, cdim), F32),
                        pltpu.VMEM((SSD_GROUPS, D_STATE, d_inner // SSD_GROUPS), F32)],
        compiler_params=_cparams(("arbitrary", "arbitrary")),
        name="ssd_prompt",
    )(xbc, dtr, zs, cw, cb, dtb, alog, dsk, gn, e_mat)


def _idx_sample_kernel(pt_ref, qi_ref, kiw_ref, *rest, n_pages):
    pages, s_ref = rest[:n_pages], rest[n_pages]
    qi = qi_ref[0]
    kiw = kiw_ref[0]
    q8 = jnp.concatenate([qi[:, h * IDX_DIM:(h + 1) * IDX_DIM] for h in range(IDX_HEADS)]
                         + [jnp.zeros((8 - IDX_HEADS, IDX_DIM), F32)], axis=0)
    w_col = jnp.broadcast_to(kiw, (TILE, TILE)).T[IDX_DIM:IDX_DIM + 8, 0:1]
    w_col = w_col * (IDX_HEADS ** -0.5 * IDX_DIM ** -0.5)
    q_hi, q_lo = _split_bf16(q8)
    for p in range(n_pages):
        k_hi, k_lo = _split_bf16(pages[p][0])
        d = _dot3(q_hi, q_lo, k_hi, k_lo)
        s_ref[p, 0] = jnp.sum(w_col * jnp.maximum(d, 0.0), axis=0, keepdims=True)
    d_new = jnp.sum(q8 * kiw[:, :IDX_DIM], axis=1, keepdims=True)
    s_new = jnp.sum(w_col * jnp.maximum(d_new, 0.0), axis=0, keepdims=True)
    lane = lax.broadcasted_iota(I32, (1, TILE), 1)
    s_ref[n_pages, 0] = jnp.where(lane == 0, s_new, 0.0)


def _idx_sample(page_table, qi_s, kiw_s, cache_kidx):
    ns, n_pages = page_table.shape
    n_slots = n_pages + 1
    page = lambda p: pl.BlockSpec((1, IDX_DIM, TILE), lambda s, pt: (pt[s, p], 0, 0))
    one = lambda s, pt: (s, 0, 0)
    grid_spec = pltpu.PrefetchScalarGridSpec(
        num_scalar_prefetch=1, grid=(ns,),
        in_specs=[pl.BlockSpec((1, 1, qi_s.shape[2]), one), pl.BlockSpec((1, 1, TILE), one)]
        + [page(p) for p in range(n_pages)],
        out_specs=pl.BlockSpec((n_slots, 1, 1, TILE), lambda s, pt: (0, s, 0, 0)))
    return pl.pallas_call(
        functools.partial(_idx_sample_kernel, n_pages=n_pages),
        grid_spec=grid_spec,
        out_shape=jax.ShapeDtypeStruct((n_slots, ns, 1, TILE), F32),
        compiler_params=_cparams(("arbitrary",)),
        name="idx_sample",
    )(page_table, qi_s, kiw_s, *([cache_kidx] * n_pages))


def _select_sample_kernel(s_ref, bias_ref, sc_ref, *, n_pages, ksel, n_bits):
    n_slots = n_pages + 1
    rows = lax.broadcasted_iota(I32, (TILE, TILE), 0)
    for p in range(n_slots):
        sc = s_ref[p].T
        if p == n_pages:
            sc = jnp.where(rows == 0, sc, jnp.nan)
        sc_ref[p] = sc
    for p in range(n_slots, sc_ref.shape[0]):
        sc_ref[p] = jnp.full((TILE, TILE), jnp.nan, F32)
    n_groups = sc_ref.shape[0] // KEY_GROUP
    tv, lim = _select_topk(sc_ref, n_groups, ksel, n_bits)
    rows_f = rows.astype(F32)
    for p in range(n_slots):
        sc = sc_ref[p]
        sel = (sc > tv) | ((sc == tv) & ((rows_f + float(p * TILE)) < lim))
        bias_ref[p] = jnp.where(sel, 0.0, NEG_BIG).T


def _select_sample(scores, ksel):
    n_slots, ns, _ = scores.shape
    n_pages = n_slots - 1
    n_bits = (n_slots * TILE - 1).bit_length()
    return pl.pallas_call(
        functools.partial(_select_sample_kernel, n_pages=n_pages, ksel=ksel, n_bits=n_bits),
        out_shape=jax.ShapeDtypeStruct((n_slots, ns, TILE), F32),
        scratch_shapes=[pltpu.VMEM((-(-n_slots // KEY_GROUP) * KEY_GROUP, TILE, TILE), F32)],
        compiler_params=pltpu.CompilerParams(vmem_limit_bytes=VMEM_LIMIT),
        name="select_sample",
    )(scores)


def _attn_sample_kernel(pt_ref, q_ref, kn_ref, vn_ref, bias_ref, *rest, n_pages):
    kpages, vpages, o_ref = rest[:n_pages], rest[n_pages:2 * n_pages], rest[2 * n_pages]
    kvw = N_KV_HEADS * HEAD_DIM
    q = q_ref[0].astype(F32)
    lane = lax.broadcasted_iota(I32, (N_HEADS, kvw), 1)
    head = lax.broadcasted_iota(I32, (N_HEADS, kvw), 0)
    own = jnp.right_shift(lane, 6) == jnp.right_shift(head, 2)
    qrows = []
    for h in range(N_HEADS):
        qh = q[:, h * HEAD_DIM:(h + 1) * HEAD_DIM]
        qrows.append(jnp.concatenate([qh] * N_KV_HEADS, axis=1))
    qf = jnp.where(own, jnp.concatenate(qrows, axis=0), 0.0)
    qw = qf.astype(BF16)
    nt = (((1,), (1,)), ((), ()))
    scores = []
    for p in range(n_pages):
        kp = kpages[p][0].astype(BF16)
        s = jnp.dot(qw, kp, preferred_element_type=F32)
        scores.append(s + bias_ref[p, 0])
    s_new = jnp.sum(qf * kn_ref[0], axis=1, keepdims=True)
    lane1 = lax.broadcasted_iota(I32, (1, TILE), 1)
    scores.append(jnp.where(lane1 == 0, s_new, NEG_BIG) + bias_ref[n_pages, 0])
    m = scores[0].max(axis=1, keepdims=True)
    for s in scores[1:]:
        m = jnp.maximum(m, s.max(axis=1, keepdims=True))
    den = jnp.zeros((N_HEADS, 1), F32)
    out = jnp.zeros((N_HEADS, kvw), F32)
    for p in range(n_pages):
        e = jnp.exp2(scores[p] - m)
        den = den + e.sum(axis=1, keepdims=True)
        out = out + lax.dot_general(e.astype(BF16), vpages[p][0].astype(BF16), nt,
                                    preferred_element_type=F32)
    e = jnp.exp2(scores[n_pages] - m)
    den = den + e.sum(axis=1, keepdims=True)
    out = out + e[:, 0:1] * vn_ref[0]
    out = out / den
    rolled = jnp.concatenate([pltpu.roll(out[:, i * TILE:(i + 1) * TILE], HEAD_DIM, 1)
                              for i in range(kvw // TILE)], axis=1)
    half = lax.broadcasted_iota(I32, (1, TILE), 1) < HEAD_DIM
    pieces = []
    for m2 in range(N_HEADS // 2):
        g = (2 * m2) // Q_PER_KV
        blk = (g // 2) * TILE
        a_src, b_src = (rolled, out) if g % 2 == 1 else (out, rolled)
        a = a_src[2 * m2:2 * m2 + 1, blk:blk + TILE]
        b = b_src[2 * m2 + 1:2 * m2 + 2, blk:blk + TILE]
        pieces.append(jnp.where(half, a, b))
    o_ref[0] = jnp.concatenate(pieces, axis=1)


def _attn_sample(page_table, q_s, k_s, v_s, bias, cache_k, cache_v):
    ns, n_pages = page_table.shape
    kvw = N_KV_HEADS * HEAD_DIM
    qw = N_HEADS * HEAD_DIM
    page = lambda p: pl.BlockSpec((1, kvw, TILE), lambda s, pt: (pt[s, p], 0, 0))
    one = lambda s, pt: (s, 0, 0)
    grid_spec = pltpu.PrefetchScalarGridSpec(
        num_scalar_prefetch=1, grid=(ns,),
        in_specs=[pl.BlockSpec((1, 1, qw), one), pl.BlockSpec((1, 1, kvw), one),
                  pl.BlockSpec((1, 1, kvw), one),
                  pl.BlockSpec((n_pages + 1, 1, 1, TILE), lambda s, pt: (0, s, 0, 0))]
        + [page(p) for p in range(n_pages)] * 2,
        out_specs=pl.BlockSpec((1, 1, qw), one))
    return pl.pallas_call(
        functools.partial(_attn_sample_kernel, n_pages=n_pages),
        grid_spec=grid_spec,
        out_shape=jax.ShapeDtypeStruct((ns, 1, qw), F32),
        compiler_params=_cparams(("arbitrary",)),
        name="attn_sample",
    )(page_table, q_s, k_s, v_s, bias, *([cache_k] * n_pages), *([cache_v] * n_pages))


def _ssd_sample_kernel(xbc_ref, dtr_ref, sc_ref, h0_ref, cw_ref, cb_ref, dtb_ref, alog_ref, dsk_ref,
                       e_ref, y_ref, h_ref, conv_ref, *, n_heads, sb):
    d_inner = n_heads * SSD_HEAD_DIM
    hpg = n_heads // SSD_GROUPS
    new = xbc_ref[...]
    cdim = new.shape[1]
    u = cb_ref[...] + cw_ref[CONV_W - 1:CONV_W, :] * new
    for k in range(CONV_W - 1):
        u = u + cw_ref[k:k + 1, :] * sc_ref[:, k * cdim:(k + 1) * cdim]
    for k in range(CONV_W - 2):
        conv_ref[:, k * cdim:(k + 1) * cdim] = sc_ref[:, (k + 1) * cdim:(k + 2) * cdim]
    conv_ref[:, (CONV_W - 2) * cdim:] = new
    act = _silu(u)
    xs = act[:, :d_inner]
    bm = act[:, d_inner:d_inner + SSD_GROUPS * D_STATE]
    cm = act[:, d_inner + SSD_GROUPS * D_STATE:]
    dt = _softplus(dtr_ref[...] + dtb_ref[...])
    decay = jnp.exp(dt * (-jnp.exp(alog_ref[...])))
    dec_x = _expand_heads(decay, e_ref)
    xdt = xs * _expand_heads(dt, e_ref)
    for i in range(sb):
        bcols, ccols = [], []
        for g in range(SSD_GROUPS):
            brow = bm[i:i + 1, g * D_STATE:(g + 1) * D_STATE]
            crow = cm[i:i + 1, g * D_STATE:(g + 1) * D_STATE]
            bcols.append(jnp.broadcast_to(brow, (TILE, D_STATE)).T)
            ccols.append(jnp.broadcast_to(crow, (TILE, D_STATE)).T)
        pieces = []
        for pr in range(n_heads // 2):
            g = (2 * pr) // hpg
            sl = slice(pr * TILE, (pr + 1) * TILE)
            ht = h0_ref[i, sl, :].T
            hn = ht * dec_x[i:i + 1, sl] + bcols[g] * xdt[i:i + 1, sl]
            pieces.append(jnp.sum(hn * ccols[g], axis=0, keepdims=True))
            h_ref[i, sl, :] = hn.T
        y_ref[i:i + 1, :] = jnp.concatenate(pieces, axis=1) + dsk_ref[...] * xs[i:i + 1, :]


def _ssd_sample(xbc_s, dtr_s, state_conv, state_ssm, cw, cb, dtb, alog, dsk, e_mat, n_heads, sb):
    ns, cdim = xbc_s.shape
    d_inner = n_heads * SSD_HEAD_DIM
    row = lambda i: (i, 0)
    row3 = lambda i: (i, 0, 0)
    fix = lambda i: (0, 0)
    return pl.pallas_call(
        functools.partial(_ssd_sample_kernel, n_heads=n_heads, sb=sb),
        grid=(ns // sb,),
        in_specs=[pl.BlockSpec((sb, cdim), row), pl.BlockSpec((sb, TILE), row),
                  pl.BlockSpec((sb, (CONV_W - 1) * cdim), row), pl.BlockSpec((sb, d_inner, D_STATE), row3),
                  pl.BlockSpec(cw.shape, fix), pl.BlockSpec(cb.shape, fix), pl.BlockSpec(dtb.shape, fix),
                  pl.BlockSpec(alog.shape, fix), pl.BlockSpec(dsk.shape, fix), pl.BlockSpec(e_mat.shape, fix)],
        out_specs=[pl.BlockSpec((sb, d_inner), row), pl.BlockSpec((sb, d_inner, D_STATE), row3),
                   pl.BlockSpec((sb, (CONV_W - 1) * cdim), row)],
        out_shape=[jax.ShapeDtypeStruct((ns, d_inner), F32),
                   jax.ShapeDtypeStruct((ns, d_inner, D_STATE), F32),
                   jax.ShapeDtypeStruct((ns, (CONV_W - 1) * cdim), F32)],
        compiler_params=_cparams(("arbitrary",)),
        name="ssd_sample",
    )(xbc_s, dtr_s, state_conv, state_ssm, cw, cb, dtb, alog, dsk, e_mat)


def _fill_sample_kernel(attn_any, ssd_any, a_ref, y_ref, zs_ref, gn_ref, attn_ref, ssd_ref):
    del attn_any, ssd_any
    attn_ref[...] = a_ref[...].astype(BF16)
    y = y_ref[...] * zs_ref[...].astype(F32)
    gn_w = y.shape[1] // SSD_GROUPS
    outs = []
    for g in range(SSD_GROUPS):
        yg = y[:, g * gn_w:(g + 1) * gn_w]
        outs.append(yg * lax.rsqrt(jnp.mean(yg * yg, axis=-1, keepdims=True) + EPS))
    ssd_ref[...] = (jnp.concatenate(outs, axis=1) * gn_ref[...]).astype(BF16)


def _fill_sample(attn, ssd, attn_s, y_s, zs, gn, blk):
    ns = attn_s.shape[0]
    last = lambda i: (blk, 0)
    fix = lambda i: (0, 0)
    return pl.pallas_call(
        _fill_sample_kernel,
        grid=(1,),
        in_specs=[pl.BlockSpec(memory_space=pl.ANY), pl.BlockSpec(memory_space=pl.ANY),
                  pl.BlockSpec(attn_s.shape, fix), pl.BlockSpec(y_s.shape, fix),
                  pl.BlockSpec((ns, zs.shape[1]), last), pl.BlockSpec(gn.shape, fix)],
        out_specs=[pl.BlockSpec((ns, attn.shape[1]), last), pl.BlockSpec((ns, ssd.shape[1]), last)],
        out_shape=[jax.ShapeDtypeStruct(attn.shape, attn.dtype), jax.ShapeDtypeStruct(ssd.shape, ssd.dtype)],
        input_output_aliases={0: 0, 1: 1},
        compiler_params=_cparams(("arbitrary",)),
        name="fill_sample",
    )(attn, ssd, attn_s, y_s, zs, gn)


def _mix_kernel(x_ref, a_ref, s_ref, gt_ref, wao_ref, wso_ref, wo_ref, o_ref):
    d = x_ref.shape[1]
    gt = gt_ref[...].astype(F32)
    ao = jnp.dot(a_ref[...], wao_ref[...], preferred_element_type=F32)
    so = jnp.dot(s_ref[...], wso_ref[...], preferred_element_type=F32)
    mixed = (gt[:, :d] * ao + gt[:, d:] * so).astype(BF16)
    o_ref[...] = x_ref[...] + jnp.dot(mixed, wo_ref[...], preferred_element_type=F32)


def _mix(x1, attn, ssd, gates, wao, wso, wo, tm, rows):
    d = x1.shape[1]
    r0, r = rows
    blk0 = r0 // tm
    row = lambda i: (blk0 + i, 0)
    return pl.pallas_call(
        _mix_kernel,
        grid=(r // tm,),
        in_specs=[pl.BlockSpec((tm, d), row), pl.BlockSpec((tm, attn.shape[1]), row),
                  pl.BlockSpec((tm, ssd.shape[1]), row), pl.BlockSpec((tm, gates.shape[1]), row),
                  _const_spec(wao.shape), _const_spec(wso.shape), _const_spec(wo.shape)],
        out_specs=pl.BlockSpec((tm, d), lambda i: (i, 0)),
        out_shape=jax.ShapeDtypeStruct((r, d), F32),
        compiler_params=_cparams(("arbitrary",)),
        name="mix",
    )(x1, attn, ssd, gates, wao, wso, wo)


def kernel(x_prompt, x_sample, cache_k, cache_v, cache_kidx, state_ssm, state_conv, page_table, meta_tokens,
           g_ffn1, w_ffn1_gate, w_ffn1_up, w_ffn1_down, g_mix, w_in, g_q, g_k, conv_w, conv_b, dt_bias, a_log,
           d_skip, g_ssd_norm, w_attn_out, w_ssd_out, w_o, g_ffn2, w_ffn2_gate, w_ffn2_up, w_ffn2_down):
    depth = w_in.shape[0]
    assert depth == 1, "single-layer step"
    n_batch, seq, d_model = x_prompt.shape
    ns = x_sample.shape[0]
    assert x_sample.shape[1] == 1 and seq % TILE == 0 and ns == TILE
    page = cache_k.shape[2]
    assert page == TILE
    l_seq = seq + N_META
    r_tok = n_batch * seq
    r_total = r_tok + ns + TILE
    meta_blk = (r_tok + ns) // TILE
    n_heads = state_ssm.shape[2]
    d_inner = n_heads * SSD_HEAD_DIM
    assert n_heads <= TILE and n_heads % (2 * SSD_GROUPS) == 0
    cdim = d_inner + 2 * SSD_GROUPS * D_STATE
    nq, nkv = N_HEADS * HEAD_DIM, N_KV_HEADS * HEAD_DIM
    nqi = IDX_HEADS * IDX_DIM
    tm = _row_tile(r_total // TILE, 6)
    tm_tok = _row_tile(r_tok // TILE, 8)

    w = w_in[0]
    offs = [0]
    for n in (nq, nkv, nkv, nqi, IDX_DIM, IDX_HEADS, d_inner, cdim, n_heads, d_model, d_model):
        offs.append(offs[-1] + n)
    col = lambda i: w[:, offs[i]:offs[i + 1]]
    wqk = jnp.concatenate([col(0), col(1)], axis=1).astype(BF16)
    wv = col(2).astype(BF16)
    widx = jnp.concatenate([col(3), col(4), col(5), jnp.zeros((d_model, TILE - IDX_DIM - IDX_HEADS), F32)], axis=1)
    wz = col(6).astype(BF16)
    wx = col(7).astype(BF16)
    wdt = jnp.concatenate([col(8), jnp.zeros((d_model, TILE - n_heads), F32)], axis=1).astype(BF16)
    wgate = jnp.concatenate([col(9), col(10)], axis=1).astype(BF16)
    n_seg = (nq + nkv) // HEAD_DIM
    seg = (jnp.arange(nq + nkv)[:, None] // HEAD_DIM == jnp.arange(TILE)[None, :]).astype(BF16)
    segt = seg.T
    del n_seg
    gqk = jnp.concatenate([jnp.tile(g_q[0], N_HEADS), jnp.tile(g_k[0], N_KV_HEADS)])[None, :]
    row1 = lambda v: v.reshape(1, -1)
    pad_heads = lambda v, fill: jnp.concatenate([v, jnp.full((TILE - n_heads,), fill, F32)])[None, :]
    dtb = pad_heads(dt_bias[0], -1e4)
    alog = pad_heads(a_log[0], 0.0)
    dsk = jnp.repeat(d_skip[0], SSD_HEAD_DIM)[None, :]
    e_mat = (jnp.arange(TILE)[:, None] == jnp.arange(d_inner)[None, :] // SSD_HEAD_DIM).astype(F32)
    gn = row1(g_ssd_norm[0])

    x_all = jnp.concatenate([x_prompt.reshape(r_tok, d_model), x_sample.reshape(ns, d_model),
                             jnp.zeros((PAD_FRONT, d_model), F32), meta_tokens.astype(F32)], axis=0)

    x1, hb = _ffn(x_all, row1(g_ffn1[0]), row1(g_mix[0]), w_ffn1_gate[0].astype(BF16),
                  w_ffn1_up[0].astype(BF16), w_ffn1_down[0].astype(BF16), tm)
    wih = widx.astype(BF16)
    wil = (widx - wih.astype(F32)).astype(BF16)
    q, k, v, kb, vb, qi, kiw = _proj_attn(x1, row1(g_mix[0]), wqk, wv, wih, wil, seg, segt, gqk, tm)
    xbc, dtr = _proj_xbc(hb, wx, wdt, tm)
    zs, gates = _proj_gate(hb, wz, wgate, tm)

    ksel_p = min(TOP_K_MAX, l_seq // 4)
    attn = _dsa_prompt(q, qi, kiw, kb, vb, n_batch, seq, meta_blk, ksel_p)
    ssd, ssm_p, conv_p = _ssd_prompt(xbc, dtr, zs, conv_w[0], row1(conv_b[0]), dtb, alog, dsk, gn, e_mat,
                                     n_batch, seq, meta_blk, n_heads)

    n_pages = page_table.shape[1]
    ksel_s = min(TOP_K_MAX, (n_pages * page + 1) // 4)
    smp = lambda a: a[r_tok:r_tok + ns]
    kidx_t = jnp.transpose(cache_kidx[0], (0, 2, 1))
    scores = _idx_sample(page_table, smp(qi).reshape(ns, 1, nqi), smp(kiw).reshape(ns, 1, TILE), kidx_t)
    bias = _select_sample(scores.reshape(n_pages + 1, ns, TILE), ksel_s).reshape(n_pages + 1, ns, 1, TILE)
    ck = jnp.transpose(cache_k[0], (0, 2, 3, 1)).reshape(cache_k.shape[1], nkv, page)
    cv = jnp.transpose(cache_v[0], (0, 2, 3, 1)).reshape(cache_v.shape[1], nkv, page)
    attn_s = _attn_sample(page_table, smp(q).reshape(ns, 1, nq), smp(k).reshape(ns, 1, nkv),
                          smp(v).reshape(ns, 1, nkv), bias, ck, cv)
    y_s, ssm_s, conv_s = _ssd_sample(smp(xbc), smp(dtr), state_conv[0].reshape(ns, -1), state_ssm[0].reshape(ns, d_inner, D_STATE),
                                     conv_w[0], row1(conv_b[0]), dtb, alog, dsk, e_mat, n_heads, 8)
    attn, ssd = _fill_sample(attn, ssd, attn_s.reshape(ns, nq), y_s, zs, gn, r_tok // ns)

    wao, wso, wo = w_attn_out[0].astype(BF16), w_ssd_out[0].astype(BF16), w_o[0].astype(BF16)
    ffn2 = (row1(g_ffn2[0]), row1(g_ffn2[0]), w_ffn2_gate[0].astype(BF16), w_ffn2_up[0].astype(BF16),
            w_ffn2_down[0].astype(BF16))
    x2 = _mix(x1, attn, ssd, gates, wao, wso, wo, tm_tok, (0, r_tok))
    y_prompt = _ffn(x2, *ffn2, tm_tok, with_norm=False)[0].reshape(n_batch, seq, d_model)
    x2_s = _mix(x1, attn, ssd, gates, wao, wso, wo, ns, (r_tok, ns))
    y_sample = _ffn(x2_s, *ffn2, ns, with_norm=False)[0].reshape(ns, 1, d_model)

    meta_rows = lambda a: jnp.broadcast_to(a[r_total - N_META:][None], (n_batch, N_META, a.shape[1]))
    pr = lambda a: jnp.concatenate([meta_rows(a), a[:r_tok].reshape(n_batch, seq, -1)], axis=1)
    k_prompt = pr(k).reshape(1, n_batch, l_seq, N_KV_HEADS, HEAD_DIM)
    v_prompt = pr(v).reshape(1, n_batch, l_seq, N_KV_HEADS, HEAD_DIM)
    kidx_prompt = pr(kiw)[:, :, :IDX_DIM].reshape(1, n_batch, l_seq, IDX_DIM)
    ssm_prompt = ssm_p.reshape(1, n_batch, n_heads, SSD_HEAD_DIM, D_STATE)
    conv_prompt = conv_p.reshape(1, n_batch, CONV_W - 1, cdim)
    k_sample = smp(k).reshape(1, ns, 1, N_KV_HEADS, HEAD_DIM)
    v_sample = smp(v).reshape(1, ns, 1, N_KV_HEADS, HEAD_DIM)
    kidx_sample = smp(kiw)[:, :IDX_DIM].reshape(1, ns, 1, IDX_DIM)
    ssm_sample = ssm_s.reshape(1, ns, n_heads, SSD_HEAD_DIM, D_STATE)
    conv_sample = conv_s.reshape(1, ns, CONV_W - 1, cdim)
    return (y_prompt, y_sample, k_prompt, v_prompt, kidx_prompt, ssm_prompt, conv_prompt,
            k_sample, v_sample, kidx_sample, ssm_sample, conv_sample)
```

```python
import functools

import jax
import jax.numpy as jnp
from jax import lax
from jax.experimental import pallas as pl
from jax.experimental.pallas import tpu as pltpu

F32 = jnp.float32
BF16 = jnp.bfloat16
I32 = jnp.int32
HIGHEST = lax.Precision.HIGHEST

EPS = 1e-6
N_META = 16
N_HEADS = 16
N_KV_HEADS = 4
HEAD_DIM = 64
Q_PER_KV = N_HEADS // N_KV_HEADS
IDX_HEADS = 4
IDX_DIM = 64
TOP_K_MAX = 256
SSD_HEAD_DIM = 64
SSD_GROUPS = 4
D_STATE = 128
CONV_W = 4
TILE = 128
PAD_FRONT = TILE - N_META
INT_MIN = -(2 ** 31)
NEG_BIG = -(2.0 ** 100)
Q_SCALE = HEAD_DIM ** -0.5 * 1.4426950408889634
VMEM_LIMIT = 56 * 1024 * 1024


def _cparams(sem, flags=None):
    return pltpu.CompilerParams(dimension_semantics=sem, vmem_limit_bytes=VMEM_LIMIT, flags=flags)


def _const_spec(shape):
    nd = len(shape)
    return pl.BlockSpec(shape, lambda *_: (0,) * nd, pipeline_mode=pl.Buffered(1))


def _rms(x, g):
    return x * lax.rsqrt(jnp.mean(x * x, axis=-1, keepdims=True) + EPS) * g


def _silu(x):
    return x * jax.nn.sigmoid(x)


def _row_tile(n_tiles, cap):
    for k in range(cap, 0, -1):
        if n_tiles % k == 0:
            return k * TILE
    return TILE


def _ffn_kernel(x_ref, g_ref, g2_ref, wg_ref, wu_ref, wd_ref, o_ref, h2_ref=None, *, ck):
    x = x_ref[...]
    h = _rms(x, g_ref[...]).astype(BF16)
    acc = jnp.zeros(x.shape, F32)
    for c in range(wg_ref.shape[1] // ck):
        sl = slice(c * ck, (c + 1) * ck)
        gate = jnp.dot(h, wg_ref[:, sl], preferred_element_type=F32)
        up = jnp.dot(h, wu_ref[:, sl], preferred_element_type=F32)
        act = (_silu(gate) * up).astype(BF16)
        acc = acc + jnp.dot(act, wd_ref[sl, :], preferred_element_type=F32)
    y = x + 0.5 * acc
    o_ref[...] = y
    if h2_ref is not None:
        h2_ref[...] = _rms(y, g2_ref[...]).astype(BF16)


def _ffn(x, g, g2, wg, wu, wd, tm, rows=None, with_norm=True):
    d = x.shape[1]
    r0, r = (0, x.shape[0]) if rows is None else rows
    dff = wg.shape[1]
    blk0 = r0 // tm
    out_specs = [pl.BlockSpec((tm, d), lambda i: (i, 0))]
    out_shape = [jax.ShapeDtypeStruct((r, d), F32)]
    if with_norm:
        out_specs.append(pl.BlockSpec((tm, d), lambda i: (i, 0)))
        out_shape.append(jax.ShapeDtypeStruct((r, d), BF16))
    return pl.pallas_call(
        functools.partial(_ffn_kernel, ck=256),
        grid=(r // tm,),
        in_specs=[pl.BlockSpec((tm, d), lambda i: (blk0 + i, 0)), _const_spec((1, d)), _const_spec((1, d)),
                  _const_spec((d, dff)), _const_spec((d, dff)), _const_spec((dff, d))],
        out_specs=out_specs,
        out_shape=out_shape,
        compiler_params=_cparams(("arbitrary",)),
        name="ffn",
    )(x, g, g2, wg, wu, wd)


def _proj_attn_kernel(x_ref, g_ref, wqk_ref, wv_ref, wih_ref, wil_ref, seg_ref, segt_ref, gqk_ref,
                      q_ref, k_ref, v_ref, kb_ref, vb_ref, qi_ref, kiw_ref):
    h = _rms(x_ref[...], g_ref[...])
    hb, h_lo = _split_bf16(h)
    qk = jnp.dot(hb, wqk_ref[...], preferred_element_type=F32)
    ss = jnp.dot((qk * qk).astype(BF16), seg_ref[...], preferred_element_type=F32)
    r = lax.rsqrt(ss * (1.0 / HEAD_DIM) + EPS)
    r_hi = r.astype(BF16)
    r_lo = (r - r_hi.astype(F32)).astype(BF16)
    rx = (jnp.dot(r_hi, segt_ref[...], preferred_element_type=F32)
          + jnp.dot(r_lo, segt_ref[...], preferred_element_type=F32))
    qkn = qk * rx * gqk_ref[...]
    nq = q_ref.shape[1]
    q_ref[...] = (qkn[:, :nq] * Q_SCALE).astype(BF16)
    k = qkn[:, nq:]
    k_ref[...] = k
    kb_ref[...] = k.astype(BF16)
    v = jnp.dot(hb, wv_ref[...], preferred_element_type=F32)
    v_ref[...] = v
    vb_ref[...] = v.astype(BF16)
    idx = _dot3(hb, h_lo, wih_ref[...], wil_ref[...])
    nqi = qi_ref.shape[1]
    qi_ref[...] = idx[:, :nqi]
    kiw_ref[...] = idx[:, nqi:]


def _proj_attn(x1, g, wqk, wv, wih, wil, seg, segt, gqk, tm):
    r, d = x1.shape
    nqk, nv = wqk.shape[1], wv.shape[1]
    nq = N_HEADS * HEAD_DIM
    nqi = IDX_HEADS * IDX_DIM
    row = lambda i: (i, 0)
    outs = [(nq, BF16), (nqk - nq, F32), (nv, F32), (nqk - nq, BF16), (nv, BF16), (nqi, F32), (TILE, F32)]
    return pl.pallas_call(
        _proj_attn_kernel,
        grid=(r // tm,),
        in_specs=[pl.BlockSpec((tm, d), row), _const_spec((1, d)), _const_spec(wqk.shape),
                  _const_spec(wv.shape), _const_spec(wih.shape), _const_spec(wil.shape), _const_spec(seg.shape),
                  _const_spec(segt.shape), _const_spec(gqk.shape)],
        out_specs=[pl.BlockSpec((tm, n), row) for n, _ in outs],
        out_shape=[jax.ShapeDtypeStruct((r, n), dt) for n, dt in outs],
        compiler_params=_cparams(("arbitrary",)),
        name="proj_attn",
    )(x1, g, wqk, wv, wih, wil, seg, segt, gqk)


def _proj_xbc_kernel(h_ref, wx_ref, wdt_ref, xbc_ref, dt_ref):
    h = h_ref[...]
    xbc_ref[...] = jnp.dot(h, wx_ref[...], preferred_element_type=F32)
    dt_ref[...] = jnp.dot(h, wdt_ref[...], preferred_element_type=F32)


def _proj_xbc(hb, wx, wdt, tm):
    r, d = hb.shape
    row = lambda i: (i, 0)
    return pl.pallas_call(
        _proj_xbc_kernel,
        grid=(r // tm,),
        in_specs=[pl.BlockSpec((tm, d), row), _const_spec(wx.shape), _const_spec(wdt.shape)],
        out_specs=[pl.BlockSpec((tm, wx.shape[1]), row), pl.BlockSpec((tm, wdt.shape[1]), row)],
        out_shape=[jax.ShapeDtypeStruct((r, wx.shape[1]), F32),
                   jax.ShapeDtypeStruct((r, wdt.shape[1]), F32)],
        compiler_params=_cparams(("arbitrary",)),
        name="proj_xbc",
    )(hb, wx, wdt)


def _proj_gate_kernel(h_ref, wz_ref, wg_ref, zs_ref, gt_ref):
    h = h_ref[...]
    zs_ref[...] = _silu(jnp.dot(h, wz_ref[...], preferred_element_type=F32)).astype(BF16)
    gt_ref[...] = jax.nn.sigmoid(jnp.dot(h, wg_ref[...], preferred_element_type=F32)).astype(BF16)


def _proj_gate(hb, wz, wg, tm):
    r, d = hb.shape
    row = lambda i: (i, 0)
    return pl.pallas_call(
        _proj_gate_kernel,
        grid=(r // tm,),
        in_specs=[pl.BlockSpec((tm, d), row), _const_spec(wz.shape), _const_spec(wg.shape)],
        out_specs=[pl.BlockSpec((tm, wz.shape[1]), row), pl.BlockSpec((tm, wg.shape[1]), row)],
        out_shape=[jax.ShapeDtypeStruct((r, wz.shape[1]), BF16),
                   jax.ShapeDtypeStruct((r, wg.shape[1]), BF16)],
        compiler_params=_cparams(("arbitrary",)),
        name="proj_gate",
    )(hb, wz, wg)


KEY_GROUP = 4


def _key_value(key):
    mag = jnp.where(key < 0, jnp.where(key == INT_MIN, 0x7F800000, -key), key)
    v = pltpu.bitcast(mag, F32)
    return jnp.where(key < 0, -v, v)


def _tree_sum(x):
    while x.shape[0] > 1:
        h = x.shape[0] // 2
        x = x[:h] + x[h:]
    return x[0]


def _count(sc_ref, n_groups, pred):
    def body(gi, acc):
        s = sc_ref[pl.ds(gi * KEY_GROUP, KEY_GROUP)]
        hit = jnp.where(pred(s, gi * (KEY_GROUP * TILE)), 1.0, 0.0)
        return acc + _tree_sum(hit.reshape(KEY_GROUP * TILE // 8, 8, TILE))
    part = lax.fori_loop(0, n_groups, body, jnp.zeros((8, TILE), F32))
    return jnp.sum(part, axis=0, keepdims=True)


def _any(flag):
    return jnp.max(jnp.where(flag, 1, 0)) > 0


def _kth_bitwise(sc_ref, n_groups, ksel):
    n_ge0 = _count(sc_ref, n_groups, lambda s, r0: s >= 0.0)
    t = jnp.where(n_ge0 >= ksel, jnp.zeros((1, TILE), I32), jnp.full((1, TILE), INT_MIN, I32))

    def bit_step(i, t):
        cand = t | jnp.left_shift(jnp.int32(1), 30 - i)
        cv = _key_value(cand)
        n = _count(sc_ref, n_groups, lambda s, r0: s >= cv)
        return jnp.where(n >= ksel, cand, t)

    return lax.fori_loop(0, 31, bit_step, t)


def _tie_search(sc_ref, n_groups, tv, need, n_bits):
    shape = (KEY_GROUP, TILE, TILE)
    rows = (lax.broadcasted_iota(I32, shape, 0) * TILE + lax.broadcasted_iota(I32, shape, 1)).astype(F32)

    def bit_step(i, p):
        cand = p | jnp.left_shift(jnp.int32(1), n_bits - 1 - i)
        cf = cand.astype(F32)
        n = _count(sc_ref, n_groups, lambda s, r0: (s == tv) & ((rows + jnp.asarray(r0, F32)) < cf))
        return jnp.where(n < need, cand, p)
    return lax.fori_loop(0, n_bits, bit_step, jnp.zeros((1, TILE), I32)) + 1


def _select_topk(sc_ref, n_groups, ksel, n_bits):
    no_limit = jnp.full((1, TILE), float(1 << n_bits), F32)
    t = _kth_bitwise(sc_ref, n_groups, ksel)
    tv = _key_value(t)
    n_gt = _count(sc_ref, n_groups, lambda s, r0: s > tv)
    n_eq = _count(sc_ref, n_groups, lambda s, r0: s == tv)
    need = ksel - n_gt
    conflict = (t != INT_MIN) & (n_eq > need)
    lim = lax.cond(_any(conflict), lambda: _tie_search(sc_ref, n_groups, tv, need, n_bits).astype(F32),
                   lambda: no_limit)
    return tv, jnp.where(conflict, lim, no_limit)


def _split_bf16(x):
    hi = x.astype(BF16)
    return hi, (x - hi.astype(F32)).astype(BF16)


def _dot3(a_hi, a_lo, b_hi, b_lo):
    return (jnp.dot(a_hi, b_hi, preferred_element_type=F32)
            + jnp.dot(a_hi, b_lo, preferred_element_type=F32)
            + jnp.dot(a_lo, b_hi, preferred_element_type=F32))


VROWS = HEAD_DIM + 16


def _dsa_prompt_kernel(q_ref, qi_ref, kiwq_ref, kb_ref, vb_ref, kiw_ref, kbm_ref, vbm_ref, kiwm_ref, o_ref,
                       sc_ref, kk_ref, vt_ref, kihl_ref, rhs_ref, qa_ref, qb_ref,
                       m_ref, acc_ref, *, ksel, n_bits):
    b = pl.program_id(0)
    j = pl.program_id(1)
    nt = vb_ref.shape[0] // TILE + 1
    n_groups = lax.shift_right_logical(j + KEY_GROUP, KEY_GROUP.bit_length() - 1)
    n_pairs = lax.shift_right_logical(j + 2, 1)
    nkv = N_KV_HEADS
    gw = Q_PER_KV * TILE
    eye = (lax.broadcasted_iota(I32, (TILE, TILE), 0) == lax.broadcasted_iota(I32, (TILE, TILE), 1))

    @pl.when((b == 0) & (j == 0))
    def _():
        rhs_ref[...] = jnp.zeros(rhs_ref.shape, rhs_ref.dtype)
        ident = jnp.where(eye, 1.0, 0.0).astype(BF16)
        for g in range(nkv):
            for hh in range(Q_PER_KV):
                rhs_ref[g, 0:TILE, hh * TILE:(hh + 1) * TILE] = ident
        qa_ref[...] = jnp.zeros(qa_ref.shape, qa_ref.dtype)
        qb_ref[...] = jnp.zeros(qb_ref.shape, qb_ref.dtype)
        kk_ref[...] = jnp.zeros(kk_ref.shape, kk_ref.dtype)
        kihl_ref[...] = jnp.zeros(kihl_ref.shape, kihl_ref.dtype)
        ones_row = jnp.where(lax.broadcasted_iota(I32, (VROWS - HEAD_DIM, TILE), 0) == 0, 1.0, 0.0).astype(BF16)
        for c in range(vt_ref.shape[0]):
            vt_ref[c] = jnp.zeros(vt_ref.shape[1:], vt_ref.dtype)
            for g in range(nkv):
                vt_ref[c, g * VROWS + HEAD_DIM:(g + 1) * VROWS, :] = ones_row

    @pl.when(j == 0)
    def _():
        def stage(c, kb, vb, kiw):
            kk_ref[c] = kb
            vt = vb.astype(F32).T.astype(BF16)
            for g in range(nkv):
                vt_ref[c, g * VROWS:g * VROWS + HEAD_DIM, :] = vt[g * HEAD_DIM:(g + 1) * HEAD_DIM, :]
            hi, lo = _split_bf16(kiw)
            kihl_ref[c] = jnp.concatenate([hi, lo], axis=1)

        stage(0, kbm_ref[...], vbm_ref[...], kiwm_ref[...])

        def prep(c, _):
            rows = pl.ds(pl.multiple_of((c - 1) * TILE, TILE), TILE)
            stage(c, kb_ref[rows, :], vb_ref[rows, :], kiw_ref[rows, :])
            return 0
        lax.fori_loop(1, nt, prep, 0)

    qt = q_ref[...].astype(F32).T.astype(BF16)
    for h in range(N_HEADS):
        g, hh = divmod(h, Q_PER_KV)
        r0 = TILE + (g % 2) * HEAD_DIM
        rhs_ref[g, r0:r0 + HEAD_DIM, hh * TILE:(hh + 1) * TILE] = qt[h * HEAD_DIM:(h + 1) * HEAD_DIM, :]
    qit = qi_ref[...].T
    for h in range(IDX_HEADS):
        hi, lo = _split_bf16(qit[h * IDX_DIM:(h + 1) * IDX_DIM, :])
        qa_ref[0:IDX_DIM, h * TILE:(h + 1) * TILE] = hi
        qa_ref[TILE:TILE + IDX_DIM, h * TILE:(h + 1) * TILE] = hi
        qb_ref[0:IDX_DIM, h * TILE:(h + 1) * TILE] = lo
    wq = kiwq_ref[...].T[IDX_DIM:IDX_DIM + 8, :]
    wq = wq * (IDX_HEADS ** -0.5 * IDX_DIM ** -0.5)

    grp_rows = KEY_GROUP * TILE
    q_pos = j * TILE + lax.broadcasted_iota(I32, (grp_rows, TILE), 1)
    k_row = lax.broadcasted_iota(I32, (grp_rows, TILE), 0)

    def score_group(gi, _):
        khl = kihl_ref[pl.ds(gi * KEY_GROUP, KEY_GROUP)].reshape(grp_rows, 2 * TILE)
        d = (jnp.dot(khl, qa_ref[...], preferred_element_type=F32)
             + jnp.dot(khl, qb_ref[...], preferred_element_type=F32))
        s = jnp.zeros((grp_rows, TILE), F32)
        for h in range(IDX_HEADS):
            s = s + wq[h:h + 1, :] * jnp.maximum(d[:, h * TILE:(h + 1) * TILE], 0.0)
        k_pos = k_row + gi * grp_rows
        valid = (k_pos <= q_pos) & (k_pos >= PAD_FRONT)
        sc_ref[pl.ds(gi * KEY_GROUP, KEY_GROUP)] = jnp.where(valid, s, jnp.nan).reshape(KEY_GROUP, TILE, TILE)
        return 0
    lax.fori_loop(0, n_groups, score_group, 0)

    tv, lim = _select_topk(sc_ref, n_groups, ksel, n_bits)

    m_ref[...] = jnp.full(m_ref.shape, NEG_BIG, F32)
    acc_ref[...] = jnp.zeros(acc_ref.shape, F32)

    pair_shape = (2, TILE, TILE)
    pair_row = (lax.broadcasted_iota(I32, pair_shape, 0) * TILE
                + lax.broadcasted_iota(I32, pair_shape, 1)).astype(F32)

    def attend_pair(i, _):
        sc = sc_ref[pl.ds(2 * i, 2)]
        sel = (sc > tv) | ((sc == tv) & ((pair_row + (i * (2 * TILE)).astype(F32)) < lim))
        bias = jnp.where(sel, 0.0, NEG_BIG).astype(BF16).reshape(2 * TILE, TILE)
        kc = kk_ref[pl.ds(2 * i, 2)].reshape(2 * TILE, nkv * HEAD_DIM)
        vt = jnp.concatenate([vt_ref[2 * i], vt_ref[2 * i + 1]], axis=1)
        m_all = m_ref[...]
        s = []
        for g in range(nkv):
            half = (g // 2) * TILE
            lhs = jnp.concatenate([bias, kc[:, half:half + TILE]], axis=1)
            s.append(jnp.dot(lhs, rhs_ref[g], preferred_element_type=F32))
        m_new = []
        for g in range(nkv):
            m_g = jnp.maximum(m_all[g:g + 1, :], jnp.max(s[g], axis=0, keepdims=True))
            alpha = jnp.exp2(m_all[g:g + 1, :] - m_g)
            p = jnp.exp2(s[g] - m_g).astype(BF16)
            pv = jnp.dot(vt[g * VROWS:(g + 1) * VROWS, :], p, preferred_element_type=F32)
            acc_ref[g] = alpha * acc_ref[g] + pv
            m_new.append(m_g)
        m_ref[...] = jnp.concatenate(m_new + [m_all[nkv:, :]], axis=0)
        return 0
    lax.fori_loop(0, n_pairs, attend_pair, 0)

    pieces = []
    for g in range(nkv):
        acc = acc_ref[g]
        o = acc[:HEAD_DIM] / acc[HEAD_DIM:HEAD_DIM + 1]
        for hh in range(Q_PER_KV):
            pieces.append(o[:, hh * TILE:(hh + 1) * TILE])
    o_ref[...] = jnp.concatenate(pieces, axis=0).T.astype(BF16)


def _seq_tile_map(seq, meta_blk, per_seq=False):
    n_tok = seq // TILE
    if per_seq:
        return lambda b, j: (jnp.where(j == 0, meta_blk + 1 + b, b * n_tok + j - 1), 0)
    return lambda b, j: (jnp.where(j == 0, meta_blk, b * n_tok + j - 1), 0)


def _dsa_prompt(q, qi, kiw, kb, vb, n_batch, seq, meta_blk, ksel):
    lp = seq + TILE
    nt = lp // TILE
    nt_pad = -(-nt // KEY_GROUP) * KEY_GROUP
    r_total = q.shape[0]
    qw = N_HEADS * HEAD_DIM
    kvw = N_KV_HEADS * HEAD_DIM
    gw = Q_PER_KV * TILE
    tile = _seq_tile_map(seq, meta_blk)
    toks = lambda b, j: (b, 0)
    meta = lambda b, j: (meta_blk, 0)
    n_bits = max(1, (lp - 1).bit_length())
    return pl.pallas_call(
        functools.partial(_dsa_prompt_kernel, ksel=ksel, n_bits=n_bits),
        grid=(n_batch, nt),
        in_specs=[pl.BlockSpec((TILE, qw), tile), pl.BlockSpec((TILE, IDX_HEADS * IDX_DIM), tile),
                  pl.BlockSpec((TILE, TILE), tile), pl.BlockSpec((seq, kvw), toks),
                  pl.BlockSpec((seq, kvw), toks), pl.BlockSpec((seq, TILE), toks),
                  pl.BlockSpec((TILE, kvw), meta), pl.BlockSpec((TILE, kvw), meta),
                  pl.BlockSpec((TILE, TILE), meta)],
        out_specs=pl.BlockSpec((TILE, qw), _seq_tile_map(seq, meta_blk, per_seq=True)),
        out_shape=jax.ShapeDtypeStruct((r_total + n_batch * TILE, qw), BF16),
        scratch_shapes=[pltpu.VMEM((nt_pad, TILE, TILE), F32),
                        pltpu.VMEM((nt_pad, TILE, kvw), BF16),
                        pltpu.VMEM((nt_pad, N_KV_HEADS * VROWS, TILE), BF16),
                        pltpu.VMEM((nt_pad, TILE, 2 * TILE), BF16),
                        pltpu.VMEM((N_KV_HEADS, 2 * TILE, gw), BF16),
                        pltpu.VMEM((2 * TILE, IDX_HEADS * TILE), BF16),
                        pltpu.VMEM((2 * TILE, IDX_HEADS * TILE), BF16),
                        pltpu.VMEM((8, gw), F32),
                        pltpu.VMEM((N_KV_HEADS, VROWS, gw), F32)],
        compiler_params=_cparams(("arbitrary", "arbitrary")),
        name="dsa_prompt",
    )(q, qi, kiw, kb, vb, kiw, kb, vb, kiw)


def _expand_heads(v, e_ref):
    return jnp.dot(v, e_ref[...], preferred_element_type=F32, precision=HIGHEST)


def _softplus(x):
    return jnp.maximum(x, 0.0) + jnp.log1p(jnp.exp(-jnp.abs(x)))


def _ssd_prompt_kernel(xbc_ref, dtr_ref, zs_ref, cw_ref, cb_ref, dtb_ref, alog_ref, dsk_ref, gn_ref,
                       e_ref, y_ref, ssm_ref, conv_ref, xp_ref, ht_ref, *, n_heads):
    c = pl.program_id(1)
    nc = pl.num_programs(1)
    d_inner = n_heads * SSD_HEAD_DIM
    hpg = n_heads // SSD_GROUPS
    gwid = hpg * SSD_HEAD_DIM
    gn_w = d_inner // SSD_GROUPS

    @pl.when(c == 0)
    def _():
        xp_ref[0:8, :] = jnp.zeros((8, xp_ref.shape[1]), F32)
        ht_ref[...] = jnp.zeros(ht_ref.shape, F32)

    xp_ref[8:8 + TILE, :] = xbc_ref[...]
    u = cb_ref[...] + cw_ref[CONV_W - 1:CONV_W, :] * xp_ref[8:8 + TILE, :]
    for k in range(CONV_W - 1):
        off = 8 - (CONV_W - 1) + k
        u = u + cw_ref[k:k + 1, :] * xp_ref[off:off + TILE, :]
    tail = xp_ref[8 + TILE - (CONV_W - 1):8 + TILE, :]
    xp_ref[8 - (CONV_W - 1):8, :] = tail

    row = lax.broadcasted_iota(I32, (TILE, 1), 0)
    live = jnp.where((c > 0) | (row >= PAD_FRONT), 1.0, 0.0)
    act = _silu(u) * live
    xs = act[:, :d_inner]
    bm = act[:, d_inner:d_inner + SSD_GROUPS * D_STATE]
    cm = act[:, d_inner + SSD_GROUPS * D_STATE:]

    dt = _softplus(dtr_ref[...] + dtb_ref[...]) * live
    a = dt * (-jnp.exp(alog_ref[...]))
    ri = lax.broadcasted_iota(I32, (TILE, TILE), 0)
    ci = lax.broadcasted_iota(I32, (TILE, TILE), 1)
    tril = ri >= ci
    acs = jnp.dot(jnp.where(tril, 1.0, 0.0), a, preferred_element_type=F32, precision=HIGHEST)
    acs_t = acs.T
    dt_t = dt.T
    acs_last = acs[TILE - 1:TILE, :]
    w_in = _expand_heads(jnp.exp(acs_last - acs) * dt, e_ref)
    w_out = _expand_heads(jnp.exp(acs), e_ref)
    xw = (xs * w_in).astype(BF16)
    xsb = xs.astype(BF16)
    lane = lax.broadcasted_iota(I32, (1, gwid), 1)

    ys = []
    for g in range(SSD_GROUPS):
        bg = bm[:, g * D_STATE:(g + 1) * D_STATE]
        cg = cm[:, g * D_STATE:(g + 1) * D_STATE].astype(BF16)
        bgt = bg.T.astype(BF16)
        cb = jnp.dot(cg, bgt, preferred_element_type=F32)
        xw_g = xw[:, g * gwid:(g + 1) * gwid]
        xs_g = xsb[:, g * gwid:(g + 1) * gwid]
        h_prev = ht_ref[g]
        y_off = jnp.dot(cg, h_prev.astype(BF16), preferred_element_type=F32)
        y_g = y_off * w_out[:, g * gwid:(g + 1) * gwid]
        for e0 in range(0, hpg, 2):
            ms, xm = [], []
            for e in (e0, e0 + 1):
                he = g * hpg + e
                seg = acs[:, he:he + 1] - acs_t[he:he + 1, :]
                ms.append((cb * jnp.exp(jnp.where(tril, seg, -jnp.inf)) * dt_t[he:he + 1, :]).astype(BF16))
                only = (lane >= e * SSD_HEAD_DIM) & (lane < (e + 1) * SSD_HEAD_DIM)
                xm.append(jnp.where(only, xs_g, jnp.zeros_like(xs_g)))
            y_g = y_g + jnp.dot(jnp.concatenate(ms, axis=1), jnp.concatenate(xm, axis=0),
                                preferred_element_type=F32)
        states = jnp.dot(bgt, xw_g, preferred_element_type=F32)
        ht_ref[g] = h_prev * w_out[TILE - 1:TILE, g * gwid:(g + 1) * gwid] + states
        ys.append(y_g)
    y = jnp.concatenate(ys, axis=1) + dsk_ref[...] * xs
    y = y * zs_ref[...].astype(F32)
    outs = []
    for g in range(SSD_GROUPS):
        yg = y[:, g * gn_w:(g + 1) * gn_w]
        outs.append(yg * lax.rsqrt(jnp.mean(yg * yg, axis=-1, keepdims=True) + EPS))
    y_ref[...] = (jnp.concatenate(outs, axis=1) * gn_ref[...]).astype(BF16)

    @pl.when(c == nc - 1)
    def _():
        for g in range(SSD_GROUPS):
            ssm_ref[0, g * gwid:(g + 1) * gwid, :] = ht_ref[g].T
        conv_ref[0] = tail


def _ssd_prompt(xbc, dtr, zs, cw, cb, dtb, alog, dsk, gn, e_mat, n_batch, seq, meta_blk, n_heads):
    nt = seq // TILE + 1
    r_total = xbc.shape[0]
    d_inner = n_heads * SSD_HEAD_DIM
    cdim = xbc.shape[1]
    tile = _seq_tile_map(seq, meta_blk)
    fix = lambda b, c: (0, 0)
    return pl.pallas_call(
        functools.partial(_ssd_prompt_kernel, n_heads=n_heads),
        grid=(n_batch, nt),
        in_specs=[pl.BlockSpec((TILE, cdim), tile), pl.BlockSpec((TILE, TILE), tile),
                  pl.BlockSpec((TILE, d_inner), tile),
                  pl.BlockSpec(cw.shape, fix), pl.BlockSpec(cb.shape, fix), pl.BlockSpec(dtb.shape, fix),
                  pl.BlockSpec(alog.shape, fix), pl.BlockSpec(dsk.shape, fix), pl.BlockSpec(gn.shape, fix),
                  pl.BlockSpec(e_mat.shape, fix)],
        out_specs=[pl.BlockSpec((TILE, d_inner), _seq_tile_map(seq, meta_blk, per_seq=True)),
                   pl.BlockSpec((1, d_inner, D_STATE), lambda b, c: (b, 0, 0)),
                   pl.BlockSpec((1, CONV_W - 1, cdim), lambda b, c: (b, 0, 0))],
        out_shape=[jax.ShapeDtypeStruct((r_total + n_batch * TILE, d_inner), BF16),
                   jax.ShapeDtypeStruct((n_batch, d_inner, D_STATE), F32),
                   jax.ShapeDtypeStruct((n_batch, CONV_W - 1, cdim), F32)],
        scratch_shapes=[pltpu.VMEM((8 + TILE, cdim), F32),
                        pltpu.VMEM((SSD_GROUPS, D_STATE, d_inner // SSD_GROUPS), F32)],
        compiler_params=_cparams(("arbitrary", "arbitrary")),
        name="ssd_prompt",
    )(xbc, dtr, zs, cw, cb, dtb, alog, dsk, gn, e_mat)


def _idx_sample_kernel(pt_ref, qi_ref, kiw_ref, *rest, n_pages):
    pages, s_ref = rest[:n_pages], rest[n_pages]
    qi = qi_ref[0]
    kiw = kiw_ref[0]
    q8 = jnp.concatenate([qi[:, h * IDX_DIM:(h + 1) * IDX_DIM] for h in range(IDX_HEADS)]
                         + [jnp.zeros((8 - IDX_HEADS, IDX_DIM), F32)], axis=0)
    w_col = jnp.broadcast_to(kiw, (TILE, TILE)).T[IDX_DIM:IDX_DIM + 8, 0:1]
    w_col = w_col * (IDX_HEADS ** -0.5 * IDX_DIM ** -0.5)
    q_hi, q_lo = _split_bf16(q8)
    q3 = jnp.concatenate([q_hi, q_lo, q_hi], axis=1)
    for p in range(n_pages):
        k_hi, k_lo = _split_bf16(pages[p][0])
        d = jnp.dot(q3, jnp.concatenate([k_hi, k_hi, k_lo], axis=0), preferred_element_type=F32)
        s_ref[p, 0] = jnp.sum(w_col * jnp.maximum(d, 0.0), axis=0, keepdims=True)
    d_new = jnp.sum(q8 * kiw[:, :IDX_DIM], axis=1, keepdims=True)
    s_new = jnp.sum(w_col * jnp.maximum(d_new, 0.0), axis=0, keepdims=True)
    lane = lax.broadcasted_iota(I32, (1, TILE), 1)
    s_ref[n_pages, 0] = jnp.where(lane == 0, s_new, 0.0)


def _idx_sample(page_table, qi_s, kiw_s, cache_kidx):
    ns, n_pages = page_table.shape
    n_slots = n_pages + 1
    page = lambda p: pl.BlockSpec((1, IDX_DIM, TILE), lambda s, pt: (pt[s, p], 0, 0))
    one = lambda s, pt: (s, 0, 0)
    grid_spec = pltpu.PrefetchScalarGridSpec(
        num_scalar_prefetch=1, grid=(ns,),
        in_specs=[pl.BlockSpec((1, 1, qi_s.shape[2]), one), pl.BlockSpec((1, 1, TILE), one)]
        + [page(p) for p in range(n_pages)],
        out_specs=pl.BlockSpec((n_slots, 1, 1, TILE), lambda s, pt: (0, s, 0, 0)))
    return pl.pallas_call(
        functools.partial(_idx_sample_kernel, n_pages=n_pages),
        grid_spec=grid_spec,
        out_shape=jax.ShapeDtypeStruct((n_slots, ns, 1, TILE), F32),
        compiler_params=_cparams(("arbitrary",)),
        name="idx_sample",
    )(page_table, qi_s, kiw_s, *([cache_kidx] * n_pages))


def _select_sample_kernel(s_ref, bias_ref, sc_ref, *, n_pages, ksel, n_bits):
    n_slots = n_pages + 1
    rows = lax.broadcasted_iota(I32, (TILE, TILE), 0)
    for p in range(n_slots):
        sc = s_ref[p].T
        if p == n_pages:
            sc = jnp.where(rows == 0, sc, jnp.nan)
        sc_ref[p] = sc
    for p in range(n_slots, sc_ref.shape[0]):
        sc_ref[p] = jnp.full((TILE, TILE), jnp.nan, F32)
    n_groups = sc_ref.shape[0] // KEY_GROUP
    tv, lim = _select_topk(sc_ref, n_groups, ksel, n_bits)
    rows_f = rows.astype(F32)
    for p in range(n_slots):
        sc = sc_ref[p]
        sel = (sc > tv) | ((sc == tv) & ((rows_f + float(p * TILE)) < lim))
        bias_ref[p] = jnp.where(sel, 0.0, NEG_BIG).T


def _select_sample(scores, ksel):
    n_slots, ns, _ = scores.shape
    n_pages = n_slots - 1
    n_bits = (n_slots * TILE - 1).bit_length()
    return pl.pallas_call(
        functools.partial(_select_sample_kernel, n_pages=n_pages, ksel=ksel, n_bits=n_bits),
        out_shape=jax.ShapeDtypeStruct((n_slots, ns, TILE), F32),
        scratch_shapes=[pltpu.VMEM((-(-n_slots // KEY_GROUP) * KEY_GROUP, TILE, TILE), F32)],
        compiler_params=pltpu.CompilerParams(vmem_limit_bytes=VMEM_LIMIT),
        name="select_sample",
    )(scores)


def _attn_sample_kernel(pt_ref, q_ref, kn_ref, vn_ref, bias_ref, *rest, n_pages):
    kpages, vpages, o_ref = rest[:n_pages], rest[n_pages:2 * n_pages], rest[2 * n_pages]
    kvw = N_KV_HEADS * HEAD_DIM
    q = q_ref[0].astype(F32)
    lane = lax.broadcasted_iota(I32, (N_HEADS, kvw), 1)
    head = lax.broadcasted_iota(I32, (N_HEADS, kvw), 0)
    own = jnp.right_shift(lane, 6) == jnp.right_shift(head, 2)
    qrows = []
    for h in range(N_HEADS):
        qh = q[:, h * HEAD_DIM:(h + 1) * HEAD_DIM]
        qrows.append(jnp.concatenate([qh] * N_KV_HEADS, axis=1))
    qf = jnp.where(own, jnp.concatenate(qrows, axis=0), 0.0)
    qw = qf.astype(BF16)
    nt = (((1,), (1,)), ((), ()))
    scores = []
    for p in range(n_pages):
        kp = kpages[p][0].astype(BF16)
        s = jnp.dot(qw, kp, preferred_element_type=F32)
        scores.append(s + bias_ref[p, 0])
    s_new = jnp.sum(qf * kn_ref[0], axis=1, keepdims=True)
    lane1 = lax.broadcasted_iota(I32, (1, TILE), 1)
    scores.append(jnp.where(lane1 == 0, s_new, NEG_BIG) + bias_ref[n_pages, 0])
    m = scores[0].max(axis=1, keepdims=True)
    for s in scores[1:]:
        m = jnp.maximum(m, s.max(axis=1, keepdims=True))
    den = jnp.zeros((N_HEADS, 1), F32)
    out = jnp.zeros((N_HEADS, kvw), F32)
    for p in range(n_pages):
        e = jnp.exp2(scores[p] - m)
        den = den + e.sum(axis=1, keepdims=True)
        out = out + lax.dot_general(e.astype(BF16), vpages[p][0].astype(BF16), nt,
                                    preferred_element_type=F32)
    e = jnp.exp2(scores[n_pages] - m)
    den = den + e.sum(axis=1, keepdims=True)
    out = out + e[:, 0:1] * vn_ref[0]
    out = out / den
    rolled = jnp.concatenate([pltpu.roll(out[:, i * TILE:(i + 1) * TILE], HEAD_DIM, 1)
                              for i in range(kvw // TILE)], axis=1)
    half = lax.broadcasted_iota(I32, (1, TILE), 1) < HEAD_DIM
    pieces = []
    for m2 in range(N_HEADS // 2):
        g = (2 * m2) // Q_PER_KV
        blk = (g // 2) * TILE
        a_src, b_src = (rolled, out) if g % 2 == 1 else (out, rolled)
        a = a_src[2 * m2:2 * m2 + 1, blk:blk + TILE]
        b = b_src[2 * m2 + 1:2 * m2 + 2, blk:blk + TILE]
        pieces.append(jnp.where(half, a, b))
    o_ref[0] = jnp.concatenate(pieces, axis=1)


def _attn_sample(page_table, q_s, k_s, v_s, bias, cache_k, cache_v):
    ns, n_pages = page_table.shape
    kvw = N_KV_HEADS * HEAD_DIM
    qw = N_HEADS * HEAD_DIM
    page = lambda p: pl.BlockSpec((1, kvw, TILE), lambda s, pt: (pt[s, p], 0, 0))
    one = lambda s, pt: (s, 0, 0)
    grid_spec = pltpu.PrefetchScalarGridSpec(
        num_scalar_prefetch=1, grid=(ns,),
        in_specs=[pl.BlockSpec((1, 1, qw), one), pl.BlockSpec((1, 1, kvw), one),
                  pl.BlockSpec((1, 1, kvw), one),
                  pl.BlockSpec((n_pages + 1, 1, 1, TILE), lambda s, pt: (0, s, 0, 0))]
        + [page(p) for p in range(n_pages)] * 2,
        out_specs=pl.BlockSpec((1, 1, qw), one))
    return pl.pallas_call(
        functools.partial(_attn_sample_kernel, n_pages=n_pages),
        grid_spec=grid_spec,
        out_shape=jax.ShapeDtypeStruct((ns, 1, qw), F32),
        compiler_params=_cparams(("arbitrary",)),
        name="attn_sample",
    )(page_table, q_s, k_s, v_s, bias, *([cache_k] * n_pages), *([cache_v] * n_pages))


def _ssd_sample_kernel(xbc_ref, dtr_ref, sc_ref, h0_ref, cw_ref, cb_ref, dtb_ref, alog_ref, dsk_ref,
                       e_ref, y_ref, h_ref, conv_ref, *, n_heads, sb):
    d_inner = n_heads * SSD_HEAD_DIM
    hpg = n_heads // SSD_GROUPS
    new = xbc_ref[...]
    cdim = new.shape[1]
    u = cb_ref[...] + cw_ref[CONV_W - 1:CONV_W, :] * new
    for k in range(CONV_W - 1):
        u = u + cw_ref[k:k + 1, :] * sc_ref[:, k * cdim:(k + 1) * cdim]
    for k in range(CONV_W - 2):
        conv_ref[:, k * cdim:(k + 1) * cdim] = sc_ref[:, (k + 1) * cdim:(k + 2) * cdim]
    conv_ref[:, (CONV_W - 2) * cdim:] = new
    act = _silu(u)
    xs = act[:, :d_inner]
    bm = act[:, d_inner:d_inner + SSD_GROUPS * D_STATE]
    cm = act[:, d_inner + SSD_GROUPS * D_STATE:]
    dt = _softplus(dtr_ref[...] + dtb_ref[...])
    decay = jnp.exp(dt * (-jnp.exp(alog_ref[...])))
    dec_x = _expand_heads(decay, e_ref)
    xdt = xs * _expand_heads(dt, e_ref)
    for i in range(sb):
        bcols, ccols = [], []
        for g in range(SSD_GROUPS):
            brow = bm[i:i + 1, g * D_STATE:(g + 1) * D_STATE]
            crow = cm[i:i + 1, g * D_STATE:(g + 1) * D_STATE]
            bcols.append(jnp.broadcast_to(brow, (TILE, D_STATE)).T)
            ccols.append(jnp.broadcast_to(crow, (TILE, D_STATE)).T)
        pieces = []
        for pr in range(n_heads // 2):
            g = (2 * pr) // hpg
            sl = slice(pr * TILE, (pr + 1) * TILE)
            ht = h0_ref[i, sl, :].T
            hn = ht * dec_x[i:i + 1, sl] + bcols[g] * xdt[i:i + 1, sl]
            pieces.append(jnp.sum(hn * ccols[g], axis=0, keepdims=True))
            h_ref[i, sl, :] = hn.T
        y_ref[i:i + 1, :] = jnp.concatenate(pieces, axis=1) + dsk_ref[...] * xs[i:i + 1, :]


def _ssd_sample(xbc_s, dtr_s, state_conv, state_ssm, cw, cb, dtb, alog, dsk, e_mat, n_heads, sb):
    ns, cdim = xbc_s.shape
    d_inner = n_heads * SSD_HEAD_DIM
    row = lambda i: (i, 0)
    row3 = lambda i: (i, 0, 0)
    fix = lambda i: (0, 0)
    return pl.pallas_call(
        functools.partial(_ssd_sample_kernel, n_heads=n_heads, sb=sb),
        grid=(ns // sb,),
        in_specs=[pl.BlockSpec((sb, cdim), row), pl.BlockSpec((sb, TILE), row),
                  pl.BlockSpec((sb, (CONV_W - 1) * cdim), row), pl.BlockSpec((sb, d_inner, D_STATE), row3),
                  pl.BlockSpec(cw.shape, fix), pl.BlockSpec(cb.shape, fix), pl.BlockSpec(dtb.shape, fix),
                  pl.BlockSpec(alog.shape, fix), pl.BlockSpec(dsk.shape, fix), pl.BlockSpec(e_mat.shape, fix)],
        out_specs=[pl.BlockSpec((sb, d_inner), row), pl.BlockSpec((sb, d_inner, D_STATE), row3),
                   pl.BlockSpec((sb, (CONV_W - 1) * cdim), row)],
        out_shape=[jax.ShapeDtypeStruct((ns, d_inner), F32),
                   jax.ShapeDtypeStruct((ns, d_inner, D_STATE), F32),
                   jax.ShapeDtypeStruct((ns, (CONV_W - 1) * cdim), F32)],
        compiler_params=_cparams(("arbitrary",)),
        name="ssd_sample",
    )(xbc_s, dtr_s, state_conv, state_ssm, cw, cb, dtb, alog, dsk, e_mat)


def _fill_sample_kernel(attn_any, ssd_any, a_ref, y_ref, zs_ref, gn_ref, attn_ref, ssd_ref):
    del attn_any, ssd_any
    attn_ref[...] = a_ref[...].astype(BF16)
    y = y_ref[...] * zs_ref[...].astype(F32)
    gn_w = y.shape[1] // SSD_GROUPS
    outs = []
    for g in range(SSD_GROUPS):
        yg = y[:, g * gn_w:(g + 1) * gn_w]
        outs.append(yg * lax.rsqrt(jnp.mean(yg * yg, axis=-1, keepdims=True) + EPS))
    ssd_ref[...] = (jnp.concatenate(outs, axis=1) * gn_ref[...]).astype(BF16)


def _fill_sample(attn, ssd, attn_s, y_s, zs, gn, blk):
    ns = attn_s.shape[0]
    last = lambda i: (blk, 0)
    fix = lambda i: (0, 0)
    return pl.pallas_call(
        _fill_sample_kernel,
        grid=(1,),
        in_specs=[pl.BlockSpec(memory_space=pl.ANY), pl.BlockSpec(memory_space=pl.ANY),
                  pl.BlockSpec(attn_s.shape, fix), pl.BlockSpec(y_s.shape, fix),
                  pl.BlockSpec((ns, zs.shape[1]), last), pl.BlockSpec(gn.shape, fix)],
        out_specs=[pl.BlockSpec((ns, attn.shape[1]), last), pl.BlockSpec((ns, ssd.shape[1]), last)],
        out_shape=[jax.ShapeDtypeStruct(attn.shape, attn.dtype), jax.ShapeDtypeStruct(ssd.shape, ssd.dtype)],
        input_output_aliases={0: 0, 1: 1},
        compiler_params=_cparams(("arbitrary",)),
        name="fill_sample",
    )(attn, ssd, attn_s, y_s, zs, gn)


def _mix_kernel(x_ref, a_ref, s_ref, gt_ref, wao_ref, wso_ref, wo_ref, o_ref):
    d = x_ref.shape[1]
    gt = gt_ref[...].astype(F32)
    ao = jnp.dot(a_ref[...], wao_ref[...], preferred_element_type=F32)
    so = jnp.dot(s_ref[...], wso_ref[...], preferred_element_type=F32)
    mixed = (gt[:, :d] * ao + gt[:, d:] * so).astype(BF16)
    o_ref[...] = x_ref[...] + jnp.dot(mixed, wo_ref[...], preferred_element_type=F32)


def _mix(x1, attn, ssd, gates, wao, wso, wo, tm, rows):
    d = x1.shape[1]
    r0, r = rows
    blk0 = r0 // tm
    row = lambda i: (blk0 + i, 0)
    return pl.pallas_call(
        _mix_kernel,
        grid=(r // tm,),
        in_specs=[pl.BlockSpec((tm, d), row), pl.BlockSpec((tm, attn.shape[1]), row),
                  pl.BlockSpec((tm, ssd.shape[1]), row), pl.BlockSpec((tm, gates.shape[1]), row),
                  _const_spec(wao.shape), _const_spec(wso.shape), _const_spec(wo.shape)],
        out_specs=pl.BlockSpec((tm, d), lambda i: (i, 0)),
        out_shape=jax.ShapeDtypeStruct((r, d), F32),
        compiler_params=_cparams(("arbitrary",)),
        name="mix",
    )(x1, attn, ssd, gates, wao, wso, wo)


def _prompt_cache_kernel(k_ref, v_ref, kiw_ref, km_ref, vm_ref, kiwm_ref, kt_ref, vt_ref, kit_ref):
    n_tiles = k_ref.shape[0] // TILE
    for src, meta, dst, width in ((k_ref, km_ref, kt_ref, k_ref.shape[1]), (v_ref, vm_ref, vt_ref, v_ref.shape[1]),
                                  (kiw_ref, kiwm_ref, kit_ref, IDX_DIM)):
        dst[0, :, 0:N_META] = meta[...].T[:width, PAD_FRONT:]
        for j in range(n_tiles):
            dst[0, :, N_META + j * TILE:N_META + (j + 1) * TILE] = src[j * TILE:(j + 1) * TILE, :].T[:width, :]


def _prompt_cache(k, v, kiw, n_batch, seq, meta_blk):
    l_seq = seq + N_META
    kvw = k.shape[1]
    toks = lambda b: (b, 0)
    meta = lambda b: (meta_blk, 0)
    out = lambda b: (b, 0, 0)
    return pl.pallas_call(
        _prompt_cache_kernel,
        grid=(n_batch,),
        in_specs=[pl.BlockSpec((seq, kvw), toks), pl.BlockSpec((seq, kvw), toks), pl.BlockSpec((seq, TILE), toks),
                  pl.BlockSpec((TILE, kvw), meta), pl.BlockSpec((TILE, kvw), meta), pl.BlockSpec((TILE, TILE), meta)],
        out_specs=[pl.BlockSpec((1, kvw, l_seq), out), pl.BlockSpec((1, kvw, l_seq), out),
                   pl.BlockSpec((1, IDX_DIM, l_seq), out)],
        out_shape=[jax.ShapeDtypeStruct((n_batch, kvw, l_seq), F32), jax.ShapeDtypeStruct((n_batch, kvw, l_seq), F32),
                   jax.ShapeDtypeStruct((n_batch, IDX_DIM, l_seq), F32)],
        compiler_params=_cparams(("arbitrary",)),
        name="prompt_cache",
    )(k, v, kiw, k, v, kiw)


def kernel(x_prompt, x_sample, cache_k, cache_v, cache_kidx, state_ssm, state_conv, page_table, meta_tokens,
           g_ffn1, w_ffn1_gate, w_ffn1_up, w_ffn1_down, g_mix, w_in, g_q, g_k, conv_w, conv_b, dt_bias, a_log,
           d_skip, g_ssd_norm, w_attn_out, w_ssd_out, w_o, g_ffn2, w_ffn2_gate, w_ffn2_up, w_ffn2_down):
    depth = w_in.shape[0]
    assert depth == 1, "single-layer step"
    n_batch, seq, d_model = x_prompt.shape
    ns = x_sample.shape[0]
    assert x_sample.shape[1] == 1 and seq % TILE == 0 and ns == TILE
    page = cache_k.shape[2]
    assert page == TILE
    l_seq = seq + N_META
    r_tok = n_batch * seq
    r_total = r_tok + ns + TILE
    meta_blk = (r_tok + ns) // TILE
    n_heads = state_ssm.shape[2]
    d_inner = n_heads * SSD_HEAD_DIM
    assert n_heads <= TILE and n_heads % (2 * SSD_GROUPS) == 0
    cdim = d_inner + 2 * SSD_GROUPS * D_STATE
    nq, nkv = N_HEADS * HEAD_DIM, N_KV_HEADS * HEAD_DIM
    nqi = IDX_HEADS * IDX_DIM
    tm = _row_tile(r_total // TILE, 6)
    tm_tok = _row_tile(r_tok // TILE, 8)

    w = w_in[0]
    offs = [0]
    for n in (nq, nkv, nkv, nqi, IDX_DIM, IDX_HEADS, d_inner, cdim, n_heads, d_model, d_model):
        offs.append(offs[-1] + n)
    col = lambda i: w[:, offs[i]:offs[i + 1]]
    wqk = jnp.concatenate([col(0), col(1)], axis=1).astype(BF16)
    wv = col(2).astype(BF16)
    widx = jnp.concatenate([col(3), col(4), col(5), jnp.zeros((d_model, TILE - IDX_DIM - IDX_HEADS), F32)], axis=1)
    wz = col(6).astype(BF16)
    wx = col(7).astype(BF16)
    wdt = jnp.concatenate([col(8), jnp.zeros((d_model, TILE - n_heads), F32)], axis=1).astype(BF16)
    wgate = jnp.concatenate([col(9), col(10)], axis=1).astype(BF16)
    n_seg = (nq + nkv) // HEAD_DIM
    seg = (jnp.arange(nq + nkv)[:, None] // HEAD_DIM == jnp.arange(TILE)[None, :]).astype(BF16)
    segt = seg.T
    del n_seg
    gqk = jnp.concatenate([jnp.tile(g_q[0], N_HEADS), jnp.tile(g_k[0], N_KV_HEADS)])[None, :]
    row1 = lambda v: v.reshape(1, -1)
    pad_heads = lambda v, fill: jnp.concatenate([v, jnp.full((TILE - n_heads,), fill, F32)])[None, :]
    dtb = pad_heads(dt_bias[0], -1e4)
    alog = pad_heads(a_log[0], 0.0)
    dsk = jnp.repeat(d_skip[0], SSD_HEAD_DIM)[None, :]
    e_mat = (jnp.arange(TILE)[:, None] == jnp.arange(d_inner)[None, :] // SSD_HEAD_DIM).astype(F32)
    gn = row1(g_ssd_norm[0])

    x_all = jnp.concatenate([x_prompt.reshape(r_tok, d_model), x_sample.reshape(ns, d_model),
                             jnp.zeros((PAD_FRONT, d_model), F32), meta_tokens.astype(F32)], axis=0)

    x1, hb = _ffn(x_all, row1(g_ffn1[0]), row1(g_mix[0]), w_ffn1_gate[0].astype(BF16),
                  w_ffn1_up[0].astype(BF16), w_ffn1_down[0].astype(BF16), tm)
    wih = widx.astype(BF16)
    wil = (widx - wih.astype(F32)).astype(BF16)
    q, k, v, kb, vb, qi, kiw = _proj_attn(x1, row1(g_mix[0]), wqk, wv, wih, wil, seg, segt, gqk, tm)
    xbc, dtr = _proj_xbc(hb, wx, wdt, tm)
    zs, gates = _proj_gate(hb, wz, wgate, tm)

    ksel_p = min(TOP_K_MAX, l_seq // 4)
    attn = _dsa_prompt(q, qi, kiw, kb, vb, n_batch, seq, meta_blk, ksel_p)
    ssd, ssm_p, conv_p = _ssd_prompt(xbc, dtr, zs, conv_w[0], row1(conv_b[0]), dtb, alog, dsk, gn, e_mat,
                                     n_batch, seq, meta_blk, n_heads)

    n_pages = page_table.shape[1]
    ksel_s = min(TOP_K_MAX, (n_pages * page + 1) // 4)
    smp = lambda a: a[r_tok:r_tok + ns]
    kidx_t = jnp.transpose(cache_kidx[0], (0, 2, 1))
    scores = _idx_sample(page_table, smp(qi).reshape(ns, 1, nqi), smp(kiw).reshape(ns, 1, TILE), kidx_t)
    bias = _select_sample(scores.reshape(n_pages + 1, ns, TILE), ksel_s).reshape(n_pages + 1, ns, 1, TILE)
    ck = jnp.transpose(cache_k[0], (0, 2, 3, 1)).reshape(cache_k.shape[1], nkv, page)
    cv = jnp.transpose(cache_v[0], (0, 2, 3, 1)).reshape(cache_v.shape[1], nkv, page)
    attn_s = _attn_sample(page_table, smp(q).reshape(ns, 1, nq), smp(k).reshape(ns, 1, nkv),
                          smp(v).reshape(ns, 1, nkv), bias, ck, cv)
    y_s, ssm_s, conv_s = _ssd_sample(smp(xbc), smp(dtr), state_conv[0].reshape(ns, -1), state_ssm[0].reshape(ns, d_inner, D_STATE),
                                     conv_w[0], row1(conv_b[0]), dtb, alog, dsk, e_mat, n_heads, 8)
    attn, ssd = _fill_sample(attn, ssd, attn_s.reshape(ns, nq), y_s, zs, gn, r_tok // ns)

    wao, wso, wo = w_attn_out[0].astype(BF16), w_ssd_out[0].astype(BF16), w_o[0].astype(BF16)
    ffn2 = (row1(g_ffn2[0]), row1(g_ffn2[0]), w_ffn2_gate[0].astype(BF16), w_ffn2_up[0].astype(BF16),
            w_ffn2_down[0].astype(BF16))
    x2 = _mix(x1, attn, ssd, gates, wao, wso, wo, tm_tok, (0, r_tok))
    y_prompt = _ffn(x2, *ffn2, tm_tok, with_norm=False)[0].reshape(n_batch, seq, d_model)
    x2_s = _mix(x1, attn, ssd, gates, wao, wso, wo, ns, (r_tok, ns))
    y_sample = _ffn(x2_s, *ffn2, ns, with_norm=False)[0].reshape(ns, 1, d_model)

    kt, vt, kit = _prompt_cache(k, v, kiw, n_batch, seq, meta_blk)
    heads_last = lambda a: jnp.transpose(a.reshape(1, n_batch, N_KV_HEADS, HEAD_DIM, l_seq), (0, 1, 4, 2, 3))
    k_prompt = heads_last(kt)
    v_prompt = heads_last(vt)
    kidx_prompt = jnp.transpose(kit, (0, 2, 1))[None]
    ssm_prompt = ssm_p.reshape(1, n_batch, n_heads, SSD_HEAD_DIM, D_STATE)
    conv_prompt = conv_p.reshape(1, n_batch, CONV_W - 1, cdim)
    k_sample = smp(k).reshape(1, ns, 1, N_KV_HEADS, HEAD_DIM)
    v_sample = smp(v).reshape(1, ns, 1, N_KV_HEADS, HEAD_DIM)
    kidx_sample = smp(kiw)[:, :IDX_DIM].reshape(1, ns, 1, IDX_DIM)
    ssm_sample = ssm_s.reshape(1, ns, n_heads, SSD_HEAD_DIM, D_STATE)
    conv_sample = conv_s.reshape(1, ns, CONV_W - 1, cdim)
    return (y_prompt, y_sample, k_prompt, v_prompt, kidx_prompt, ssm_prompt, conv_prompt,
            k_sample, v_sample, kidx_sample, ssm_sample, conv_sample)
```

```python
import functools

import jax
import jax.numpy as jnp
from jax import lax
from jax.experimental import pallas as pl
from jax.experimental.pallas import tpu as pltpu

F32 = jnp.float32
BF16 = jnp.bfloat16
I32 = jnp.int32
HIGHEST = lax.Precision.HIGHEST

EPS = 1e-6
N_META = 16
N_HEADS = 16
N_KV_HEADS = 4
HEAD_DIM = 64
Q_PER_KV = N_HEADS // N_KV_HEADS
IDX_HEADS = 4
IDX_DIM = 64
TOP_K_MAX = 256
SSD_HEAD_DIM = 64
SSD_GROUPS = 4
D_STATE = 128
CONV_W = 4
TILE = 128
PAD_FRONT = TILE - N_META
INT_MIN = -(2 ** 31)
NEG_BIG = -(2.0 ** 100)
Q_SCALE = HEAD_DIM ** -0.5 * 1.4426950408889634
VMEM_LIMIT = 56 * 1024 * 1024


def _cparams(sem, flags=None):
    return pltpu.CompilerParams(dimension_semantics=sem, vmem_limit_bytes=VMEM_LIMIT, flags=flags)


def _const_spec(shape):
    nd = len(shape)
    return pl.BlockSpec(shape, lambda *_: (0,) * nd, pipeline_mode=pl.Buffered(1))


def _rms(x, g):
    return x * lax.rsqrt(jnp.mean(x * x, axis=-1, keepdims=True) + EPS) * g


def _silu(x):
    return x * jax.nn.sigmoid(x)


def _row_tile(n_tiles, cap):
    for k in range(cap, 0, -1):
        if n_tiles % k == 0:
            return k * TILE
    return TILE


def _ffn_kernel(x_ref, g_ref, g2_ref, wg_ref, wu_ref, wd_ref, o_ref, h2_ref=None, *, ck):
    x = x_ref[...]
    h = _rms(x, g_ref[...]).astype(BF16)
    acc = jnp.zeros(x.shape, F32)
    for c in range(wg_ref.shape[1] // ck):
        sl = slice(c * ck, (c + 1) * ck)
        gate = jnp.dot(h, wg_ref[:, sl], preferred_element_type=F32)
        up = jnp.dot(h, wu_ref[:, sl], preferred_element_type=F32)
        act = (_silu(gate) * up).astype(BF16)
        acc = acc + jnp.dot(act, wd_ref[sl, :], preferred_element_type=F32)
    y = x + 0.5 * acc
    o_ref[...] = y
    if h2_ref is not None:
        h2_ref[...] = _rms(y, g2_ref[...]).astype(BF16)


def _ffn(x, g, g2, wg, wu, wd, tm, rows=None, with_norm=True):
    d = x.shape[1]
    r0, r = (0, x.shape[0]) if rows is None else rows
    dff = wg.shape[1]
    blk0 = r0 // tm
    out_specs = [pl.BlockSpec((tm, d), lambda i: (i, 0))]
    out_shape = [jax.ShapeDtypeStruct((r, d), F32)]
    if with_norm:
        out_specs.append(pl.BlockSpec((tm, d), lambda i: (i, 0)))
        out_shape.append(jax.ShapeDtypeStruct((r, d), BF16))
    return pl.pallas_call(
        functools.partial(_ffn_kernel, ck=256),
        grid=(r // tm,),
        in_specs=[pl.BlockSpec((tm, d), lambda i: (blk0 + i, 0)), _const_spec((1, d)), _const_spec((1, d)),
                  _const_spec((d, dff)), _const_spec((d, dff)), _const_spec((dff, d))],
        out_specs=out_specs,
        out_shape=out_shape,
        compiler_params=_cparams(("arbitrary",)),
        name="ffn",
    )(x, g, g2, wg, wu, wd)


def _proj_attn_kernel(x_ref, g_ref, wqk_ref, wv_ref, wih_ref, wil_ref, seg_ref, segt_ref, gqk_ref,
                      q_ref, k_ref, v_ref, kb_ref, vb_ref, qi_ref, kiw_ref):
    h = _rms(x_ref[...], g_ref[...])
    hb, h_lo = _split_bf16(h)
    qk = jnp.dot(hb, wqk_ref[...], preferred_element_type=F32)
    ss = jnp.dot((qk * qk).astype(BF16), seg_ref[...], preferred_element_type=F32)
    r = lax.rsqrt(ss * (1.0 / HEAD_DIM) + EPS)
    r_hi = r.astype(BF16)
    r_lo = (r - r_hi.astype(F32)).astype(BF16)
    rx = (jnp.dot(r_hi, segt_ref[...], preferred_element_type=F32)
          + jnp.dot(r_lo, segt_ref[...], preferred_element_type=F32))
    qkn = qk * rx * gqk_ref[...]
    nq = q_ref.shape[1]
    q_ref[...] = (qkn[:, :nq] * Q_SCALE).astype(BF16)
    k = qkn[:, nq:]
    k_ref[...] = k
    kb_ref[...] = k.astype(BF16)
    v = jnp.dot(hb, wv_ref[...], preferred_element_type=F32)
    v_ref[...] = v
    vb_ref[...] = v.astype(BF16)
    idx = _dot3(hb, h_lo, wih_ref[...], wil_ref[...])
    nqi = qi_ref.shape[1]
    qi_ref[...] = idx[:, :nqi]
    kiw_ref[...] = idx[:, nqi:]


def _proj_attn(x1, g, wqk, wv, wih, wil, seg, segt, gqk, tm):
    r, d = x1.shape
    nqk, nv = wqk.shape[1], wv.shape[1]
    nq = N_HEADS * HEAD_DIM
    nqi = IDX_HEADS * IDX_DIM
    row = lambda i: (i, 0)
    outs = [(nq, BF16), (nqk - nq, F32), (nv, F32), (nqk - nq, BF16), (nv, BF16), (nqi, F32), (TILE, F32)]
    return pl.pallas_call(
        _proj_attn_kernel,
        grid=(r // tm,),
        in_specs=[pl.BlockSpec((tm, d), row), _const_spec((1, d)), _const_spec(wqk.shape),
                  _const_spec(wv.shape), _const_spec(wih.shape), _const_spec(wil.shape), _const_spec(seg.shape),
                  _const_spec(segt.shape), _const_spec(gqk.shape)],
        out_specs=[pl.BlockSpec((tm, n), row) for n, _ in outs],
        out_shape=[jax.ShapeDtypeStruct((r, n), dt) for n, dt in outs],
        compiler_params=_cparams(("arbitrary",)),
        name="proj_attn",
    )(x1, g, wqk, wv, wih, wil, seg, segt, gqk)


def _proj_xbc_kernel(h_ref, wx_ref, wdt_ref, xbc_ref, dt_ref):
    h = h_ref[...]
    xbc_ref[...] = jnp.dot(h, wx_ref[...], preferred_element_type=F32)
    dt_ref[...] = jnp.dot(h, wdt_ref[...], preferred_element_type=F32)


def _proj_xbc(hb, wx, wdt, tm):
    r, d = hb.shape
    row = lambda i: (i, 0)
    return pl.pallas_call(
        _proj_xbc_kernel,
        grid=(r // tm,),
        in_specs=[pl.BlockSpec((tm, d), row), _const_spec(wx.shape), _const_spec(wdt.shape)],
        out_specs=[pl.BlockSpec((tm, wx.shape[1]), row), pl.BlockSpec((tm, wdt.shape[1]), row)],
        out_shape=[jax.ShapeDtypeStruct((r, wx.shape[1]), F32),
                   jax.ShapeDtypeStruct((r, wdt.shape[1]), F32)],
        compiler_params=_cparams(("arbitrary",)),
        name="proj_xbc",
    )(hb, wx, wdt)


def _proj_gate_kernel(h_ref, wz_ref, wg_ref, zs_ref, gt_ref):
    h = h_ref[...]
    zs_ref[...] = _silu(jnp.dot(h, wz_ref[...], preferred_element_type=F32)).astype(BF16)
    gt_ref[...] = jax.nn.sigmoid(jnp.dot(h, wg_ref[...], preferred_element_type=F32)).astype(BF16)


def _proj_gate(hb, wz, wg, tm):
    r, d = hb.shape
    row = lambda i: (i, 0)
    return pl.pallas_call(
        _proj_gate_kernel,
        grid=(r // tm,),
        in_specs=[pl.BlockSpec((tm, d), row), _const_spec(wz.shape), _const_spec(wg.shape)],
        out_specs=[pl.BlockSpec((tm, wz.shape[1]), row), pl.BlockSpec((tm, wg.shape[1]), row)],
        out_shape=[jax.ShapeDtypeStruct((r, wz.shape[1]), BF16),
                   jax.ShapeDtypeStruct((r, wg.shape[1]), BF16)],
        compiler_params=_cparams(("arbitrary",)),
        name="proj_gate",
    )(hb, wz, wg)


KEY_GROUP = 4


def _key_value(key):
    mag = jnp.where(key < 0, jnp.where(key == INT_MIN, 0x7F800000, -key), key)
    v = pltpu.bitcast(mag, F32)
    return jnp.where(key < 0, -v, v)


def _tree_sum(x):
    while x.shape[0] > 1:
        h = x.shape[0] // 2
        x = x[:h] + x[h:]
    return x[0]


def _count(sc_ref, n_groups, pred):
    def body(gi, acc):
        s = sc_ref[pl.ds(gi * KEY_GROUP, KEY_GROUP)]
        hit = jnp.where(pred(s, gi * (KEY_GROUP * TILE)), 1.0, 0.0)
        return acc + _tree_sum(hit.reshape(KEY_GROUP * TILE // 8, 8, TILE))
    part = lax.fori_loop(0, n_groups, body, jnp.zeros((8, TILE), F32))
    return jnp.sum(part, axis=0, keepdims=True)


def _any(flag):
    return jnp.max(jnp.where(flag, 1, 0)) > 0


def _kth_bitwise(sc_ref, n_groups, ksel):
    n_ge0 = _count(sc_ref, n_groups, lambda s, r0: s >= 0.0)
    t = jnp.where(n_ge0 >= ksel, jnp.zeros((1, TILE), I32), jnp.full((1, TILE), INT_MIN, I32))

    def bit_step(i, t):
        cand = t | jnp.left_shift(jnp.int32(1), 30 - i)
        cv = _key_value(cand)
        n = _count(sc_ref, n_groups, lambda s, r0: s >= cv)
        return jnp.where(n >= ksel, cand, t)

    return lax.fori_loop(0, 31, bit_step, t)


def _tie_search(sc_ref, n_groups, tv, need, n_bits):
    shape = (KEY_GROUP, TILE, TILE)
    rows = (lax.broadcasted_iota(I32, shape, 0) * TILE + lax.broadcasted_iota(I32, shape, 1)).astype(F32)

    def bit_step(i, p):
        cand = p | jnp.left_shift(jnp.int32(1), n_bits - 1 - i)
        cf = cand.astype(F32)
        n = _count(sc_ref, n_groups, lambda s, r0: (s == tv) & ((rows + jnp.asarray(r0, F32)) < cf))
        return jnp.where(n < need, cand, p)
    return lax.fori_loop(0, n_bits, bit_step, jnp.zeros((1, TILE), I32)) + 1


def _select_topk(sc_ref, n_groups, ksel, n_bits):
    no_limit = jnp.full((1, TILE), float(1 << n_bits), F32)
    t = _kth_bitwise(sc_ref, n_groups, ksel)
    tv = _key_value(t)
    n_gt = _count(sc_ref, n_groups, lambda s, r0: s > tv)
    n_eq = _count(sc_ref, n_groups, lambda s, r0: s == tv)
    need = ksel - n_gt
    conflict = (t != INT_MIN) & (n_eq > need)
    lim = lax.cond(_any(conflict), lambda: _tie_search(sc_ref, n_groups, tv, need, n_bits).astype(F32),
                   lambda: no_limit)
    return tv, jnp.where(conflict, lim, no_limit)


def _split_bf16(x):
    hi = x.astype(BF16)
    return hi, (x - hi.astype(F32)).astype(BF16)


def _dot3(a_hi, a_lo, b_hi, b_lo):
    return (jnp.dot(a_hi, b_hi, preferred_element_type=F32)
            + jnp.dot(a_hi, b_lo, preferred_element_type=F32)
            + jnp.dot(a_lo, b_hi, preferred_element_type=F32))


VROWS = HEAD_DIM + 16


def _dsa_prompt_kernel(q_ref, qi_ref, kiwq_ref, kb_ref, vb_ref, kiw_ref, kbm_ref, vbm_ref, kiwm_ref, o_ref,
                       sc_ref, kk_ref, vt_ref, kihl_ref, rhs_ref, qa_ref, qb_ref,
                       m_ref, acc_ref, *, ksel, n_bits):
    b = pl.program_id(0)
    j = pl.program_id(1)
    nt = vb_ref.shape[0] // TILE + 1
    n_groups = lax.shift_right_logical(j + KEY_GROUP, KEY_GROUP.bit_length() - 1)
    n_pairs = lax.shift_right_logical(j + 2, 1)
    nkv = N_KV_HEADS
    gw = Q_PER_KV * TILE
    eye = (lax.broadcasted_iota(I32, (TILE, TILE), 0) == lax.broadcasted_iota(I32, (TILE, TILE), 1))

    @pl.when((b == 0) & (j == 0))
    def _():
        rhs_ref[...] = jnp.zeros(rhs_ref.shape, rhs_ref.dtype)
        ident = jnp.where(eye, 1.0, 0.0).astype(BF16)
        for g in range(nkv):
            for hh in range(Q_PER_KV):
                rhs_ref[g, 0:TILE, hh * TILE:(hh + 1) * TILE] = ident
        qa_ref[...] = jnp.zeros(qa_ref.shape, qa_ref.dtype)
        qb_ref[...] = jnp.zeros(qb_ref.shape, qb_ref.dtype)
        kk_ref[...] = jnp.zeros(kk_ref.shape, kk_ref.dtype)
        kihl_ref[...] = jnp.zeros(kihl_ref.shape, kihl_ref.dtype)
        ones_row = jnp.where(lax.broadcasted_iota(I32, (VROWS - HEAD_DIM, TILE), 0) == 0, 1.0, 0.0).astype(BF16)
        for c in range(vt_ref.shape[0]):
            vt_ref[c] = jnp.zeros(vt_ref.shape[1:], vt_ref.dtype)
            for g in range(nkv):
                vt_ref[c, g * VROWS + HEAD_DIM:(g + 1) * VROWS, :] = ones_row

    @pl.when(j == 0)
    def _():
        def stage(c, kb, vb, kiw):
            kk_ref[c] = kb
            vt = vb.astype(F32).T.astype(BF16)
            for g in range(nkv):
                vt_ref[c, g * VROWS:g * VROWS + HEAD_DIM, :] = vt[g * HEAD_DIM:(g + 1) * HEAD_DIM, :]
            hi, lo = _split_bf16(kiw)
            kihl_ref[c] = jnp.concatenate([hi, lo], axis=1)

        stage(0, kbm_ref[...], vbm_ref[...], kiwm_ref[...])

        def prep(c, _):
            rows = pl.ds(pl.multiple_of((c - 1) * TILE, TILE), TILE)
            stage(c, kb_ref[rows, :], vb_ref[rows, :], kiw_ref[rows, :])
            return 0
        lax.fori_loop(1, nt, prep, 0)

    qt = q_ref[...].astype(F32).T.astype(BF16)
    for h in range(N_HEADS):
        g, hh = divmod(h, Q_PER_KV)
        r0 = TILE + (g % 2) * HEAD_DIM
        rhs_ref[g, r0:r0 + HEAD_DIM, hh * TILE:(hh + 1) * TILE] = qt[h * HEAD_DIM:(h + 1) * HEAD_DIM, :]
    qit = qi_ref[...].T
    for h in range(IDX_HEADS):
        hi, lo = _split_bf16(qit[h * IDX_DIM:(h + 1) * IDX_DIM, :])
        qa_ref[0:IDX_DIM, h * TILE:(h + 1) * TILE] = hi
        qa_ref[TILE:TILE + IDX_DIM, h * TILE:(h + 1) * TILE] = hi
        qb_ref[0:IDX_DIM, h * TILE:(h + 1) * TILE] = lo
    wq = kiwq_ref[...].T[IDX_DIM:IDX_DIM + 8, :]
    wq = wq * (IDX_HEADS ** -0.5 * IDX_DIM ** -0.5)

    grp_rows = KEY_GROUP * TILE
    q_pos = j * TILE + lax.broadcasted_iota(I32, (grp_rows, TILE), 1)
    k_row = lax.broadcasted_iota(I32, (grp_rows, TILE), 0)

    def score_group(gi, _):
        khl = kihl_ref[pl.ds(gi * KEY_GROUP, KEY_GROUP)].reshape(grp_rows, 2 * TILE)
        d = (jnp.dot(khl, qa_ref[...], preferred_element_type=F32)
             + jnp.dot(khl, qb_ref[...], preferred_element_type=F32))
        s = jnp.zeros((grp_rows, TILE), F32)
        for h in range(IDX_HEADS):
            s = s + wq[h:h + 1, :] * jnp.maximum(d[:, h * TILE:(h + 1) * TILE], 0.0)
        k_pos = k_row + gi * grp_rows
        valid = (k_pos <= q_pos) & (k_pos >= PAD_FRONT)
        sc_ref[pl.ds(gi * KEY_GROUP, KEY_GROUP)] = jnp.where(valid, s, jnp.nan).reshape(KEY_GROUP, TILE, TILE)
        return 0
    lax.fori_loop(0, n_groups, score_group, 0)

    tv, lim = _select_topk(sc_ref, n_groups, ksel, n_bits)

    m_ref[...] = jnp.full(m_ref.shape, NEG_BIG, F32)
    acc_ref[...] = jnp.zeros(acc_ref.shape, F32)

    pair_shape = (2, TILE, TILE)
    pair_row = (lax.broadcasted_iota(I32, pair_shape, 0) * TILE
                + lax.broadcasted_iota(I32, pair_shape, 1)).astype(F32)

    def attend_pair(i, _):
        sc = sc_ref[pl.ds(2 * i, 2)]
        sel = (sc > tv) | ((sc == tv) & ((pair_row + (i * (2 * TILE)).astype(F32)) < lim))
        bias = jnp.where(sel, 0.0, NEG_BIG).astype(BF16).reshape(2 * TILE, TILE)
        kc = kk_ref[pl.ds(2 * i, 2)].reshape(2 * TILE, nkv * HEAD_DIM)
        vt = jnp.concatenate([vt_ref[2 * i], vt_ref[2 * i + 1]], axis=1)
        m_all = m_ref[...]
        s = []
        for g in range(nkv):
            half = (g // 2) * TILE
            lhs = jnp.concatenate([bias, kc[:, half:half + TILE]], axis=1)
            s.append(jnp.dot(lhs, rhs_ref[g], preferred_element_type=F32))
        m_new = []
        for g in range(nkv):
            m_g = jnp.maximum(m_all[g:g + 1, :], jnp.max(s[g], axis=0, keepdims=True))
            alpha = jnp.exp2(m_all[g:g + 1, :] - m_g)
            p = jnp.exp2(s[g] - m_g).astype(BF16)
            pv = jnp.dot(vt[g * VROWS:(g + 1) * VROWS, :], p, preferred_element_type=F32)
            acc_ref[g] = alpha * acc_ref[g] + pv
            m_new.append(m_g)
        m_ref[...] = jnp.concatenate(m_new + [m_all[nkv:, :]], axis=0)
        return 0
    lax.fori_loop(0, n_pairs, attend_pair, 0)

    pieces = []
    for g in range(nkv):
        acc = acc_ref[g]
        o = acc[:HEAD_DIM] / acc[HEAD_DIM:HEAD_DIM + 1]
        for hh in range(Q_PER_KV):
            pieces.append(o[:, hh * TILE:(hh + 1) * TILE])
    o_ref[...] = jnp.concatenate(pieces, axis=0).T.astype(BF16)


def _seq_tile_map(seq, meta_blk, per_seq=False):
    n_tok = seq // TILE
    if per_seq:
        return lambda b, j: (jnp.where(j == 0, meta_blk + 1 + b, b * n_tok + j - 1), 0)
    return lambda b, j: (jnp.where(j == 0, meta_blk, b * n_tok + j - 1), 0)


def _dsa_prompt(q, qi, kiw, kb, vb, n_batch, seq, meta_blk, ksel):
    lp = seq + TILE
    nt = lp // TILE
    nt_pad = -(-nt // KEY_GROUP) * KEY_GROUP
    r_total = q.shape[0]
    qw = N_HEADS * HEAD_DIM
    kvw = N_KV_HEADS * HEAD_DIM
    gw = Q_PER_KV * TILE
    tile = _seq_tile_map(seq, meta_blk)
    toks = lambda b, j: (b, 0)
    meta = lambda b, j: (meta_blk, 0)
    n_bits = max(1, (lp - 1).bit_length())
    return pl.pallas_call(
        functools.partial(_dsa_prompt_kernel, ksel=ksel, n_bits=n_bits),
        grid=(n_batch, nt),
        in_specs=[pl.BlockSpec((TILE, qw), tile), pl.BlockSpec((TILE, IDX_HEADS * IDX_DIM), tile),
                  pl.BlockSpec((TILE, TILE), tile), pl.BlockSpec((seq, kvw), toks),
                  pl.BlockSpec((seq, kvw), toks), pl.BlockSpec((seq, TILE), toks),
                  pl.BlockSpec((TILE, kvw), meta), pl.BlockSpec((TILE, kvw), meta),
                  pl.BlockSpec((TILE, TILE), meta)],
        out_specs=pl.BlockSpec((TILE, qw), _seq_tile_map(seq, meta_blk, per_seq=True)),
        out_shape=jax.ShapeDtypeStruct((r_total + n_batch * TILE, qw), BF16),
        scratch_shapes=[pltpu.VMEM((nt_pad, TILE, TILE), F32),
                        pltpu.VMEM((nt_pad, TILE, kvw), BF16),
                        pltpu.VMEM((nt_pad, N_KV_HEADS * VROWS, TILE), BF16),
                        pltpu.VMEM((nt_pad, TILE, 2 * TILE), BF16),
                        pltpu.VMEM((N_KV_HEADS, 2 * TILE, gw), BF16),
                        pltpu.VMEM((2 * TILE, IDX_HEADS * TILE), BF16),
                        pltpu.VMEM((2 * TILE, IDX_HEADS * TILE), BF16),
                        pltpu.VMEM((8, gw), F32),
                        pltpu.VMEM((N_KV_HEADS, VROWS, gw), F32)],
        compiler_params=_cparams(("arbitrary", "arbitrary")),
        name="dsa_prompt",
    )(q, qi, kiw, kb, vb, kiw, kb, vb, kiw)


def _expand_heads(v, e_ref):
    n_heads = e_ref.shape[1] // SSD_HEAD_DIM
    lane = lax.broadcasted_iota(I32, (1, v.shape[1]), 1)
    v = jnp.where(lane < n_heads, v, 0.0)
    hi = v.astype(BF16).astype(F32)
    r1 = v - hi
    mid = r1.astype(BF16).astype(F32)
    lo = (r1 - mid).astype(BF16).astype(F32)
    packed = (hi + pltpu.roll(mid, n_heads, 1) + pltpu.roll(lo, 2 * n_heads, 1)).astype(BF16)
    return jnp.dot(packed, e_ref[...], preferred_element_type=F32)


def _softplus(x):
    return jnp.maximum(x, 0.0) + jnp.log1p(jnp.exp(-jnp.abs(x)))


def _ssd_prompt_kernel(xbc_ref, dtr_ref, zs_ref, cw_ref, cb_ref, dtb_ref, alog_ref, dsk_ref, gn_ref,
                       e_ref, y_ref, ssm_ref, conv_ref, xp_ref, ht_ref, *, n_heads):
    c = pl.program_id(1)
    nc = pl.num_programs(1)
    d_inner = n_heads * SSD_HEAD_DIM
    hpg = n_heads // SSD_GROUPS
    gwid = hpg * SSD_HEAD_DIM
    gn_w = d_inner // SSD_GROUPS

    @pl.when(c == 0)
    def _():
        xp_ref[0:8, :] = jnp.zeros((8, xp_ref.shape[1]), F32)
        ht_ref[...] = jnp.zeros(ht_ref.shape, F32)

    xp_ref[8:8 + TILE, :] = xbc_ref[...]
    u = cb_ref[...] + cw_ref[CONV_W - 1:CONV_W, :] * xp_ref[8:8 + TILE, :]
    for k in range(CONV_W - 1):
        off = 8 - (CONV_W - 1) + k
        u = u + cw_ref[k:k + 1, :] * xp_ref[off:off + TILE, :]
    tail = xp_ref[8 + TILE - (CONV_W - 1):8 + TILE, :]
    xp_ref[8 - (CONV_W - 1):8, :] = tail

    row = lax.broadcasted_iota(I32, (TILE, 1), 0)
    live = jnp.where((c > 0) | (row >= PAD_FRONT), 1.0, 0.0)
    act = _silu(u) * live
    xs = act[:, :d_inner]
    bm = act[:, d_inner:d_inner + SSD_GROUPS * D_STATE]
    cm = act[:, d_inner + SSD_GROUPS * D_STATE:]

    dt = _softplus(dtr_ref[...] + dtb_ref[...]) * live
    a = dt * (-jnp.exp(alog_ref[...]))
    ri = lax.broadcasted_iota(I32, (TILE, TILE), 0)
    ci = lax.broadcasted_iota(I32, (TILE, TILE), 1)
    tril = ri >= ci
    acs = jnp.dot(jnp.where(tril, 1.0, 0.0), a, preferred_element_type=F32, precision=HIGHEST)
    acs_t = acs.T
    dt_t = dt.T
    acs_last = acs[TILE - 1:TILE, :]
    w_in = _expand_heads(jnp.exp(acs_last - acs) * dt, e_ref)
    w_out = _expand_heads(jnp.exp(acs), e_ref)
    xw = (xs * w_in).astype(BF16)
    xsb = xs.astype(BF16)
    first_head = lax.broadcasted_iota(I32, (1, 2 * SSD_HEAD_DIM), 1) < SSD_HEAD_DIM

    ys = []
    for g in range(SSD_GROUPS):
        bg = bm[:, g * D_STATE:(g + 1) * D_STATE]
        cg = cm[:, g * D_STATE:(g + 1) * D_STATE].astype(BF16)
        bgt = bg.T.astype(BF16)
        cb = jnp.dot(cg, bgt, preferred_element_type=F32)
        xw_g = xw[:, g * gwid:(g + 1) * gwid]
        xs_g = xsb[:, g * gwid:(g + 1) * gwid]
        h_prev = ht_ref[g]
        y_off = jnp.dot(cg, h_prev.astype(BF16), preferred_element_type=F32)
        y_g = y_off * w_out[:, g * gwid:(g + 1) * gwid]
        pairs = []
        for e0 in range(0, hpg, 2):
            x_pair = xs_g[:, e0 * SSD_HEAD_DIM:(e0 + 2) * SSD_HEAD_DIM]
            prods = []
            for e in (e0, e0 + 1):
                he = g * hpg + e
                seg = acs[:, he:he + 1] - acs_t[he:he + 1, :]
                m = (cb * jnp.exp(jnp.where(tril, seg, -jnp.inf)) * dt_t[he:he + 1, :]).astype(BF16)
                prods.append(jnp.dot(m, x_pair, preferred_element_type=F32))
            pairs.append(jnp.where(first_head, prods[0], prods[1]))
        y_g = y_g + jnp.concatenate(pairs, axis=1)
        states = jnp.dot(bgt, xw_g, preferred_element_type=F32)
        ht_ref[g] = h_prev * w_out[TILE - 1:TILE, g * gwid:(g + 1) * gwid] + states
        ys.append(y_g)
    y = jnp.concatenate(ys, axis=1) + dsk_ref[...] * xs
    y = y * zs_ref[...].astype(F32)
    outs = []
    for g in range(SSD_GROUPS):
        yg = y[:, g * gn_w:(g + 1) * gn_w]
        outs.append(yg * lax.rsqrt(jnp.mean(yg * yg, axis=-1, keepdims=True) + EPS))
    y_ref[...] = (jnp.concatenate(outs, axis=1) * gn_ref[...]).astype(BF16)

    @pl.when(c == nc - 1)
    def _():
        for g in range(SSD_GROUPS):
            ssm_ref[0, g * gwid:(g + 1) * gwid, :] = ht_ref[g].T
        conv_ref[0] = tail


def _ssd_prompt(xbc, dtr, zs, cw, cb, dtb, alog, dsk, gn, e_mat, n_batch, seq, meta_blk, n_heads):
    nt = seq // TILE + 1
    r_total = xbc.shape[0]
    d_inner = n_heads * SSD_HEAD_DIM
    cdim = xbc.shape[1]
    tile = _seq_tile_map(seq, meta_blk)
    fix = lambda b, c: (0, 0)
    return pl.pallas_call(
        functools.partial(_ssd_prompt_kernel, n_heads=n_heads),
        grid=(n_batch, nt),
        in_specs=[pl.BlockSpec((TILE, cdim), tile), pl.BlockSpec((TILE, TILE), tile),
                  pl.BlockSpec((TILE, d_inner), tile),
                  pl.BlockSpec(cw.shape, fix), pl.BlockSpec(cb.shape, fix), pl.BlockSpec(dtb.shape, fix),
                  pl.BlockSpec(alog.shape, fix), pl.BlockSpec(dsk.shape, fix), pl.BlockSpec(gn.shape, fix),
                  pl.BlockSpec(e_mat.shape, fix)],
        out_specs=[pl.BlockSpec((TILE, d_inner), _seq_tile_map(seq, meta_blk, per_seq=True)),
                   pl.BlockSpec((1, d_inner, D_STATE), lambda b, c: (b, 0, 0)),
                   pl.BlockSpec((1, CONV_W - 1, cdim), lambda b, c: (b, 0, 0))],
        out_shape=[jax.ShapeDtypeStruct((r_total + n_batch * TILE, d_inner), BF16),
                   jax.ShapeDtypeStruct((n_batch, d_inner, D_STATE), F32),
                   jax.ShapeDtypeStruct((n_batch, CONV_W - 1, cdim), F32)],
        scratch_shapes=[pltpu.VMEM((8 + TILE, cdim), F32),
                        pltpu.VMEM((SSD_GROUPS, D_STATE, d_inner // SSD_GROUPS), F32)],
        compiler_params=_cparams(("arbitrary", "arbitrary")),
        name="ssd_prompt",
    )(xbc, dtr, zs, cw, cb, dtb, alog, dsk, gn, e_mat)


def _idx_sample_kernel(pt_ref, qi_ref, kiw_ref, *rest, n_pages):
    pages, s_ref = rest[:n_pages], rest[n_pages]
    qi = qi_ref[0]
    kiw = kiw_ref[0]
    q8 = jnp.concatenate([qi[:, h * IDX_DIM:(h + 1) * IDX_DIM] for h in range(IDX_HEADS)]
                         + [jnp.zeros((8 - IDX_HEADS, IDX_DIM), F32)], axis=0)
    w_col = jnp.broadcast_to(kiw, (TILE, TILE)).T[IDX_DIM:IDX_DIM + 8, 0:1]
    w_col = w_col * (IDX_HEADS ** -0.5 * IDX_DIM ** -0.5)
    q_hi, q_lo = _split_bf16(q8)
    q3 = jnp.concatenate([q_hi, q_lo, q_hi], axis=1)
    for p in range(n_pages):
        k_hi, k_lo = _split_bf16(pages[p][0])
        d = jnp.dot(q3, jnp.concatenate([k_hi, k_hi, k_lo], axis=0), preferred_element_type=F32)
        s_ref[p, 0] = jnp.sum(w_col * jnp.maximum(d, 0.0), axis=0, keepdims=True)
    d_new = jnp.sum(q8 * kiw[:, :IDX_DIM], axis=1, keepdims=True)
    s_new = jnp.sum(w_col * jnp.maximum(d_new, 0.0), axis=0, keepdims=True)
    lane = lax.broadcasted_iota(I32, (1, TILE), 1)
    s_ref[n_pages, 0] = jnp.where(lane == 0, s_new, 0.0)


def _idx_sample(page_table, qi_s, kiw_s, cache_kidx):
    ns, n_pages = page_table.shape
    n_slots = n_pages + 1
    page = lambda p: pl.BlockSpec((1, IDX_DIM, TILE), lambda s, pt: (pt[s, p], 0, 0))
    one = lambda s, pt: (s, 0, 0)
    grid_spec = pltpu.PrefetchScalarGridSpec(
        num_scalar_prefetch=1, grid=(ns,),
        in_specs=[pl.BlockSpec((1, 1, qi_s.shape[2]), one), pl.BlockSpec((1, 1, TILE), one)]
        + [page(p) for p in range(n_pages)],
        out_specs=pl.BlockSpec((n_slots, 1, 1, TILE), lambda s, pt: (0, s, 0, 0)))
    return pl.pallas_call(
        functools.partial(_idx_sample_kernel, n_pages=n_pages),
        grid_spec=grid_spec,
        out_shape=jax.ShapeDtypeStruct((n_slots, ns, 1, TILE), F32),
        compiler_params=_cparams(("arbitrary",)),
        name="idx_sample",
    )(page_table, qi_s, kiw_s, *([cache_kidx] * n_pages))


def _select_sample_kernel(s_ref, bias_ref, sc_ref, *, n_pages, ksel, n_bits):
    n_slots = n_pages + 1
    rows = lax.broadcasted_iota(I32, (TILE, TILE), 0)
    for p in range(n_slots):
        sc = s_ref[p].T
        if p == n_pages:
            sc = jnp.where(rows == 0, sc, jnp.nan)
        sc_ref[p] = sc
    for p in range(n_slots, sc_ref.shape[0]):
        sc_ref[p] = jnp.full((TILE, TILE), jnp.nan, F32)
    n_groups = sc_ref.shape[0] // KEY_GROUP
    tv, lim = _select_topk(sc_ref, n_groups, ksel, n_bits)
    rows_f = rows.astype(F32)
    for p in range(n_slots):
        sc = sc_ref[p]
        sel = (sc > tv) | ((sc == tv) & ((rows_f + float(p * TILE)) < lim))
        bias_ref[p] = jnp.where(sel, 0.0, NEG_BIG).T


def _select_sample(scores, ksel):
    n_slots, ns, _ = scores.shape
    n_pages = n_slots - 1
    n_bits = (n_slots * TILE - 1).bit_length()
    return pl.pallas_call(
        functools.partial(_select_sample_kernel, n_pages=n_pages, ksel=ksel, n_bits=n_bits),
        out_shape=jax.ShapeDtypeStruct((n_slots, ns, TILE), F32),
        scratch_shapes=[pltpu.VMEM((-(-n_slots // KEY_GROUP) * KEY_GROUP, TILE, TILE), F32)],
        compiler_params=pltpu.CompilerParams(vmem_limit_bytes=VMEM_LIMIT),
        name="select_sample",
    )(scores)


def _attn_sample_kernel(pt_ref, q_ref, kn_ref, vn_ref, bias_ref, *rest, n_pages):
    kpages, vpages, o_ref = rest[:n_pages], rest[n_pages:2 * n_pages], rest[2 * n_pages]
    kvw = N_KV_HEADS * HEAD_DIM
    q = q_ref[0].astype(F32)
    lane = lax.broadcasted_iota(I32, (N_HEADS, kvw), 1)
    head = lax.broadcasted_iota(I32, (N_HEADS, kvw), 0)
    own = jnp.right_shift(lane, 6) == jnp.right_shift(head, 2)
    qrows = []
    for h in range(N_HEADS):
        qh = q[:, h * HEAD_DIM:(h + 1) * HEAD_DIM]
        qrows.append(jnp.concatenate([qh] * N_KV_HEADS, axis=1))
    qf = jnp.where(own, jnp.concatenate(qrows, axis=0), 0.0)
    qw = qf.astype(BF16)
    nt = (((1,), (1,)), ((), ()))
    scores = []
    for p in range(n_pages):
        kp = kpages[p][0].astype(BF16)
        s = jnp.dot(qw, kp, preferred_element_type=F32)
        scores.append(s + bias_ref[p, 0])
    s_new = jnp.sum(qf * kn_ref[0], axis=1, keepdims=True)
    lane1 = lax.broadcasted_iota(I32, (1, TILE), 1)
    scores.append(jnp.where(lane1 == 0, s_new, NEG_BIG) + bias_ref[n_pages, 0])
    m = scores[0].max(axis=1, keepdims=True)
    for s in scores[1:]:
        m = jnp.maximum(m, s.max(axis=1, keepdims=True))
    den = jnp.zeros((N_HEADS, 1), F32)
    out = jnp.zeros((N_HEADS, kvw), F32)
    for p in range(n_pages):
        e = jnp.exp2(scores[p] - m)
        den = den + e.sum(axis=1, keepdims=True)
        out = out + lax.dot_general(e.astype(BF16), vpages[p][0].astype(BF16), nt,
                                    preferred_element_type=F32)
    e = jnp.exp2(scores[n_pages] - m)
    den = den + e.sum(axis=1, keepdims=True)
    out = out + e[:, 0:1] * vn_ref[0]
    out = out / den
    rolled = jnp.concatenate([pltpu.roll(out[:, i * TILE:(i + 1) * TILE], HEAD_DIM, 1)
                              for i in range(kvw // TILE)], axis=1)
    half = lax.broadcasted_iota(I32, (1, TILE), 1) < HEAD_DIM
    pieces = []
    for m2 in range(N_HEADS // 2):
        g = (2 * m2) // Q_PER_KV
        blk = (g // 2) * TILE
        a_src, b_src = (rolled, out) if g % 2 == 1 else (out, rolled)
        a = a_src[2 * m2:2 * m2 + 1, blk:blk + TILE]
        b = b_src[2 * m2 + 1:2 * m2 + 2, blk:blk + TILE]
        pieces.append(jnp.where(half, a, b))
    o_ref[0] = jnp.concatenate(pieces, axis=1)


def _attn_sample(page_table, q_s, k_s, v_s, bias, cache_k, cache_v):
    ns, n_pages = page_table.shape
    kvw = N_KV_HEADS * HEAD_DIM
    qw = N_HEADS * HEAD_DIM
    page = lambda p: pl.BlockSpec((1, kvw, TILE), lambda s, pt: (pt[s, p], 0, 0))
    one = lambda s, pt: (s, 0, 0)
    grid_spec = pltpu.PrefetchScalarGridSpec(
        num_scalar_prefetch=1, grid=(ns,),
        in_specs=[pl.BlockSpec((1, 1, qw), one), pl.BlockSpec((1, 1, kvw), one),
                  pl.BlockSpec((1, 1, kvw), one),
                  pl.BlockSpec((n_pages + 1, 1, 1, TILE), lambda s, pt: (0, s, 0, 0))]
        + [page(p) for p in range(n_pages)] * 2,
        out_specs=pl.BlockSpec((1, 1, qw), one))
    return pl.pallas_call(
        functools.partial(_attn_sample_kernel, n_pages=n_pages),
        grid_spec=grid_spec,
        out_shape=jax.ShapeDtypeStruct((ns, 1, qw), F32),
        compiler_params=_cparams(("arbitrary",)),
        name="attn_sample",
    )(page_table, q_s, k_s, v_s, bias, *([cache_k] * n_pages), *([cache_v] * n_pages))


def _ssd_sample_kernel(xbc_ref, dtr_ref, sc_ref, h0_ref, cw_ref, cb_ref, dtb_ref, alog_ref, dsk_ref,
                       e_ref, y_ref, h_ref, conv_ref, *, n_heads, sb):
    d_inner = n_heads * SSD_HEAD_DIM
    hpg = n_heads // SSD_GROUPS
    new = xbc_ref[...]
    cdim = new.shape[1]
    u = cb_ref[...] + cw_ref[CONV_W - 1:CONV_W, :] * new
    for k in range(CONV_W - 1):
        u = u + cw_ref[k:k + 1, :] * sc_ref[:, k * cdim:(k + 1) * cdim]
    for k in range(CONV_W - 2):
        conv_ref[:, k * cdim:(k + 1) * cdim] = sc_ref[:, (k + 1) * cdim:(k + 2) * cdim]
    conv_ref[:, (CONV_W - 2) * cdim:] = new
    act = _silu(u)
    xs = act[:, :d_inner]
    bm = act[:, d_inner:d_inner + SSD_GROUPS * D_STATE]
    cm = act[:, d_inner + SSD_GROUPS * D_STATE:]
    dt = _softplus(dtr_ref[...] + dtb_ref[...])
    decay = jnp.exp(dt * (-jnp.exp(alog_ref[...])))
    dec_x = _expand_heads(decay, e_ref)
    xdt = xs * _expand_heads(dt, e_ref)
    for i in range(sb):
        bcols, ccols = [], []
        for g in range(SSD_GROUPS):
            brow = bm[i:i + 1, g * D_STATE:(g + 1) * D_STATE]
            crow = cm[i:i + 1, g * D_STATE:(g + 1) * D_STATE]
            bcols.append(jnp.broadcast_to(brow, (TILE, D_STATE)).T)
            ccols.append(jnp.broadcast_to(crow, (TILE, D_STATE)).T)
        pieces = []
        for pr in range(n_heads // 2):
            g = (2 * pr) // hpg
            sl = slice(pr * TILE, (pr + 1) * TILE)
            ht = h0_ref[i, sl, :].T
            hn = ht * dec_x[i:i + 1, sl] + bcols[g] * xdt[i:i + 1, sl]
            pieces.append(jnp.sum(hn * ccols[g], axis=0, keepdims=True))
            h_ref[i, sl, :] = hn.T
        y_ref[i:i + 1, :] = jnp.concatenate(pieces, axis=1) + dsk_ref[...] * xs[i:i + 1, :]


def _ssd_sample(xbc_s, dtr_s, state_conv, state_ssm, cw, cb, dtb, alog, dsk, e_mat, n_heads, sb):
    ns, cdim = xbc_s.shape
    d_inner = n_heads * SSD_HEAD_DIM
    row = lambda i: (i, 0)
    row3 = lambda i: (i, 0, 0)
    fix = lambda i: (0, 0)
    return pl.pallas_call(
        functools.partial(_ssd_sample_kernel, n_heads=n_heads, sb=sb),
        grid=(ns // sb,),
        in_specs=[pl.BlockSpec((sb, cdim), row), pl.BlockSpec((sb, TILE), row),
                  pl.BlockSpec((sb, (CONV_W - 1) * cdim), row), pl.BlockSpec((sb, d_inner, D_STATE), row3),
                  pl.BlockSpec(cw.shape, fix), pl.BlockSpec(cb.shape, fix), pl.BlockSpec(dtb.shape, fix),
                  pl.BlockSpec(alog.shape, fix), pl.BlockSpec(dsk.shape, fix), pl.BlockSpec(e_mat.shape, fix)],
        out_specs=[pl.BlockSpec((sb, d_inner), row), pl.BlockSpec((sb, d_inner, D_STATE), row3),
                   pl.BlockSpec((sb, (CONV_W - 1) * cdim), row)],
        out_shape=[jax.ShapeDtypeStruct((ns, d_inner), F32),
                   jax.ShapeDtypeStruct((ns, d_inner, D_STATE), F32),
                   jax.ShapeDtypeStruct((ns, (CONV_W - 1) * cdim), F32)],
        compiler_params=_cparams(("arbitrary",)),
        name="ssd_sample",
    )(xbc_s, dtr_s, state_conv, state_ssm, cw, cb, dtb, alog, dsk, e_mat)


def _fill_sample_kernel(attn_any, ssd_any, a_ref, y_ref, zs_ref, gn_ref, attn_ref, ssd_ref):
    del attn_any, ssd_any
    attn_ref[...] = a_ref[...].astype(BF16)
    y = y_ref[...] * zs_ref[...].astype(F32)
    gn_w = y.shape[1] // SSD_GROUPS
    outs = []
    for g in range(SSD_GROUPS):
        yg = y[:, g * gn_w:(g + 1) * gn_w]
        outs.append(yg * lax.rsqrt(jnp.mean(yg * yg, axis=-1, keepdims=True) + EPS))
    ssd_ref[...] = (jnp.concatenate(outs, axis=1) * gn_ref[...]).astype(BF16)


def _fill_sample(attn, ssd, attn_s, y_s, zs, gn, blk):
    ns = attn_s.shape[0]
    last = lambda i: (blk, 0)
    fix = lambda i: (0, 0)
    return pl.pallas_call(
        _fill_sample_kernel,
        grid=(1,),
        in_specs=[pl.BlockSpec(memory_space=pl.ANY), pl.BlockSpec(memory_space=pl.ANY),
                  pl.BlockSpec(attn_s.shape, fix), pl.BlockSpec(y_s.shape, fix),
                  pl.BlockSpec((ns, zs.shape[1]), last), pl.BlockSpec(gn.shape, fix)],
        out_specs=[pl.BlockSpec((ns, attn.shape[1]), last), pl.BlockSpec((ns, ssd.shape[1]), last)],
        out_shape=[jax.ShapeDtypeStruct(attn.shape, attn.dtype), jax.ShapeDtypeStruct(ssd.shape, ssd.dtype)],
        input_output_aliases={0: 0, 1: 1},
        compiler_params=_cparams(("arbitrary",)),
        name="fill_sample",
    )(attn, ssd, attn_s, y_s, zs, gn)


def _mix_kernel(x_ref, a_ref, s_ref, gt_ref, wao_ref, wso_ref, wo_ref, o_ref):
    d = x_ref.shape[1]
    gt = gt_ref[...].astype(F32)
    ao = jnp.dot(a_ref[...], wao_ref[...], preferred_element_type=F32)
    so = jnp.dot(s_ref[...], wso_ref[...], preferred_element_type=F32)
    mixed = (gt[:, :d] * ao + gt[:, d:] * so).astype(BF16)
    o_ref[...] = x_ref[...] + jnp.dot(mixed, wo_ref[...], preferred_element_type=F32)


def _mix(x1, attn, ssd, gates, wao, wso, wo, tm, rows):
    d = x1.shape[1]
    r0, r = rows
    blk0 = r0 // tm
    row = lambda i: (blk0 + i, 0)
    return pl.pallas_call(
        _mix_kernel,
        grid=(r // tm,),
        in_specs=[pl.BlockSpec((tm, d), row), pl.BlockSpec((tm, attn.shape[1]), row),
                  pl.BlockSpec((tm, ssd.shape[1]), row), pl.BlockSpec((tm, gates.shape[1]), row),
                  _const_spec(wao.shape), _const_spec(wso.shape), _const_spec(wo.shape)],
        out_specs=pl.BlockSpec((tm, d), lambda i: (i, 0)),
        out_shape=jax.ShapeDtypeStruct((r, d), F32),
        compiler_params=_cparams(("arbitrary",)),
        name="mix",
    )(x1, attn, ssd, gates, wao, wso, wo)


def _prompt_cache_kernel(k_ref, v_ref, kiw_ref, km_ref, vm_ref, kiwm_ref, kt_ref, vt_ref, kit_ref):
    n_tiles = k_ref.shape[0] // TILE
    for src, meta, dst, width in ((k_ref, km_ref, kt_ref, k_ref.shape[1]), (v_ref, vm_ref, vt_ref, v_ref.shape[1]),
                                  (kiw_ref, kiwm_ref, kit_ref, IDX_DIM)):
        dst[0, :, 0:N_META] = meta[...].T[:width, PAD_FRONT:]
        for j in range(n_tiles):
            dst[0, :, N_META + j * TILE:N_META + (j + 1) * TILE] = src[j * TILE:(j + 1) * TILE, :].T[:width, :]


def _prompt_cache(k, v, kiw, n_batch, seq, meta_blk):
    l_seq = seq + N_META
    kvw = k.shape[1]
    toks = lambda b: (b, 0)
    meta = lambda b: (meta_blk, 0)
    out = lambda b: (b, 0, 0)
    return pl.pallas_call(
        _prompt_cache_kernel,
        grid=(n_batch,),
        in_specs=[pl.BlockSpec((seq, kvw), toks), pl.BlockSpec((seq, kvw), toks), pl.BlockSpec((seq, TILE), toks),
                  pl.BlockSpec((TILE, kvw), meta), pl.BlockSpec((TILE, kvw), meta), pl.BlockSpec((TILE, TILE), meta)],
        out_specs=[pl.BlockSpec((1, kvw, l_seq), out), pl.BlockSpec((1, kvw, l_seq), out),
                   pl.BlockSpec((1, IDX_DIM, l_seq), out)],
        out_shape=[jax.ShapeDtypeStruct((n_batch, kvw, l_seq), F32), jax.ShapeDtypeStruct((n_batch, kvw, l_seq), F32),
                   jax.ShapeDtypeStruct((n_batch, IDX_DIM, l_seq), F32)],
        compiler_params=_cparams(("arbitrary",)),
        name="prompt_cache",
    )(k, v, kiw, k, v, kiw)


def kernel(x_prompt, x_sample, cache_k, cache_v, cache_kidx, state_ssm, state_conv, page_table, meta_tokens,
           g_ffn1, w_ffn1_gate, w_ffn1_up, w_ffn1_down, g_mix, w_in, g_q, g_k, conv_w, conv_b, dt_bias, a_log,
           d_skip, g_ssd_norm, w_attn_out, w_ssd_out, w_o, g_ffn2, w_ffn2_gate, w_ffn2_up, w_ffn2_down):
    depth = w_in.shape[0]
    assert depth == 1, "single-layer step"
    n_batch, seq, d_model = x_prompt.shape
    ns = x_sample.shape[0]
    assert x_sample.shape[1] == 1 and seq % TILE == 0 and ns == TILE
    page = cache_k.shape[2]
    assert page == TILE
    l_seq = seq + N_META
    r_tok = n_batch * seq
    r_total = r_tok + ns + TILE
    meta_blk = (r_tok + ns) // TILE
    n_heads = state_ssm.shape[2]
    d_inner = n_heads * SSD_HEAD_DIM
    assert n_heads <= TILE and n_heads % (2 * SSD_GROUPS) == 0
    cdim = d_inner + 2 * SSD_GROUPS * D_STATE
    nq, nkv = N_HEADS * HEAD_DIM, N_KV_HEADS * HEAD_DIM
    nqi = IDX_HEADS * IDX_DIM
    tm = _row_tile(r_total // TILE, 6)
    tm_tok = _row_tile(r_tok // TILE, 8)

    w = w_in[0]
    offs = [0]
    for n in (nq, nkv, nkv, nqi, IDX_DIM, IDX_HEADS, d_inner, cdim, n_heads, d_model, d_model):
        offs.append(offs[-1] + n)
    col = lambda i: w[:, offs[i]:offs[i + 1]]
    wqk = jnp.concatenate([col(0), col(1)], axis=1).astype(BF16)
    wv = col(2).astype(BF16)
    widx = jnp.concatenate([col(3), col(4), col(5), jnp.zeros((d_model, TILE - IDX_DIM - IDX_HEADS), F32)], axis=1)
    wz = col(6).astype(BF16)
    wx = col(7).astype(BF16)
    wdt = jnp.concatenate([col(8), jnp.zeros((d_model, TILE - n_heads), F32)], axis=1).astype(BF16)
    wgate = jnp.concatenate([col(9), col(10)], axis=1).astype(BF16)
    n_seg = (nq + nkv) // HEAD_DIM
    seg = (jnp.arange(nq + nkv)[:, None] // HEAD_DIM == jnp.arange(TILE)[None, :]).astype(BF16)
    segt = seg.T
    del n_seg
    gqk = jnp.concatenate([jnp.tile(g_q[0], N_HEADS), jnp.tile(g_k[0], N_KV_HEADS)])[None, :]
    row1 = lambda v: v.reshape(1, -1)
    pad_heads = lambda v, fill: jnp.concatenate([v, jnp.full((TILE - n_heads,), fill, F32)])[None, :]
    dtb = pad_heads(dt_bias[0], -1e4)
    alog = pad_heads(a_log[0], 0.0)
    dsk = jnp.repeat(d_skip[0], SSD_HEAD_DIM)[None, :]
    assert 3 * n_heads <= TILE
    e_row = jnp.arange(TILE)[:, None]
    e_mat = ((e_row < 3 * n_heads) & (e_row % n_heads == jnp.arange(d_inner)[None, :] // SSD_HEAD_DIM)).astype(BF16)
    gn = row1(g_ssd_norm[0])

    x_all = jnp.concatenate([x_prompt.reshape(r_tok, d_model), x_sample.reshape(ns, d_model),
                             jnp.zeros((PAD_FRONT, d_model), F32), meta_tokens.astype(F32)], axis=0)

    x1, hb = _ffn(x_all, row1(g_ffn1[0]), row1(g_mix[0]), w_ffn1_gate[0].astype(BF16),
                  w_ffn1_up[0].astype(BF16), w_ffn1_down[0].astype(BF16), tm)
    wih = widx.astype(BF16)
    wil = (widx - wih.astype(F32)).astype(BF16)
    q, k, v, kb, vb, qi, kiw = _proj_attn(x1, row1(g_mix[0]), wqk, wv, wih, wil, seg, segt, gqk, tm)
    xbc, dtr = _proj_xbc(hb, wx, wdt, tm)
    zs, gates = _proj_gate(hb, wz, wgate, tm)

    ksel_p = min(TOP_K_MAX, l_seq // 4)
    attn = _dsa_prompt(q, qi, kiw, kb, vb, n_batch, seq, meta_blk, ksel_p)
    ssd, ssm_p, conv_p = _ssd_prompt(xbc, dtr, zs, conv_w[0], row1(conv_b[0]), dtb, alog, dsk, gn, e_mat,
                                     n_batch, seq, meta_blk, n_heads)

    n_pages = page_table.shape[1]
    ksel_s = min(TOP_K_MAX, (n_pages * page + 1) // 4)
    smp = lambda a: a[r_tok:r_tok + ns]
    kidx_t = jnp.transpose(cache_kidx[0], (0, 2, 1))
    scores = _idx_sample(page_table, smp(qi).reshape(ns, 1, nqi), smp(kiw).reshape(ns, 1, TILE), kidx_t)
    bias = _select_sample(scores.reshape(n_pages + 1, ns, TILE), ksel_s).reshape(n_pages + 1, ns, 1, TILE)
    ck = jnp.transpose(cache_k[0], (0, 2, 3, 1)).reshape(cache_k.shape[1], nkv, page)
    cv = jnp.transpose(cache_v[0], (0, 2, 3, 1)).reshape(cache_v.shape[1], nkv, page)
    attn_s = _attn_sample(page_table, smp(q).reshape(ns, 1, nq), smp(k).reshape(ns, 1, nkv),
                          smp(v).reshape(ns, 1, nkv), bias, ck, cv)
    y_s, ssm_s, conv_s = _ssd_sample(smp(xbc), smp(dtr), state_conv[0].reshape(ns, -1), state_ssm[0].reshape(ns, d_inner, D_STATE),
                                     conv_w[0], row1(conv_b[0]), dtb, alog, dsk, e_mat, n_heads, 8)
    attn, ssd = _fill_sample(attn, ssd, attn_s.reshape(ns, nq), y_s, zs, gn, r_tok // ns)

    wao, wso, wo = w_attn_out[0].astype(BF16), w_ssd_out[0].astype(BF16), w_o[0].astype(BF16)
    ffn2 = (row1(g_ffn2[0]), row1(g_ffn2[0]), w_ffn2_gate[0].astype(BF16), w_ffn2_up[0].astype(BF16),
            w_ffn2_down[0].astype(BF16))
    x2 = _mix(x1, attn, ssd, gates, wao, wso, wo, tm_tok, (0, r_tok))
    y_prompt = _ffn(x2, *ffn2, tm_tok, with_norm=False)[0].reshape(n_batch, seq, d_model)
    x2_s = _mix(x1, attn, ssd, gates, wao, wso, wo, ns, (r_tok, ns))
    y_sample = _ffn(x2_s, *ffn2, ns, with_norm=False)[0].reshape(ns, 1, d_model)

    kt, vt, kit = _prompt_cache(k, v, kiw, n_batch, seq, meta_blk)
    heads_last = lambda a: jnp.transpose(a.reshape(1, n_batch, N_KV_HEADS, HEAD_DIM, l_seq), (0, 1, 4, 2, 3))
    k_prompt = heads_last(kt)
    v_prompt = heads_last(vt)
    kidx_prompt = jnp.transpose(kit, (0, 2, 1))[None]
    ssm_prompt = ssm_p.reshape(1, n_batch, n_heads, SSD_HEAD_DIM, D_STATE)
    conv_prompt = conv_p.reshape(1, n_batch, CONV_W - 1, cdim)
    k_sample = smp(k).reshape(1, ns, 1, N_KV_HEADS, HEAD_DIM)
    v_sample = smp(v).reshape(1, ns, 1, N_KV_HEADS, HEAD_DIM)
    kidx_sample = smp(kiw)[:, :IDX_DIM].reshape(1, ns, 1, IDX_DIM)
    ssm_sample = ssm_s.reshape(1, ns, n_heads, SSD_HEAD_DIM, D_STATE)
    conv_sample = conv_s.reshape(1, ns, CONV_W - 1, cdim)
    return (y_prompt, y_sample, k_prompt, v_prompt, kidx_prompt, ssm_prompt, conv_prompt,
            k_sample, v_sample, kidx_sample, ssm_sample, conv_sample)
```

```python
import functools

import jax
import jax.numpy as jnp
from jax import lax
from jax.experimental import pallas as pl
from jax.experimental.pallas import tpu as pltpu

F32 = jnp.float32
BF16 = jnp.bfloat16
I32 = jnp.int32
HIGHEST = lax.Precision.HIGHEST

EPS = 1e-6
N_META = 16
N_HEADS = 16
N_KV_HEADS = 4
HEAD_DIM = 64
Q_PER_KV = N_HEADS // N_KV_HEADS
IDX_HEADS = 4
IDX_DIM = 64
TOP_K_MAX = 256
SSD_HEAD_DIM = 64
SSD_GROUPS = 4
D_STATE = 128
CONV_W = 4
TILE = 128
PAD_FRONT = TILE - N_META
INT_MIN = -(2 ** 31)
NEG_BIG = -(2.0 ** 100)
Q_SCALE = HEAD_DIM ** -0.5 * 1.4426950408889634
VMEM_LIMIT = 56 * 1024 * 1024


def _cparams(sem, flags=None):
    return pltpu.CompilerParams(dimension_semantics=sem, vmem_limit_bytes=VMEM_LIMIT, flags=flags)


def _const_spec(shape):
    nd = len(shape)
    return pl.BlockSpec(shape, lambda *_: (0,) * nd, pipeline_mode=pl.Buffered(1))


def _rms(x, g):
    return x * lax.rsqrt(jnp.mean(x * x, axis=-1, keepdims=True) + EPS) * g


def _silu(x):
    return x * jax.nn.sigmoid(x)


def _row_tile(n_tiles, cap):
    for k in range(cap, 0, -1):
        if n_tiles % k == 0:
            return k * TILE
    return TILE


def _ffn_kernel(x_ref, g_ref, g2_ref, wg_ref, wu_ref, wd_ref, o_ref, h2_ref=None, *, ck):
    x = x_ref[...]
    h = _rms(x, g_ref[...]).astype(BF16)
    acc = jnp.zeros(x.shape, F32)
    for c in range(wg_ref.shape[1] // ck):
        sl = slice(c * ck, (c + 1) * ck)
        gate = jnp.dot(h, wg_ref[:, sl], preferred_element_type=F32)
        up = jnp.dot(h, wu_ref[:, sl], preferred_element_type=F32)
        act = (_silu(gate) * up).astype(BF16)
        acc = acc + jnp.dot(act, wd_ref[sl, :], preferred_element_type=F32)
    y = x + 0.5 * acc
    o_ref[...] = y
    if h2_ref is not None:
        h2_ref[...] = _rms(y, g2_ref[...]).astype(BF16)


def _ffn(x, g, g2, wg, wu, wd, tm, rows=None, with_norm=True):
    d = x.shape[1]
    r0, r = (0, x.shape[0]) if rows is None else rows
    dff = wg.shape[1]
    blk0 = r0 // tm
    out_specs = [pl.BlockSpec((tm, d), lambda i: (i, 0))]
    out_shape = [jax.ShapeDtypeStruct((r, d), F32)]
    if with_norm:
        out_specs.append(pl.BlockSpec((tm, d), lambda i: (i, 0)))
        out_shape.append(jax.ShapeDtypeStruct((r, d), BF16))
    return pl.pallas_call(
        functools.partial(_ffn_kernel, ck=256),
        grid=(r // tm,),
        in_specs=[pl.BlockSpec((tm, d), lambda i: (blk0 + i, 0)), _const_spec((1, d)), _const_spec((1, d)),
                  _const_spec((d, dff)), _const_spec((d, dff)), _const_spec((dff, d))],
        out_specs=out_specs,
        out_shape=out_shape,
        compiler_params=_cparams(("arbitrary",)),
        name="ffn",
    )(x, g, g2, wg, wu, wd)


def _proj_attn_kernel(x_ref, g_ref, wqk_ref, wv_ref, wih_ref, wil_ref, seg_ref, segt_ref, gqk_ref,
                      q_ref, k_ref, v_ref, kb_ref, vb_ref, qi_ref, kiw_ref):
    h = _rms(x_ref[...], g_ref[...])
    hb, h_lo = _split_bf16(h)
    qk = jnp.dot(hb, wqk_ref[...], preferred_element_type=F32)
    ss = jnp.dot((qk * qk).astype(BF16), seg_ref[...], preferred_element_type=F32)
    r = lax.rsqrt(ss * (1.0 / HEAD_DIM) + EPS)
    r_hi = r.astype(BF16)
    r_lo = (r - r_hi.astype(F32)).astype(BF16)
    rx = (jnp.dot(r_hi, segt_ref[...], preferred_element_type=F32)
          + jnp.dot(r_lo, segt_ref[...], preferred_element_type=F32))
    qkn = qk * rx * gqk_ref[...]
    nq = q_ref.shape[1]
    q_ref[...] = (qkn[:, :nq] * Q_SCALE).astype(BF16)
    k = qkn[:, nq:]
    k_ref[...] = k
    kb_ref[...] = k.astype(BF16)
    v = jnp.dot(hb, wv_ref[...], preferred_element_type=F32)
    v_ref[...] = v
    vb_ref[...] = v.astype(BF16)
    idx = _dot3(hb, h_lo, wih_ref[...], wil_ref[...])
    nqi = qi_ref.shape[1]
    qi_ref[...] = idx[:, :nqi]
    kiw_ref[...] = idx[:, nqi:]


def _proj_attn(x1, g, wqk, wv, wih, wil, seg, segt, gqk, tm):
    r, d = x1.shape
    nqk, nv = wqk.shape[1], wv.shape[1]
    nq = N_HEADS * HEAD_DIM
    nqi = IDX_HEADS * IDX_DIM
    row = lambda i: (i, 0)
    outs = [(nq, BF16), (nqk - nq, F32), (nv, F32), (nqk - nq, BF16), (nv, BF16), (nqi, F32), (TILE, F32)]
    return pl.pallas_call(
        _proj_attn_kernel,
        grid=(r // tm,),
        in_specs=[pl.BlockSpec((tm, d), row), _const_spec((1, d)), _const_spec(wqk.shape),
                  _const_spec(wv.shape), _const_spec(wih.shape), _const_spec(wil.shape), _const_spec(seg.shape),
                  _const_spec(segt.shape), _const_spec(gqk.shape)],
        out_specs=[pl.BlockSpec((tm, n), row) for n, _ in outs],
        out_shape=[jax.ShapeDtypeStruct((r, n), dt) for n, dt in outs],
        compiler_params=_cparams(("arbitrary",)),
        name="proj_attn",
    )(x1, g, wqk, wv, wih, wil, seg, segt, gqk)


def _proj_xbc_kernel(h_ref, wx_ref, wdt_ref, xbc_ref, dt_ref):
    h = h_ref[...]
    xbc_ref[...] = jnp.dot(h, wx_ref[...], preferred_element_type=F32)
    dt_ref[...] = jnp.dot(h, wdt_ref[...], preferred_element_type=F32)


def _proj_xbc(hb, wx, wdt, tm):
    r, d = hb.shape
    row = lambda i: (i, 0)
    return pl.pallas_call(
        _proj_xbc_kernel,
        grid=(r // tm,),
        in_specs=[pl.BlockSpec((tm, d), row), _const_spec(wx.shape), _const_spec(wdt.shape)],
        out_specs=[pl.BlockSpec((tm, wx.shape[1]), row), pl.BlockSpec((tm, wdt.shape[1]), row)],
        out_shape=[jax.ShapeDtypeStruct((r, wx.shape[1]), F32),
                   jax.ShapeDtypeStruct((r, wdt.shape[1]), F32)],
        compiler_params=_cparams(("arbitrary",)),
        name="proj_xbc",
    )(hb, wx, wdt)


def _proj_gate_kernel(h_ref, wz_ref, wg_ref, zs_ref, gt_ref):
    h = h_ref[...]
    zs_ref[...] = _silu(jnp.dot(h, wz_ref[...], preferred_element_type=F32)).astype(BF16)
    gt_ref[...] = jax.nn.sigmoid(jnp.dot(h, wg_ref[...], preferred_element_type=F32)).astype(BF16)


def _proj_gate(hb, wz, wg, tm):
    r, d = hb.shape
    row = lambda i: (i, 0)
    return pl.pallas_call(
        _proj_gate_kernel,
        grid=(r // tm,),
        in_specs=[pl.BlockSpec((tm, d), row), _const_spec(wz.shape), _const_spec(wg.shape)],
        out_specs=[pl.BlockSpec((tm, wz.shape[1]), row), pl.BlockSpec((tm, wg.shape[1]), row)],
        out_shape=[jax.ShapeDtypeStruct((r, wz.shape[1]), BF16),
                   jax.ShapeDtypeStruct((r, wg.shape[1]), BF16)],
        compiler_params=_cparams(("arbitrary",)),
        name="proj_gate",
    )(hb, wz, wg)


KEY_GROUP = 4


def _key_value(key):
    mag = jnp.where(key < 0, jnp.where(key == INT_MIN, 0x7F800000, -key), key)
    v = pltpu.bitcast(mag, F32)
    return jnp.where(key < 0, -v, v)


def _tree_sum(x):
    while x.shape[0] > 1:
        h = x.shape[0] // 2
        x = x[:h] + x[h:]
    return x[0]


def _count(sc_ref, n_groups, pred):
    def body(gi, acc):
        s = sc_ref[pl.ds(gi * KEY_GROUP, KEY_GROUP)]
        hit = jnp.where(pred(s, gi * (KEY_GROUP * TILE)), 1.0, 0.0)
        return acc + _tree_sum(hit.reshape(KEY_GROUP * TILE // 8, 8, TILE))
    part = lax.fori_loop(0, n_groups, body, jnp.zeros((8, TILE), F32))
    return jnp.sum(part, axis=0, keepdims=True)


def _any(flag):
    return jnp.max(jnp.where(flag, 1, 0)) > 0


def _kth_bitwise(sc_ref, n_groups, ksel):
    n_ge0 = _count(sc_ref, n_groups, lambda s, r0: s >= 0.0)
    t = jnp.where(n_ge0 >= ksel, jnp.zeros((1, TILE), I32), jnp.full((1, TILE), INT_MIN, I32))

    def bit_step(i, t):
        cand = t | jnp.left_shift(jnp.int32(1), 30 - i)
        cv = _key_value(cand)
        n = _count(sc_ref, n_groups, lambda s, r0: s >= cv)
        return jnp.where(n >= ksel, cand, t)

    return lax.fori_loop(0, 31, bit_step, t)


def _tie_search(sc_ref, n_groups, tv, need, n_bits):
    shape = (KEY_GROUP, TILE, TILE)
    rows = (lax.broadcasted_iota(I32, shape, 0) * TILE + lax.broadcasted_iota(I32, shape, 1)).astype(F32)

    def bit_step(i, p):
        cand = p | jnp.left_shift(jnp.int32(1), n_bits - 1 - i)
        cf = cand.astype(F32)
        n = _count(sc_ref, n_groups, lambda s, r0: (s == tv) & ((rows + jnp.asarray(r0, F32)) < cf))
        return jnp.where(n < need, cand, p)
    return lax.fori_loop(0, n_bits, bit_step, jnp.zeros((1, TILE), I32)) + 1


def _select_topk(sc_ref, n_groups, ksel, n_bits):
    no_limit = jnp.full((1, TILE), float(1 << n_bits), F32)
    t = _kth_bitwise(sc_ref, n_groups, ksel)
    tv = _key_value(t)
    n_gt = _count(sc_ref, n_groups, lambda s, r0: s > tv)
    n_eq = _count(sc_ref, n_groups, lambda s, r0: s == tv)
    need = ksel - n_gt
    conflict = (t != INT_MIN) & (n_eq > need)
    lim = lax.cond(_any(conflict), lambda: _tie_search(sc_ref, n_groups, tv, need, n_bits).astype(F32),
                   lambda: no_limit)
    return tv, jnp.where(conflict, lim, no_limit)


def _split_bf16(x):
    hi = x.astype(BF16)
    return hi, (x - hi.astype(F32)).astype(BF16)


def _dot3(a_hi, a_lo, b_hi, b_lo):
    return (jnp.dot(a_hi, b_hi, preferred_element_type=F32)
            + jnp.dot(a_hi, b_lo, preferred_element_type=F32)
            + jnp.dot(a_lo, b_hi, preferred_element_type=F32))


VROWS = HEAD_DIM + 16


def _dsa_prompt_kernel(q_ref, qi_ref, kiwq_ref, kb_ref, vb_ref, kiw_ref, kbm_ref, vbm_ref, kiwm_ref, o_ref,
                       sc_ref, kk_ref, vt_ref, kihl_ref, rhs_ref, qa_ref, qb_ref,
                       m_ref, acc_ref, *, ksel, n_bits):
    b = pl.program_id(0)
    j = pl.program_id(1)
    nt = vb_ref.shape[0] // TILE + 1
    n_groups = lax.shift_right_logical(j + KEY_GROUP, KEY_GROUP.bit_length() - 1)
    n_pairs = lax.shift_right_logical(j + 2, 1)
    nkv = N_KV_HEADS
    gw = Q_PER_KV * TILE
    eye = (lax.broadcasted_iota(I32, (TILE, TILE), 0) == lax.broadcasted_iota(I32, (TILE, TILE), 1))

    @pl.when((b == 0) & (j == 0))
    def _():
        rhs_ref[...] = jnp.zeros(rhs_ref.shape, rhs_ref.dtype)
        ident = jnp.where(eye, 1.0, 0.0).astype(BF16)
        for g in range(nkv):
            for hh in range(Q_PER_KV):
                rhs_ref[g, 0:TILE, hh * TILE:(hh + 1) * TILE] = ident
        qa_ref[...] = jnp.zeros(qa_ref.shape, qa_ref.dtype)
        qb_ref[...] = jnp.zeros(qb_ref.shape, qb_ref.dtype)
        kk_ref[...] = jnp.zeros(kk_ref.shape, kk_ref.dtype)
        kihl_ref[...] = jnp.zeros(kihl_ref.shape, kihl_ref.dtype)
        ones_row = jnp.where(lax.broadcasted_iota(I32, (VROWS - HEAD_DIM, TILE), 0) == 0, 1.0, 0.0).astype(BF16)
        for c in range(vt_ref.shape[0]):
            vt_ref[c] = jnp.zeros(vt_ref.shape[1:], vt_ref.dtype)
            for g in range(nkv):
                vt_ref[c, g * VROWS + HEAD_DIM:(g + 1) * VROWS, :] = ones_row

    @pl.when(j == 0)
    def _():
        def stage(c, kb, vb, kiw):
            kk_ref[c] = kb
            vt = vb.astype(F32).T.astype(BF16)
            for g in range(nkv):
                vt_ref[c, g * VROWS:g * VROWS + HEAD_DIM, :] = vt[g * HEAD_DIM:(g + 1) * HEAD_DIM, :]
            hi, lo = _split_bf16(kiw)
            kihl_ref[c] = jnp.concatenate([hi, lo], axis=1)

        stage(0, kbm_ref[...], vbm_ref[...], kiwm_ref[...])

        def prep(c, _):
            rows = pl.ds(pl.multiple_of((c - 1) * TILE, TILE), TILE)
            stage(c, kb_ref[rows, :], vb_ref[rows, :], kiw_ref[rows, :])
            return 0
        lax.fori_loop(1, nt, prep, 0)

    qt = q_ref[...].astype(F32).T.astype(BF16)
    for h in range(N_HEADS):
        g, hh = divmod(h, Q_PER_KV)
        r0 = TILE + (g % 2) * HEAD_DIM
        rhs_ref[g, r0:r0 + HEAD_DIM, hh * TILE:(hh + 1) * TILE] = qt[h * HEAD_DIM:(h + 1) * HEAD_DIM, :]
    qit = qi_ref[...].T
    for h in range(IDX_HEADS):
        hi, lo = _split_bf16(qit[h * IDX_DIM:(h + 1) * IDX_DIM, :])
        qa_ref[0:IDX_DIM, h * TILE:(h + 1) * TILE] = hi
        qa_ref[TILE:TILE + IDX_DIM, h * TILE:(h + 1) * TILE] = hi
        qb_ref[0:IDX_DIM, h * TILE:(h + 1) * TILE] = lo
    wq = kiwq_ref[...].T[IDX_DIM:IDX_DIM + 8, :]
    wq = wq * (IDX_HEADS ** -0.5 * IDX_DIM ** -0.5)

    grp_rows = KEY_GROUP * TILE
    q_pos = j * TILE + lax.broadcasted_iota(I32, (grp_rows, TILE), 1)
    k_row = lax.broadcasted_iota(I32, (grp_rows, TILE), 0)

    def score_group(gi, _):
        khl = kihl_ref[pl.ds(gi * KEY_GROUP, KEY_GROUP)].reshape(grp_rows, 2 * TILE)
        d = (jnp.dot(khl, qa_ref[...], preferred_element_type=F32)
             + jnp.dot(khl, qb_ref[...], preferred_element_type=F32))
        s = jnp.zeros((grp_rows, TILE), F32)
        for h in range(IDX_HEADS):
            s = s + wq[h:h + 1, :] * jnp.maximum(d[:, h * TILE:(h + 1) * TILE], 0.0)
        k_pos = k_row + gi * grp_rows
        valid = (k_pos <= q_pos) & (k_pos >= PAD_FRONT)
        sc_ref[pl.ds(gi * KEY_GROUP, KEY_GROUP)] = jnp.where(valid, s, jnp.nan).reshape(KEY_GROUP, TILE, TILE)
        return 0
    lax.fori_loop(0, n_groups, score_group, 0)

    tv, lim = _select_topk(sc_ref, n_groups, ksel, n_bits)

    m_ref[...] = jnp.full(m_ref.shape, NEG_BIG, F32)
    acc_ref[...] = jnp.zeros(acc_ref.shape, F32)

    pair_shape = (2, TILE, TILE)
    pair_row = (lax.broadcasted_iota(I32, pair_shape, 0) * TILE
                + lax.broadcasted_iota(I32, pair_shape, 1)).astype(F32)

    def attend_pair(i, _):
        sc = sc_ref[pl.ds(2 * i, 2)]
        sel = (sc > tv) | ((sc == tv) & ((pair_row + (i * (2 * TILE)).astype(F32)) < lim))
        bias = jnp.where(sel, 0.0, NEG_BIG).astype(BF16).reshape(2 * TILE, TILE)
        kc = kk_ref[pl.ds(2 * i, 2)].reshape(2 * TILE, nkv * HEAD_DIM)
        vt = jnp.concatenate([vt_ref[2 * i], vt_ref[2 * i + 1]], axis=1)
        m_all = m_ref[...]
        s = []
        for g in range(nkv):
            half = (g // 2) * TILE
            lhs = jnp.concatenate([bias, kc[:, half:half + TILE]], axis=1)
            s.append(jnp.dot(lhs, rhs_ref[g], preferred_element_type=F32))
        m_new = []
        for g in range(nkv):
            m_g = jnp.maximum(m_all[g:g + 1, :], jnp.max(s[g], axis=0, keepdims=True))
            alpha = jnp.exp2(m_all[g:g + 1, :] - m_g)
            p = jnp.exp2(s[g] - m_g).astype(BF16)
            pv = jnp.dot(vt[g * VROWS:(g + 1) * VROWS, :], p, preferred_element_type=F32)
            acc_ref[g] = alpha * acc_ref[g] + pv
            m_new.append(m_g)
        m_ref[...] = jnp.concatenate(m_new + [m_all[nkv:, :]], axis=0)
        return 0
    lax.fori_loop(0, n_pairs, attend_pair, 0)

    pieces = []
    for g in range(nkv):
        acc = acc_ref[g]
        o = acc[:HEAD_DIM] / acc[HEAD_DIM:HEAD_DIM + 1]
        for hh in range(Q_PER_KV):
            pieces.append(o[:, hh * TILE:(hh + 1) * TILE])
    o_ref[...] = jnp.concatenate(pieces, axis=0).T.astype(BF16)


def _seq_tile_map(seq, meta_blk, per_seq=False):
    n_tok = seq // TILE
    if per_seq:
        return lambda b, j: (jnp.where(j == 0, meta_blk + 1 + b, b * n_tok + j - 1), 0)
    return lambda b, j: (jnp.where(j == 0, meta_blk, b * n_tok + j - 1), 0)


def _dsa_prompt(q, qi, kiw, kb, vb, n_batch, seq, meta_blk, ksel):
    lp = seq + TILE
    nt = lp // TILE
    nt_pad = -(-nt // KEY_GROUP) * KEY_GROUP
    r_total = q.shape[0]
    qw = N_HEADS * HEAD_DIM
    kvw = N_KV_HEADS * HEAD_DIM
    gw = Q_PER_KV * TILE
    tile = _seq_tile_map(seq, meta_blk)
    toks = lambda b, j: (b, 0)
    meta = lambda b, j: (meta_blk, 0)
    n_bits = max(1, (lp - 1).bit_length())
    return pl.pallas_call(
        functools.partial(_dsa_prompt_kernel, ksel=ksel, n_bits=n_bits),
        grid=(n_batch, nt),
        in_specs=[pl.BlockSpec((TILE, qw), tile), pl.BlockSpec((TILE, IDX_HEADS * IDX_DIM), tile),
                  pl.BlockSpec((TILE, TILE), tile), pl.BlockSpec((seq, kvw), toks),
                  pl.BlockSpec((seq, kvw), toks), pl.BlockSpec((seq, TILE), toks),
                  pl.BlockSpec((TILE, kvw), meta), pl.BlockSpec((TILE, kvw), meta),
                  pl.BlockSpec((TILE, TILE), meta)],
        out_specs=pl.BlockSpec((TILE, qw), _seq_tile_map(seq, meta_blk, per_seq=True)),
        out_shape=jax.ShapeDtypeStruct((r_total + n_batch * TILE, qw), BF16),
        scratch_shapes=[pltpu.VMEM((nt_pad, TILE, TILE), F32),
                        pltpu.VMEM((nt_pad, TILE, kvw), BF16),
                        pltpu.VMEM((nt_pad, N_KV_HEADS * VROWS, TILE), BF16),
                        pltpu.VMEM((nt_pad, TILE, 2 * TILE), BF16),
                        pltpu.VMEM((N_KV_HEADS, 2 * TILE, gw), BF16),
                        pltpu.VMEM((2 * TILE, IDX_HEADS * TILE), BF16),
                        pltpu.VMEM((2 * TILE, IDX_HEADS * TILE), BF16),
                        pltpu.VMEM((8, gw), F32),
                        pltpu.VMEM((N_KV_HEADS, VROWS, gw), F32)],
        compiler_params=_cparams(("arbitrary", "arbitrary")),
        name="dsa_prompt",
    )(q, qi, kiw, kb, vb, kiw, kb, vb, kiw)


def _expand_heads(v, e_ref):
    n_heads = e_ref.shape[1] // SSD_HEAD_DIM
    lane = lax.broadcasted_iota(I32, (1, v.shape[1]), 1)
    v = jnp.where(lane < n_heads, v, 0.0)
    hi = v.astype(BF16).astype(F32)
    r1 = v - hi
    mid = r1.astype(BF16).astype(F32)
    lo = (r1 - mid).astype(BF16).astype(F32)
    packed = (hi + pltpu.roll(mid, n_heads, 1) + pltpu.roll(lo, 2 * n_heads, 1)).astype(BF16)
    return jnp.dot(packed, e_ref[...], preferred_element_type=F32)


def _softplus(x):
    return jnp.maximum(x, 0.0) + jnp.log1p(jnp.exp(-jnp.abs(x)))


def _ssd_prompt_kernel(xbc_ref, dtr_ref, zs_ref, cw_ref, cb_ref, dtb_ref, alog_ref, dsk_ref, gn_ref,
                       e_ref, y_ref, ssm_ref, conv_ref, xp_ref, ht_ref, *, n_heads):
    c = pl.program_id(1)
    nc = pl.num_programs(1)
    d_inner = n_heads * SSD_HEAD_DIM
    hpg = n_heads // SSD_GROUPS
    gwid = hpg * SSD_HEAD_DIM
    gn_w = d_inner // SSD_GROUPS

    @pl.when(c == 0)
    def _():
        xp_ref[0:8, :] = jnp.zeros((8, xp_ref.shape[1]), F32)
        ht_ref[...] = jnp.zeros(ht_ref.shape, F32)

    xp_ref[8:8 + TILE, :] = xbc_ref[...]
    u = cb_ref[...] + cw_ref[CONV_W - 1:CONV_W, :] * xp_ref[8:8 + TILE, :]
    for k in range(CONV_W - 1):
        off = 8 - (CONV_W - 1) + k
        u = u + cw_ref[k:k + 1, :] * xp_ref[off:off + TILE, :]
    tail = xp_ref[8 + TILE - (CONV_W - 1):8 + TILE, :]
    xp_ref[8 - (CONV_W - 1):8, :] = tail

    row = lax.broadcasted_iota(I32, (TILE, 1), 0)
    live = jnp.where((c > 0) | (row >= PAD_FRONT), 1.0, 0.0)
    act = _silu(u) * live
    xs = act[:, :d_inner]
    bm = act[:, d_inner:d_inner + SSD_GROUPS * D_STATE]
    cm = act[:, d_inner + SSD_GROUPS * D_STATE:]

    dt = _softplus(dtr_ref[...] + dtb_ref[...]) * live
    a = dt * (-jnp.exp(alog_ref[...]))
    ri = lax.broadcasted_iota(I32, (TILE, TILE), 0)
    ci = lax.broadcasted_iota(I32, (TILE, TILE), 1)
    tril = ri >= ci
    acs = jnp.dot(jnp.where(tril, 1.0, 0.0), a, preferred_element_type=F32, precision=HIGHEST)
    acs_t = acs.T
    dt_t = dt.T
    acs_last = acs[TILE - 1:TILE, :]
    w_in = _expand_heads(jnp.exp(acs_last - acs) * dt, e_ref)
    w_out = _expand_heads(jnp.exp(acs), e_ref)
    xw = (xs * w_in).astype(BF16)
    xsb = xs.astype(BF16)
    first_head = lax.broadcasted_iota(I32, (1, 2 * SSD_HEAD_DIM), 1) < SSD_HEAD_DIM

    ys = []
    for g in range(SSD_GROUPS):
        bg = bm[:, g * D_STATE:(g + 1) * D_STATE]
        cg = cm[:, g * D_STATE:(g + 1) * D_STATE].astype(BF16)
        bgt = bg.T.astype(BF16)
        cb = jnp.dot(cg, bgt, preferred_element_type=F32)
        xw_g = xw[:, g * gwid:(g + 1) * gwid]
        xs_g = xsb[:, g * gwid:(g + 1) * gwid]
        h_prev = ht_ref[g]
        y_off = jnp.dot(cg, h_prev.astype(BF16), preferred_element_type=F32)
        y_g = y_off * w_out[:, g * gwid:(g + 1) * gwid]
        pairs = []
        for e0 in range(0, hpg, 2):
            x_pair = xs_g[:, e0 * SSD_HEAD_DIM:(e0 + 2) * SSD_HEAD_DIM]
            prods = []
            for e in (e0, e0 + 1):
                he = g * hpg + e
                seg = acs[:, he:he + 1] - acs_t[he:he + 1, :]
                m = (cb * jnp.exp(jnp.where(tril, seg, -jnp.inf)) * dt_t[he:he + 1, :]).astype(BF16)
                prods.append(jnp.dot(m, x_pair, preferred_element_type=F32))
            pairs.append(jnp.where(first_head, prods[0], prods[1]))
        y_g = y_g + jnp.concatenate(pairs, axis=1)
        states = jnp.dot(bgt, xw_g, preferred_element_type=F32)
        ht_ref[g] = h_prev * w_out[TILE - 1:TILE, g * gwid:(g + 1) * gwid] + states
        ys.append(y_g)
    y = jnp.concatenate(ys, axis=1) + dsk_ref[...] * xs
    y = y * zs_ref[...].astype(F32)
    outs = []
    for g in range(SSD_GROUPS):
        yg = y[:, g * gn_w:(g + 1) * gn_w]
        outs.append(yg * lax.rsqrt(jnp.mean(yg * yg, axis=-1, keepdims=True) + EPS))
    y_ref[...] = (jnp.concatenate(outs, axis=1) * gn_ref[...]).astype(BF16)

    @pl.when(c == nc - 1)
    def _():
        for g in range(SSD_GROUPS):
            ssm_ref[0, g * gwid:(g + 1) * gwid, :] = ht_ref[g].T
        conv_ref[0] = tail


def _ssd_prompt(xbc, dtr, zs, cw, cb, dtb, alog, dsk, gn, e_mat, n_batch, seq, meta_blk, n_heads):
    nt = seq // TILE + 1
    r_total = xbc.shape[0]
    d_inner = n_heads * SSD_HEAD_DIM
    cdim = xbc.shape[1]
    tile = _seq_tile_map(seq, meta_blk)
    fix = lambda b, c: (0, 0)
    return pl.pallas_call(
        functools.partial(_ssd_prompt_kernel, n_heads=n_heads),
        grid=(n_batch, nt),
        in_specs=[pl.BlockSpec((TILE, cdim), tile), pl.BlockSpec((TILE, TILE), tile),
                  pl.BlockSpec((TILE, d_inner), tile),
                  pl.BlockSpec(cw.shape, fix), pl.BlockSpec(cb.shape, fix), pl.BlockSpec(dtb.shape, fix),
                  pl.BlockSpec(alog.shape, fix), pl.BlockSpec(dsk.shape, fix), pl.BlockSpec(gn.shape, fix),
                  pl.BlockSpec(e_mat.shape, fix)],
        out_specs=[pl.BlockSpec((TILE, d_inner), _seq_tile_map(seq, meta_blk, per_seq=True)),
                   pl.BlockSpec((1, d_inner, D_STATE), lambda b, c: (b, 0, 0)),
                   pl.BlockSpec((1, CONV_W - 1, cdim), lambda b, c: (b, 0, 0))],
        out_shape=[jax.ShapeDtypeStruct((r_total + n_batch * TILE, d_inner), BF16),
                   jax.ShapeDtypeStruct((n_batch, d_inner, D_STATE), F32),
                   jax.ShapeDtypeStruct((n_batch, CONV_W - 1, cdim), F32)],
        scratch_shapes=[pltpu.VMEM((8 + TILE, cdim), F32),
                        pltpu.VMEM((SSD_GROUPS, D_STATE, d_inner // SSD_GROUPS), F32)],
        compiler_params=_cparams(("arbitrary", "arbitrary")),
        name="ssd_prompt",
    )(xbc, dtr, zs, cw, cb, dtb, alog, dsk, gn, e_mat)


def _idx_sample_kernel(pt_ref, qi_ref, kiw_ref, *rest, n_pages):
    pages, s_ref = rest[:n_pages], rest[n_pages]
    qi = qi_ref[0]
    kiw = kiw_ref[0]
    q8 = jnp.concatenate([qi[:, h * IDX_DIM:(h + 1) * IDX_DIM] for h in range(IDX_HEADS)]
                         + [jnp.zeros((8 - IDX_HEADS, IDX_DIM), F32)], axis=0)
    w_col = jnp.broadcast_to(kiw, (TILE, TILE)).T[IDX_DIM:IDX_DIM + 8, 0:1]
    w_col = w_col * (IDX_HEADS ** -0.5 * IDX_DIM ** -0.5)
    q_hi, q_lo = _split_bf16(q8)
    q3 = jnp.concatenate([q_hi, q_lo, q_hi], axis=1)
    for p in range(n_pages):
        k_hi, k_lo = _split_bf16(pages[p][0])
        d = jnp.dot(q3, jnp.concatenate([k_hi, k_hi, k_lo], axis=0), preferred_element_type=F32)
        s_ref[p, 0] = jnp.sum(w_col * jnp.maximum(d, 0.0), axis=0, keepdims=True)
    d_new = jnp.sum(q8 * kiw[:, :IDX_DIM], axis=1, keepdims=True)
    s_new = jnp.sum(w_col * jnp.maximum(d_new, 0.0), axis=0, keepdims=True)
    lane = lax.broadcasted_iota(I32, (1, TILE), 1)
    s_ref[n_pages, 0] = jnp.where(lane == 0, s_new, 0.0)


def _idx_sample(page_table, qi_s, kiw_s, cache_kidx):
    ns, n_pages = page_table.shape
    n_slots = n_pages + 1
    page = lambda p: pl.BlockSpec((1, IDX_DIM, TILE), lambda s, pt: (pt[s, p], 0, 0))
    one = lambda s, pt: (s, 0, 0)
    grid_spec = pltpu.PrefetchScalarGridSpec(
        num_scalar_prefetch=1, grid=(ns,),
        in_specs=[pl.BlockSpec((1, 1, qi_s.shape[2]), one), pl.BlockSpec((1, 1, TILE), one)]
        + [page(p) for p in range(n_pages)],
        out_specs=pl.BlockSpec((n_slots, 1, 1, TILE), lambda s, pt: (0, s, 0, 0)))
    return pl.pallas_call(
        functools.partial(_idx_sample_kernel, n_pages=n_pages),
        grid_spec=grid_spec,
        out_shape=jax.ShapeDtypeStruct((n_slots, ns, 1, TILE), F32),
        compiler_params=_cparams(("arbitrary",)),
        name="idx_sample",
    )(page_table, qi_s, kiw_s, *([cache_kidx] * n_pages))


def _select_sample_kernel(s_ref, bias_ref, sc_ref, *, n_pages, ksel, n_bits):
    n_slots = n_pages + 1
    rows = lax.broadcasted_iota(I32, (TILE, TILE), 0)
    for p in range(n_slots):
        sc = s_ref[p].T
        if p == n_pages:
            sc = jnp.where(rows == 0, sc, jnp.nan)
        sc_ref[p] = sc
    for p in range(n_slots, sc_ref.shape[0]):
        sc_ref[p] = jnp.full((TILE, TILE), jnp.nan, F32)
    n_groups = sc_ref.shape[0] // KEY_GROUP
    tv, lim = _select_topk(sc_ref, n_groups, ksel, n_bits)
    rows_f = rows.astype(F32)
    for p in range(n_slots):
        sc = sc_ref[p]
        sel = (sc > tv) | ((sc == tv) & ((rows_f + float(p * TILE)) < lim))
        bias_ref[p] = jnp.where(sel, 0.0, NEG_BIG).T


def _select_sample(scores, ksel):
    n_slots, ns, _ = scores.shape
    n_pages = n_slots - 1
    n_bits = (n_slots * TILE - 1).bit_length()
    return pl.pallas_call(
        functools.partial(_select_sample_kernel, n_pages=n_pages, ksel=ksel, n_bits=n_bits),
        out_shape=jax.ShapeDtypeStruct((n_slots, ns, TILE), F32),
        scratch_shapes=[pltpu.VMEM((-(-n_slots // KEY_GROUP) * KEY_GROUP, TILE, TILE), F32)],
        compiler_params=pltpu.CompilerParams(vmem_limit_bytes=VMEM_LIMIT),
        name="select_sample",
    )(scores)


def _attn_sample_kernel(pt_ref, q_ref, kn_ref, vn_ref, bias_ref, *rest, n_pages):
    kpages, vpages, o_ref = rest[:n_pages], rest[n_pages:2 * n_pages], rest[2 * n_pages]
    kvw = N_KV_HEADS * HEAD_DIM
    q = q_ref[0].astype(F32)
    lane = lax.broadcasted_iota(I32, (N_HEADS, kvw), 1)
    head = lax.broadcasted_iota(I32, (N_HEADS, kvw), 0)
    own = jnp.right_shift(lane, 6) == jnp.right_shift(head, 2)
    qrows = []
    for h in range(N_HEADS):
        qh = q[:, h * HEAD_DIM:(h + 1) * HEAD_DIM]
        qrows.append(jnp.concatenate([qh] * N_KV_HEADS, axis=1))
    qf = jnp.where(own, jnp.concatenate(qrows, axis=0), 0.0)
    qw = qf.astype(BF16)
    nt = (((1,), (1,)), ((), ()))
    scores = []
    for p in range(n_pages):
        kp = kpages[p][0].astype(BF16)
        s = jnp.dot(qw, kp, preferred_element_type=F32)
        scores.append(s + bias_ref[p, 0])
    s_new = jnp.sum(qf * kn_ref[0], axis=1, keepdims=True)
    lane1 = lax.broadcasted_iota(I32, (1, TILE), 1)
    scores.append(jnp.where(lane1 == 0, s_new, NEG_BIG) + bias_ref[n_pages, 0])
    m = scores[0].max(axis=1, keepdims=True)
    for s in scores[1:]:
        m = jnp.maximum(m, s.max(axis=1, keepdims=True))
    den = jnp.zeros((N_HEADS, 1), F32)
    out = jnp.zeros((N_HEADS, kvw), F32)
    for p in range(n_pages):
        e = jnp.exp2(scores[p] - m)
        den = den + e.sum(axis=1, keepdims=True)
        out = out + lax.dot_general(e.astype(BF16), vpages[p][0].astype(BF16), nt,
                                    preferred_element_type=F32)
    e = jnp.exp2(scores[n_pages] - m)
    den = den + e.sum(axis=1, keepdims=True)
    out = out + e[:, 0:1] * vn_ref[0]
    out = out / den
    rolled = jnp.concatenate([pltpu.roll(out[:, i * TILE:(i + 1) * TILE], HEAD_DIM, 1)
                              for i in range(kvw // TILE)], axis=1)
    half = lax.broadcasted_iota(I32, (1, TILE), 1) < HEAD_DIM
    pieces = []
    for m2 in range(N_HEADS // 2):
        g = (2 * m2) // Q_PER_KV
        blk = (g // 2) * TILE
        a_src, b_src = (rolled, out) if g % 2 == 1 else (out, rolled)
        a = a_src[2 * m2:2 * m2 + 1, blk:blk + TILE]
        b = b_src[2 * m2 + 1:2 * m2 + 2, blk:blk + TILE]
        pieces.append(jnp.where(half, a, b))
    o_ref[0] = jnp.concatenate(pieces, axis=1)


def _attn_sample(page_table, q_s, k_s, v_s, bias, cache_k, cache_v):
    ns, n_pages = page_table.shape
    kvw = N_KV_HEADS * HEAD_DIM
    qw = N_HEADS * HEAD_DIM
    page = lambda p: pl.BlockSpec((1, kvw, TILE), lambda s, pt: (pt[s, p], 0, 0))
    one = lambda s, pt: (s, 0, 0)
    grid_spec = pltpu.PrefetchScalarGridSpec(
        num_scalar_prefetch=1, grid=(ns,),
        in_specs=[pl.BlockSpec((1, 1, qw), one), pl.BlockSpec((1, 1, kvw), one),
                  pl.BlockSpec((1, 1, kvw), one),
                  pl.BlockSpec((n_pages + 1, 1, 1, TILE), lambda s, pt: (0, s, 0, 0))]
        + [page(p) for p in range(n_pages)] * 2,
        out_specs=pl.BlockSpec((1, 1, qw), one))
    return pl.pallas_call(
        functools.partial(_attn_sample_kernel, n_pages=n_pages),
        grid_spec=grid_spec,
        out_shape=jax.ShapeDtypeStruct((ns, 1, qw), F32),
        compiler_params=_cparams(("arbitrary",)),
        name="attn_sample",
    )(page_table, q_s, k_s, v_s, bias, *([cache_k] * n_pages), *([cache_v] * n_pages))


def _ssd_sample_kernel(xbc_ref, dtr_ref, sc_ref, h0_ref, cw_ref, cb_ref, dtb_ref, alog_ref, dsk_ref,
                       e_ref, y_ref, h_ref, conv_ref, *, n_heads, sb):
    d_inner = n_heads * SSD_HEAD_DIM
    hpg = n_heads // SSD_GROUPS
    new = xbc_ref[...]
    cdim = new.shape[1]
    u = cb_ref[...] + cw_ref[CONV_W - 1:CONV_W, :] * new
    for k in range(CONV_W - 1):
        u = u + cw_ref[k:k + 1, :] * sc_ref[:, k * cdim:(k + 1) * cdim]
    for k in range(CONV_W - 2):
        conv_ref[:, k * cdim:(k + 1) * cdim] = sc_ref[:, (k + 1) * cdim:(k + 2) * cdim]
    conv_ref[:, (CONV_W - 2) * cdim:] = new
    act = _silu(u)
    xs = act[:, :d_inner]
    bm = act[:, d_inner:d_inner + SSD_GROUPS * D_STATE]
    cm = act[:, d_inner + SSD_GROUPS * D_STATE:]
    dt = _softplus(dtr_ref[...] + dtb_ref[...])
    decay = jnp.exp(dt * (-jnp.exp(alog_ref[...])))
    dec_x = _expand_heads(decay, e_ref)
    xdt = xs * _expand_heads(dt, e_ref)
    for i in range(sb):
        bcols, ccols = [], []
        for g in range(SSD_GROUPS):
            brow = bm[i:i + 1, g * D_STATE:(g + 1) * D_STATE]
            crow = cm[i:i + 1, g * D_STATE:(g + 1) * D_STATE]
            bcols.append(jnp.broadcast_to(brow, (TILE, D_STATE)).T)
            ccols.append(jnp.broadcast_to(crow, (TILE, D_STATE)).T)
        pieces = []
        for pr in range(n_heads // 2):
            g = (2 * pr) // hpg
            sl = slice(pr * TILE, (pr + 1) * TILE)
            ht = h0_ref[i, sl, :].T
            hn = ht * dec_x[i:i + 1, sl] + bcols[g] * xdt[i:i + 1, sl]
            pieces.append(jnp.sum(hn * ccols[g], axis=0, keepdims=True))
            h_ref[i, sl, :] = hn.T
        y_ref[i:i + 1, :] = jnp.concatenate(pieces, axis=1) + dsk_ref[...] * xs[i:i + 1, :]


def _ssd_sample(xbc_s, dtr_s, state_conv, state_ssm, cw, cb, dtb, alog, dsk, e_mat, n_heads, sb):
    ns, cdim = xbc_s.shape
    d_inner = n_heads * SSD_HEAD_DIM
    row = lambda i: (i, 0)
    row3 = lambda i: (i, 0, 0)
    fix = lambda i: (0, 0)
    return pl.pallas_call(
        functools.partial(_ssd_sample_kernel, n_heads=n_heads, sb=sb),
        grid=(ns // sb,),
        in_specs=[pl.BlockSpec((sb, cdim), row), pl.BlockSpec((sb, TILE), row),
                  pl.BlockSpec((sb, (CONV_W - 1) * cdim), row), pl.BlockSpec((sb, d_inner, D_STATE), row3),
                  pl.BlockSpec(cw.shape, fix), pl.BlockSpec(cb.shape, fix), pl.BlockSpec(dtb.shape, fix),
                  pl.BlockSpec(alog.shape, fix), pl.BlockSpec(dsk.shape, fix), pl.BlockSpec(e_mat.shape, fix)],
        out_specs=[pl.BlockSpec((sb, d_inner), row), pl.BlockSpec((sb, d_inner, D_STATE), row3),
                   pl.BlockSpec((sb, (CONV_W - 1) * cdim), row)],
        out_shape=[jax.ShapeDtypeStruct((ns, d_inner), F32),
                   jax.ShapeDtypeStruct((ns, d_inner, D_STATE), F32),
                   jax.ShapeDtypeStruct((ns, (CONV_W - 1) * cdim), F32)],
        compiler_params=_cparams(("arbitrary",)),
        name="ssd_sample",
    )(xbc_s, dtr_s, state_conv, state_ssm, cw, cb, dtb, alog, dsk, e_mat)


def _fill_sample_kernel(attn_any, ssd_any, a_ref, y_ref, zs_ref, gn_ref, attn_ref, ssd_ref):
    del attn_any, ssd_any
    attn_ref[...] = a_ref[...].astype(BF16)
    y = y_ref[...] * zs_ref[...].astype(F32)
    gn_w = y.shape[1] // SSD_GROUPS
    outs = []
    for g in range(SSD_GROUPS):
        yg = y[:, g * gn_w:(g + 1) * gn_w]
        outs.append(yg * lax.rsqrt(jnp.mean(yg * yg, axis=-1, keepdims=True) + EPS))
    ssd_ref[...] = (jnp.concatenate(outs, axis=1) * gn_ref[...]).astype(BF16)


def _fill_sample(attn, ssd, attn_s, y_s, zs, gn, blk):
    ns = attn_s.shape[0]
    last = lambda i: (blk, 0)
    fix = lambda i: (0, 0)
    return pl.pallas_call(
        _fill_sample_kernel,
        grid=(1,),
        in_specs=[pl.BlockSpec(memory_space=pl.ANY), pl.BlockSpec(memory_space=pl.ANY),
                  pl.BlockSpec(attn_s.shape, fix), pl.BlockSpec(y_s.shape, fix),
                  pl.BlockSpec((ns, zs.shape[1]), last), pl.BlockSpec(gn.shape, fix)],
        out_specs=[pl.BlockSpec((ns, attn.shape[1]), last), pl.BlockSpec((ns, ssd.shape[1]), last)],
        out_shape=[jax.ShapeDtypeStruct(attn.shape, attn.dtype), jax.ShapeDtypeStruct(ssd.shape, ssd.dtype)],
        input_output_aliases={0: 0, 1: 1},
        compiler_params=_cparams(("arbitrary",)),
        name="fill_sample",
    )(attn, ssd, attn_s, y_s, zs, gn)


def _mix_kernel(x_ref, a_ref, s_ref, gt_ref, wao_ref, wso_ref, wo_ref, o_ref):
    d = x_ref.shape[1]
    gt = gt_ref[...].astype(F32)
    ao = jnp.dot(a_ref[...], wao_ref[...], preferred_element_type=F32)
    so = jnp.dot(s_ref[...], wso_ref[...], preferred_element_type=F32)
    mixed = (gt[:, :d] * ao + gt[:, d:] * so).astype(BF16)
    o_ref[...] = x_ref[...] + jnp.dot(mixed, wo_ref[...], preferred_element_type=F32)


def _mix(x1, attn, ssd, gates, wao, wso, wo, tm, rows):
    d = x1.shape[1]
    r0, r = rows
    blk0 = r0 // tm
    row = lambda i: (blk0 + i, 0)
    return pl.pallas_call(
        _mix_kernel,
        grid=(r // tm,),
        in_specs=[pl.BlockSpec((tm, d), row), pl.BlockSpec((tm, attn.shape[1]), row),
                  pl.BlockSpec((tm, ssd.shape[1]), row), pl.BlockSpec((tm, gates.shape[1]), row),
                  _const_spec(wao.shape), _const_spec(wso.shape), _const_spec(wo.shape)],
        out_specs=pl.BlockSpec((tm, d), lambda i: (i, 0)),
        out_shape=jax.ShapeDtypeStruct((r, d), F32),
        compiler_params=_cparams(("arbitrary",)),
        name="mix",
    )(x1, attn, ssd, gates, wao, wso, wo)


def _split_weight_kernel(wt_ref, hi_ref, lo_ref, *, n_cols):
    i = pl.program_id(0)
    row = i * TILE + lax.broadcasted_iota(I32, (TILE, 1), 0)
    w = jnp.where(row < n_cols, wt_ref[...], 0.0)
    hi, lo = _split_bf16(w.T)
    hi_ref[...] = hi
    lo_ref[...] = lo


def _split_weight(wt):
    n_cols, d = wt.shape
    n_blk = pl.cdiv(n_cols, TILE)
    return pl.pallas_call(
        functools.partial(_split_weight_kernel, n_cols=n_cols),
        grid=(n_blk,),
        in_specs=[pl.BlockSpec((TILE, d), lambda i: (i, 0))],
        out_specs=[pl.BlockSpec((d, TILE), lambda i: (0, i)), pl.BlockSpec((d, TILE), lambda i: (0, i))],
        out_shape=[jax.ShapeDtypeStruct((d, n_blk * TILE), BF16), jax.ShapeDtypeStruct((d, n_blk * TILE), BF16)],
        compiler_params=_cparams(("arbitrary",)),
        name="split_weight",
    )(wt)


def _prompt_cache_kernel(k_ref, v_ref, kiw_ref, km_ref, vm_ref, kiwm_ref, kt_ref, vt_ref, kit_ref):
    n_tiles = k_ref.shape[0] // TILE
    for src, meta, dst, width in ((k_ref, km_ref, kt_ref, k_ref.shape[1]), (v_ref, vm_ref, vt_ref, v_ref.shape[1]),
                                  (kiw_ref, kiwm_ref, kit_ref, IDX_DIM)):
        dst[0, :, 0:N_META] = meta[...].T[:width, PAD_FRONT:]
        for j in range(n_tiles):
            dst[0, :, N_META + j * TILE:N_META + (j + 1) * TILE] = src[j * TILE:(j + 1) * TILE, :].T[:width, :]


def _prompt_cache(k, v, kiw, n_batch, seq, meta_blk):
    l_seq = seq + N_META
    kvw = k.shape[1]
    toks = lambda b: (b, 0)
    meta = lambda b: (meta_blk, 0)
    out = lambda b: (b, 0, 0)
    return pl.pallas_call(
        _prompt_cache_kernel,
        grid=(n_batch,),
        in_specs=[pl.BlockSpec((seq, kvw), toks), pl.BlockSpec((seq, kvw), toks), pl.BlockSpec((seq, TILE), toks),
                  pl.BlockSpec((TILE, kvw), meta), pl.BlockSpec((TILE, kvw), meta), pl.BlockSpec((TILE, TILE), meta)],
        out_specs=[pl.BlockSpec((1, kvw, l_seq), out), pl.BlockSpec((1, kvw, l_seq), out),
                   pl.BlockSpec((1, IDX_DIM, l_seq), out)],
        out_shape=[jax.ShapeDtypeStruct((n_batch, kvw, l_seq), F32), jax.ShapeDtypeStruct((n_batch, kvw, l_seq), F32),
                   jax.ShapeDtypeStruct((n_batch, IDX_DIM, l_seq), F32)],
        compiler_params=_cparams(("arbitrary",)),
        name="prompt_cache",
    )(k, v, kiw, k, v, kiw)


def kernel(x_prompt, x_sample, cache_k, cache_v, cache_kidx, state_ssm, state_conv, page_table, meta_tokens,
           g_ffn1, w_ffn1_gate, w_ffn1_up, w_ffn1_down, g_mix, w_in, g_q, g_k, conv_w, conv_b, dt_bias, a_log,
           d_skip, g_ssd_norm, w_attn_out, w_ssd_out, w_o, g_ffn2, w_ffn2_gate, w_ffn2_up, w_ffn2_down):
    depth = w_in.shape[0]
    assert depth == 1, "single-layer step"
    n_batch, seq, d_model = x_prompt.shape
    ns = x_sample.shape[0]
    assert x_sample.shape[1] == 1 and seq % TILE == 0 and ns == TILE
    page = cache_k.shape[2]
    assert page == TILE
    l_seq = seq + N_META
    r_tok = n_batch * seq
    r_total = r_tok + ns + TILE
    meta_blk = (r_tok + ns) // TILE
    n_heads = state_ssm.shape[2]
    d_inner = n_heads * SSD_HEAD_DIM
    assert n_heads <= TILE and n_heads % (2 * SSD_GROUPS) == 0
    cdim = d_inner + 2 * SSD_GROUPS * D_STATE
    nq, nkv = N_HEADS * HEAD_DIM, N_KV_HEADS * HEAD_DIM
    nqi = IDX_HEADS * IDX_DIM
    tm = _row_tile(r_total // TILE, 6)
    tm_tok = _row_tile(r_tok // TILE, 8)

    w_hi, w_lo = _split_weight(jnp.transpose(w_in[0]))
    offs = [0]
    for n in (nq, nkv, nkv, nqi, IDX_DIM, IDX_HEADS, d_inner, cdim, n_heads, d_model, d_model):
        offs.append(offs[-1] + n)
    col = lambda i, w=w_hi: w[:, offs[i]:offs[i + 1]]
    wqk = jnp.concatenate([col(0), col(1)], axis=1)
    wv = col(2)
    idx_cols = lambda w: jnp.concatenate([col(3, w), col(4, w), col(5, w),
                                          jnp.zeros((d_model, TILE - IDX_DIM - IDX_HEADS), BF16)], axis=1)
    wih, wil = idx_cols(w_hi), idx_cols(w_lo)
    wz = col(6)
    wx = col(7)
    wdt = jnp.concatenate([col(8), jnp.zeros((d_model, TILE - n_heads), BF16)], axis=1)
    wgate = jnp.concatenate([col(9), col(10)], axis=1)
    n_seg = (nq + nkv) // HEAD_DIM
    seg = (jnp.arange(nq + nkv)[:, None] // HEAD_DIM == jnp.arange(TILE)[None, :]).astype(BF16)
    segt = seg.T
    del n_seg
    gqk = jnp.concatenate([jnp.tile(g_q[0], N_HEADS), jnp.tile(g_k[0], N_KV_HEADS)])[None, :]
    row1 = lambda v: v.reshape(1, -1)
    pad_heads = lambda v, fill: jnp.concatenate([v, jnp.full((TILE - n_heads,), fill, F32)])[None, :]
    dtb = pad_heads(dt_bias[0], -1e4)
    alog = pad_heads(a_log[0], 0.0)
    dsk = jnp.repeat(d_skip[0], SSD_HEAD_DIM)[None, :]
    assert 3 * n_heads <= TILE
    e_row = jnp.arange(TILE)[:, None]
    e_mat = ((e_row < 3 * n_heads) & (e_row % n_heads == jnp.arange(d_inner)[None, :] // SSD_HEAD_DIM)).astype(BF16)
    gn = row1(g_ssd_norm[0])

    x_all = jnp.concatenate([x_prompt.reshape(r_tok, d_model), x_sample.reshape(ns, d_model),
                             jnp.zeros((PAD_FRONT, d_model), F32), meta_tokens.astype(F32)], axis=0)

    x1, hb = _ffn(x_all, row1(g_ffn1[0]), row1(g_mix[0]), w_ffn1_gate[0].astype(BF16),
                  w_ffn1_up[0].astype(BF16), w_ffn1_down[0].astype(BF16), tm)
    q, k, v, kb, vb, qi, kiw = _proj_attn(x1, row1(g_mix[0]), wqk, wv, wih, wil, seg, segt, gqk, tm)
    xbc, dtr = _proj_xbc(hb, wx, wdt, tm)
    zs, gates = _proj_gate(hb, wz, wgate, tm)

    ksel_p = min(TOP_K_MAX, l_seq // 4)
    attn = _dsa_prompt(q, qi, kiw, kb, vb, n_batch, seq, meta_blk, ksel_p)
    ssd, ssm_p, conv_p = _ssd_prompt(xbc, dtr, zs, conv_w[0], row1(conv_b[0]), dtb, alog, dsk, gn, e_mat,
                                     n_batch, seq, meta_blk, n_heads)

    n_pages = page_table.shape[1]
    ksel_s = min(TOP_K_MAX, (n_pages * page + 1) // 4)
    smp = lambda a: a[r_tok:r_tok + ns]
    kidx_t = jnp.transpose(cache_kidx[0], (0, 2, 1))
    scores = _idx_sample(page_table, smp(qi).reshape(ns, 1, nqi), smp(kiw).reshape(ns, 1, TILE), kidx_t)
    bias = _select_sample(scores.reshape(n_pages + 1, ns, TILE), ksel_s).reshape(n_pages + 1, ns, 1, TILE)
    ck = jnp.transpose(cache_k[0], (0, 2, 3, 1)).reshape(cache_k.shape[1], nkv, page)
    cv = jnp.transpose(cache_v[0], (0, 2, 3, 1)).reshape(cache_v.shape[1], nkv, page)
    attn_s = _attn_sample(page_table, smp(q).reshape(ns, 1, nq), smp(k).reshape(ns, 1, nkv),
                          smp(v).reshape(ns, 1, nkv), bias, ck, cv)
    y_s, ssm_s, conv_s = _ssd_sample(smp(xbc), smp(dtr), state_conv[0].reshape(ns, -1), state_ssm[0].reshape(ns, d_inner, D_STATE),
                                     conv_w[0], row1(conv_b[0]), dtb, alog, dsk, e_mat, n_heads, 8)
    attn, ssd = _fill_sample(attn, ssd, attn_s.reshape(ns, nq), y_s, zs, gn, r_tok // ns)

    wao, wso, wo = w_attn_out[0].astype(BF16), w_ssd_out[0].astype(BF16), w_o[0].astype(BF16)
    ffn2 = (row1(g_ffn2[0]), row1(g_ffn2[0]), w_ffn2_gate[0].astype(BF16), w_ffn2_up[0].astype(BF16),
            w_ffn2_down[0].astype(BF16))
    x2 = _mix(x1, attn, ssd, gates, wao, wso, wo, tm_tok, (0, r_tok))
    y_prompt = _ffn(x2, *ffn2, tm_tok, with_norm=False)[0].reshape(n_batch, seq, d_model)
    x2_s = _mix(x1, attn, ssd, gates, wao, wso, wo, ns, (r_tok, ns))
    y_sample = _ffn(x2_s, *ffn2, ns, with_norm=False)[0].reshape(ns, 1, d_model)

    kt, vt, kit = _prompt_cache(k, v, kiw, n_batch, seq, meta_blk)
    heads_last = lambda a: jnp.transpose(a.reshape(1, n_batch, N_KV_HEADS, HEAD_DIM, l_seq), (0, 1, 4, 2, 3))
    k_prompt = heads_last(kt)
    v_prompt = heads_last(vt)
    kidx_prompt = jnp.transpose(kit, (0, 2, 1))[None]
    ssm_prompt = ssm_p.reshape(1, n_batch, n_heads, SSD_HEAD_DIM, D_STATE)
    conv_prompt = conv_p.reshape(1, n_batch, CONV_W - 1, cdim)
    k_sample = smp(k).reshape(1, ns, 1, N_KV_HEADS, HEAD_DIM)
    v_sample = smp(v).reshape(1, ns, 1, N_KV_HEADS, HEAD_DIM)
    kidx_sample = smp(kiw)[:, :IDX_DIM].reshape(1, ns, 1, IDX_DIM)
    ssm_sample = ssm_s.reshape(1, ns, n_heads, SSD_HEAD_DIM, D_STATE)
    conv_sample = conv_s.reshape(1, ns, CONV_W - 1, cdim)
    return (y_prompt, y_sample, k_prompt, v_prompt, kidx_prompt, ssm_prompt, conv_prompt,
            k_sample, v_sample, kidx_sample, ssm_sample, conv_sample)
```

```python
import functools

import jax
import jax.numpy as jnp
from jax import lax
from jax.experimental import pallas as pl
from jax.experimental.pallas import tpu as pltpu

F32 = jnp.float32
BF16 = jnp.bfloat16
I32 = jnp.int32
HIGHEST = lax.Precision.HIGHEST

EPS = 1e-6
N_META = 16
N_HEADS = 16
N_KV_HEADS = 4
HEAD_DIM = 64
Q_PER_KV = N_HEADS // N_KV_HEADS
IDX_HEADS = 4
IDX_DIM = 64
TOP_K_MAX = 256
SSD_HEAD_DIM = 64
SSD_GROUPS = 4
D_STATE = 128
CONV_W = 4
TILE = 128
PAD_FRONT = TILE - N_META
INT_MIN = -(2 ** 31)
NEG_BIG = -(2.0 ** 100)
Q_SCALE = HEAD_DIM ** -0.5 * 1.4426950408889634
VMEM_LIMIT = 56 * 1024 * 1024


def _cparams(sem, flags=None):
    return pltpu.CompilerParams(dimension_semantics=sem, vmem_limit_bytes=VMEM_LIMIT, flags=flags)


def _const_spec(shape):
    nd = len(shape)
    return pl.BlockSpec(shape, lambda *_: (0,) * nd, pipeline_mode=pl.Buffered(1))


def _rms(x, g):
    return x * lax.rsqrt(jnp.mean(x * x, axis=-1, keepdims=True) + EPS) * g


def _silu(x):
    return x * jax.nn.sigmoid(x)


def _row_tile(n_tiles, cap):
    for k in range(cap, 0, -1):
        if n_tiles % k == 0:
            return k * TILE
    return TILE


def _ffn_kernel(*refs, ck, n_into):
    x_ref, g_ref, g2_ref, wg_ref, wu_ref, wd_ref, o_ref = refs[n_into:n_into + 7]
    h2_ref = refs[n_into + 7] if len(refs) > n_into + 7 else None
    x = x_ref[...]
    h = _rms(x, g_ref[...]).astype(BF16)
    acc = jnp.zeros(x.shape, F32)
    for c in range(wg_ref.shape[1] // ck):
        sl = slice(c * ck, (c + 1) * ck)
        gate = jnp.dot(h, wg_ref[:, sl], preferred_element_type=F32)
        up = jnp.dot(h, wu_ref[:, sl], preferred_element_type=F32)
        act = (_silu(gate) * up).astype(BF16)
        acc = acc + jnp.dot(act, wd_ref[sl, :], preferred_element_type=F32)
    y = x + 0.5 * acc
    o_ref[...] = y
    if h2_ref is not None:
        h2_ref[...] = _rms(y, g2_ref[...]).astype(BF16)


def _ffn(x, g, g2, wg, wu, wd, tm, with_norm=True, out_rows=None, out_row0=0, into=()):
    r, d = x.shape
    out_rows = r if out_rows is None else out_rows
    dff = wg.shape[1]
    blk0 = out_row0 // tm
    out_specs = [pl.BlockSpec((tm, d), lambda i: (blk0 + i, 0))]
    out_shape = [jax.ShapeDtypeStruct((out_rows, d), F32)]
    if with_norm:
        out_specs.append(pl.BlockSpec((tm, d), lambda i: (blk0 + i, 0)))
        out_shape.append(jax.ShapeDtypeStruct((out_rows, d), BF16))
    return pl.pallas_call(
        functools.partial(_ffn_kernel, ck=256, n_into=len(into)),
        grid=(r // tm,),
        in_specs=[pl.BlockSpec(memory_space=pl.ANY)] * len(into)
        + [pl.BlockSpec((tm, d), lambda i: (i, 0)), _const_spec((1, d)), _const_spec((1, d)),
           _const_spec((d, dff)), _const_spec((d, dff)), _const_spec((dff, d))],
        out_specs=out_specs,
        out_shape=out_shape,
        input_output_aliases={i: i for i in range(len(into))},
        compiler_params=_cparams(("arbitrary",)),
        name="ffn",
    )(*into, x, g, g2, wg, wu, wd)


def _proj_attn_kernel(x_ref, g_ref, wqk_ref, wv_ref, wih_ref, wil_ref, seg_ref, segt_ref, gqk_ref,
                      q_ref, k_ref, v_ref, kb_ref, vb_ref, qi_ref, kiw_ref):
    h = _rms(x_ref[...], g_ref[...])
    hb, h_lo = _split_bf16(h)
    qk = jnp.dot(hb, wqk_ref[...], preferred_element_type=F32)
    ss = jnp.dot((qk * qk).astype(BF16), seg_ref[...], preferred_element_type=F32)
    r = lax.rsqrt(ss * (1.0 / HEAD_DIM) + EPS)
    r_hi = r.astype(BF16)
    r_lo = (r - r_hi.astype(F32)).astype(BF16)
    rx = (jnp.dot(r_hi, segt_ref[...], preferred_element_type=F32)
          + jnp.dot(r_lo, segt_ref[...], preferred_element_type=F32))
    qkn = qk * rx * gqk_ref[...]
    nq = q_ref.shape[1]
    q_ref[...] = (qkn[:, :nq] * Q_SCALE).astype(BF16)
    k = qkn[:, nq:]
    k_ref[...] = k
    kb_ref[...] = k.astype(BF16)
    v = jnp.dot(hb, wv_ref[...], preferred_element_type=F32)
    v_ref[...] = v
    vb_ref[...] = v.astype(BF16)
    idx = _dot3(hb, h_lo, wih_ref[...], wil_ref[...])
    nqi = qi_ref.shape[1]
    qi_ref[...] = idx[:, :nqi]
    kiw_ref[...] = idx[:, nqi:]


def _proj_attn(x1, g, wqk, wv, wih, wil, seg, segt, gqk, tm):
    r, d = x1.shape
    nqk, nv = wqk.shape[1], wv.shape[1]
    nq = N_HEADS * HEAD_DIM
    nqi = IDX_HEADS * IDX_DIM
    row = lambda i: (i, 0)
    outs = [(nq, BF16), (nqk - nq, F32), (nv, F32), (nqk - nq, BF16), (nv, BF16), (nqi, F32), (TILE, F32)]
    return pl.pallas_call(
        _proj_attn_kernel,
        grid=(r // tm,),
        in_specs=[pl.BlockSpec((tm, d), row), _const_spec((1, d)), _const_spec(wqk.shape),
                  _const_spec(wv.shape), _const_spec(wih.shape), _const_spec(wil.shape), _const_spec(seg.shape),
                  _const_spec(segt.shape), _const_spec(gqk.shape)],
        out_specs=[pl.BlockSpec((tm, n), row) for n, _ in outs],
        out_shape=[jax.ShapeDtypeStruct((r, n), dt) for n, dt in outs],
        compiler_params=_cparams(("arbitrary",)),
        name="proj_attn",
    )(x1, g, wqk, wv, wih, wil, seg, segt, gqk)


def _proj_xbc_kernel(h_ref, wx_ref, wdt_ref, xbc_ref, dt_ref):
    h = h_ref[...]
    xbc_ref[...] = jnp.dot(h, wx_ref[...], preferred_element_type=F32)
    dt_ref[...] = jnp.dot(h, wdt_ref[...], preferred_element_type=F32)


def _proj_xbc(hb, wx, wdt, tm):
    r, d = hb.shape
    row = lambda i: (i, 0)
    return pl.pallas_call(
        _proj_xbc_kernel,
        grid=(r // tm,),
        in_specs=[pl.BlockSpec((tm, d), row), _const_spec(wx.shape), _const_spec(wdt.shape)],
        out_specs=[pl.BlockSpec((tm, wx.shape[1]), row), pl.BlockSpec((tm, wdt.shape[1]), row)],
        out_shape=[jax.ShapeDtypeStruct((r, wx.shape[1]), F32),
                   jax.ShapeDtypeStruct((r, wdt.shape[1]), F32)],
        compiler_params=_cparams(("arbitrary",)),
        name="proj_xbc",
    )(hb, wx, wdt)


def _proj_gate_kernel(h_ref, wz_ref, wg_ref, zs_ref, gt_ref):
    h = h_ref[...]
    zs_ref[...] = _silu(jnp.dot(h, wz_ref[...], preferred_element_type=F32)).astype(BF16)
    gt_ref[...] = jax.nn.sigmoid(jnp.dot(h, wg_ref[...], preferred_element_type=F32)).astype(BF16)


def _proj_gate(hb, wz, wg, tm):
    r, d = hb.shape
    row = lambda i: (i, 0)
    return pl.pallas_call(
        _proj_gate_kernel,
        grid=(r // tm,),
        in_specs=[pl.BlockSpec((tm, d), row), _const_spec(wz.shape), _const_spec(wg.shape)],
        out_specs=[pl.BlockSpec((tm, wz.shape[1]), row), pl.BlockSpec((tm, wg.shape[1]), row)],
        out_shape=[jax.ShapeDtypeStruct((r, wz.shape[1]), BF16),
                   jax.ShapeDtypeStruct((r, wg.shape[1]), BF16)],
        compiler_params=_cparams(("arbitrary",)),
        name="proj_gate",
    )(hb, wz, wg)


KEY_GROUP = 4


def _key_value(key):
    mag = jnp.where(key < 0, jnp.where(key == INT_MIN, 0x7F800000, -key), key)
    v = pltpu.bitcast(mag, F32)
    return jnp.where(key < 0, -v, v)


def _tree_sum(x):
    while x.shape[0] > 1:
        h = x.shape[0] // 2
        x = x[:h] + x[h:]
    return x[0]


def _count(sc_ref, n_groups, pred):
    def body(gi, acc):
        s = sc_ref[pl.ds(gi * KEY_GROUP, KEY_GROUP)]
        hit = jnp.where(pred(s, gi * (KEY_GROUP * TILE)), 1.0, 0.0)
        return acc + _tree_sum(hit.reshape(KEY_GROUP * TILE // 8, 8, TILE))
    part = lax.fori_loop(0, n_groups, body, jnp.zeros((8, TILE), F32))
    return jnp.sum(part, axis=0, keepdims=True)


def _any(flag):
    return jnp.max(jnp.where(flag, 1, 0)) > 0


def _kth_bitwise(sc_ref, n_groups, ksel):
    n_ge0 = _count(sc_ref, n_groups, lambda s, r0: s >= 0.0)
    t = jnp.where(n_ge0 >= ksel, jnp.zeros((1, TILE), I32), jnp.full((1, TILE), INT_MIN, I32))

    def bit_step(i, t):
        cand = t | jnp.left_shift(jnp.int32(1), 30 - i)
        cv = _key_value(cand)
        n = _count(sc_ref, n_groups, lambda s, r0: s >= cv)
        return jnp.where(n >= ksel, cand, t)

    return lax.fori_loop(0, 31, bit_step, t)


def _tie_search(sc_ref, n_groups, tv, need, n_bits):
    shape = (KEY_GROUP, TILE, TILE)
    rows = (lax.broadcasted_iota(I32, shape, 0) * TILE + lax.broadcasted_iota(I32, shape, 1)).astype(F32)

    def bit_step(i, p):
        cand = p | jnp.left_shift(jnp.int32(1), n_bits - 1 - i)
        cf = cand.astype(F32)
        n = _count(sc_ref, n_groups, lambda s, r0: (s == tv) & ((rows + jnp.asarray(r0, F32)) < cf))
        return jnp.where(n < need, cand, p)
    return lax.fori_loop(0, n_bits, bit_step, jnp.zeros((1, TILE), I32)) + 1


def _select_topk(sc_ref, n_groups, ksel, n_bits):
    no_limit = jnp.full((1, TILE), float(1 << n_bits), F32)
    t = _kth_bitwise(sc_ref, n_groups, ksel)
    tv = _key_value(t)
    n_gt = _count(sc_ref, n_groups, lambda s, r0: s > tv)
    n_eq = _count(sc_ref, n_groups, lambda s, r0: s == tv)
    need = ksel - n_gt
    conflict = (t != INT_MIN) & (n_eq > need)
    lim = lax.cond(_any(conflict), lambda: _tie_search(sc_ref, n_groups, tv, need, n_bits).astype(F32),
                   lambda: no_limit)
    return tv, jnp.where(conflict, lim, no_limit)


def _split_bf16(x):
    hi = x.astype(BF16)
    return hi, (x - hi.astype(F32)).astype(BF16)


def _dot3(a_hi, a_lo, b_hi, b_lo):
    return (jnp.dot(a_hi, b_hi, preferred_element_type=F32)
            + jnp.dot(a_hi, b_lo, preferred_element_type=F32)
            + jnp.dot(a_lo, b_hi, preferred_element_type=F32))


VROWS = HEAD_DIM + 16


def _dsa_prompt_kernel(q_ref, qi_ref, kiwq_ref, kb_ref, vb_ref, kiw_ref, kbm_ref, vbm_ref, kiwm_ref, o_ref,
                       sc_ref, kk_ref, vt_ref, kihl_ref, rhs_ref, qa_ref, qb_ref,
                       m_ref, acc_ref, *, ksel, n_bits):
    b = pl.program_id(0)
    j = pl.program_id(1)
    nt = vb_ref.shape[0] // TILE + 1
    n_groups = lax.shift_right_logical(j + KEY_GROUP, KEY_GROUP.bit_length() - 1)
    n_pairs = lax.shift_right_logical(j + 2, 1)
    nkv = N_KV_HEADS
    gw = Q_PER_KV * TILE
    eye = (lax.broadcasted_iota(I32, (TILE, TILE), 0) == lax.broadcasted_iota(I32, (TILE, TILE), 1))

    @pl.when((b == 0) & (j == 0))
    def _():
        rhs_ref[...] = jnp.zeros(rhs_ref.shape, rhs_ref.dtype)
        ident = jnp.where(eye, 1.0, 0.0).astype(BF16)
        for g in range(nkv):
            for hh in range(Q_PER_KV):
                rhs_ref[g, 0:TILE, hh * TILE:(hh + 1) * TILE] = ident
        qa_ref[...] = jnp.zeros(qa_ref.shape, qa_ref.dtype)
        qb_ref[...] = jnp.zeros(qb_ref.shape, qb_ref.dtype)
        kk_ref[...] = jnp.zeros(kk_ref.shape, kk_ref.dtype)
        kihl_ref[...] = jnp.zeros(kihl_ref.shape, kihl_ref.dtype)
        ones_row = jnp.where(lax.broadcasted_iota(I32, (VROWS - HEAD_DIM, TILE), 0) == 0, 1.0, 0.0).astype(BF16)
        for c in range(vt_ref.shape[0]):
            vt_ref[c] = jnp.zeros(vt_ref.shape[1:], vt_ref.dtype)
            for g in range(nkv):
                vt_ref[c, g * VROWS + HEAD_DIM:(g + 1) * VROWS, :] = ones_row

    @pl.when(j == 0)
    def _():
        def stage(c, kb, vb, kiw):
            kk_ref[c] = kb
            vt = vb.astype(F32).T.astype(BF16)
            for g in range(nkv):
                vt_ref[c, g * VROWS:g * VROWS + HEAD_DIM, :] = vt[g * HEAD_DIM:(g + 1) * HEAD_DIM, :]
            hi, lo = _split_bf16(kiw)
            kihl_ref[c] = jnp.concatenate([hi, lo], axis=1)

        stage(0, kbm_ref[...], vbm_ref[...], kiwm_ref[...])

        def prep(c, _):
            rows = pl.ds(pl.multiple_of((c - 1) * TILE, TILE), TILE)
            stage(c, kb_ref[rows, :], vb_ref[rows, :], kiw_ref[rows, :])
            return 0
        lax.fori_loop(1, nt, prep, 0)

    qt = q_ref[...].astype(F32).T.astype(BF16)
    for h in range(N_HEADS):
        g, hh = divmod(h, Q_PER_KV)
        r0 = TILE + (g % 2) * HEAD_DIM
        rhs_ref[g, r0:r0 + HEAD_DIM, hh * TILE:(hh + 1) * TILE] = qt[h * HEAD_DIM:(h + 1) * HEAD_DIM, :]
    qit = qi_ref[...].T
    for h in range(IDX_HEADS):
        hi, lo = _split_bf16(qit[h * IDX_DIM:(h + 1) * IDX_DIM, :])
        qa_ref[0:IDX_DIM, h * TILE:(h + 1) * TILE] = hi
        qa_ref[TILE:TILE + IDX_DIM, h * TILE:(h + 1) * TILE] = hi
        qb_ref[0:IDX_DIM, h * TILE:(h + 1) * TILE] = lo
    wq = kiwq_ref[...].T[IDX_DIM:IDX_DIM + 8, :]
    wq = wq * (IDX_HEADS ** -0.5 * IDX_DIM ** -0.5)

    grp_rows = KEY_GROUP * TILE
    q_pos = j * TILE + lax.broadcasted_iota(I32, (grp_rows, TILE), 1)
    k_row = lax.broadcasted_iota(I32, (grp_rows, TILE), 0)

    def score_group(gi, _):
        khl = kihl_ref[pl.ds(gi * KEY_GROUP, KEY_GROUP)].reshape(grp_rows, 2 * TILE)
        d = (jnp.dot(khl, qa_ref[...], preferred_element_type=F32)
             + jnp.dot(khl, qb_ref[...], preferred_element_type=F32))
        s = jnp.zeros((grp_rows, TILE), F32)
        for h in range(IDX_HEADS):
            s = s + wq[h:h + 1, :] * jnp.maximum(d[:, h * TILE:(h + 1) * TILE], 0.0)
        k_pos = k_row + gi * grp_rows
        valid = (k_pos <= q_pos) & (k_pos >= PAD_FRONT)
        sc_ref[pl.ds(gi * KEY_GROUP, KEY_GROUP)] = jnp.where(valid, s, jnp.nan).reshape(KEY_GROUP, TILE, TILE)
        return 0
    lax.fori_loop(0, n_groups, score_group, 0)

    tv, lim = _select_topk(sc_ref, n_groups, ksel, n_bits)

    m_ref[...] = jnp.full(m_ref.shape, NEG_BIG, F32)
    acc_ref[...] = jnp.zeros(acc_ref.shape, F32)

    pair_shape = (2, TILE, TILE)
    pair_row = (lax.broadcasted_iota(I32, pair_shape, 0) * TILE
                + lax.broadcasted_iota(I32, pair_shape, 1)).astype(F32)

    def attend_pair(i, _):
        sc = sc_ref[pl.ds(2 * i, 2)]
        sel = (sc > tv) | ((sc == tv) & ((pair_row + (i * (2 * TILE)).astype(F32)) < lim))
        bias = jnp.where(sel, 0.0, NEG_BIG).astype(BF16).reshape(2 * TILE, TILE)
        kc = kk_ref[pl.ds(2 * i, 2)].reshape(2 * TILE, nkv * HEAD_DIM)
        vt = jnp.concatenate([vt_ref[2 * i], vt_ref[2 * i + 1]], axis=1)
        m_all = m_ref[...]
        s = []
        for g in range(nkv):
            half = (g // 2) * TILE
            lhs = jnp.concatenate([bias, kc[:, half:half + TILE]], axis=1)
            s.append(jnp.dot(lhs, rhs_ref[g], preferred_element_type=F32))
        m_new = []
        for g in range(nkv):
            m_g = jnp.maximum(m_all[g:g + 1, :], jnp.max(s[g], axis=0, keepdims=True))
            alpha = jnp.exp2(m_all[g:g + 1, :] - m_g)
            p = jnp.exp2(s[g] - m_g).astype(BF16)
            pv = jnp.dot(vt[g * VROWS:(g + 1) * VROWS, :], p, preferred_element_type=F32)
            acc_ref[g] = alpha * acc_ref[g] + pv
            m_new.append(m_g)
        m_ref[...] = jnp.concatenate(m_new + [m_all[nkv:, :]], axis=0)
        return 0
    lax.fori_loop(0, n_pairs, attend_pair, 0)

    pieces = []
    for g in range(nkv):
        acc = acc_ref[g]
        o = acc[:HEAD_DIM] / acc[HEAD_DIM:HEAD_DIM + 1]
        for hh in range(Q_PER_KV):
            pieces.append(o[:, hh * TILE:(hh + 1) * TILE])
    o_ref[...] = jnp.concatenate(pieces, axis=0).T.astype(BF16)


def _seq_tile_map(seq, meta_blk, per_seq=False):
    n_tok = seq // TILE
    if per_seq:
        return lambda b, j: (jnp.where(j == 0, meta_blk + 1 + b, b * n_tok + j - 1), 0)
    return lambda b, j: (jnp.where(j == 0, meta_blk, b * n_tok + j - 1), 0)


def _dsa_prompt(q, qi, kiw, kb, vb, n_batch, seq, meta_blk, ksel):
    lp = seq + TILE
    nt = lp // TILE
    nt_pad = -(-nt // KEY_GROUP) * KEY_GROUP
    r_total = q.shape[0]
    qw = N_HEADS * HEAD_DIM
    kvw = N_KV_HEADS * HEAD_DIM
    gw = Q_PER_KV * TILE
    tile = _seq_tile_map(seq, meta_blk)
    toks = lambda b, j: (b, 0)
    meta = lambda b, j: (meta_blk, 0)
    n_bits = max(1, (lp - 1).bit_length())
    return pl.pallas_call(
        functools.partial(_dsa_prompt_kernel, ksel=ksel, n_bits=n_bits),
        grid=(n_batch, nt),
        in_specs=[pl.BlockSpec((TILE, qw), tile), pl.BlockSpec((TILE, IDX_HEADS * IDX_DIM), tile),
                  pl.BlockSpec((TILE, TILE), tile), pl.BlockSpec((seq, kvw), toks),
                  pl.BlockSpec((seq, kvw), toks), pl.BlockSpec((seq, TILE), toks),
                  pl.BlockSpec((TILE, kvw), meta), pl.BlockSpec((TILE, kvw), meta),
                  pl.BlockSpec((TILE, TILE), meta)],
        out_specs=pl.BlockSpec((TILE, qw), _seq_tile_map(seq, meta_blk, per_seq=True)),
        out_shape=jax.ShapeDtypeStruct((r_total + n_batch * TILE, qw), BF16),
        scratch_shapes=[pltpu.VMEM((nt_pad, TILE, TILE), F32),
                        pltpu.VMEM((nt_pad, TILE, kvw), BF16),
                        pltpu.VMEM((nt_pad, N_KV_HEADS * VROWS, TILE), BF16),
                        pltpu.VMEM((nt_pad, TILE, 2 * TILE), BF16),
                        pltpu.VMEM((N_KV_HEADS, 2 * TILE, gw), BF16),
                        pltpu.VMEM((2 * TILE, IDX_HEADS * TILE), BF16),
                        pltpu.VMEM((2 * TILE, IDX_HEADS * TILE), BF16),
                        pltpu.VMEM((8, gw), F32),
                        pltpu.VMEM((N_KV_HEADS, VROWS, gw), F32)],
        compiler_params=_cparams(("arbitrary", "arbitrary")),
        name="dsa_prompt",
    )(q, qi, kiw, kb, vb, kiw, kb, vb, kiw)


def _expand_heads(v, e_ref):
    n_heads = e_ref.shape[1] // SSD_HEAD_DIM
    lane = lax.broadcasted_iota(I32, (1, v.shape[1]), 1)
    v = jnp.where(lane < n_heads, v, 0.0)
    hi = v.astype(BF16).astype(F32)
    r1 = v - hi
    mid = r1.astype(BF16).astype(F32)
    lo = (r1 - mid).astype(BF16).astype(F32)
    packed = (hi + pltpu.roll(mid, n_heads, 1) + pltpu.roll(lo, 2 * n_heads, 1)).astype(BF16)
    return jnp.dot(packed, e_ref[...], preferred_element_type=F32)


def _softplus(x):
    return jnp.maximum(x, 0.0) + jnp.log1p(jnp.exp(-jnp.abs(x)))


def _ssd_prompt_kernel(xbc_ref, dtr_ref, zs_ref, cw_ref, cb_ref, dtb_ref, alog_ref, dsk_ref, gn_ref,
                       e_ref, y_ref, ssm_ref, conv_ref, xp_ref, ht_ref, *, n_heads):
    c = pl.program_id(1)
    nc = pl.num_programs(1)
    d_inner = n_heads * SSD_HEAD_DIM
    hpg = n_heads // SSD_GROUPS
    gwid = hpg * SSD_HEAD_DIM
    gn_w = d_inner // SSD_GROUPS

    @pl.when(c == 0)
    def _():
        xp_ref[0:8, :] = jnp.zeros((8, xp_ref.shape[1]), F32)
        ht_ref[...] = jnp.zeros(ht_ref.shape, F32)

    xp_ref[8:8 + TILE, :] = xbc_ref[...]
    u = cb_ref[...] + cw_ref[CONV_W - 1:CONV_W, :] * xp_ref[8:8 + TILE, :]
    for k in range(CONV_W - 1):
        off = 8 - (CONV_W - 1) + k
        u = u + cw_ref[k:k + 1, :] * xp_ref[off:off + TILE, :]
    tail = xp_ref[8 + TILE - (CONV_W - 1):8 + TILE, :]
    xp_ref[8 - (CONV_W - 1):8, :] = tail

    row = lax.broadcasted_iota(I32, (TILE, 1), 0)
    live = jnp.where((c > 0) | (row >= PAD_FRONT), 1.0, 0.0)
    act = _silu(u) * live
    xs = act[:, :d_inner]
    bm = act[:, d_inner:d_inner + SSD_GROUPS * D_STATE]
    cm = act[:, d_inner + SSD_GROUPS * D_STATE:]

    dt = _softplus(dtr_ref[...] + dtb_ref[...]) * live
    a = dt * (-jnp.exp(alog_ref[...]))
    ri = lax.broadcasted_iota(I32, (TILE, TILE), 0)
    ci = lax.broadcasted_iota(I32, (TILE, TILE), 1)
    tril = ri >= ci
    acs = jnp.dot(jnp.where(tril, 1.0, 0.0), a, preferred_element_type=F32, precision=HIGHEST)
    acs_t = acs.T
    dt_t = dt.T
    acs_last = acs[TILE - 1:TILE, :]
    w_in = _expand_heads(jnp.exp(acs_last - acs) * dt, e_ref)
    w_out = _expand_heads(jnp.exp(acs), e_ref)
    xw = (xs * w_in).astype(BF16)
    xsb = xs.astype(BF16)
    first_head = lax.broadcasted_iota(I32, (1, 2 * SSD_HEAD_DIM), 1) < SSD_HEAD_DIM

    ys = []
    for g in range(SSD_GROUPS):
        bg = bm[:, g * D_STATE:(g + 1) * D_STATE]
        cg = cm[:, g * D_STATE:(g + 1) * D_STATE].astype(BF16)
        bgt = bg.T.astype(BF16)
        cb = jnp.dot(cg, bgt, preferred_element_type=F32)
        xw_g = xw[:, g * gwid:(g + 1) * gwid]
        xs_g = xsb[:, g * gwid:(g + 1) * gwid]
        h_prev = ht_ref[g]
        y_off = jnp.dot(cg, h_prev.astype(BF16), preferred_element_type=F32)
        y_g = y_off * w_out[:, g * gwid:(g + 1) * gwid]
        pairs = []
        for e0 in range(0, hpg, 2):
            x_pair = xs_g[:, e0 * SSD_HEAD_DIM:(e0 + 2) * SSD_HEAD_DIM]
            prods = []
            for e in (e0, e0 + 1):
                he = g * hpg + e
                seg = acs[:, he:he + 1] - acs_t[he:he + 1, :]
                m = (cb * jnp.exp(jnp.where(tril, seg, -jnp.inf)) * dt_t[he:he + 1, :]).astype(BF16)
                prods.append(jnp.dot(m, x_pair, preferred_element_type=F32))
            pairs.append(jnp.where(first_head, prods[0], prods[1]))
        y_g = y_g + jnp.concatenate(pairs, axis=1)
        states = jnp.dot(bgt, xw_g, preferred_element_type=F32)
        ht_ref[g] = h_prev * w_out[TILE - 1:TILE, g * gwid:(g + 1) * gwid] + states
        ys.append(y_g)
    y = jnp.concatenate(ys, axis=1) + dsk_ref[...] * xs
    y = y * zs_ref[...].astype(F32)
    outs = []
    for g in range(SSD_GROUPS):
        yg = y[:, g * gn_w:(g + 1) * gn_w]
        outs.append(yg * lax.rsqrt(jnp.mean(yg * yg, axis=-1, keepdims=True) + EPS))
    y_ref[...] = (jnp.concatenate(outs, axis=1) * gn_ref[...]).astype(BF16)

    @pl.when(c == nc - 1)
    def _():
        for g in range(SSD_GROUPS):
            ssm_ref[0, g * gwid:(g + 1) * gwid, :] = ht_ref[g].T
        conv_ref[0] = tail


def _ssd_prompt(xbc, dtr, zs, cw, cb, dtb, alog, dsk, gn, e_mat, n_batch, seq, meta_blk, n_heads):
    nt = seq // TILE + 1
    r_total = xbc.shape[0]
    d_inner = n_heads * SSD_HEAD_DIM
    cdim = xbc.shape[1]
    tile = _seq_tile_map(seq, meta_blk)
    fix = lambda b, c: (0, 0)
    return pl.pallas_call(
        functools.partial(_ssd_prompt_kernel, n_heads=n_heads),
        grid=(n_batch, nt),
        in_specs=[pl.BlockSpec((TILE, cdim), tile), pl.BlockSpec((TILE, TILE), tile),
                  pl.BlockSpec((TILE, d_inner), tile),
                  pl.BlockSpec(cw.shape, fix), pl.BlockSpec(cb.shape, fix), pl.BlockSpec(dtb.shape, fix),
                  pl.BlockSpec(alog.shape, fix), pl.BlockSpec(dsk.shape, fix), pl.BlockSpec(gn.shape, fix),
                  pl.BlockSpec(e_mat.shape, fix)],
        out_specs=[pl.BlockSpec((TILE, d_inner), _seq_tile_map(seq, meta_blk, per_seq=True)),
                   pl.BlockSpec((1, d_inner, D_STATE), lambda b, c: (b, 0, 0)),
                   pl.BlockSpec((1, CONV_W - 1, cdim), lambda b, c: (b, 0, 0))],
        out_shape=[jax.ShapeDtypeStruct((r_total + n_batch * TILE, d_inner), BF16),
                   jax.ShapeDtypeStruct((n_batch, d_inner, D_STATE), F32),
                   jax.ShapeDtypeStruct((n_batch, CONV_W - 1, cdim), F32)],
        scratch_shapes=[pltpu.VMEM((8 + TILE, cdim), F32),
                        pltpu.VMEM((SSD_GROUPS, D_STATE, d_inner // SSD_GROUPS), F32)],
        compiler_params=_cparams(("arbitrary", "arbitrary")),
        name="ssd_prompt",
    )(xbc, dtr, zs, cw, cb, dtb, alog, dsk, gn, e_mat)


IDX_SEQS = 2


def _idx_sample_kernel(pt_ref, qi_ref, kiw_ref, *rest, n_pages):
    pages, s_ref = rest[:IDX_SEQS * n_pages], rest[IDX_SEQS * n_pages]
    lane = lax.broadcasted_iota(I32, (1, TILE), 1)
    for u in range(IDX_SEQS):
        qi = qi_ref[u]
        kiw = kiw_ref[u]
        q8 = jnp.concatenate([qi[:, h * IDX_DIM:(h + 1) * IDX_DIM] for h in range(IDX_HEADS)]
                             + [jnp.zeros((8 - IDX_HEADS, IDX_DIM), F32)], axis=0)
        w_col = jnp.broadcast_to(kiw, (TILE, TILE)).T[IDX_DIM:IDX_DIM + 8, 0:1]
        w_col = w_col * (IDX_HEADS ** -0.5 * IDX_DIM ** -0.5)
        q_hi, q_lo = _split_bf16(q8)
        q3 = jnp.concatenate([q_hi, q_lo, q_hi], axis=1)
        for p in range(n_pages):
            k_hi, k_lo = _split_bf16(pages[u * n_pages + p][0])
            d = jnp.dot(q3, jnp.concatenate([k_hi, k_hi, k_lo], axis=0), preferred_element_type=F32)
            s_ref[p, u] = jnp.sum(w_col * jnp.maximum(d, 0.0), axis=0, keepdims=True)
        d_new = jnp.sum(q8 * kiw[:, :IDX_DIM], axis=1, keepdims=True)
        s_new = jnp.sum(w_col * jnp.maximum(d_new, 0.0), axis=0, keepdims=True)
        s_ref[n_pages, u] = jnp.where(lane == 0, s_new, 0.0)


def _idx_sample(page_table, qi_s, kiw_s, cache_kidx):
    ns, n_pages = page_table.shape
    n_slots = n_pages + 1
    assert ns % IDX_SEQS == 0
    page = lambda u, p: pl.BlockSpec((1, IDX_DIM, TILE), lambda s, pt: (pt[IDX_SEQS * s + u, p], 0, 0))
    some = lambda s, pt: (s, 0, 0)
    grid_spec = pltpu.PrefetchScalarGridSpec(
        num_scalar_prefetch=1, grid=(ns // IDX_SEQS,),
        in_specs=[pl.BlockSpec((IDX_SEQS, 1, qi_s.shape[2]), some), pl.BlockSpec((IDX_SEQS, 1, TILE), some)]
        + [page(u, p) for u in range(IDX_SEQS) for p in range(n_pages)],
        out_specs=pl.BlockSpec((n_slots, IDX_SEQS, 1, TILE), lambda s, pt: (0, s, 0, 0)))
    return pl.pallas_call(
        functools.partial(_idx_sample_kernel, n_pages=n_pages),
        grid_spec=grid_spec,
        out_shape=jax.ShapeDtypeStruct((n_slots, ns, 1, TILE), F32),
        compiler_params=_cparams(("arbitrary",)),
        name="idx_sample",
    )(page_table, qi_s, kiw_s, *([cache_kidx] * (IDX_SEQS * n_pages)))


def _select_sample_kernel(s_ref, bias_ref, sc_ref, *, n_pages, ksel, n_bits):
    n_slots = n_pages + 1
    rows = lax.broadcasted_iota(I32, (TILE, TILE), 0)
    for p in range(n_slots):
        sc = s_ref[p].T
        if p == n_pages:
            sc = jnp.where(rows == 0, sc, jnp.nan)
        sc_ref[p] = sc
    for p in range(n_slots, sc_ref.shape[0]):
        sc_ref[p] = jnp.full((TILE, TILE), jnp.nan, F32)
    n_groups = sc_ref.shape[0] // KEY_GROUP
    tv, lim = _select_topk(sc_ref, n_groups, ksel, n_bits)
    rows_f = rows.astype(F32)
    for p in range(n_slots):
        sc = sc_ref[p]
        sel = (sc > tv) | ((sc == tv) & ((rows_f + float(p * TILE)) < lim))
        bias_ref[p] = jnp.where(sel, 0.0, NEG_BIG).T


def _select_sample(scores, ksel):
    n_slots, ns, _ = scores.shape
    n_pages = n_slots - 1
    n_bits = (n_slots * TILE - 1).bit_length()
    return pl.pallas_call(
        functools.partial(_select_sample_kernel, n_pages=n_pages, ksel=ksel, n_bits=n_bits),
        out_shape=jax.ShapeDtypeStruct((n_slots, ns, TILE), F32),
        scratch_shapes=[pltpu.VMEM((-(-n_slots // KEY_GROUP) * KEY_GROUP, TILE, TILE), F32)],
        compiler_params=pltpu.CompilerParams(vmem_limit_bytes=VMEM_LIMIT),
        name="select_sample",
    )(scores)


def _attn_sample_kernel(pt_ref, q_ref, kn_ref, vn_ref, bias_ref, *rest, n_pages):
    kpages, vpages, o_ref = rest[:n_pages], rest[n_pages:2 * n_pages], rest[2 * n_pages]
    kvw = N_KV_HEADS * HEAD_DIM
    q = q_ref[0].astype(F32)
    lane = lax.broadcasted_iota(I32, (N_HEADS, kvw), 1)
    head = lax.broadcasted_iota(I32, (N_HEADS, kvw), 0)
    own = jnp.right_shift(lane, 6) == jnp.right_shift(head, 2)
    qrows = []
    for h in range(N_HEADS):
        qh = q[:, h * HEAD_DIM:(h + 1) * HEAD_DIM]
        qrows.append(jnp.concatenate([qh] * N_KV_HEADS, axis=1))
    qf = jnp.where(own, jnp.concatenate(qrows, axis=0), 0.0)
    qw = qf.astype(BF16)
    nt = (((1,), (1,)), ((), ()))
    scores = []
    for p in range(n_pages):
        kp = kpages[p][0].astype(BF16)
        s = jnp.dot(qw, kp, preferred_element_type=F32)
        scores.append(s + bias_ref[p, 0])
    s_new = jnp.sum(qf * kn_ref[0], axis=1, keepdims=True)
    lane1 = lax.broadcasted_iota(I32, (1, TILE), 1)
    scores.append(jnp.where(lane1 == 0, s_new, NEG_BIG) + bias_ref[n_pages, 0])
    m = scores[0].max(axis=1, keepdims=True)
    for s in scores[1:]:
        m = jnp.maximum(m, s.max(axis=1, keepdims=True))
    den = jnp.zeros((N_HEADS, 1), F32)
    out = jnp.zeros((N_HEADS, kvw), F32)
    for p in range(n_pages):
        e = jnp.exp2(scores[p] - m)
        den = den + e.sum(axis=1, keepdims=True)
        out = out + lax.dot_general(e.astype(BF16), vpages[p][0].astype(BF16), nt,
                                    preferred_element_type=F32)
    e = jnp.exp2(scores[n_pages] - m)
    den = den + e.sum(axis=1, keepdims=True)
    out = out + e[:, 0:1] * vn_ref[0]
    out = out / den
    rolled = jnp.concatenate([pltpu.roll(out[:, i * TILE:(i + 1) * TILE], HEAD_DIM, 1)
                              for i in range(kvw // TILE)], axis=1)
    half = lax.broadcasted_iota(I32, (1, TILE), 1) < HEAD_DIM
    pieces = []
    for m2 in range(N_HEADS // 2):
        g = (2 * m2) // Q_PER_KV
        blk = (g // 2) * TILE
        a_src, b_src = (rolled, out) if g % 2 == 1 else (out, rolled)
        a = a_src[2 * m2:2 * m2 + 1, blk:blk + TILE]
        b = b_src[2 * m2 + 1:2 * m2 + 2, blk:blk + TILE]
        pieces.append(jnp.where(half, a, b))
    o_ref[0] = jnp.concatenate(pieces, axis=1)


def _attn_sample(page_table, q_s, k_s, v_s, bias, cache_k, cache_v):
    ns, n_pages = page_table.shape
    kvw = N_KV_HEADS * HEAD_DIM
    qw = N_HEADS * HEAD_DIM
    page = lambda p: pl.BlockSpec((1, kvw, TILE), lambda s, pt: (pt[s, p], 0, 0))
    one = lambda s, pt: (s, 0, 0)
    grid_spec = pltpu.PrefetchScalarGridSpec(
        num_scalar_prefetch=1, grid=(ns,),
        in_specs=[pl.BlockSpec((1, 1, qw), one), pl.BlockSpec((1, 1, kvw), one),
                  pl.BlockSpec((1, 1, kvw), one),
                  pl.BlockSpec((n_pages + 1, 1, 1, TILE), lambda s, pt: (0, s, 0, 0))]
        + [page(p) for p in range(n_pages)] * 2,
        out_specs=pl.BlockSpec((1, 1, qw), one))
    return pl.pallas_call(
        functools.partial(_attn_sample_kernel, n_pages=n_pages),
        grid_spec=grid_spec,
        out_shape=jax.ShapeDtypeStruct((ns, 1, qw), F32),
        compiler_params=_cparams(("arbitrary",)),
        name="attn_sample",
    )(page_table, q_s, k_s, v_s, bias, *([cache_k] * n_pages), *([cache_v] * n_pages))


def _ssd_sample_kernel(xbc_ref, dtr_ref, sc_ref, h0_ref, cw_ref, cb_ref, dtb_ref, alog_ref, dsk_ref,
                       e_ref, y_ref, h_ref, conv_ref, *, n_heads, sb):
    d_inner = n_heads * SSD_HEAD_DIM
    hpg = n_heads // SSD_GROUPS
    new = xbc_ref[...]
    cdim = new.shape[1]
    u = cb_ref[...] + cw_ref[CONV_W - 1:CONV_W, :] * new
    for k in range(CONV_W - 1):
        u = u + cw_ref[k:k + 1, :] * sc_ref[:, k * cdim:(k + 1) * cdim]
    for k in range(CONV_W - 2):
        conv_ref[:, k * cdim:(k + 1) * cdim] = sc_ref[:, (k + 1) * cdim:(k + 2) * cdim]
    conv_ref[:, (CONV_W - 2) * cdim:] = new
    act = _silu(u)
    xs = act[:, :d_inner]
    bm = act[:, d_inner:d_inner + SSD_GROUPS * D_STATE]
    cm = act[:, d_inner + SSD_GROUPS * D_STATE:]
    dt = _softplus(dtr_ref[...] + dtb_ref[...])
    decay = jnp.exp(dt * (-jnp.exp(alog_ref[...])))
    dec_x = _expand_heads(decay, e_ref)
    xdt = xs * _expand_heads(dt, e_ref)
    for i in range(sb):
        bcols, ccols = [], []
        for g in range(SSD_GROUPS):
            brow = bm[i:i + 1, g * D_STATE:(g + 1) * D_STATE]
            crow = cm[i:i + 1, g * D_STATE:(g + 1) * D_STATE]
            bcols.append(jnp.broadcast_to(brow, (TILE, D_STATE)).T)
            ccols.append(jnp.broadcast_to(crow, (TILE, D_STATE)).T)
        pieces = []
        for pr in range(n_heads // 2):
            g = (2 * pr) // hpg
            sl = slice(pr * TILE, (pr + 1) * TILE)
            ht = h0_ref[i, sl, :].T
            hn = ht * dec_x[i:i + 1, sl] + bcols[g] * xdt[i:i + 1, sl]
            pieces.append(jnp.sum(hn * ccols[g], axis=0, keepdims=True))
            h_ref[i, sl, :] = hn.T
        y_ref[i:i + 1, :] = jnp.concatenate(pieces, axis=1) + dsk_ref[...] * xs[i:i + 1, :]


def _ssd_sample(xbc_s, dtr_s, state_conv, state_ssm, cw, cb, dtb, alog, dsk, e_mat, n_heads, sb):
    ns, cdim = xbc_s.shape
    d_inner = n_heads * SSD_HEAD_DIM
    row = lambda i: (i, 0)
    row3 = lambda i: (i, 0, 0)
    fix = lambda i: (0, 0)
    return pl.pallas_call(
        functools.partial(_ssd_sample_kernel, n_heads=n_heads, sb=sb),
        grid=(ns // sb,),
        in_specs=[pl.BlockSpec((sb, cdim), row), pl.BlockSpec((sb, TILE), row),
                  pl.BlockSpec((sb, (CONV_W - 1) * cdim), row), pl.BlockSpec((sb, d_inner, D_STATE), row3),
                  pl.BlockSpec(cw.shape, fix), pl.BlockSpec(cb.shape, fix), pl.BlockSpec(dtb.shape, fix),
                  pl.BlockSpec(alog.shape, fix), pl.BlockSpec(dsk.shape, fix), pl.BlockSpec(e_mat.shape, fix)],
        out_specs=[pl.BlockSpec((sb, d_inner), row), pl.BlockSpec((sb, d_inner, D_STATE), row3),
                   pl.BlockSpec((sb, (CONV_W - 1) * cdim), row)],
        out_shape=[jax.ShapeDtypeStruct((ns, d_inner), F32),
                   jax.ShapeDtypeStruct((ns, d_inner, D_STATE), F32),
                   jax.ShapeDtypeStruct((ns, (CONV_W - 1) * cdim), F32)],
        compiler_params=_cparams(("arbitrary",)),
        name="ssd_sample",
    )(xbc_s, dtr_s, state_conv, state_ssm, cw, cb, dtb, alog, dsk, e_mat)


def _fill_sample_kernel(attn_any, ssd_any, a_ref, y_ref, zs_ref, gn_ref, attn_ref, ssd_ref):
    del attn_any, ssd_any
    attn_ref[...] = a_ref[...].astype(BF16)
    y = y_ref[...] * zs_ref[...].astype(F32)
    gn_w = y.shape[1] // SSD_GROUPS
    outs = []
    for g in range(SSD_GROUPS):
        yg = y[:, g * gn_w:(g + 1) * gn_w]
        outs.append(yg * lax.rsqrt(jnp.mean(yg * yg, axis=-1, keepdims=True) + EPS))
    ssd_ref[...] = (jnp.concatenate(outs, axis=1) * gn_ref[...]).astype(BF16)


def _fill_sample(attn, ssd, attn_s, y_s, zs, gn, blk):
    ns = attn_s.shape[0]
    last = lambda i: (blk, 0)
    fix = lambda i: (0, 0)
    return pl.pallas_call(
        _fill_sample_kernel,
        grid=(1,),
        in_specs=[pl.BlockSpec(memory_space=pl.ANY), pl.BlockSpec(memory_space=pl.ANY),
                  pl.BlockSpec(attn_s.shape, fix), pl.BlockSpec(y_s.shape, fix),
                  pl.BlockSpec((ns, zs.shape[1]), last), pl.BlockSpec(gn.shape, fix)],
        out_specs=[pl.BlockSpec((ns, attn.shape[1]), last), pl.BlockSpec((ns, ssd.shape[1]), last)],
        out_shape=[jax.ShapeDtypeStruct(attn.shape, attn.dtype), jax.ShapeDtypeStruct(ssd.shape, ssd.dtype)],
        input_output_aliases={0: 0, 1: 1},
        compiler_params=_cparams(("arbitrary",)),
        name="fill_sample",
    )(attn, ssd, attn_s, y_s, zs, gn)


def _mix_kernel(x_ref, a_ref, s_ref, gt_ref, wao_ref, wso_ref, wo_ref, o_ref):
    d = x_ref.shape[1]
    gt = gt_ref[...].astype(F32)
    ao = jnp.dot(a_ref[...], wao_ref[...], preferred_element_type=F32)
    so = jnp.dot(s_ref[...], wso_ref[...], preferred_element_type=F32)
    mixed = (gt[:, :d] * ao + gt[:, d:] * so).astype(BF16)
    o_ref[...] = x_ref[...] + jnp.dot(mixed, wo_ref[...], preferred_element_type=F32)


def _mix(x1, attn, ssd, gates, wao, wso, wo, tm, rows):
    d = x1.shape[1]
    r0, r = rows
    blk0 = r0 // tm
    row = lambda i: (blk0 + i, 0)
    return pl.pallas_call(
        _mix_kernel,
        grid=(r // tm,),
        in_specs=[pl.BlockSpec((tm, d), row), pl.BlockSpec((tm, attn.shape[1]), row),
                  pl.BlockSpec((tm, ssd.shape[1]), row), pl.BlockSpec((tm, gates.shape[1]), row),
                  _const_spec(wao.shape), _const_spec(wso.shape), _const_spec(wo.shape)],
        out_specs=pl.BlockSpec((tm, d), lambda i: (i, 0)),
        out_shape=jax.ShapeDtypeStruct((r, d), F32),
        compiler_params=_cparams(("arbitrary",)),
        name="mix",
    )(x1, attn, ssd, gates, wao, wso, wo)


def _split_weight_kernel(wt_ref, hi_ref, lo_ref, *, n_cols):
    blk = wt_ref.shape[0]
    row = pl.program_id(0) * blk + lax.broadcasted_iota(I32, (blk, 1), 0)
    w = jnp.where(row < n_cols, wt_ref[...], 0.0)
    hi, lo = _split_bf16(w.T)
    hi_ref[...] = hi
    lo_ref[...] = lo


def _split_weight(wt):
    n_cols, d = wt.shape
    blk = 4 * TILE
    n_blk = pl.cdiv(n_cols, blk)
    return pl.pallas_call(
        functools.partial(_split_weight_kernel, n_cols=n_cols),
        grid=(n_blk,),
        in_specs=[pl.BlockSpec((blk, d), lambda i: (i, 0))],
        out_specs=[pl.BlockSpec((d, blk), lambda i: (0, i)), pl.BlockSpec((d, blk), lambda i: (0, i))],
        out_shape=[jax.ShapeDtypeStruct((d, n_blk * blk), BF16), jax.ShapeDtypeStruct((d, n_blk * blk), BF16)],
        compiler_params=_cparams(("arbitrary",)),
        name="split_weight",
    )(wt)


def _prompt_cache_kernel(k_ref, v_ref, kiw_ref, km_ref, vm_ref, kiwm_ref, kt_ref, vt_ref, kit_ref):
    n_tiles = k_ref.shape[0] // TILE
    for src, meta, dst, width in ((k_ref, km_ref, kt_ref, k_ref.shape[1]), (v_ref, vm_ref, vt_ref, v_ref.shape[1]),
                                  (kiw_ref, kiwm_ref, kit_ref, IDX_DIM)):
        dst[0, :, 0:N_META] = meta[...].T[:width, PAD_FRONT:]
        for j in range(n_tiles):
            dst[0, :, N_META + j * TILE:N_META + (j + 1) * TILE] = src[j * TILE:(j + 1) * TILE, :].T[:width, :]


def _prompt_cache(k, v, kiw, n_batch, seq, meta_blk):
    l_seq = seq + N_META
    kvw = k.shape[1]
    toks = lambda b: (b, 0)
    meta = lambda b: (meta_blk, 0)
    out = lambda b: (b, 0, 0)
    return pl.pallas_call(
        _prompt_cache_kernel,
        grid=(n_batch,),
        in_specs=[pl.BlockSpec((seq, kvw), toks), pl.BlockSpec((seq, kvw), toks), pl.BlockSpec((seq, TILE), toks),
                  pl.BlockSpec((TILE, kvw), meta), pl.BlockSpec((TILE, kvw), meta), pl.BlockSpec((TILE, TILE), meta)],
        out_specs=[pl.BlockSpec((1, kvw, l_seq), out), pl.BlockSpec((1, kvw, l_seq), out),
                   pl.BlockSpec((1, IDX_DIM, l_seq), out)],
        out_shape=[jax.ShapeDtypeStruct((n_batch, kvw, l_seq), F32), jax.ShapeDtypeStruct((n_batch, kvw, l_seq), F32),
                   jax.ShapeDtypeStruct((n_batch, IDX_DIM, l_seq), F32)],
        compiler_params=_cparams(("arbitrary",)),
        name="prompt_cache",
    )(k, v, kiw, k, v, kiw)


def kernel(x_prompt, x_sample, cache_k, cache_v, cache_kidx, state_ssm, state_conv, page_table, meta_tokens,
           g_ffn1, w_ffn1_gate, w_ffn1_up, w_ffn1_down, g_mix, w_in, g_q, g_k, conv_w, conv_b, dt_bias, a_log,
           d_skip, g_ssd_norm, w_attn_out, w_ssd_out, w_o, g_ffn2, w_ffn2_gate, w_ffn2_up, w_ffn2_down):
    depth = w_in.shape[0]
    assert depth == 1, "single-layer step"
    n_batch, seq, d_model = x_prompt.shape
    ns = x_sample.shape[0]
    assert x_sample.shape[1] == 1 and seq % TILE == 0 and ns == TILE
    page = cache_k.shape[2]
    assert page == TILE
    l_seq = seq + N_META
    r_tok = n_batch * seq
    r_total = r_tok + ns + TILE
    meta_blk = (r_tok + ns) // TILE
    n_heads = state_ssm.shape[2]
    d_inner = n_heads * SSD_HEAD_DIM
    assert n_heads <= TILE and n_heads % (2 * SSD_GROUPS) == 0
    cdim = d_inner + 2 * SSD_GROUPS * D_STATE
    nq, nkv = N_HEADS * HEAD_DIM, N_KV_HEADS * HEAD_DIM
    nqi = IDX_HEADS * IDX_DIM
    tm = _row_tile(r_total // TILE, 6)
    tm_tok = _row_tile(r_tok // TILE, 8)

    w_hi, w_lo = _split_weight(jnp.transpose(w_in[0]))
    offs = [0]
    for n in (nq, nkv, nkv, nqi, IDX_DIM, IDX_HEADS, d_inner, cdim, n_heads, d_model, d_model):
        offs.append(offs[-1] + n)
    col = lambda i, w=w_hi: w[:, offs[i]:offs[i + 1]]
    wqk = jnp.concatenate([col(0), col(1)], axis=1)
    wv = col(2)
    idx_cols = lambda w: jnp.concatenate([col(3, w), col(4, w), col(5, w),
                                          jnp.zeros((d_model, TILE - IDX_DIM - IDX_HEADS), BF16)], axis=1)
    wih, wil = idx_cols(w_hi), idx_cols(w_lo)
    wz = col(6)
    wx = col(7)
    wdt = jnp.concatenate([col(8), jnp.zeros((d_model, TILE - n_heads), BF16)], axis=1)
    wgate = jnp.concatenate([col(9), col(10)], axis=1)
    n_seg = (nq + nkv) // HEAD_DIM
    seg = (jnp.arange(nq + nkv)[:, None] // HEAD_DIM == jnp.arange(TILE)[None, :]).astype(BF16)
    segt = seg.T
    del n_seg
    gqk = jnp.concatenate([jnp.tile(g_q[0], N_HEADS), jnp.tile(g_k[0], N_KV_HEADS)])[None, :]
    row1 = lambda v: v.reshape(1, -1)
    pad_heads = lambda v, fill: jnp.concatenate([v, jnp.full((TILE - n_heads,), fill, F32)])[None, :]
    dtb = pad_heads(dt_bias[0], -1e4)
    alog = pad_heads(a_log[0], 0.0)
    dsk = jnp.repeat(d_skip[0], SSD_HEAD_DIM)[None, :]
    assert 3 * n_heads <= TILE
    e_row = jnp.arange(TILE)[:, None]
    e_mat = ((e_row < 3 * n_heads) & (e_row % n_heads == jnp.arange(d_inner)[None, :] // SSD_HEAD_DIM)).astype(BF16)
    gn = row1(g_ssd_norm[0])

    ffn1 = (row1(g_ffn1[0]), row1(g_mix[0]), w_ffn1_gate[0].astype(BF16), w_ffn1_up[0].astype(BF16),
            w_ffn1_down[0].astype(BF16))
    x_tail = jnp.concatenate([x_sample.reshape(ns, d_model), jnp.zeros((PAD_FRONT, d_model), F32),
                              meta_tokens.astype(F32)], axis=0)
    assert r_tok % (ns + TILE) == 0
    x1, hb = _ffn(x_prompt.reshape(r_tok, d_model), *ffn1, tm_tok, out_rows=r_total)
    x1, hb = _ffn(x_tail, *ffn1, ns + TILE, out_rows=r_total, out_row0=r_tok, into=(x1, hb))
    q, k, v, kb, vb, qi, kiw = _proj_attn(x1, row1(g_mix[0]), wqk, wv, wih, wil, seg, segt, gqk, tm)
    xbc, dtr = _proj_xbc(hb, wx, wdt, tm)
    zs, gates = _proj_gate(hb, wz, wgate, tm)

    ksel_p = min(TOP_K_MAX, l_seq // 4)
    attn = _dsa_prompt(q, qi, kiw, kb, vb, n_batch, seq, meta_blk, ksel_p)
    ssd, ssm_p, conv_p = _ssd_prompt(xbc, dtr, zs, conv_w[0], row1(conv_b[0]), dtb, alog, dsk, gn, e_mat,
                                     n_batch, seq, meta_blk, n_heads)

    n_pages = page_table.shape[1]
    ksel_s = min(TOP_K_MAX, (n_pages * page + 1) // 4)
    smp = lambda a: a[r_tok:r_tok + ns]
    kidx_t = jnp.transpose(cache_kidx[0], (0, 2, 1))
    scores = _idx_sample(page_table, smp(qi).reshape(ns, 1, nqi), smp(kiw).reshape(ns, 1, TILE), kidx_t)
    bias = _select_sample(scores.reshape(n_pages + 1, ns, TILE), ksel_s).reshape(n_pages + 1, ns, 1, TILE)
    ck = jnp.transpose(cache_k[0], (0, 2, 3, 1)).reshape(cache_k.shape[1], nkv, page)
    cv = jnp.transpose(cache_v[0], (0, 2, 3, 1)).reshape(cache_v.shape[1], nkv, page)
    attn_s = _attn_sample(page_table, smp(q).reshape(ns, 1, nq), smp(k).reshape(ns, 1, nkv),
                          smp(v).reshape(ns, 1, nkv), bias, ck, cv)
    y_s, ssm_s, conv_s = _ssd_sample(smp(xbc), smp(dtr), state_conv[0].reshape(ns, -1), state_ssm[0].reshape(ns, d_inner, D_STATE),
                                     conv_w[0], row1(conv_b[0]), dtb, alog, dsk, e_mat, n_heads, 8)
    attn, ssd = _fill_sample(attn, ssd, attn_s.reshape(ns, nq), y_s, zs, gn, r_tok // ns)

    wao, wso, wo = w_attn_out[0].astype(BF16), w_ssd_out[0].astype(BF16), w_o[0].astype(BF16)
    ffn2 = (row1(g_ffn2[0]), row1(g_ffn2[0]), w_ffn2_gate[0].astype(BF16), w_ffn2_up[0].astype(BF16),
            w_ffn2_down[0].astype(BF16))
    x2 = _mix(x1, attn, ssd, gates, wao, wso, wo, tm_tok, (0, r_tok))
    y_prompt = _ffn(x2, *ffn2, tm_tok, with_norm=False)[0].reshape(n_batch, seq, d_model)
    x2_s = _mix(x1, attn, ssd, gates, wao, wso, wo, ns, (r_tok, ns))
    y_sample = _ffn(x2_s, *ffn2, ns, with_norm=False)[0].reshape(ns, 1, d_model)

    kt, vt, kit = _prompt_cache(k, v, kiw, n_batch, seq, meta_blk)
    heads_last = lambda a: jnp.transpose(a.reshape(1, n_batch, N_KV_HEADS, HEAD_DIM, l_seq), (0, 1, 4, 2, 3))
    k_prompt = heads_last(kt)
    v_prompt = heads_last(vt)
    kidx_prompt = jnp.transpose(kit, (0, 2, 1))[None]
    ssm_prompt = ssm_p.reshape(1, n_batch, n_heads, SSD_HEAD_DIM, D_STATE)
    conv_prompt = conv_p.reshape(1, n_batch, CONV_W - 1, cdim)
    k_sample = smp(k).reshape(1, ns, 1, N_KV_HEADS, HEAD_DIM)
    v_sample = smp(v).reshape(1, ns, 1, N_KV_HEADS, HEAD_DIM)
    kidx_sample = smp(kiw)[:, :IDX_DIM].reshape(1, ns, 1, IDX_DIM)
    ssm_sample = ssm_s.reshape(1, ns, n_heads, SSD_HEAD_DIM, D_STATE)
    conv_sample = conv_s.reshape(1, ns, CONV_W - 1, cdim)
    return (y_prompt, y_sample, k_prompt, v_prompt, kidx_prompt, ssm_prompt, conv_prompt,
            k_sample, v_sample, kidx_sample, ssm_sample, conv_sample)
```

```python
import functools

import jax
import jax.numpy as jnp
from jax import lax
from jax.experimental import pallas as pl
from jax.experimental.pallas import tpu as pltpu

F32 = jnp.float32
BF16 = jnp.bfloat16
I32 = jnp.int32
HIGHEST = lax.Precision.HIGHEST

EPS = 1e-6
N_META = 16
N_HEADS = 16
N_KV_HEADS = 4
HEAD_DIM = 64
Q_PER_KV = N_HEADS // N_KV_HEADS
IDX_HEADS = 4
IDX_DIM = 64
TOP_K_MAX = 256
SSD_HEAD_DIM = 64
SSD_GROUPS = 4
D_STATE = 128
CONV_W = 4
TILE = 128
PAD_FRONT = TILE - N_META
INT_MIN = -(2 ** 31)
NEG_BIG = -(2.0 ** 100)
Q_SCALE = HEAD_DIM ** -0.5 * 1.4426950408889634
VMEM_LIMIT = 56 * 1024 * 1024


def _cparams(sem, flags=None):
    return pltpu.CompilerParams(dimension_semantics=sem, vmem_limit_bytes=VMEM_LIMIT, flags=flags)


def _const_spec(shape):
    nd = len(shape)
    return pl.BlockSpec(shape, lambda *_: (0,) * nd, pipeline_mode=pl.Buffered(1))


def _rms(x, g):
    return x * lax.rsqrt(jnp.mean(x * x, axis=-1, keepdims=True) + EPS) * g


def _silu(x):
    return x * jax.nn.sigmoid(x)


def _row_tile(n_tiles, cap):
    for k in range(cap, 0, -1):
        if n_tiles % k == 0:
            return k * TILE
    return TILE


def _ffn_kernel(*refs, ck, n_into):
    x_ref, g_ref, g2_ref, wg_ref, wu_ref, wd_ref, o_ref = refs[n_into:n_into + 7]
    h2_ref = refs[n_into + 7] if len(refs) > n_into + 7 else None
    x = x_ref[...]
    h = _rms(x, g_ref[...]).astype(BF16)
    acc = jnp.zeros(x.shape, F32)
    for c in range(wg_ref.shape[1] // ck):
        sl = slice(c * ck, (c + 1) * ck)
        gate = jnp.dot(h, wg_ref[:, sl], preferred_element_type=F32)
        up = jnp.dot(h, wu_ref[:, sl], preferred_element_type=F32)
        act = (_silu(gate) * up).astype(BF16)
        acc = acc + jnp.dot(act, wd_ref[sl, :], preferred_element_type=F32)
    y = x + 0.5 * acc
    o_ref[...] = y
    if h2_ref is not None:
        h2_ref[...] = _rms(y, g2_ref[...]).astype(BF16)


def _ffn(x, g, g2, wg, wu, wd, tm, with_norm=True, out_rows=None, out_row0=0, into=()):
    r, d = x.shape
    out_rows = r if out_rows is None else out_rows
    dff = wg.shape[1]
    blk0 = out_row0 // tm
    out_specs = [pl.BlockSpec((tm, d), lambda i: (blk0 + i, 0))]
    out_shape = [jax.ShapeDtypeStruct((out_rows, d), F32)]
    if with_norm:
        out_specs.append(pl.BlockSpec((tm, d), lambda i: (blk0 + i, 0)))
        out_shape.append(jax.ShapeDtypeStruct((out_rows, d), BF16))
    return pl.pallas_call(
        functools.partial(_ffn_kernel, ck=256, n_into=len(into)),
        grid=(r // tm,),
        in_specs=[pl.BlockSpec(memory_space=pl.ANY)] * len(into)
        + [pl.BlockSpec((tm, d), lambda i: (i, 0)), _const_spec((1, d)), _const_spec((1, d)),
           _const_spec((d, dff)), _const_spec((d, dff)), _const_spec((dff, d))],
        out_specs=out_specs,
        out_shape=out_shape,
        input_output_aliases={i: i for i in range(len(into))},
        compiler_params=_cparams(("arbitrary",)),
        name="ffn",
    )(*into, x, g, g2, wg, wu, wd)


def _proj_attn_kernel(x_ref, g_ref, wqk_ref, wv_ref, wih_ref, wil_ref, seg_ref, segt_ref, gqk_ref,
                      q_ref, k_ref, v_ref, kb_ref, vb_ref, qi_ref, kiw_ref):
    h = _rms(x_ref[...], g_ref[...])
    hb, h_lo = _split_bf16(h)
    qk = jnp.dot(hb, wqk_ref[...], preferred_element_type=F32)
    ss = jnp.dot((qk * qk).astype(BF16), seg_ref[...], preferred_element_type=F32)
    r = lax.rsqrt(ss * (1.0 / HEAD_DIM) + EPS)
    r_hi = r.astype(BF16)
    r_lo = (r - r_hi.astype(F32)).astype(BF16)
    rx = (jnp.dot(r_hi, segt_ref[...], preferred_element_type=F32)
          + jnp.dot(r_lo, segt_ref[...], preferred_element_type=F32))
    qkn = qk * rx * gqk_ref[...]
    nq = q_ref.shape[1]
    q_ref[...] = (qkn[:, :nq] * Q_SCALE).astype(BF16)
    k = qkn[:, nq:]
    k_ref[...] = k
    kb_ref[...] = k.astype(BF16)
    v = jnp.dot(hb, wv_ref[...], preferred_element_type=F32)
    v_ref[...] = v
    vb_ref[...] = v.astype(BF16)
    idx = _dot3(hb, h_lo, wih_ref[...], wil_ref[...])
    nqi = qi_ref.shape[1]
    qi_ref[...] = idx[:, :nqi]
    kiw_ref[...] = idx[:, nqi:]


def _proj_attn(x1, g, wqk, wv, wih, wil, seg, segt, gqk, tm):
    r, d = x1.shape
    nqk, nv = wqk.shape[1], wv.shape[1]
    nq = N_HEADS * HEAD_DIM
    nqi = IDX_HEADS * IDX_DIM
    row = lambda i: (i, 0)
    outs = [(nq, BF16), (nqk - nq, F32), (nv, F32), (nqk - nq, BF16), (nv, BF16), (nqi, F32), (TILE, F32)]
    return pl.pallas_call(
        _proj_attn_kernel,
        grid=(r // tm,),
        in_specs=[pl.BlockSpec((tm, d), row), _const_spec((1, d)), _const_spec(wqk.shape),
                  _const_spec(wv.shape), _const_spec(wih.shape), _const_spec(wil.shape), _const_spec(seg.shape),
                  _const_spec(segt.shape), _const_spec(gqk.shape)],
        out_specs=[pl.BlockSpec((tm, n), row) for n, _ in outs],
        out_shape=[jax.ShapeDtypeStruct((r, n), dt) for n, dt in outs],
        compiler_params=_cparams(("arbitrary",)),
        name="proj_attn",
    )(x1, g, wqk, wv, wih, wil, seg, segt, gqk)


def _proj_xbc_kernel(h_ref, wx_ref, wdt_ref, xbc_ref, dt_ref):
    h = h_ref[...]
    xbc_ref[...] = jnp.dot(h, wx_ref[...], preferred_element_type=F32)
    dt_ref[...] = jnp.dot(h, wdt_ref[...], preferred_element_type=F32)


def _proj_xbc(hb, wx, wdt, tm):
    r, d = hb.shape
    row = lambda i: (i, 0)
    return pl.pallas_call(
        _proj_xbc_kernel,
        grid=(r // tm,),
        in_specs=[pl.BlockSpec((tm, d), row), _const_spec(wx.shape), _const_spec(wdt.shape)],
        out_specs=[pl.BlockSpec((tm, wx.shape[1]), row), pl.BlockSpec((tm, wdt.shape[1]), row)],
        out_shape=[jax.ShapeDtypeStruct((r, wx.shape[1]), F32),
                   jax.ShapeDtypeStruct((r, wdt.shape[1]), F32)],
        compiler_params=_cparams(("arbitrary",)),
        name="proj_xbc",
    )(hb, wx, wdt)


def _proj_gate_kernel(h_ref, wz_ref, wg_ref, zs_ref, gt_ref):
    h = h_ref[...]
    zs_ref[...] = _silu(jnp.dot(h, wz_ref[...], preferred_element_type=F32)).astype(BF16)
    gt_ref[...] = jax.nn.sigmoid(jnp.dot(h, wg_ref[...], preferred_element_type=F32)).astype(BF16)


def _proj_gate(hb, wz, wg, tm):
    r, d = hb.shape
    row = lambda i: (i, 0)
    return pl.pallas_call(
        _proj_gate_kernel,
        grid=(r // tm,),
        in_specs=[pl.BlockSpec((tm, d), row), _const_spec(wz.shape), _const_spec(wg.shape)],
        out_specs=[pl.BlockSpec((tm, wz.shape[1]), row), pl.BlockSpec((tm, wg.shape[1]), row)],
        out_shape=[jax.ShapeDtypeStruct((r, wz.shape[1]), BF16),
                   jax.ShapeDtypeStruct((r, wg.shape[1]), BF16)],
        compiler_params=_cparams(("arbitrary",)),
        name="proj_gate",
    )(hb, wz, wg)


KEY_GROUP = 4


def _key_value(key):
    mag = jnp.where(key < 0, jnp.where(key == INT_MIN, 0x7F800000, -key), key)
    v = pltpu.bitcast(mag, F32)
    return jnp.where(key < 0, -v, v)


def _tree_sum(x):
    while x.shape[0] > 1:
        h = x.shape[0] // 2
        x = x[:h] + x[h:]
    return x[0]


def _count(sc_ref, n_groups, pred):
    def body(gi, acc):
        s = sc_ref[pl.ds(gi * KEY_GROUP, KEY_GROUP)]
        hit = jnp.where(pred(s, gi * (KEY_GROUP * TILE)), 1.0, 0.0)
        return acc + _tree_sum(hit.reshape(KEY_GROUP * TILE // 8, 8, TILE))
    part = lax.fori_loop(0, n_groups, body, jnp.zeros((8, TILE), F32))
    return jnp.sum(part, axis=0, keepdims=True)


def _any(flag):
    return jnp.max(jnp.where(flag, 1, 0)) > 0


def _kth_bitwise(sc_ref, n_groups, ksel):
    n_ge0 = _count(sc_ref, n_groups, lambda s, r0: s >= 0.0)
    t = jnp.where(n_ge0 >= ksel, jnp.zeros((1, TILE), I32), jnp.full((1, TILE), INT_MIN, I32))

    def bit_step(i, t):
        cand = t | jnp.left_shift(jnp.int32(1), 30 - i)
        cv = _key_value(cand)
        n = _count(sc_ref, n_groups, lambda s, r0: s >= cv)
        return jnp.where(n >= ksel, cand, t)

    return lax.fori_loop(0, 31, bit_step, t)


def _tie_search(sc_ref, n_groups, tv, need, n_bits):
    shape = (KEY_GROUP, TILE, TILE)
    rows = (lax.broadcasted_iota(I32, shape, 0) * TILE + lax.broadcasted_iota(I32, shape, 1)).astype(F32)

    def bit_step(i, p):
        cand = p | jnp.left_shift(jnp.int32(1), n_bits - 1 - i)
        cf = cand.astype(F32)
        n = _count(sc_ref, n_groups, lambda s, r0: (s == tv) & ((rows + jnp.asarray(r0, F32)) < cf))
        return jnp.where(n < need, cand, p)
    return lax.fori_loop(0, n_bits, bit_step, jnp.zeros((1, TILE), I32)) + 1


def _select_topk(sc_ref, n_groups, ksel, n_bits):
    no_limit = jnp.full((1, TILE), float(1 << n_bits), F32)
    t = _kth_bitwise(sc_ref, n_groups, ksel)
    tv = _key_value(t)
    n_gt = _count(sc_ref, n_groups, lambda s, r0: s > tv)
    n_eq = _count(sc_ref, n_groups, lambda s, r0: s == tv)
    need = ksel - n_gt
    conflict = (t != INT_MIN) & (n_eq > need)
    lim = lax.cond(_any(conflict), lambda: _tie_search(sc_ref, n_groups, tv, need, n_bits).astype(F32),
                   lambda: no_limit)
    return tv, jnp.where(conflict, lim, no_limit)


def _split_bf16(x):
    hi = x.astype(BF16)
    return hi, (x - hi.astype(F32)).astype(BF16)


def _dot3(a_hi, a_lo, b_hi, b_lo):
    return (jnp.dot(a_hi, b_hi, preferred_element_type=F32)
            + jnp.dot(a_hi, b_lo, preferred_element_type=F32)
            + jnp.dot(a_lo, b_hi, preferred_element_type=F32))


VROWS = HEAD_DIM + 16


def _dsa_prompt_kernel(q_ref, qi_ref, kiwq_ref, kb_ref, vb_ref, kiw_ref, kbm_ref, vbm_ref, kiwm_ref, o_ref,
                       sc_ref, kk_ref, vt_ref, kihl_ref, rhs_ref, qa_ref,
                       m_ref, acc_ref, *, ksel, n_bits):
    b = pl.program_id(0)
    j = pl.program_id(1)
    nt = vb_ref.shape[0] // TILE + 1
    n_groups = lax.shift_right_logical(j + KEY_GROUP, KEY_GROUP.bit_length() - 1)
    n_pairs = lax.shift_right_logical(j + 2, 1)
    nkv = N_KV_HEADS
    gw = Q_PER_KV * TILE
    eye = (lax.broadcasted_iota(I32, (TILE, TILE), 0) == lax.broadcasted_iota(I32, (TILE, TILE), 1))

    @pl.when((b == 0) & (j == 0))
    def _():
        rhs_ref[...] = jnp.zeros(rhs_ref.shape, rhs_ref.dtype)
        ident = jnp.where(eye, 1.0, 0.0).astype(BF16)
        for g in range(nkv):
            for hh in range(Q_PER_KV):
                rhs_ref[g, 0:TILE, hh * TILE:(hh + 1) * TILE] = ident
        qa_ref[...] = jnp.zeros(qa_ref.shape, qa_ref.dtype)
        kk_ref[...] = jnp.zeros(kk_ref.shape, kk_ref.dtype)
        kihl_ref[...] = jnp.zeros(kihl_ref.shape, kihl_ref.dtype)
        ones_row = jnp.where(lax.broadcasted_iota(I32, (VROWS - HEAD_DIM, TILE), 0) == 0, 1.0, 0.0).astype(BF16)
        for c in range(vt_ref.shape[0]):
            vt_ref[c] = jnp.zeros(vt_ref.shape[1:], vt_ref.dtype)
            for g in range(nkv):
                vt_ref[c, g * VROWS + HEAD_DIM:(g + 1) * VROWS, :] = ones_row

    @pl.when(j == 0)
    def _():
        def stage(c, kb, vb, kiw):
            kk_ref[c] = kb
            vt = vb.astype(F32).T.astype(BF16)
            for g in range(nkv):
                vt_ref[c, g * VROWS:g * VROWS + HEAD_DIM, :] = vt[g * HEAD_DIM:(g + 1) * HEAD_DIM, :]
            ki = kiw[:, :IDX_DIM]
            hi = ki.astype(BF16).astype(F32)
            lo = (ki - hi).astype(BF16).astype(F32)
            kihl_ref[c] = jnp.concatenate([hi, lo, hi, jnp.zeros_like(hi)], axis=1).astype(BF16)

        stage(0, kbm_ref[...], vbm_ref[...], kiwm_ref[...])

        def prep(c, _):
            rows = pl.ds(pl.multiple_of((c - 1) * TILE, TILE), TILE)
            stage(c, kb_ref[rows, :], vb_ref[rows, :], kiw_ref[rows, :])
            return 0
        lax.fori_loop(1, nt, prep, 0)

    qt = q_ref[...].astype(F32).T.astype(BF16)
    for h in range(N_HEADS):
        g, hh = divmod(h, Q_PER_KV)
        r0 = TILE + (g % 2) * HEAD_DIM
        rhs_ref[g, r0:r0 + HEAD_DIM, hh * TILE:(hh + 1) * TILE] = qt[h * HEAD_DIM:(h + 1) * HEAD_DIM, :]
    qit = qi_ref[...].T
    for h in range(IDX_HEADS):
        hi, lo = _split_bf16(qit[h * IDX_DIM:(h + 1) * IDX_DIM, :])
        qa_ref[0:IDX_DIM, h * TILE:(h + 1) * TILE] = hi
        qa_ref[IDX_DIM:2 * IDX_DIM, h * TILE:(h + 1) * TILE] = hi
        qa_ref[2 * IDX_DIM:3 * IDX_DIM, h * TILE:(h + 1) * TILE] = lo
    wq = kiwq_ref[...].T[IDX_DIM:IDX_DIM + 8, :]
    wq = wq * (IDX_HEADS ** -0.5 * IDX_DIM ** -0.5)

    grp_rows = KEY_GROUP * TILE
    q_pos = j * TILE + lax.broadcasted_iota(I32, (grp_rows, TILE), 1)
    k_row = lax.broadcasted_iota(I32, (grp_rows, TILE), 0)

    def score_group(gi, _):
        khl = kihl_ref[pl.ds(gi * KEY_GROUP, KEY_GROUP)].reshape(grp_rows, 2 * TILE)
        d = jnp.dot(khl, qa_ref[...], preferred_element_type=F32)
        s = jnp.zeros((grp_rows, TILE), F32)
        for h in range(IDX_HEADS):
            s = s + wq[h:h + 1, :] * jnp.maximum(d[:, h * TILE:(h + 1) * TILE], 0.0)
        k_pos = k_row + gi * grp_rows
        valid = (k_pos <= q_pos) & (k_pos >= PAD_FRONT)
        sc_ref[pl.ds(gi * KEY_GROUP, KEY_GROUP)] = jnp.where(valid, s, jnp.nan).reshape(KEY_GROUP, TILE, TILE)
        return 0
    lax.fori_loop(0, n_groups, score_group, 0)

    tv, lim = _select_topk(sc_ref, n_groups, ksel, n_bits)

    m_ref[...] = jnp.full(m_ref.shape, NEG_BIG, F32)
    acc_ref[...] = jnp.zeros(acc_ref.shape, F32)

    pair_shape = (2, TILE, TILE)
    pair_row = (lax.broadcasted_iota(I32, pair_shape, 0) * TILE
                + lax.broadcasted_iota(I32, pair_shape, 1)).astype(F32)

    def attend_pair(i, _):
        sc = sc_ref[pl.ds(2 * i, 2)]
        sel = (sc > tv) | ((sc == tv) & ((pair_row + (i * (2 * TILE)).astype(F32)) < lim))
        bias = jnp.where(sel, 0.0, NEG_BIG).astype(BF16).reshape(2 * TILE, TILE)
        kc = kk_ref[pl.ds(2 * i, 2)].reshape(2 * TILE, nkv * HEAD_DIM)
        vt = jnp.concatenate([vt_ref[2 * i], vt_ref[2 * i + 1]], axis=1)
        m_all = m_ref[...]
        s = []
        for g in range(nkv):
            half = (g // 2) * TILE
            lhs = jnp.concatenate([bias, kc[:, half:half + TILE]], axis=1)
            s.append(jnp.dot(lhs, rhs_ref[g], preferred_element_type=F32))
        m_new = []
        for g in range(nkv):
            m_g = jnp.maximum(m_all[g:g + 1, :], jnp.max(s[g], axis=0, keepdims=True))
            alpha = jnp.exp2(m_all[g:g + 1, :] - m_g)
            p = jnp.exp2(s[g] - m_g).astype(BF16)
            pv = jnp.dot(vt[g * VROWS:(g + 1) * VROWS, :], p, preferred_element_type=F32)
            acc_ref[g] = alpha * acc_ref[g] + pv
            m_new.append(m_g)
        m_ref[...] = jnp.concatenate(m_new + [m_all[nkv:, :]], axis=0)
        return 0
    lax.fori_loop(0, n_pairs, attend_pair, 0)

    pieces = []
    for g in range(nkv):
        acc = acc_ref[g]
        o = acc[:HEAD_DIM] / acc[HEAD_DIM:HEAD_DIM + 1]
        for hh in range(Q_PER_KV):
            pieces.append(o[:, hh * TILE:(hh + 1) * TILE])
    o_ref[...] = jnp.concatenate(pieces, axis=0).T.astype(BF16)


def _seq_tile_map(seq, meta_blk, per_seq=False):
    n_tok = seq // TILE
    if per_seq:
        return lambda b, j: (jnp.where(j == 0, meta_blk + 1 + b, b * n_tok + j - 1), 0)
    return lambda b, j: (jnp.where(j == 0, meta_blk, b * n_tok + j - 1), 0)


def _dsa_prompt(q, qi, kiw, kb, vb, n_batch, seq, meta_blk, ksel):
    lp = seq + TILE
    nt = lp // TILE
    nt_pad = -(-nt // KEY_GROUP) * KEY_GROUP
    r_total = q.shape[0]
    qw = N_HEADS * HEAD_DIM
    kvw = N_KV_HEADS * HEAD_DIM
    gw = Q_PER_KV * TILE
    tile = _seq_tile_map(seq, meta_blk)
    toks = lambda b, j: (b, 0)
    meta = lambda b, j: (meta_blk, 0)
    n_bits = max(1, (lp - 1).bit_length())
    return pl.pallas_call(
        functools.partial(_dsa_prompt_kernel, ksel=ksel, n_bits=n_bits),
        grid=(n_batch, nt),
        in_specs=[pl.BlockSpec((TILE, qw), tile), pl.BlockSpec((TILE, IDX_HEADS * IDX_DIM), tile),
                  pl.BlockSpec((TILE, TILE), tile), pl.BlockSpec((seq, kvw), toks),
                  pl.BlockSpec((seq, kvw), toks), pl.BlockSpec((seq, TILE), toks),
                  pl.BlockSpec((TILE, kvw), meta), pl.BlockSpec((TILE, kvw), meta),
                  pl.BlockSpec((TILE, TILE), meta)],
        out_specs=pl.BlockSpec((TILE, qw), _seq_tile_map(seq, meta_blk, per_seq=True)),
        out_shape=jax.ShapeDtypeStruct((r_total + n_batch * TILE, qw), BF16),
        scratch_shapes=[pltpu.VMEM((nt_pad, TILE, TILE), F32),
                        pltpu.VMEM((nt_pad, TILE, kvw), BF16),
                        pltpu.VMEM((nt_pad, N_KV_HEADS * VROWS, TILE), BF16),
                        pltpu.VMEM((nt_pad, TILE, 2 * TILE), BF16),
                        pltpu.VMEM((N_KV_HEADS, 2 * TILE, gw), BF16),
                        pltpu.VMEM((2 * TILE, IDX_HEADS * TILE), BF16),
                        pltpu.VMEM((8, gw), F32),
                        pltpu.VMEM((N_KV_HEADS, VROWS, gw), F32)],
        compiler_params=_cparams(("arbitrary", "arbitrary")),
        name="dsa_prompt",
    )(q, qi, kiw, kb, vb, kiw, kb, vb, kiw)


def _expand_heads(v, e_ref):
    n_heads = e_ref.shape[1] // SSD_HEAD_DIM
    lane = lax.broadcasted_iota(I32, (1, v.shape[1]), 1)
    v = jnp.where(lane < n_heads, v, 0.0)
    hi = v.astype(BF16).astype(F32)
    r1 = v - hi
    mid = r1.astype(BF16).astype(F32)
    lo = (r1 - mid).astype(BF16).astype(F32)
    packed = (hi + pltpu.roll(mid, n_heads, 1) + pltpu.roll(lo, 2 * n_heads, 1)).astype(BF16)
    return jnp.dot(packed, e_ref[...], preferred_element_type=F32)


def _softplus(x):
    return jnp.maximum(x, 0.0) + jnp.log1p(jnp.exp(-jnp.abs(x)))


def _ssd_prompt_kernel(xbc_ref, dtr_ref, zs_ref, cw_ref, cb_ref, dtb_ref, alog_ref, dsk_ref, gn_ref,
                       e_ref, y_ref, ssm_ref, conv_ref, xp_ref, ht_ref, *, n_heads):
    c = pl.program_id(1)
    nc = pl.num_programs(1)
    d_inner = n_heads * SSD_HEAD_DIM
    hpg = n_heads // SSD_GROUPS
    gwid = hpg * SSD_HEAD_DIM
    gn_w = d_inner // SSD_GROUPS

    @pl.when(c == 0)
    def _():
        xp_ref[0:8, :] = jnp.zeros((8, xp_ref.shape[1]), F32)
        ht_ref[...] = jnp.zeros(ht_ref.shape, F32)

    xp_ref[8:8 + TILE, :] = xbc_ref[...]
    u = cb_ref[...] + cw_ref[CONV_W - 1:CONV_W, :] * xp_ref[8:8 + TILE, :]
    for k in range(CONV_W - 1):
        off = 8 - (CONV_W - 1) + k
        u = u + cw_ref[k:k + 1, :] * xp_ref[off:off + TILE, :]
    tail = xp_ref[8 + TILE - (CONV_W - 1):8 + TILE, :]
    xp_ref[8 - (CONV_W - 1):8, :] = tail

    row = lax.broadcasted_iota(I32, (TILE, 1), 0)
    live = jnp.where((c > 0) | (row >= PAD_FRONT), 1.0, 0.0)
    act = _silu(u) * live
    xs = act[:, :d_inner]
    bm = act[:, d_inner:d_inner + SSD_GROUPS * D_STATE]
    cm = act[:, d_inner + SSD_GROUPS * D_STATE:]

    dt = _softplus(dtr_ref[...] + dtb_ref[...]) * live
    a = dt * (-jnp.exp(alog_ref[...]))
    ri = lax.broadcasted_iota(I32, (TILE, TILE), 0)
    ci = lax.broadcasted_iota(I32, (TILE, TILE), 1)
    tril = ri >= ci
    acs = jnp.dot(jnp.where(tril, 1.0, 0.0), a, preferred_element_type=F32, precision=HIGHEST)
    acs_t = acs.T
    dt_t = dt.T
    acs_last = acs[TILE - 1:TILE, :]
    w_in = _expand_heads(jnp.exp(acs_last - acs) * dt, e_ref)
    w_out = _expand_heads(jnp.exp(acs), e_ref)
    xw = (xs * w_in).astype(BF16)
    xsb = xs.astype(BF16)
    first_head = lax.broadcasted_iota(I32, (1, 2 * SSD_HEAD_DIM), 1) < SSD_HEAD_DIM

    ys = []
    for g in range(SSD_GROUPS):
        bg = bm[:, g * D_STATE:(g + 1) * D_STATE]
        cg = cm[:, g * D_STATE:(g + 1) * D_STATE].astype(BF16)
        bgt = bg.T.astype(BF16)
        cb = jnp.dot(cg, bgt, preferred_element_type=F32)
        xw_g = xw[:, g * gwid:(g + 1) * gwid]
        xs_g = xsb[:, g * gwid:(g + 1) * gwid]
        h_prev = ht_ref[g]
        y_off = jnp.dot(cg, h_prev.astype(BF16), preferred_element_type=F32)
        y_g = y_off * w_out[:, g * gwid:(g + 1) * gwid]
        pairs = []
        for e0 in range(0, hpg, 2):
            x_pair = xs_g[:, e0 * SSD_HEAD_DIM:(e0 + 2) * SSD_HEAD_DIM]
            prods = []
            for e in (e0, e0 + 1):
                he = g * hpg + e
                seg = acs[:, he:he + 1] - acs_t[he:he + 1, :]
                m = (cb * jnp.exp(jnp.where(tril, seg, -jnp.inf)) * dt_t[he:he + 1, :]).astype(BF16)
                prods.append(jnp.dot(m, x_pair, preferred_element_type=F32))
            pairs.append(jnp.where(first_head, prods[0], prods[1]))
        y_g = y_g + jnp.concatenate(pairs, axis=1)
        states = jnp.dot(bgt, xw_g, preferred_element_type=F32)
        ht_ref[g] = h_prev * w_out[TILE - 1:TILE, g * gwid:(g + 1) * gwid] + states
        ys.append(y_g)
    y = jnp.concatenate(ys, axis=1) + dsk_ref[...] * xs
    y = y * zs_ref[...].astype(F32)
    outs = []
    for g in range(SSD_GROUPS):
        yg = y[:, g * gn_w:(g + 1) * gn_w]
        outs.append(yg * lax.rsqrt(jnp.mean(yg * yg, axis=-1, keepdims=True) + EPS))
    y_ref[...] = (jnp.concatenate(outs, axis=1) * gn_ref[...]).astype(BF16)

    @pl.when(c == nc - 1)
    def _():
        for g in range(SSD_GROUPS):
            ssm_ref[0, g * gwid:(g + 1) * gwid, :] = ht_ref[g].T
        conv_ref[0] = tail


def _ssd_prompt(xbc, dtr, zs, cw, cb, dtb, alog, dsk, gn, e_mat, n_batch, seq, meta_blk, n_heads):
    nt = seq // TILE + 1
    r_total = xbc.shape[0]
    d_inner = n_heads * SSD_HEAD_DIM
    cdim = xbc.shape[1]
    tile = _seq_tile_map(seq, meta_blk)
    fix = lambda b, c: (0, 0)
    return pl.pallas_call(
        functools.partial(_ssd_prompt_kernel, n_heads=n_heads),
        grid=(n_batch, nt),
        in_specs=[pl.BlockSpec((TILE, cdim), tile), pl.BlockSpec((TILE, TILE), tile),
                  pl.BlockSpec((TILE, d_inner), tile),
                  pl.BlockSpec(cw.shape, fix), pl.BlockSpec(cb.shape, fix), pl.BlockSpec(dtb.shape, fix),
                  pl.BlockSpec(alog.shape, fix), pl.BlockSpec(dsk.shape, fix), pl.BlockSpec(gn.shape, fix),
                  pl.BlockSpec(e_mat.shape, fix)],
        out_specs=[pl.BlockSpec((TILE, d_inner), _seq_tile_map(seq, meta_blk, per_seq=True)),
                   pl.BlockSpec((1, d_inner, D_STATE), lambda b, c: (b, 0, 0)),
                   pl.BlockSpec((1, CONV_W - 1, cdim), lambda b, c: (b, 0, 0))],
        out_shape=[jax.ShapeDtypeStruct((r_total + n_batch * TILE, d_inner), BF16),
                   jax.ShapeDtypeStruct((n_batch, d_inner, D_STATE), F32),
                   jax.ShapeDtypeStruct((n_batch, CONV_W - 1, cdim), F32)],
        scratch_shapes=[pltpu.VMEM((8 + TILE, cdim), F32),
                        pltpu.VMEM((SSD_GROUPS, D_STATE, d_inner // SSD_GROUPS), F32)],
        compiler_params=_cparams(("arbitrary", "arbitrary")),
        name="ssd_prompt",
    )(xbc, dtr, zs, cw, cb, dtb, alog, dsk, gn, e_mat)


IDX_SEQS = 2


def _idx_sample_kernel(pt_ref, qi_ref, kiw_ref, *rest, n_pages):
    pages, s_ref = rest[:IDX_SEQS * n_pages], rest[IDX_SEQS * n_pages]
    lane = lax.broadcasted_iota(I32, (1, TILE), 1)
    for u in range(IDX_SEQS):
        qi = qi_ref[u]
        kiw = kiw_ref[u]
        q8 = jnp.concatenate([qi[:, h * IDX_DIM:(h + 1) * IDX_DIM] for h in range(IDX_HEADS)]
                             + [jnp.zeros((8 - IDX_HEADS, IDX_DIM), F32)], axis=0)
        w_col = jnp.broadcast_to(kiw, (TILE, TILE)).T[IDX_DIM:IDX_DIM + 8, 0:1]
        w_col = w_col * (IDX_HEADS ** -0.5 * IDX_DIM ** -0.5)
        q_hi, q_lo = _split_bf16(q8)
        q3 = jnp.concatenate([q_hi, q_lo, q_hi], axis=1)
        for p in range(n_pages):
            k_hi, k_lo = _split_bf16(pages[u * n_pages + p][0])
            d = jnp.dot(q3, jnp.concatenate([k_hi, k_hi, k_lo], axis=0), preferred_element_type=F32)
            s_ref[p, u] = jnp.sum(w_col * jnp.maximum(d, 0.0), axis=0, keepdims=True)
        d_new = jnp.sum(q8 * kiw[:, :IDX_DIM], axis=1, keepdims=True)
        s_new = jnp.sum(w_col * jnp.maximum(d_new, 0.0), axis=0, keepdims=True)
        s_ref[n_pages, u] = jnp.where(lane == 0, s_new, 0.0)


def _idx_sample(page_table, qi_s, kiw_s, cache_kidx):
    ns, n_pages = page_table.shape
    n_slots = n_pages + 1
    assert ns % IDX_SEQS == 0
    page = lambda u, p: pl.BlockSpec((1, IDX_DIM, TILE), lambda s, pt: (pt[IDX_SEQS * s + u, p], 0, 0))
    some = lambda s, pt: (s, 0, 0)
    grid_spec = pltpu.PrefetchScalarGridSpec(
        num_scalar_prefetch=1, grid=(ns // IDX_SEQS,),
        in_specs=[pl.BlockSpec((IDX_SEQS, 1, qi_s.shape[2]), some), pl.BlockSpec((IDX_SEQS, 1, TILE), some)]
        + [page(u, p) for u in range(IDX_SEQS) for p in range(n_pages)],
        out_specs=pl.BlockSpec((n_slots, IDX_SEQS, 1, TILE), lambda s, pt: (0, s, 0, 0)))
    return pl.pallas_call(
        functools.partial(_idx_sample_kernel, n_pages=n_pages),
        grid_spec=grid_spec,
        out_shape=jax.ShapeDtypeStruct((n_slots, ns, 1, TILE), F32),
        compiler_params=_cparams(("arbitrary",)),
        name="idx_sample",
    )(page_table, qi_s, kiw_s, *([cache_kidx] * (IDX_SEQS * n_pages)))


def _select_sample_kernel(s_ref, bias_ref, sc_ref, *, n_pages, ksel, n_bits):
    n_slots = n_pages + 1
    rows = lax.broadcasted_iota(I32, (TILE, TILE), 0)
    for p in range(n_slots):
        sc = s_ref[p].T
        if p == n_pages:
            sc = jnp.where(rows == 0, sc, jnp.nan)
        sc_ref[p] = sc
    for p in range(n_slots, sc_ref.shape[0]):
        sc_ref[p] = jnp.full((TILE, TILE), jnp.nan, F32)
    n_groups = sc_ref.shape[0] // KEY_GROUP
    tv, lim = _select_topk(sc_ref, n_groups, ksel, n_bits)
    rows_f = rows.astype(F32)
    for p in range(n_slots):
        sc = sc_ref[p]
        sel = (sc > tv) | ((sc == tv) & ((rows_f + float(p * TILE)) < lim))
        bias_ref[p] = jnp.where(sel, 0.0, NEG_BIG).T


def _select_sample(scores, ksel):
    n_slots, ns, _ = scores.shape
    n_pages = n_slots - 1
    n_bits = (n_slots * TILE - 1).bit_length()
    return pl.pallas_call(
        functools.partial(_select_sample_kernel, n_pages=n_pages, ksel=ksel, n_bits=n_bits),
        out_shape=jax.ShapeDtypeStruct((n_slots, ns, TILE), F32),
        scratch_shapes=[pltpu.VMEM((-(-n_slots // KEY_GROUP) * KEY_GROUP, TILE, TILE), F32)],
        compiler_params=pltpu.CompilerParams(vmem_limit_bytes=VMEM_LIMIT),
        name="select_sample",
    )(scores)


def _attn_sample_kernel(pt_ref, q_ref, kn_ref, vn_ref, bias_ref, *rest, n_pages):
    kpages, vpages, o_ref = rest[:n_pages], rest[n_pages:2 * n_pages], rest[2 * n_pages]
    kvw = N_KV_HEADS * HEAD_DIM
    q = q_ref[0].astype(F32)
    lane = lax.broadcasted_iota(I32, (N_HEADS, kvw), 1)
    head = lax.broadcasted_iota(I32, (N_HEADS, kvw), 0)
    own = jnp.right_shift(lane, 6) == jnp.right_shift(head, 2)
    qrows = []
    for h in range(N_HEADS):
        qh = q[:, h * HEAD_DIM:(h + 1) * HEAD_DIM]
        qrows.append(jnp.concatenate([qh] * N_KV_HEADS, axis=1))
    qf = jnp.where(own, jnp.concatenate(qrows, axis=0), 0.0)
    qw = qf.astype(BF16)
    nt = (((1,), (1,)), ((), ()))
    scores = []
    for p in range(n_pages):
        kp = kpages[p][0].astype(BF16)
        s = jnp.dot(qw, kp, preferred_element_type=F32)
        scores.append(s + bias_ref[p, 0])
    s_new = jnp.sum(qf * kn_ref[0], axis=1, keepdims=True)
    lane1 = lax.broadcasted_iota(I32, (1, TILE), 1)
    scores.append(jnp.where(lane1 == 0, s_new, NEG_BIG) + bias_ref[n_pages, 0])
    m = scores[0].max(axis=1, keepdims=True)
    for s in scores[1:]:
        m = jnp.maximum(m, s.max(axis=1, keepdims=True))
    den = jnp.zeros((N_HEADS, 1), F32)
    out = jnp.zeros((N_HEADS, kvw), F32)
    for p in range(n_pages):
        e = jnp.exp2(scores[p] - m)
        den = den + e.sum(axis=1, keepdims=True)
        out = out + lax.dot_general(e.astype(BF16), vpages[p][0].astype(BF16), nt,
                                    preferred_element_type=F32)
    e = jnp.exp2(scores[n_pages] - m)
    den = den + e.sum(axis=1, keepdims=True)
    out = out + e[:, 0:1] * vn_ref[0]
    out = out / den
    rolled = jnp.concatenate([pltpu.roll(out[:, i * TILE:(i + 1) * TILE], HEAD_DIM, 1)
                              for i in range(kvw // TILE)], axis=1)
    half = lax.broadcasted_iota(I32, (1, TILE), 1) < HEAD_DIM
    pieces = []
    for m2 in range(N_HEADS // 2):
        g = (2 * m2) // Q_PER_KV
        blk = (g // 2) * TILE
        a_src, b_src = (rolled, out) if g % 2 == 1 else (out, rolled)
        a = a_src[2 * m2:2 * m2 + 1, blk:blk + TILE]
        b = b_src[2 * m2 + 1:2 * m2 + 2, blk:blk + TILE]
        pieces.append(jnp.where(half, a, b))
    o_ref[0] = jnp.concatenate(pieces, axis=1)


def _attn_sample(page_table, q_s, k_s, v_s, bias, cache_k, cache_v):
    ns, n_pages = page_table.shape
    kvw = N_KV_HEADS * HEAD_DIM
    qw = N_HEADS * HEAD_DIM
    page = lambda p: pl.BlockSpec((1, kvw, TILE), lambda s, pt: (pt[s, p], 0, 0))
    one = lambda s, pt: (s, 0, 0)
    grid_spec = pltpu.PrefetchScalarGridSpec(
        num_scalar_prefetch=1, grid=(ns,),
        in_specs=[pl.BlockSpec((1, 1, qw), one), pl.BlockSpec((1, 1, kvw), one),
                  pl.BlockSpec((1, 1, kvw), one),
                  pl.BlockSpec((n_pages + 1, 1, 1, TILE), lambda s, pt: (0, s, 0, 0))]
        + [page(p) for p in range(n_pages)] * 2,
        out_specs=pl.BlockSpec((1, 1, qw), one))
    return pl.pallas_call(
        functools.partial(_attn_sample_kernel, n_pages=n_pages),
        grid_spec=grid_spec,
        out_shape=jax.ShapeDtypeStruct((ns, 1, qw), F32),
        compiler_params=_cparams(("arbitrary",)),
        name="attn_sample",
    )(page_table, q_s, k_s, v_s, bias, *([cache_k] * n_pages), *([cache_v] * n_pages))


def _ssd_sample_kernel(xbc_ref, dtr_ref, sc_ref, h0_ref, cw_ref, cb_ref, dtb_ref, alog_ref, dsk_ref,
                       e_ref, y_ref, h_ref, conv_ref, *, n_heads, sb):
    d_inner = n_heads * SSD_HEAD_DIM
    hpg = n_heads // SSD_GROUPS
    new = xbc_ref[...]
    cdim = new.shape[1]
    u = cb_ref[...] + cw_ref[CONV_W - 1:CONV_W, :] * new
    for k in range(CONV_W - 1):
        u = u + cw_ref[k:k + 1, :] * sc_ref[:, k * cdim:(k + 1) * cdim]
    for k in range(CONV_W - 2):
        conv_ref[:, k * cdim:(k + 1) * cdim] = sc_ref[:, (k + 1) * cdim:(k + 2) * cdim]
    conv_ref[:, (CONV_W - 2) * cdim:] = new
    act = _silu(u)
    xs = act[:, :d_inner]
    bm = act[:, d_inner:d_inner + SSD_GROUPS * D_STATE]
    cm = act[:, d_inner + SSD_GROUPS * D_STATE:]
    dt = _softplus(dtr_ref[...] + dtb_ref[...])
    decay = jnp.exp(dt * (-jnp.exp(alog_ref[...])))
    dec_x = _expand_heads(decay, e_ref)
    xdt = xs * _expand_heads(dt, e_ref)
    for i in range(sb):
        bcols, ccols = [], []
        for g in range(SSD_GROUPS):
            brow = bm[i:i + 1, g * D_STATE:(g + 1) * D_STATE]
            crow = cm[i:i + 1, g * D_STATE:(g + 1) * D_STATE]
            bcols.append(jnp.broadcast_to(brow, (TILE, D_STATE)).T)
            ccols.append(jnp.broadcast_to(crow, (TILE, D_STATE)).T)
        pieces = []
        for pr in range(n_heads // 2):
            g = (2 * pr) // hpg
            sl = slice(pr * TILE, (pr + 1) * TILE)
            ht = h0_ref[i, sl, :].T
            hn = ht * dec_x[i:i + 1, sl] + bcols[g] * xdt[i:i + 1, sl]
            pieces.append(jnp.sum(hn * ccols[g], axis=0, keepdims=True))
            h_ref[i, sl, :] = hn.T
        y_ref[i:i + 1, :] = jnp.concatenate(pieces, axis=1) + dsk_ref[...] * xs[i:i + 1, :]


def _ssd_sample(xbc_s, dtr_s, state_conv, state_ssm, cw, cb, dtb, alog, dsk, e_mat, n_heads, sb):
    ns, cdim = xbc_s.shape
    d_inner = n_heads * SSD_HEAD_DIM
    row = lambda i: (i, 0)
    row3 = lambda i: (i, 0, 0)
    fix = lambda i: (0, 0)
    return pl.pallas_call(
        functools.partial(_ssd_sample_kernel, n_heads=n_heads, sb=sb),
        grid=(ns // sb,),
        in_specs=[pl.BlockSpec((sb, cdim), row), pl.BlockSpec((sb, TILE), row),
                  pl.BlockSpec((sb, (CONV_W - 1) * cdim), row), pl.BlockSpec((sb, d_inner, D_STATE), row3),
                  pl.BlockSpec(cw.shape, fix), pl.BlockSpec(cb.shape, fix), pl.BlockSpec(dtb.shape, fix),
                  pl.BlockSpec(alog.shape, fix), pl.BlockSpec(dsk.shape, fix), pl.BlockSpec(e_mat.shape, fix)],
        out_specs=[pl.BlockSpec((sb, d_inner), row), pl.BlockSpec((sb, d_inner, D_STATE), row3),
                   pl.BlockSpec((sb, (CONV_W - 1) * cdim), row)],
        out_shape=[jax.ShapeDtypeStruct((ns, d_inner), F32),
                   jax.ShapeDtypeStruct((ns, d_inner, D_STATE), F32),
                   jax.ShapeDtypeStruct((ns, (CONV_W - 1) * cdim), F32)],
        compiler_params=_cparams(("arbitrary",)),
        name="ssd_sample",
    )(xbc_s, dtr_s, state_conv, state_ssm, cw, cb, dtb, alog, dsk, e_mat)


def _fill_sample_kernel(attn_any, ssd_any, a_ref, y_ref, zs_ref, gn_ref, attn_ref, ssd_ref):
    del attn_any, ssd_any
    attn_ref[...] = a_ref[...].astype(BF16)
    y = y_ref[...] * zs_ref[...].astype(F32)
    gn_w = y.shape[1] // SSD_GROUPS
    outs = []
    for g in range(SSD_GROUPS):
        yg = y[:, g * gn_w:(g + 1) * gn_w]
        outs.append(yg * lax.rsqrt(jnp.mean(yg * yg, axis=-1, keepdims=True) + EPS))
    ssd_ref[...] = (jnp.concatenate(outs, axis=1) * gn_ref[...]).astype(BF16)


def _fill_sample(attn, ssd, attn_s, y_s, zs, gn, blk):
    ns = attn_s.shape[0]
    last = lambda i: (blk, 0)
    fix = lambda i: (0, 0)
    return pl.pallas_call(
        _fill_sample_kernel,
        grid=(1,),
        in_specs=[pl.BlockSpec(memory_space=pl.ANY), pl.BlockSpec(memory_space=pl.ANY),
                  pl.BlockSpec(attn_s.shape, fix), pl.BlockSpec(y_s.shape, fix),
                  pl.BlockSpec((ns, zs.shape[1]), last), pl.BlockSpec(gn.shape, fix)],
        out_specs=[pl.BlockSpec((ns, attn.shape[1]), last), pl.BlockSpec((ns, ssd.shape[1]), last)],
        out_shape=[jax.ShapeDtypeStruct(attn.shape, attn.dtype), jax.ShapeDtypeStruct(ssd.shape, ssd.dtype)],
        input_output_aliases={0: 0, 1: 1},
        compiler_params=_cparams(("arbitrary",)),
        name="fill_sample",
    )(attn, ssd, attn_s, y_s, zs, gn)


def _mix_kernel(x_ref, a_ref, s_ref, gt_ref, wao_ref, wso_ref, wo_ref, o_ref):
    d = x_ref.shape[1]
    gt = gt_ref[...].astype(F32)
    ao = jnp.dot(a_ref[...], wao_ref[...], preferred_element_type=F32)
    so = jnp.dot(s_ref[...], wso_ref[...], preferred_element_type=F32)
    mixed = (gt[:, :d] * ao + gt[:, d:] * so).astype(BF16)
    o_ref[...] = x_ref[...] + jnp.dot(mixed, wo_ref[...], preferred_element_type=F32)


def _mix(x1, attn, ssd, gates, wao, wso, wo, tm, rows):
    d = x1.shape[1]
    r0, r = rows
    blk0 = r0 // tm
    row = lambda i: (blk0 + i, 0)
    return pl.pallas_call(
        _mix_kernel,
        grid=(r // tm,),
        in_specs=[pl.BlockSpec((tm, d), row), pl.BlockSpec((tm, attn.shape[1]), row),
                  pl.BlockSpec((tm, ssd.shape[1]), row), pl.BlockSpec((tm, gates.shape[1]), row),
                  _const_spec(wao.shape), _const_spec(wso.shape), _const_spec(wo.shape)],
        out_specs=pl.BlockSpec((tm, d), lambda i: (i, 0)),
        out_shape=jax.ShapeDtypeStruct((r, d), F32),
        compiler_params=_cparams(("arbitrary",)),
        name="mix",
    )(x1, attn, ssd, gates, wao, wso, wo)


def _split_weight_kernel(wt_ref, hi_ref, lo_ref, *, n_cols):
    blk = wt_ref.shape[0]
    row = pl.program_id(0) * blk + lax.broadcasted_iota(I32, (blk, 1), 0)
    w = jnp.where(row < n_cols, wt_ref[...], 0.0)
    hi, lo = _split_bf16(w.T)
    hi_ref[...] = hi
    lo_ref[...] = lo


def _split_weight(wt):
    n_cols, d = wt.shape
    blk = 4 * TILE
    n_blk = pl.cdiv(n_cols, blk)
    return pl.pallas_call(
        functools.partial(_split_weight_kernel, n_cols=n_cols),
        grid=(n_blk,),
        in_specs=[pl.BlockSpec((blk, d), lambda i: (i, 0))],
        out_specs=[pl.BlockSpec((d, blk), lambda i: (0, i)), pl.BlockSpec((d, blk), lambda i: (0, i))],
        out_shape=[jax.ShapeDtypeStruct((d, n_blk * blk), BF16), jax.ShapeDtypeStruct((d, n_blk * blk), BF16)],
        compiler_params=_cparams(("arbitrary",)),
        name="split_weight",
    )(wt)


def _prompt_cache_kernel(k_ref, v_ref, kiw_ref, km_ref, vm_ref, kiwm_ref, kt_ref, vt_ref, kit_ref):
    n_tiles = k_ref.shape[0] // TILE
    for src, meta, dst, width in ((k_ref, km_ref, kt_ref, k_ref.shape[1]), (v_ref, vm_ref, vt_ref, v_ref.shape[1]),
                                  (kiw_ref, kiwm_ref, kit_ref, IDX_DIM)):
        dst[0, :, 0:N_META] = meta[...].T[:width, PAD_FRONT:]
        for j in range(n_tiles):
            dst[0, :, N_META + j * TILE:N_META + (j + 1) * TILE] = src[j * TILE:(j + 1) * TILE, :].T[:width, :]


def _prompt_cache(k, v, kiw, n_batch, seq, meta_blk):
    l_seq = seq + N_META
    kvw = k.shape[1]
    toks = lambda b: (b, 0)
    meta = lambda b: (meta_blk, 0)
    out = lambda b: (b, 0, 0)
    return pl.pallas_call(
        _prompt_cache_kernel,
        grid=(n_batch,),
        in_specs=[pl.BlockSpec((seq, kvw), toks), pl.BlockSpec((seq, kvw), toks), pl.BlockSpec((seq, TILE), toks),
                  pl.BlockSpec((TILE, kvw), meta), pl.BlockSpec((TILE, kvw), meta), pl.BlockSpec((TILE, TILE), meta)],
        out_specs=[pl.BlockSpec((1, kvw, l_seq), out), pl.BlockSpec((1, kvw, l_seq), out),
                   pl.BlockSpec((1, IDX_DIM, l_seq), out)],
        out_shape=[jax.ShapeDtypeStruct((n_batch, kvw, l_seq), F32), jax.ShapeDtypeStruct((n_batch, kvw, l_seq), F32),
                   jax.ShapeDtypeStruct((n_batch, IDX_DIM, l_seq), F32)],
        compiler_params=_cparams(("arbitrary",)),
        name="prompt_cache",
    )(k, v, kiw, k, v, kiw)


def kernel(x_prompt, x_sample, cache_k, cache_v, cache_kidx, state_ssm, state_conv, page_table, meta_tokens,
           g_ffn1, w_ffn1_gate, w_ffn1_up, w_ffn1_down, g_mix, w_in, g_q, g_k, conv_w, conv_b, dt_bias, a_log,
           d_skip, g_ssd_norm, w_attn_out, w_ssd_out, w_o, g_ffn2, w_ffn2_gate, w_ffn2_up, w_ffn2_down):
    depth = w_in.shape[0]
    assert depth == 1, "single-layer step"
    n_batch, seq, d_model = x_prompt.shape
    ns = x_sample.shape[0]
    assert x_sample.shape[1] == 1 and seq % TILE == 0 and ns == TILE
    page = cache_k.shape[2]
    assert page == TILE
    l_seq = seq + N_META
    r_tok = n_batch * seq
    r_total = r_tok + ns + TILE
    meta_blk = (r_tok + ns) // TILE
    n_heads = state_ssm.shape[2]
    d_inner = n_heads * SSD_HEAD_DIM
    assert n_heads <= TILE and n_heads % (2 * SSD_GROUPS) == 0
    cdim = d_inner + 2 * SSD_GROUPS * D_STATE
    nq, nkv = N_HEADS * HEAD_DIM, N_KV_HEADS * HEAD_DIM
    nqi = IDX_HEADS * IDX_DIM
    tm = _row_tile(r_total // TILE, 6)
    tm_tok = _row_tile(r_tok // TILE, 8)

    w_hi, w_lo = _split_weight(jnp.transpose(w_in[0]))
    offs = [0]
    for n in (nq, nkv, nkv, nqi, IDX_DIM, IDX_HEADS, d_inner, cdim, n_heads, d_model, d_model):
        offs.append(offs[-1] + n)
    col = lambda i, w=w_hi: w[:, offs[i]:offs[i + 1]]
    wqk = jnp.concatenate([col(0), col(1)], axis=1)
    wv = col(2)
    idx_cols = lambda w: jnp.concatenate([col(3, w), col(4, w), col(5, w),
                                          jnp.zeros((d_model, TILE - IDX_DIM - IDX_HEADS), BF16)], axis=1)
    wih, wil = idx_cols(w_hi), idx_cols(w_lo)
    wz = col(6)
    wx = col(7)
    wdt = jnp.concatenate([col(8), jnp.zeros((d_model, TILE - n_heads), BF16)], axis=1)
    wgate = jnp.concatenate([col(9), col(10)], axis=1)
    n_seg = (nq + nkv) // HEAD_DIM
    seg = (jnp.arange(nq + nkv)[:, None] // HEAD_DIM == jnp.arange(TILE)[None, :]).astype(BF16)
    segt = seg.T
    del n_seg
    gqk = jnp.concatenate([jnp.tile(g_q[0], N_HEADS), jnp.tile(g_k[0], N_KV_HEADS)])[None, :]
    row1 = lambda v: v.reshape(1, -1)
    pad_heads = lambda v, fill: jnp.concatenate([v, jnp.full((TILE - n_heads,), fill, F32)])[None, :]
    dtb = pad_heads(dt_bias[0], -1e4)
    alog = pad_heads(a_log[0], 0.0)
    dsk = jnp.repeat(d_skip[0], SSD_HEAD_DIM)[None, :]
    assert 3 * n_heads <= TILE
    e_row = jnp.arange(TILE)[:, None]
    e_mat = ((e_row < 3 * n_heads) & (e_row % n_heads == jnp.arange(d_inner)[None, :] // SSD_HEAD_DIM)).astype(BF16)
    gn = row1(g_ssd_norm[0])

    ffn1 = (row1(g_ffn1[0]), row1(g_mix[0]), w_ffn1_gate[0].astype(BF16), w_ffn1_up[0].astype(BF16),
            w_ffn1_down[0].astype(BF16))
    x_tail = jnp.concatenate([x_sample.reshape(ns, d_model), jnp.zeros((PAD_FRONT, d_model), F32),
                              meta_tokens.astype(F32)], axis=0)
    assert r_tok % (ns + TILE) == 0
    x1, hb = _ffn(x_prompt.reshape(r_tok, d_model), *ffn1, tm_tok, out_rows=r_total)
    x1, hb = _ffn(x_tail, *ffn1, ns + TILE, out_rows=r_total, out_row0=r_tok, into=(x1, hb))
    q, k, v, kb, vb, qi, kiw = _proj_attn(x1, row1(g_mix[0]), wqk, wv, wih, wil, seg, segt, gqk, tm)
    xbc, dtr = _proj_xbc(hb, wx, wdt, tm)
    zs, gates = _proj_gate(hb, wz, wgate, tm)

    ksel_p = min(TOP_K_MAX, l_seq // 4)
    attn = _dsa_prompt(q, qi, kiw, kb, vb, n_batch, seq, meta_blk, ksel_p)
    ssd, ssm_p, conv_p = _ssd_prompt(xbc, dtr, zs, conv_w[0], row1(conv_b[0]), dtb, alog, dsk, gn, e_mat,
                                     n_batch, seq, meta_blk, n_heads)

    n_pages = page_table.shape[1]
    ksel_s = min(TOP_K_MAX, (n_pages * page + 1) // 4)
    smp = lambda a: a[r_tok:r_tok + ns]
    kidx_t = jnp.transpose(cache_kidx[0], (0, 2, 1))
    scores = _idx_sample(page_table, smp(qi).reshape(ns, 1, nqi), smp(kiw).reshape(ns, 1, TILE), kidx_t)
    bias = _select_sample(scores.reshape(n_pages + 1, ns, TILE), ksel_s).reshape(n_pages + 1, ns, 1, TILE)
    ck = jnp.transpose(cache_k[0], (0, 2, 3, 1)).reshape(cache_k.shape[1], nkv, page)
    cv = jnp.transpose(cache_v[0], (0, 2, 3, 1)).reshape(cache_v.shape[1], nkv, page)
    attn_s = _attn_sample(page_table, smp(q).reshape(ns, 1, nq), smp(k).reshape(ns, 1, nkv),
                          smp(v).reshape(ns, 1, nkv), bias, ck, cv)
    y_s, ssm_s, conv_s = _ssd_sample(smp(xbc), smp(dtr), state_conv[0].reshape(ns, -1), state_ssm[0].reshape(ns, d_inner, D_STATE),
                                     conv_w[0], row1(conv_b[0]), dtb, alog, dsk, e_mat, n_heads, 8)
    attn, ssd = _fill_sample(attn, ssd, attn_s.reshape(ns, nq), y_s, zs, gn, r_tok // ns)

    wao, wso, wo = w_attn_out[0].astype(BF16), w_ssd_out[0].astype(BF16), w_o[0].astype(BF16)
    ffn2 = (row1(g_ffn2[0]), row1(g_ffn2[0]), w_ffn2_gate[0].astype(BF16), w_ffn2_up[0].astype(BF16),
            w_ffn2_down[0].astype(BF16))
    x2 = _mix(x1, attn, ssd, gates, wao, wso, wo, tm_tok, (0, r_tok))
    y_prompt = _ffn(x2, *ffn2, tm_tok, with_norm=False)[0].reshape(n_batch, seq, d_model)
    x2_s = _mix(x1, attn, ssd, gates, wao, wso, wo, ns, (r_tok, ns))
    y_sample = _ffn(x2_s, *ffn2, ns, with_norm=False)[0].reshape(ns, 1, d_model)

    kt, vt, kit = _prompt_cache(k, v, kiw, n_batch, seq, meta_blk)
    heads_last = lambda a: jnp.transpose(a.reshape(1, n_batch, N_KV_HEADS, HEAD_DIM, l_seq), (0, 1, 4, 2, 3))
    k_prompt = heads_last(kt)
    v_prompt = heads_last(vt)
    kidx_prompt = jnp.transpose(kit, (0, 2, 1))[None]
    ssm_prompt = ssm_p.reshape(1, n_batch, n_heads, SSD_HEAD_DIM, D_STATE)
    conv_prompt = conv_p.reshape(1, n_batch, CONV_W - 1, cdim)
    k_sample = smp(k).reshape(1, ns, 1, N_KV_HEADS, HEAD_DIM)
    v_sample = smp(v).reshape(1, ns, 1, N_KV_HEADS, HEAD_DIM)
    kidx_sample = smp(kiw)[:, :IDX_DIM].reshape(1, ns, 1, IDX_DIM)
    ssm_sample = ssm_s.reshape(1, ns, n_heads, SSD_HEAD_DIM, D_STATE)
    conv_sample = conv_s.reshape(1, ns, CONV_W - 1, cdim)
    return (y_prompt, y_sample, k_prompt, v_prompt, kidx_prompt, ssm_prompt, conv_prompt,
            k_sample, v_sample, kidx_sample, ssm_sample, conv_sample)
```

```python
import functools

import jax
import jax.numpy as jnp
from jax import lax
from jax.experimental import pallas as pl
from jax.experimental.pallas import tpu as pltpu

F32 = jnp.float32
BF16 = jnp.bfloat16
I32 = jnp.int32
HIGHEST = lax.Precision.HIGHEST

EPS = 1e-6
N_META = 16
N_HEADS = 16
N_KV_HEADS = 4
HEAD_DIM = 64
Q_PER_KV = N_HEADS // N_KV_HEADS
IDX_HEADS = 4
IDX_DIM = 64
TOP_K_MAX = 256
SSD_HEAD_DIM = 64
SSD_GROUPS = 4
D_STATE = 128
CONV_W = 4
TILE = 128
PAD_FRONT = TILE - N_META
INT_MIN = -(2 ** 31)
NEG_BIG = -(2.0 ** 100)
Q_SCALE = HEAD_DIM ** -0.5 * 1.4426950408889634
VMEM_LIMIT = 56 * 1024 * 1024


def _cparams(sem, flags=None):
    return pltpu.CompilerParams(dimension_semantics=sem, vmem_limit_bytes=VMEM_LIMIT, flags=flags)


def _const_spec(shape):
    nd = len(shape)
    return pl.BlockSpec(shape, lambda *_: (0,) * nd, pipeline_mode=pl.Buffered(1))


def _rms(x, g):
    return x * lax.rsqrt(jnp.mean(x * x, axis=-1, keepdims=True) + EPS) * g


def _silu(x):
    return x * jax.nn.sigmoid(x)


def _row_tile(n_tiles, cap):
    for k in range(cap, 0, -1):
        if n_tiles % k == 0:
            return k * TILE
    return TILE


def _ffn_kernel(x_ref, g_ref, g2_ref, wg_ref, wu_ref, wd_ref, o_ref, h2_ref=None, *, ck):
    x = x_ref[...]
    h = _rms(x, g_ref[...]).astype(BF16)
    acc = jnp.zeros(x.shape, F32)
    for c in range(wg_ref.shape[1] // ck):
        sl = slice(c * ck, (c + 1) * ck)
        gate = jnp.dot(h, wg_ref[:, sl], preferred_element_type=F32)
        up = jnp.dot(h, wu_ref[:, sl], preferred_element_type=F32)
        act = (_silu(gate) * up).astype(BF16)
        acc = acc + jnp.dot(act, wd_ref[sl, :], preferred_element_type=F32)
    y = x + 0.5 * acc
    o_ref[...] = y
    if h2_ref is not None:
        h2_ref[...] = _rms(y, g2_ref[...]).astype(BF16)


def _ffn(x, g, g2, wg, wu, wd, tm, with_norm=True):
    r, d = x.shape
    dff = wg.shape[1]
    row = lambda i: (i, 0)
    out_specs = [pl.BlockSpec((tm, d), row)]
    out_shape = [jax.ShapeDtypeStruct((r, d), F32)]
    if with_norm:
        out_specs.append(pl.BlockSpec((tm, d), row))
        out_shape.append(jax.ShapeDtypeStruct((r, d), BF16))
    return pl.pallas_call(
        functools.partial(_ffn_kernel, ck=256),
        grid=(r // tm,),
        in_specs=[pl.BlockSpec((tm, d), row), _const_spec((1, d)), _const_spec((1, d)),
                  _const_spec((d, dff)), _const_spec((d, dff)), _const_spec((dff, d))],
        out_specs=out_specs,
        out_shape=out_shape,
        compiler_params=_cparams(("arbitrary",)),
        name="ffn",
    )(x, g, g2, wg, wu, wd)


def _proj_attn_kernel(x_ref, g_ref, wqk_ref, wv_ref, wih_ref, wil_ref, seg_ref, segt_ref, gqk_ref,
                      q_ref, k_ref, v_ref, kb_ref, vb_ref, qi_ref, kiw_ref):
    h = _rms(x_ref[...], g_ref[...])
    hb, h_lo = _split_bf16(h)
    qk = jnp.dot(hb, wqk_ref[...], preferred_element_type=F32)
    ss = jnp.dot((qk * qk).astype(BF16), seg_ref[...], preferred_element_type=F32)
    r = lax.rsqrt(ss * (1.0 / HEAD_DIM) + EPS)
    r_hi = r.astype(BF16)
    r_lo = (r - r_hi.astype(F32)).astype(BF16)
    rx = (jnp.dot(r_hi, segt_ref[...], preferred_element_type=F32)
          + jnp.dot(r_lo, segt_ref[...], preferred_element_type=F32))
    qkn = qk * rx * gqk_ref[...]
    nq = q_ref.shape[1]
    q_ref[...] = (qkn[:, :nq] * Q_SCALE).astype(BF16)
    k = qkn[:, nq:]
    k_ref[...] = k
    kb_ref[...] = k.astype(BF16)
    v = jnp.dot(hb, wv_ref[...], preferred_element_type=F32)
    v_ref[...] = v
    vb_ref[...] = v.astype(BF16)
    idx = _dot3(hb, h_lo, wih_ref[...], wil_ref[...])
    nqi = qi_ref.shape[1]
    qi_ref[...] = idx[:, :nqi]
    kiw_ref[...] = idx[:, nqi:]


def _proj_attn(x1, g, wqk, wv, wih, wil, seg, segt, gqk, tm):
    r, d = x1.shape
    nqk, nv = wqk.shape[1], wv.shape[1]
    nq = N_HEADS * HEAD_DIM
    nqi = IDX_HEADS * IDX_DIM
    row = lambda i: (i, 0)
    outs = [(nq, BF16), (nqk - nq, F32), (nv, F32), (nqk - nq, BF16), (nv, BF16), (nqi, F32), (TILE, F32)]
    return pl.pallas_call(
        _proj_attn_kernel,
        grid=(r // tm,),
        in_specs=[pl.BlockSpec((tm, d), row), _const_spec((1, d)), _const_spec(wqk.shape),
                  _const_spec(wv.shape), _const_spec(wih.shape), _const_spec(wil.shape), _const_spec(seg.shape),
                  _const_spec(segt.shape), _const_spec(gqk.shape)],
        out_specs=[pl.BlockSpec((tm, n), row) for n, _ in outs],
        out_shape=[jax.ShapeDtypeStruct((r, n), dt) for n, dt in outs],
        compiler_params=_cparams(("arbitrary",)),
        name="proj_attn",
    )(x1, g, wqk, wv, wih, wil, seg, segt, gqk)


def _proj_xbc_kernel(h_ref, wx_ref, wdt_ref, xbc_ref, dt_ref):
    h = h_ref[...]
    xbc_ref[...] = jnp.dot(h, wx_ref[...], preferred_element_type=F32)
    dt_ref[...] = jnp.dot(h, wdt_ref[...], preferred_element_type=F32)


def _proj_xbc(hb, wx, wdt, tm):
    r, d = hb.shape
    row = lambda i: (i, 0)
    return pl.pallas_call(
        _proj_xbc_kernel,
        grid=(r // tm,),
        in_specs=[pl.BlockSpec((tm, d), row), _const_spec(wx.shape), _const_spec(wdt.shape)],
        out_specs=[pl.BlockSpec((tm, wx.shape[1]), row), pl.BlockSpec((tm, wdt.shape[1]), row)],
        out_shape=[jax.ShapeDtypeStruct((r, wx.shape[1]), F32),
                   jax.ShapeDtypeStruct((r, wdt.shape[1]), F32)],
        compiler_params=_cparams(("arbitrary",)),
        name="proj_xbc",
    )(hb, wx, wdt)


def _proj_gate_kernel(h_ref, wz_ref, wg_ref, zs_ref, gt_ref):
    h = h_ref[...]
    zs_ref[...] = _silu(jnp.dot(h, wz_ref[...], preferred_element_type=F32)).astype(BF16)
    gt_ref[...] = jax.nn.sigmoid(jnp.dot(h, wg_ref[...], preferred_element_type=F32)).astype(BF16)


def _proj_gate(hb, wz, wg, tm):
    r, d = hb.shape
    row = lambda i: (i, 0)
    return pl.pallas_call(
        _proj_gate_kernel,
        grid=(r // tm,),
        in_specs=[pl.BlockSpec((tm, d), row), _const_spec(wz.shape), _const_spec(wg.shape)],
        out_specs=[pl.BlockSpec((tm, wz.shape[1]), row), pl.BlockSpec((tm, wg.shape[1]), row)],
        out_shape=[jax.ShapeDtypeStruct((r, wz.shape[1]), BF16),
                   jax.ShapeDtypeStruct((r, wg.shape[1]), BF16)],
        compiler_params=_cparams(("arbitrary",)),
        name="proj_gate",
    )(hb, wz, wg)


KEY_GROUP = 4


def _key_value(key):
    mag = jnp.where(key < 0, jnp.where(key == INT_MIN, 0x7F800000, -key), key)
    v = pltpu.bitcast(mag, F32)
    return jnp.where(key < 0, -v, v)


def _tree_sum(x):
    while x.shape[0] > 1:
        h = x.shape[0] // 2
        x = x[:h] + x[h:]
    return x[0]


def _count(sc_ref, n_groups, pred):
    def body(gi, acc):
        s = sc_ref[pl.ds(gi * KEY_GROUP, KEY_GROUP)]
        hit = jnp.where(pred(s, gi * (KEY_GROUP * TILE)), 1.0, 0.0)
        return acc + _tree_sum(hit.reshape(KEY_GROUP * TILE // 8, 8, TILE))
    part = lax.fori_loop(0, n_groups, body, jnp.zeros((8, TILE), F32))
    return jnp.sum(part, axis=0, keepdims=True)


def _any(flag):
    return jnp.max(jnp.where(flag, 1, 0)) > 0


def _kth_bitwise(sc_ref, n_groups, ksel):
    n_ge0 = _count(sc_ref, n_groups, lambda s, r0: s >= 0.0)
    t = jnp.where(n_ge0 >= ksel, jnp.zeros((1, TILE), I32), jnp.full((1, TILE), INT_MIN, I32))

    def bit_step(i, t):
        cand = t | jnp.left_shift(jnp.int32(1), 30 - i)
        cv = _key_value(cand)
        n = _count(sc_ref, n_groups, lambda s, r0: s >= cv)
        return jnp.where(n >= ksel, cand, t)

    return lax.fori_loop(0, 31, bit_step, t)


def _tie_search(sc_ref, n_groups, tv, need, n_bits):
    shape = (KEY_GROUP, TILE, TILE)
    rows = (lax.broadcasted_iota(I32, shape, 0) * TILE + lax.broadcasted_iota(I32, shape, 1)).astype(F32)

    def bit_step(i, p):
        cand = p | jnp.left_shift(jnp.int32(1), n_bits - 1 - i)
        cf = cand.astype(F32)
        n = _count(sc_ref, n_groups, lambda s, r0: (s == tv) & ((rows + jnp.asarray(r0, F32)) < cf))
        return jnp.where(n < need, cand, p)
    return lax.fori_loop(0, n_bits, bit_step, jnp.zeros((1, TILE), I32)) + 1


def _select_topk(sc_ref, n_groups, ksel, n_bits):
    no_limit = jnp.full((1, TILE), float(1 << n_bits), F32)
    t = _kth_bitwise(sc_ref, n_groups, ksel)
    tv = _key_value(t)
    n_gt = _count(sc_ref, n_groups, lambda s, r0: s > tv)
    n_eq = _count(sc_ref, n_groups, lambda s, r0: s == tv)
    need = ksel - n_gt
    conflict = (t != INT_MIN) & (n_eq > need)
    lim = lax.cond(_any(conflict), lambda: _tie_search(sc_ref, n_groups, tv, need, n_bits).astype(F32),
                   lambda: no_limit)
    return tv, jnp.where(conflict, lim, no_limit)


def _split_bf16(x):
    hi = x.astype(BF16)
    return hi, (x - hi.astype(F32)).astype(BF16)


def _dot3(a_hi, a_lo, b_hi, b_lo):
    return (jnp.dot(a_hi, b_hi, preferred_element_type=F32)
            + jnp.dot(a_hi, b_lo, preferred_element_type=F32)
            + jnp.dot(a_lo, b_hi, preferred_element_type=F32))


VROWS = HEAD_DIM + 16


def _dsa_prompt_kernel(q_ref, qi_ref, kiwq_ref, kb_ref, vb_ref, kiw_ref, kbm_ref, vbm_ref, kiwm_ref, o_ref,
                       sc_ref, kk_ref, vt_ref, kihl_ref, rhs_ref, qa_ref,
                       m_ref, acc_ref, *, ksel, n_bits):
    b = pl.program_id(0)
    j = pl.program_id(1)
    nt = vb_ref.shape[0] // TILE + 1
    n_groups = lax.shift_right_logical(j + KEY_GROUP, KEY_GROUP.bit_length() - 1)
    n_pairs = lax.shift_right_logical(j + 2, 1)
    nkv = N_KV_HEADS
    gw = Q_PER_KV * TILE
    eye = (lax.broadcasted_iota(I32, (TILE, TILE), 0) == lax.broadcasted_iota(I32, (TILE, TILE), 1))

    @pl.when((b == 0) & (j == 0))
    def _():
        rhs_ref[...] = jnp.zeros(rhs_ref.shape, rhs_ref.dtype)
        ident = jnp.where(eye, 1.0, 0.0).astype(BF16)
        for g in range(nkv):
            for hh in range(Q_PER_KV):
                rhs_ref[g, 0:TILE, hh * TILE:(hh + 1) * TILE] = ident
        qa_ref[...] = jnp.zeros(qa_ref.shape, qa_ref.dtype)
        kk_ref[...] = jnp.zeros(kk_ref.shape, kk_ref.dtype)
        kihl_ref[...] = jnp.zeros(kihl_ref.shape, kihl_ref.dtype)
        ones_row = jnp.where(lax.broadcasted_iota(I32, (VROWS - HEAD_DIM, TILE), 0) == 0, 1.0, 0.0).astype(BF16)
        for c in range(vt_ref.shape[0]):
            vt_ref[c] = jnp.zeros(vt_ref.shape[1:], vt_ref.dtype)
            for g in range(nkv):
                vt_ref[c, g * VROWS + HEAD_DIM:(g + 1) * VROWS, :] = ones_row

    @pl.when(j == 0)
    def _():
        def stage(c, kb, vb, kiw):
            kk_ref[c] = kb
            vt = vb.astype(F32).T.astype(BF16)
            for g in range(nkv):
                vt_ref[c, g * VROWS:g * VROWS + HEAD_DIM, :] = vt[g * HEAD_DIM:(g + 1) * HEAD_DIM, :]
            ki = kiw[:, :IDX_DIM]
            hi = ki.astype(BF16).astype(F32)
            lo = (ki - hi).astype(BF16).astype(F32)
            kihl_ref[c] = jnp.concatenate([hi, lo, hi, jnp.zeros_like(hi)], axis=1).astype(BF16)

        stage(0, kbm_ref[...], vbm_ref[...], kiwm_ref[...])

        def prep(c, _):
            rows = pl.ds(pl.multiple_of((c - 1) * TILE, TILE), TILE)
            stage(c, kb_ref[rows, :], vb_ref[rows, :], kiw_ref[rows, :])
            return 0
        lax.fori_loop(1, nt, prep, 0)

    qt = q_ref[...].astype(F32).T.astype(BF16)
    for h in range(N_HEADS):
        g, hh = divmod(h, Q_PER_KV)
        r0 = TILE + (g % 2) * HEAD_DIM
        rhs_ref[g, r0:r0 + HEAD_DIM, hh * TILE:(hh + 1) * TILE] = qt[h * HEAD_DIM:(h + 1) * HEAD_DIM, :]
    qit = qi_ref[...].T
    for h in range(IDX_HEADS):
        hi, lo = _split_bf16(qit[h * IDX_DIM:(h + 1) * IDX_DIM, :])
        qa_ref[0:IDX_DIM, h * TILE:(h + 1) * TILE] = hi
        qa_ref[IDX_DIM:2 * IDX_DIM, h * TILE:(h + 1) * TILE] = hi
        qa_ref[2 * IDX_DIM:3 * IDX_DIM, h * TILE:(h + 1) * TILE] = lo
    wq = kiwq_ref[...].T[IDX_DIM:IDX_DIM + 8, :]
    wq = wq * (IDX_HEADS ** -0.5 * IDX_DIM ** -0.5)

    grp_rows = KEY_GROUP * TILE
    q_pos = j * TILE + lax.broadcasted_iota(I32, (grp_rows, TILE), 1)
    k_row = lax.broadcasted_iota(I32, (grp_rows, TILE), 0)

    def score_group(gi, _):
        khl = kihl_ref[pl.ds(gi * KEY_GROUP, KEY_GROUP)].reshape(grp_rows, 2 * TILE)
        d = jnp.dot(khl, qa_ref[...], preferred_element_type=F32)
        s = jnp.zeros((grp_rows, TILE), F32)
        for h in range(IDX_HEADS):
            s = s + wq[h:h + 1, :] * jnp.maximum(d[:, h * TILE:(h + 1) * TILE], 0.0)
        k_pos = k_row + gi * grp_rows
        valid = (k_pos <= q_pos) & (k_pos >= PAD_FRONT)
        sc_ref[pl.ds(gi * KEY_GROUP, KEY_GROUP)] = jnp.where(valid, s, jnp.nan).reshape(KEY_GROUP, TILE, TILE)
        return 0
    lax.fori_loop(0, n_groups, score_group, 0)

    tv, lim = _select_topk(sc_ref, n_groups, ksel, n_bits)

    m_ref[...] = jnp.full(m_ref.shape, NEG_BIG, F32)
    acc_ref[...] = jnp.zeros(acc_ref.shape, F32)

    pair_shape = (2, TILE, TILE)
    pair_row = (lax.broadcasted_iota(I32, pair_shape, 0) * TILE
                + lax.broadcasted_iota(I32, pair_shape, 1)).astype(F32)

    def attend_pair(i, _):
        sc = sc_ref[pl.ds(2 * i, 2)]
        sel = (sc > tv) | ((sc == tv) & ((pair_row + (i * (2 * TILE)).astype(F32)) < lim))
        bias = jnp.where(sel, 0.0, NEG_BIG).astype(BF16).reshape(2 * TILE, TILE)
        kc = kk_ref[pl.ds(2 * i, 2)].reshape(2 * TILE, nkv * HEAD_DIM)
        vt = jnp.concatenate([vt_ref[2 * i], vt_ref[2 * i + 1]], axis=1)
        m_all = m_ref[...]
        s = []
        for g in range(nkv):
            half = (g // 2) * TILE
            lhs = jnp.concatenate([bias, kc[:, half:half + TILE]], axis=1)
            s.append(jnp.dot(lhs, rhs_ref[g], preferred_element_type=F32))
        m_new = []
        for g in range(nkv):
            m_g = jnp.maximum(m_all[g:g + 1, :], jnp.max(s[g], axis=0, keepdims=True))
            alpha = jnp.exp2(m_all[g:g + 1, :] - m_g)
            p = jnp.exp2(s[g] - m_g).astype(BF16)
            pv = jnp.dot(vt[g * VROWS:(g + 1) * VROWS, :], p, preferred_element_type=F32)
            acc_ref[g] = alpha * acc_ref[g] + pv
            m_new.append(m_g)
        m_ref[...] = jnp.concatenate(m_new + [m_all[nkv:, :]], axis=0)
        return 0
    lax.fori_loop(0, n_pairs, attend_pair, 0)

    pieces = []
    for g in range(nkv):
        acc = acc_ref[g]
        o = acc[:HEAD_DIM] / acc[HEAD_DIM:HEAD_DIM + 1]
        for hh in range(Q_PER_KV):
            pieces.append(o[:, hh * TILE:(hh + 1) * TILE])
    o_ref[...] = jnp.concatenate(pieces, axis=0).T.astype(BF16)


def _seq_tile_map(seq, meta_blk):
    n_tok = seq // TILE
    meta = meta_blk if callable(meta_blk) else (lambda b: meta_blk)
    return lambda b, j: (jnp.where(j == 0, meta(b), b * n_tok + j - 1), 0)


def _dsa_prompt(q, qi, kiw, kb, vb, n_batch, seq, meta_blk, ksel):
    lp = seq + TILE
    nt = lp // TILE
    nt_pad = -(-nt // KEY_GROUP) * KEY_GROUP
    r_total = q.shape[0]
    qw = N_HEADS * HEAD_DIM
    kvw = N_KV_HEADS * HEAD_DIM
    gw = Q_PER_KV * TILE
    tile = _seq_tile_map(seq, meta_blk)
    toks = lambda b, j: (b, 0)
    meta = lambda b, j: (meta_blk, 0)
    n_bits = max(1, (lp - 1).bit_length())
    return pl.pallas_call(
        functools.partial(_dsa_prompt_kernel, ksel=ksel, n_bits=n_bits),
        grid=(n_batch, nt),
        in_specs=[pl.BlockSpec((TILE, qw), tile), pl.BlockSpec((TILE, IDX_HEADS * IDX_DIM), tile),
                  pl.BlockSpec((TILE, TILE), tile), pl.BlockSpec((seq, kvw), toks),
                  pl.BlockSpec((seq, kvw), toks), pl.BlockSpec((seq, TILE), toks),
                  pl.BlockSpec((TILE, kvw), meta), pl.BlockSpec((TILE, kvw), meta),
                  pl.BlockSpec((TILE, TILE), meta)],
        out_specs=pl.BlockSpec((TILE, qw), _seq_tile_map(seq, lambda b: n_batch * (seq // TILE) + b)),
        out_shape=jax.ShapeDtypeStruct((n_batch * (seq + TILE), qw), BF16),
        scratch_shapes=[pltpu.VMEM((nt_pad, TILE, TILE), F32),
                        pltpu.VMEM((nt_pad, TILE, kvw), BF16),
                        pltpu.VMEM((nt_pad, N_KV_HEADS * VROWS, TILE), BF16),
                        pltpu.VMEM((nt_pad, TILE, 2 * TILE), BF16),
                        pltpu.VMEM((N_KV_HEADS, 2 * TILE, gw), BF16),
                        pltpu.VMEM((2 * TILE, IDX_HEADS * TILE), BF16),
                        pltpu.VMEM((8, gw), F32),
                        pltpu.VMEM((N_KV_HEADS, VROWS, gw), F32)],
        compiler_params=_cparams(("arbitrary", "arbitrary")),
        name="dsa_prompt",
    )(q, qi, kiw, kb, vb, kiw, kb, vb, kiw)


def _expand_heads(v, e_ref):
    n_heads = e_ref.shape[1] // SSD_HEAD_DIM
    lane = lax.broadcasted_iota(I32, (1, v.shape[1]), 1)
    v = jnp.where(lane < n_heads, v, 0.0)
    hi = v.astype(BF16).astype(F32)
    r1 = v - hi
    mid = r1.astype(BF16).astype(F32)
    lo = (r1 - mid).astype(BF16).astype(F32)
    packed = (hi + pltpu.roll(mid, n_heads, 1) + pltpu.roll(lo, 2 * n_heads, 1)).astype(BF16)
    return jnp.dot(packed, e_ref[...], preferred_element_type=F32)


def _softplus(x):
    return jnp.maximum(x, 0.0) + jnp.log1p(jnp.exp(-jnp.abs(x)))


def _ssd_prompt_kernel(xbc_ref, dtr_ref, zs_ref, cw_ref, cb_ref, dtb_ref, alog_ref, dsk_ref, gn_ref,
                       e_ref, y_ref, ssm_ref, conv_ref, xp_ref, ht_ref, *, n_heads):
    c = pl.program_id(1)
    nc = pl.num_programs(1)
    d_inner = n_heads * SSD_HEAD_DIM
    hpg = n_heads // SSD_GROUPS
    gwid = hpg * SSD_HEAD_DIM
    gn_w = d_inner // SSD_GROUPS

    @pl.when(c == 0)
    def _():
        xp_ref[0:8, :] = jnp.zeros((8, xp_ref.shape[1]), F32)
        ht_ref[...] = jnp.zeros(ht_ref.shape, F32)

    xp_ref[8:8 + TILE, :] = xbc_ref[...]
    u = cb_ref[...] + cw_ref[CONV_W - 1:CONV_W, :] * xp_ref[8:8 + TILE, :]
    for k in range(CONV_W - 1):
        off = 8 - (CONV_W - 1) + k
        u = u + cw_ref[k:k + 1, :] * xp_ref[off:off + TILE, :]
    tail = xp_ref[8 + TILE - (CONV_W - 1):8 + TILE, :]
    xp_ref[8 - (CONV_W - 1):8, :] = tail

    row = lax.broadcasted_iota(I32, (TILE, 1), 0)
    live = jnp.where((c > 0) | (row >= PAD_FRONT), 1.0, 0.0)
    act = _silu(u) * live
    xs = act[:, :d_inner]
    bm = act[:, d_inner:d_inner + SSD_GROUPS * D_STATE]
    cm = act[:, d_inner + SSD_GROUPS * D_STATE:]

    dt = _softplus(dtr_ref[...] + dtb_ref[...]) * live
    a = dt * (-jnp.exp(alog_ref[...]))
    ri = lax.broadcasted_iota(I32, (TILE, TILE), 0)
    ci = lax.broadcasted_iota(I32, (TILE, TILE), 1)
    tril = ri >= ci
    acs = jnp.dot(jnp.where(tril, 1.0, 0.0), a, preferred_element_type=F32, precision=HIGHEST)
    acs_t = acs.T
    dt_t = dt.T
    acs_last = acs[TILE - 1:TILE, :]
    w_in = _expand_heads(jnp.exp(acs_last - acs) * dt, e_ref)
    w_out = _expand_heads(jnp.exp(acs), e_ref)
    xw = (xs * w_in).astype(BF16)
    xsb = xs.astype(BF16)
    first_head = lax.broadcasted_iota(I32, (1, 2 * SSD_HEAD_DIM), 1) < SSD_HEAD_DIM

    ys = []
    for g in range(SSD_GROUPS):
        bg = bm[:, g * D_STATE:(g + 1) * D_STATE]
        cg = cm[:, g * D_STATE:(g + 1) * D_STATE].astype(BF16)
        bgt = bg.T.astype(BF16)
        cb = jnp.dot(cg, bgt, preferred_element_type=F32)
        xw_g = xw[:, g * gwid:(g + 1) * gwid]
        xs_g = xsb[:, g * gwid:(g + 1) * gwid]
        h_prev = ht_ref[g]
        y_off = jnp.dot(cg, h_prev.astype(BF16), preferred_element_type=F32)
        y_g = y_off * w_out[:, g * gwid:(g + 1) * gwid]
        pairs = []
        for e0 in range(0, hpg, 2):
            x_pair = xs_g[:, e0 * SSD_HEAD_DIM:(e0 + 2) * SSD_HEAD_DIM]
            prods = []
            for e in (e0, e0 + 1):
                he = g * hpg + e
                seg = acs[:, he:he + 1] - acs_t[he:he + 1, :]
                m = (cb * jnp.exp(jnp.where(tril, seg, -jnp.inf)) * dt_t[he:he + 1, :]).astype(BF16)
                prods.append(jnp.dot(m, x_pair, preferred_element_type=F32))
            pairs.append(jnp.where(first_head, prods[0], prods[1]))
        y_g = y_g + jnp.concatenate(pairs, axis=1)
        states = jnp.dot(bgt, xw_g, preferred_element_type=F32)
        ht_ref[g] = h_prev * w_out[TILE - 1:TILE, g * gwid:(g + 1) * gwid] + states
        ys.append(y_g)
    y = jnp.concatenate(ys, axis=1) + dsk_ref[...] * xs
    y = y * zs_ref[...].astype(F32)
    outs = []
    for g in range(SSD_GROUPS):
        yg = y[:, g * gn_w:(g + 1) * gn_w]
        outs.append(yg * lax.rsqrt(jnp.mean(yg * yg, axis=-1, keepdims=True) + EPS))
    y_ref[...] = (jnp.concatenate(outs, axis=1) * gn_ref[...]).astype(BF16)

    @pl.when(c == nc - 1)
    def _():
        for g in range(SSD_GROUPS):
            ssm_ref[0, g * gwid:(g + 1) * gwid, :] = ht_ref[g].T
        conv_ref[0] = tail


def _ssd_prompt(xbc, dtr, zs, cw, cb, dtb, alog, dsk, gn, e_mat, n_batch, seq, meta_blk, n_heads):
    nt = seq // TILE + 1
    r_total = xbc.shape[0]
    d_inner = n_heads * SSD_HEAD_DIM
    cdim = xbc.shape[1]
    tile = _seq_tile_map(seq, meta_blk)
    fix = lambda b, c: (0, 0)
    return pl.pallas_call(
        functools.partial(_ssd_prompt_kernel, n_heads=n_heads),
        grid=(n_batch, nt),
        in_specs=[pl.BlockSpec((TILE, cdim), tile), pl.BlockSpec((TILE, TILE), tile),
                  pl.BlockSpec((TILE, d_inner), tile),
                  pl.BlockSpec(cw.shape, fix), pl.BlockSpec(cb.shape, fix), pl.BlockSpec(dtb.shape, fix),
                  pl.BlockSpec(alog.shape, fix), pl.BlockSpec(dsk.shape, fix), pl.BlockSpec(gn.shape, fix),
                  pl.BlockSpec(e_mat.shape, fix)],
        out_specs=[pl.BlockSpec((TILE, d_inner), _seq_tile_map(seq, lambda b: n_batch * (seq // TILE) + b)),
                   pl.BlockSpec((1, d_inner, D_STATE), lambda b, c: (b, 0, 0)),
                   pl.BlockSpec((1, CONV_W - 1, cdim), lambda b, c: (b, 0, 0))],
        out_shape=[jax.ShapeDtypeStruct((n_batch * (seq + TILE), d_inner), BF16),
                   jax.ShapeDtypeStruct((n_batch, d_inner, D_STATE), F32),
                   jax.ShapeDtypeStruct((n_batch, CONV_W - 1, cdim), F32)],
        scratch_shapes=[pltpu.VMEM((8 + TILE, cdim), F32),
                        pltpu.VMEM((SSD_GROUPS, D_STATE, d_inner // SSD_GROUPS), F32)],
        compiler_params=_cparams(("arbitrary", "arbitrary")),
        name="ssd_prompt",
    )(xbc, dtr, zs, cw, cb, dtb, alog, dsk, gn, e_mat)


IDX_SEQS = 2


def _idx_sample_kernel(pt_ref, qi_ref, kiw_ref, *rest, n_pages):
    pages, s_ref = rest[:IDX_SEQS * n_pages], rest[IDX_SEQS * n_pages]
    lane = lax.broadcasted_iota(I32, (1, TILE), 1)
    for u in range(IDX_SEQS):
        qi = qi_ref[u]
        kiw = kiw_ref[u]
        q8 = jnp.concatenate([qi[:, h * IDX_DIM:(h + 1) * IDX_DIM] for h in range(IDX_HEADS)]
                             + [jnp.zeros((8 - IDX_HEADS, IDX_DIM), F32)], axis=0)
        w_col = jnp.broadcast_to(kiw, (TILE, TILE)).T[IDX_DIM:IDX_DIM + 8, 0:1]
        w_col = w_col * (IDX_HEADS ** -0.5 * IDX_DIM ** -0.5)
        q_hi, q_lo = _split_bf16(q8)
        q3 = jnp.concatenate([q_hi, q_lo, q_hi], axis=1)
        for p in range(n_pages):
            k_hi, k_lo = _split_bf16(pages[u * n_pages + p][0])
            d = jnp.dot(q3, jnp.concatenate([k_hi, k_hi, k_lo], axis=0), preferred_element_type=F32)
            s_ref[p, u] = jnp.sum(w_col * jnp.maximum(d, 0.0), axis=0, keepdims=True)
        d_new = jnp.sum(q8 * kiw[:, :IDX_DIM], axis=1, keepdims=True)
        s_new = jnp.sum(w_col * jnp.maximum(d_new, 0.0), axis=0, keepdims=True)
        s_ref[n_pages, u] = jnp.where(lane == 0, s_new, 0.0)


def _idx_sample(page_table, qi_s, kiw_s, cache_kidx):
    ns, n_pages = page_table.shape
    n_slots = n_pages + 1
    assert ns % IDX_SEQS == 0
    page = lambda u, p: pl.BlockSpec((1, IDX_DIM, TILE), lambda s, pt: (pt[IDX_SEQS * s + u, p], 0, 0))
    some = lambda s, pt: (s, 0, 0)
    grid_spec = pltpu.PrefetchScalarGridSpec(
        num_scalar_prefetch=1, grid=(ns // IDX_SEQS,),
        in_specs=[pl.BlockSpec((IDX_SEQS, 1, qi_s.shape[2]), some), pl.BlockSpec((IDX_SEQS, 1, TILE), some)]
        + [page(u, p) for u in range(IDX_SEQS) for p in range(n_pages)],
        out_specs=pl.BlockSpec((n_slots, IDX_SEQS, 1, TILE), lambda s, pt: (0, s, 0, 0)))
    return pl.pallas_call(
        functools.partial(_idx_sample_kernel, n_pages=n_pages),
        grid_spec=grid_spec,
        out_shape=jax.ShapeDtypeStruct((n_slots, ns, 1, TILE), F32),
        compiler_params=_cparams(("arbitrary",)),
        name="idx_sample",
    )(page_table, qi_s, kiw_s, *([cache_kidx] * (IDX_SEQS * n_pages)))


def _select_sample_kernel(s_ref, bias_ref, sc_ref, *, n_pages, ksel, n_bits):
    n_slots = n_pages + 1
    rows = lax.broadcasted_iota(I32, (TILE, TILE), 0)
    for p in range(n_slots):
        sc = s_ref[p].T
        if p == n_pages:
            sc = jnp.where(rows == 0, sc, jnp.nan)
        sc_ref[p] = sc
    for p in range(n_slots, sc_ref.shape[0]):
        sc_ref[p] = jnp.full((TILE, TILE), jnp.nan, F32)
    n_groups = sc_ref.shape[0] // KEY_GROUP
    tv, lim = _select_topk(sc_ref, n_groups, ksel, n_bits)
    rows_f = rows.astype(F32)
    for p in range(n_slots):
        sc = sc_ref[p]
        sel = (sc > tv) | ((sc == tv) & ((rows_f + float(p * TILE)) < lim))
        bias_ref[p] = jnp.where(sel, 0.0, NEG_BIG).T


def _select_sample(scores, ksel):
    n_slots, ns, _ = scores.shape
    n_pages = n_slots - 1
    n_bits = (n_slots * TILE - 1).bit_length()
    return pl.pallas_call(
        functools.partial(_select_sample_kernel, n_pages=n_pages, ksel=ksel, n_bits=n_bits),
        out_shape=jax.ShapeDtypeStruct((n_slots, ns, TILE), F32),
        scratch_shapes=[pltpu.VMEM((-(-n_slots // KEY_GROUP) * KEY_GROUP, TILE, TILE), F32)],
        compiler_params=pltpu.CompilerParams(vmem_limit_bytes=VMEM_LIMIT),
        name="select_sample",
    )(scores)


def _attn_sample_kernel(pt_ref, q_ref, kn_ref, vn_ref, bias_ref, *rest, n_pages):
    kpages, vpages, o_ref = rest[:n_pages], rest[n_pages:2 * n_pages], rest[2 * n_pages]
    kvw = N_KV_HEADS * HEAD_DIM
    q = q_ref[0].astype(F32)
    lane = lax.broadcasted_iota(I32, (N_HEADS, kvw), 1)
    head = lax.broadcasted_iota(I32, (N_HEADS, kvw), 0)
    own = jnp.right_shift(lane, 6) == jnp.right_shift(head, 2)
    qrows = []
    for h in range(N_HEADS):
        qh = q[:, h * HEAD_DIM:(h + 1) * HEAD_DIM]
        qrows.append(jnp.concatenate([qh] * N_KV_HEADS, axis=1))
    qf = jnp.where(own, jnp.concatenate(qrows, axis=0), 0.0)
    qw = qf.astype(BF16)
    nt = (((1,), (1,)), ((), ()))
    scores = []
    for p in range(n_pages):
        kp = kpages[p][0].astype(BF16)
        s = jnp.dot(qw, kp, preferred_element_type=F32)
        scores.append(s + bias_ref[p, 0])
    s_new = jnp.sum(qf * kn_ref[0], axis=1, keepdims=True)
    lane1 = lax.broadcasted_iota(I32, (1, TILE), 1)
    scores.append(jnp.where(lane1 == 0, s_new, NEG_BIG) + bias_ref[n_pages, 0])
    m = scores[0].max(axis=1, keepdims=True)
    for s in scores[1:]:
        m = jnp.maximum(m, s.max(axis=1, keepdims=True))
    den = jnp.zeros((N_HEADS, 1), F32)
    out = jnp.zeros((N_HEADS, kvw), F32)
    for p in range(n_pages):
        e = jnp.exp2(scores[p] - m)
        den = den + e.sum(axis=1, keepdims=True)
        out = out + lax.dot_general(e.astype(BF16), vpages[p][0].astype(BF16), nt,
                                    preferred_element_type=F32)
    e = jnp.exp2(scores[n_pages] - m)
    den = den + e.sum(axis=1, keepdims=True)
    out = out + e[:, 0:1] * vn_ref[0]
    out = out / den
    rolled = jnp.concatenate([pltpu.roll(out[:, i * TILE:(i + 1) * TILE], HEAD_DIM, 1)
                              for i in range(kvw // TILE)], axis=1)
    half = lax.broadcasted_iota(I32, (1, TILE), 1) < HEAD_DIM
    pieces = []
    for m2 in range(N_HEADS // 2):
        g = (2 * m2) // Q_PER_KV
        blk = (g // 2) * TILE
        a_src, b_src = (rolled, out) if g % 2 == 1 else (out, rolled)
        a = a_src[2 * m2:2 * m2 + 1, blk:blk + TILE]
        b = b_src[2 * m2 + 1:2 * m2 + 2, blk:blk + TILE]
        pieces.append(jnp.where(half, a, b))
    o_ref[0] = jnp.concatenate(pieces, axis=1)


def _attn_sample(page_table, q_s, k_s, v_s, bias, cache_k, cache_v):
    ns, n_pages = page_table.shape
    kvw = N_KV_HEADS * HEAD_DIM
    qw = N_HEADS * HEAD_DIM
    page = lambda p: pl.BlockSpec((1, kvw, TILE), lambda s, pt: (pt[s, p], 0, 0))
    one = lambda s, pt: (s, 0, 0)
    grid_spec = pltpu.PrefetchScalarGridSpec(
        num_scalar_prefetch=1, grid=(ns,),
        in_specs=[pl.BlockSpec((1, 1, qw), one), pl.BlockSpec((1, 1, kvw), one),
                  pl.BlockSpec((1, 1, kvw), one),
                  pl.BlockSpec((n_pages + 1, 1, 1, TILE), lambda s, pt: (0, s, 0, 0))]
        + [page(p) for p in range(n_pages)] * 2,
        out_specs=pl.BlockSpec((1, 1, qw), one))
    return pl.pallas_call(
        functools.partial(_attn_sample_kernel, n_pages=n_pages),
        grid_spec=grid_spec,
        out_shape=jax.ShapeDtypeStruct((ns, 1, qw), F32),
        compiler_params=_cparams(("arbitrary",)),
        name="attn_sample",
    )(page_table, q_s, k_s, v_s, bias, *([cache_k] * n_pages), *([cache_v] * n_pages))


def _ssd_sample_kernel(xbc_ref, dtr_ref, sc_ref, h0_ref, cw_ref, cb_ref, dtb_ref, alog_ref, dsk_ref,
                       e_ref, y_ref, h_ref, conv_ref, *, n_heads, sb):
    d_inner = n_heads * SSD_HEAD_DIM
    hpg = n_heads // SSD_GROUPS
    new = xbc_ref[...]
    cdim = new.shape[1]
    u = cb_ref[...] + cw_ref[CONV_W - 1:CONV_W, :] * new
    for k in range(CONV_W - 1):
        u = u + cw_ref[k:k + 1, :] * sc_ref[:, k * cdim:(k + 1) * cdim]
    for k in range(CONV_W - 2):
        conv_ref[:, k * cdim:(k + 1) * cdim] = sc_ref[:, (k + 1) * cdim:(k + 2) * cdim]
    conv_ref[:, (CONV_W - 2) * cdim:] = new
    act = _silu(u)
    xs = act[:, :d_inner]
    bm = act[:, d_inner:d_inner + SSD_GROUPS * D_STATE]
    cm = act[:, d_inner + SSD_GROUPS * D_STATE:]
    dt = _softplus(dtr_ref[...] + dtb_ref[...])
    decay = jnp.exp(dt * (-jnp.exp(alog_ref[...])))
    dec_x = _expand_heads(decay, e_ref)
    xdt = xs * _expand_heads(dt, e_ref)
    for i in range(sb):
        bcols, ccols = [], []
        for g in range(SSD_GROUPS):
            brow = bm[i:i + 1, g * D_STATE:(g + 1) * D_STATE]
            crow = cm[i:i + 1, g * D_STATE:(g + 1) * D_STATE]
            bcols.append(jnp.broadcast_to(brow, (TILE, D_STATE)).T)
            ccols.append(jnp.broadcast_to(crow, (TILE, D_STATE)).T)
        pieces = []
        for pr in range(n_heads // 2):
            g = (2 * pr) // hpg
            sl = slice(pr * TILE, (pr + 1) * TILE)
            ht = h0_ref[i, sl, :].T
            hn = ht * dec_x[i:i + 1, sl] + bcols[g] * xdt[i:i + 1, sl]
            pieces.append(jnp.sum(hn * ccols[g], axis=0, keepdims=True))
            h_ref[i, sl, :] = hn.T
        y_ref[i:i + 1, :] = jnp.concatenate(pieces, axis=1) + dsk_ref[...] * xs[i:i + 1, :]


def _ssd_sample(xbc_s, dtr_s, state_conv, state_ssm, cw, cb, dtb, alog, dsk, e_mat, n_heads, sb):
    ns, cdim = xbc_s.shape
    d_inner = n_heads * SSD_HEAD_DIM
    row = lambda i: (i, 0)
    row3 = lambda i: (i, 0, 0)
    fix = lambda i: (0, 0)
    return pl.pallas_call(
        functools.partial(_ssd_sample_kernel, n_heads=n_heads, sb=sb),
        grid=(ns // sb,),
        in_specs=[pl.BlockSpec((sb, cdim), row), pl.BlockSpec((sb, TILE), row),
                  pl.BlockSpec((sb, (CONV_W - 1) * cdim), row), pl.BlockSpec((sb, d_inner, D_STATE), row3),
                  pl.BlockSpec(cw.shape, fix), pl.BlockSpec(cb.shape, fix), pl.BlockSpec(dtb.shape, fix),
                  pl.BlockSpec(alog.shape, fix), pl.BlockSpec(dsk.shape, fix), pl.BlockSpec(e_mat.shape, fix)],
        out_specs=[pl.BlockSpec((sb, d_inner), row), pl.BlockSpec((sb, d_inner, D_STATE), row3),
                   pl.BlockSpec((sb, (CONV_W - 1) * cdim), row)],
        out_shape=[jax.ShapeDtypeStruct((ns, d_inner), F32),
                   jax.ShapeDtypeStruct((ns, d_inner, D_STATE), F32),
                   jax.ShapeDtypeStruct((ns, (CONV_W - 1) * cdim), F32)],
        compiler_params=_cparams(("arbitrary",)),
        name="ssd_sample",
    )(xbc_s, dtr_s, state_conv, state_ssm, cw, cb, dtb, alog, dsk, e_mat)


def _sample_acts_kernel(a_ref, y_ref, zs_ref, gn_ref, attn_ref, ssd_ref):
    attn_ref[...] = a_ref[...].astype(BF16)
    y = y_ref[...] * zs_ref[...].astype(F32)
    gn_w = y.shape[1] // SSD_GROUPS
    outs = []
    for g in range(SSD_GROUPS):
        yg = y[:, g * gn_w:(g + 1) * gn_w]
        outs.append(yg * lax.rsqrt(jnp.mean(yg * yg, axis=-1, keepdims=True) + EPS))
    ssd_ref[...] = (jnp.concatenate(outs, axis=1) * gn_ref[...]).astype(BF16)


def _sample_acts(attn_s, y_s, zs, gn, blk):
    ns = attn_s.shape[0]
    fix = lambda i: (0, 0)
    return pl.pallas_call(
        _sample_acts_kernel,
        grid=(1,),
        in_specs=[pl.BlockSpec(attn_s.shape, fix), pl.BlockSpec(y_s.shape, fix),
                  pl.BlockSpec((ns, zs.shape[1]), lambda i: (blk, 0)), pl.BlockSpec(gn.shape, fix)],
        out_specs=[pl.BlockSpec(attn_s.shape, fix), pl.BlockSpec(y_s.shape, fix)],
        out_shape=[jax.ShapeDtypeStruct(attn_s.shape, BF16), jax.ShapeDtypeStruct(y_s.shape, BF16)],
        compiler_params=_cparams(("arbitrary",)),
        name="sample_acts",
    )(attn_s, y_s, zs, gn)


def _mix_kernel(x_ref, a_ref, s_ref, gt_ref, wao_ref, wso_ref, wo_ref, o_ref):
    d = x_ref.shape[1]
    gt = gt_ref[...].astype(F32)
    ao = jnp.dot(a_ref[...], wao_ref[...], preferred_element_type=F32)
    so = jnp.dot(s_ref[...], wso_ref[...], preferred_element_type=F32)
    mixed = (gt[:, :d] * ao + gt[:, d:] * so).astype(BF16)
    o_ref[...] = x_ref[...] + jnp.dot(mixed, wo_ref[...], preferred_element_type=F32)


def _mix(x1, attn, ssd, gates, wao, wso, wo, tm, rows, act_row0):
    d = x1.shape[1]
    r0, r = rows
    blk0 = r0 // tm
    act0 = act_row0 // tm
    row = lambda i: (blk0 + i, 0)
    act = lambda i: (act0 + i, 0)
    return pl.pallas_call(
        _mix_kernel,
        grid=(r // tm,),
        in_specs=[pl.BlockSpec((tm, d), row), pl.BlockSpec((tm, attn.shape[1]), act),
                  pl.BlockSpec((tm, ssd.shape[1]), act), pl.BlockSpec((tm, gates.shape[1]), row),
                  _const_spec(wao.shape), _const_spec(wso.shape), _const_spec(wo.shape)],
        out_specs=pl.BlockSpec((tm, d), lambda i: (i, 0)),
        out_shape=jax.ShapeDtypeStruct((r, d), F32),
        compiler_params=_cparams(("arbitrary",)),
        name="mix",
    )(x1, attn, ssd, gates, wao, wso, wo)


def _split_weight_kernel(wt_ref, hi_ref, lo_ref, *, n_cols):
    blk = wt_ref.shape[0]
    row = pl.program_id(0) * blk + lax.broadcasted_iota(I32, (blk, 1), 0)
    w = jnp.where(row < n_cols, wt_ref[...], 0.0)
    hi, lo = _split_bf16(w.T)
    hi_ref[...] = hi
    lo_ref[...] = lo


def _split_weight(wt):
    n_cols, d = wt.shape
    blk = 4 * TILE
    n_blk = pl.cdiv(n_cols, blk)
    return pl.pallas_call(
        functools.partial(_split_weight_kernel, n_cols=n_cols),
        grid=(n_blk,),
        in_specs=[pl.BlockSpec((blk, d), lambda i: (i, 0))],
        out_specs=[pl.BlockSpec((d, blk), lambda i: (0, i)), pl.BlockSpec((d, blk), lambda i: (0, i))],
        out_shape=[jax.ShapeDtypeStruct((d, n_blk * blk), BF16), jax.ShapeDtypeStruct((d, n_blk * blk), BF16)],
        compiler_params=_cparams(("arbitrary",)),
        name="split_weight",
    )(wt)


def _prompt_cache_kernel(k_ref, v_ref, kiw_ref, km_ref, vm_ref, kiwm_ref, kt_ref, vt_ref, kit_ref):
    n_tiles = k_ref.shape[0] // TILE
    for src, meta, dst, width in ((k_ref, km_ref, kt_ref, k_ref.shape[1]), (v_ref, vm_ref, vt_ref, v_ref.shape[1]),
                                  (kiw_ref, kiwm_ref, kit_ref, IDX_DIM)):
        dst[0, :, 0:N_META] = meta[...].T[:width, PAD_FRONT:]
        for j in range(n_tiles):
            dst[0, :, N_META + j * TILE:N_META + (j + 1) * TILE] = src[j * TILE:(j + 1) * TILE, :].T[:width, :]


def _prompt_cache(k, v, kiw, n_batch, seq, meta_blk):
    l_seq = seq + N_META
    kvw = k.shape[1]
    toks = lambda b: (b, 0)
    meta = lambda b: (meta_blk, 0)
    out = lambda b: (b, 0, 0)
    return pl.pallas_call(
        _prompt_cache_kernel,
        grid=(n_batch,),
        in_specs=[pl.BlockSpec((seq, kvw), toks), pl.BlockSpec((seq, kvw), toks), pl.BlockSpec((seq, TILE), toks),
                  pl.BlockSpec((TILE, kvw), meta), pl.BlockSpec((TILE, kvw), meta), pl.BlockSpec((TILE, TILE), meta)],
        out_specs=[pl.BlockSpec((1, kvw, l_seq), out), pl.BlockSpec((1, kvw, l_seq), out),
                   pl.BlockSpec((1, IDX_DIM, l_seq), out)],
        out_shape=[jax.ShapeDtypeStruct((n_batch, kvw, l_seq), F32), jax.ShapeDtypeStruct((n_batch, kvw, l_seq), F32),
                   jax.ShapeDtypeStruct((n_batch, IDX_DIM, l_seq), F32)],
        compiler_params=_cparams(("arbitrary",)),
        name="prompt_cache",
    )(k, v, kiw, k, v, kiw)


def kernel(x_prompt, x_sample, cache_k, cache_v, cache_kidx, state_ssm, state_conv, page_table, meta_tokens,
           g_ffn1, w_ffn1_gate, w_ffn1_up, w_ffn1_down, g_mix, w_in, g_q, g_k, conv_w, conv_b, dt_bias, a_log,
           d_skip, g_ssd_norm, w_attn_out, w_ssd_out, w_o, g_ffn2, w_ffn2_gate, w_ffn2_up, w_ffn2_down):
    depth = w_in.shape[0]
    assert depth == 1, "single-layer step"
    n_batch, seq, d_model = x_prompt.shape
    ns = x_sample.shape[0]
    assert x_sample.shape[1] == 1 and seq % TILE == 0 and ns == TILE
    page = cache_k.shape[2]
    assert page == TILE
    l_seq = seq + N_META
    r_tok = n_batch * seq
    r_total = r_tok + ns + TILE
    meta_blk = (r_tok + ns) // TILE
    n_heads = state_ssm.shape[2]
    d_inner = n_heads * SSD_HEAD_DIM
    assert n_heads <= TILE and n_heads % (2 * SSD_GROUPS) == 0
    cdim = d_inner + 2 * SSD_GROUPS * D_STATE
    nq, nkv = N_HEADS * HEAD_DIM, N_KV_HEADS * HEAD_DIM
    nqi = IDX_HEADS * IDX_DIM
    tm = _row_tile(r_total // TILE, 6)
    tm_tok = _row_tile(r_tok // TILE, 8)

    w_hi, w_lo = _split_weight(jnp.transpose(w_in[0]))
    offs = [0]
    for n in (nq, nkv, nkv, nqi, IDX_DIM, IDX_HEADS, d_inner, cdim, n_heads, d_model, d_model):
        offs.append(offs[-1] + n)
    col = lambda i, w=w_hi: w[:, offs[i]:offs[i + 1]]
    wqk = jnp.concatenate([col(0), col(1)], axis=1)
    wv = col(2)
    idx_cols = lambda w: jnp.concatenate([col(3, w), col(4, w), col(5, w),
                                          jnp.zeros((d_model, TILE - IDX_DIM - IDX_HEADS), BF16)], axis=1)
    wih, wil = idx_cols(w_hi), idx_cols(w_lo)
    wz = col(6)
    wx = col(7)
    wdt = jnp.concatenate([col(8), jnp.zeros((d_model, TILE - n_heads), BF16)], axis=1)
    wgate = jnp.concatenate([col(9), col(10)], axis=1)
    n_seg = (nq + nkv) // HEAD_DIM
    seg = (jnp.arange(nq + nkv)[:, None] // HEAD_DIM == jnp.arange(TILE)[None, :]).astype(BF16)
    segt = seg.T
    del n_seg
    gqk = jnp.concatenate([jnp.tile(g_q[0], N_HEADS), jnp.tile(g_k[0], N_KV_HEADS)])[None, :]
    row1 = lambda v: v.reshape(1, -1)
    pad_heads = lambda v, fill: jnp.concatenate([v, jnp.full((TILE - n_heads,), fill, F32)])[None, :]
    dtb = pad_heads(dt_bias[0], -1e4)
    alog = pad_heads(a_log[0], 0.0)
    dsk = jnp.repeat(d_skip[0], SSD_HEAD_DIM)[None, :]
    assert 3 * n_heads <= TILE
    e_row = jnp.arange(TILE)[:, None]
    e_mat = ((e_row < 3 * n_heads) & (e_row % n_heads == jnp.arange(d_inner)[None, :] // SSD_HEAD_DIM)).astype(BF16)
    gn = row1(g_ssd_norm[0])

    x_all = jnp.concatenate([x_prompt.reshape(r_tok, d_model), x_sample.reshape(ns, d_model),
                             jnp.zeros((PAD_FRONT, d_model), F32), meta_tokens.astype(F32)], axis=0)
    x1, hb = _ffn(x_all, row1(g_ffn1[0]), row1(g_mix[0]), w_ffn1_gate[0].astype(BF16),
                  w_ffn1_up[0].astype(BF16), w_ffn1_down[0].astype(BF16), tm)
    q, k, v, kb, vb, qi, kiw = _proj_attn(x1, row1(g_mix[0]), wqk, wv, wih, wil, seg, segt, gqk, tm)
    xbc, dtr = _proj_xbc(hb, wx, wdt, tm)
    zs, gates = _proj_gate(hb, wz, wgate, tm)

    ksel_p = min(TOP_K_MAX, l_seq // 4)
    attn = _dsa_prompt(q, qi, kiw, kb, vb, n_batch, seq, meta_blk, ksel_p)
    ssd, ssm_p, conv_p = _ssd_prompt(xbc, dtr, zs, conv_w[0], row1(conv_b[0]), dtb, alog, dsk, gn, e_mat,
                                     n_batch, seq, meta_blk, n_heads)

    n_pages = page_table.shape[1]
    ksel_s = min(TOP_K_MAX, (n_pages * page + 1) // 4)
    smp = lambda a: a[r_tok:r_tok + ns]
    kidx_t = jnp.transpose(cache_kidx[0], (0, 2, 1))
    scores = _idx_sample(page_table, smp(qi).reshape(ns, 1, nqi), smp(kiw).reshape(ns, 1, TILE), kidx_t)
    bias = _select_sample(scores.reshape(n_pages + 1, ns, TILE), ksel_s).reshape(n_pages + 1, ns, 1, TILE)
    ck = jnp.transpose(cache_k[0], (0, 2, 3, 1)).reshape(cache_k.shape[1], nkv, page)
    cv = jnp.transpose(cache_v[0], (0, 2, 3, 1)).reshape(cache_v.shape[1], nkv, page)
    attn_s = _attn_sample(page_table, smp(q).reshape(ns, 1, nq), smp(k).reshape(ns, 1, nkv),
                          smp(v).reshape(ns, 1, nkv), bias, ck, cv)
    y_s, ssm_s, conv_s = _ssd_sample(smp(xbc), smp(dtr), state_conv[0].reshape(ns, -1), state_ssm[0].reshape(ns, d_inner, D_STATE),
                                     conv_w[0], row1(conv_b[0]), dtb, alog, dsk, e_mat, n_heads, 8)
    attn_sb, ssd_sb = _sample_acts(attn_s.reshape(ns, nq), y_s, zs, gn, r_tok // ns)

    wao, wso, wo = w_attn_out[0].astype(BF16), w_ssd_out[0].astype(BF16), w_o[0].astype(BF16)
    ffn2 = (row1(g_ffn2[0]), row1(g_ffn2[0]), w_ffn2_gate[0].astype(BF16), w_ffn2_up[0].astype(BF16),
            w_ffn2_down[0].astype(BF16))
    x2 = _mix(x1, attn, ssd, gates, wao, wso, wo, tm_tok, (0, r_tok), 0)
    y_prompt = _ffn(x2, *ffn2, tm_tok, with_norm=False)[0].reshape(n_batch, seq, d_model)
    x2_s = _mix(x1, attn_sb, ssd_sb, gates, wao, wso, wo, ns, (r_tok, ns), 0)
    y_sample = _ffn(x2_s, *ffn2, ns, with_norm=False)[0].reshape(ns, 1, d_model)

    kt, vt, kit = _prompt_cache(k, v, kiw, n_batch, seq, meta_blk)
    heads_last = lambda a: jnp.transpose(a.reshape(1, n_batch, N_KV_HEADS, HEAD_DIM, l_seq), (0, 1, 4, 2, 3))
    k_prompt = heads_last(kt)
    v_prompt = heads_last(vt)
    kidx_prompt = jnp.transpose(kit, (0, 2, 1))[None]
    ssm_prompt = ssm_p.reshape(1, n_batch, n_heads, SSD_HEAD_DIM, D_STATE)
    conv_prompt = conv_p.reshape(1, n_batch, CONV_W - 1, cdim)
    k_sample = smp(k).reshape(1, ns, 1, N_KV_HEADS, HEAD_DIM)
    v_sample = smp(v).reshape(1, ns, 1, N_KV_HEADS, HEAD_DIM)
    kidx_sample = smp(kiw)[:, :IDX_DIM].reshape(1, ns, 1, IDX_DIM)
    ssm_sample = ssm_s.reshape(1, ns, n_heads, SSD_HEAD_DIM, D_STATE)
    conv_sample = conv_s.reshape(1, ns, CONV_W - 1, cdim)
    return (y_prompt, y_sample, k_prompt, v_prompt, kidx_prompt, ssm_prompt, conv_prompt,
            k_sample, v_sample, kidx_sample, ssm_sample, conv_sample)
```

```python
import functools

import jax
import jax.numpy as jnp
from jax import lax
from jax.experimental import pallas as pl
from jax.experimental.pallas import tpu as pltpu

F32 = jnp.float32
BF16 = jnp.bfloat16
I32 = jnp.int32
HIGHEST = lax.Precision.HIGHEST

EPS = 1e-6
N_META = 16
N_HEADS = 16
N_KV_HEADS = 4
HEAD_DIM = 64
Q_PER_KV = N_HEADS // N_KV_HEADS
IDX_HEADS = 4
IDX_DIM = 64
TOP_K_MAX = 256
SSD_HEAD_DIM = 64
SSD_GROUPS = 4
D_STATE = 128
CONV_W = 4
TILE = 128
PAD_FRONT = TILE - N_META
INT_MIN = -(2 ** 31)
NEG_BIG = -(2.0 ** 100)
Q_SCALE = HEAD_DIM ** -0.5 * 1.4426950408889634
VMEM_LIMIT = 56 * 1024 * 1024


def _cparams(sem, flags=None):
    return pltpu.CompilerParams(dimension_semantics=sem, vmem_limit_bytes=VMEM_LIMIT, flags=flags)


def _const_spec(shape):
    nd = len(shape)
    return pl.BlockSpec(shape, lambda *_: (0,) * nd, pipeline_mode=pl.Buffered(1))


def _rms(x, g):
    return x * lax.rsqrt(jnp.mean(x * x, axis=-1, keepdims=True) + EPS) * g


def _silu(x):
    return x * jax.nn.sigmoid(x)


def _row_tile(n_tiles, cap):
    for k in range(cap, 0, -1):
        if n_tiles % k == 0:
            return k * TILE
    return TILE


def _ffn_kernel(x_ref, g_ref, g2_ref, wg_ref, wu_ref, wd_ref, o_ref, h2_ref=None, *, ck):
    x = x_ref[...]
    h = _rms(x, g_ref[...]).astype(BF16)
    acc = jnp.zeros(x.shape, F32)
    for c in range(wg_ref.shape[1] // ck):
        sl = slice(c * ck, (c + 1) * ck)
        gate = jnp.dot(h, wg_ref[:, sl], preferred_element_type=F32)
        up = jnp.dot(h, wu_ref[:, sl], preferred_element_type=F32)
        act = (_silu(gate) * up).astype(BF16)
        acc = acc + jnp.dot(act, wd_ref[sl, :], preferred_element_type=F32)
    y = x + 0.5 * acc
    o_ref[...] = y
    if h2_ref is not None:
        h2_ref[...] = _rms(y, g2_ref[...]).astype(BF16)


def _ffn(x, g, g2, wg, wu, wd, tm, with_norm=True):
    r, d = x.shape
    dff = wg.shape[1]
    row = lambda i: (i, 0)
    out_specs = [pl.BlockSpec((tm, d), row)]
    out_shape = [jax.ShapeDtypeStruct((r, d), F32)]
    if with_norm:
        out_specs.append(pl.BlockSpec((tm, d), row))
        out_shape.append(jax.ShapeDtypeStruct((r, d), BF16))
    return pl.pallas_call(
        functools.partial(_ffn_kernel, ck=256),
        grid=(r // tm,),
        in_specs=[pl.BlockSpec((tm, d), row), _const_spec((1, d)), _const_spec((1, d)),
                  _const_spec((d, dff)), _const_spec((d, dff)), _const_spec((dff, d))],
        out_specs=out_specs,
        out_shape=out_shape,
        compiler_params=_cparams(("arbitrary",)),
        name="ffn",
    )(x, g, g2, wg, wu, wd)


def _proj_attn_kernel(x_ref, g_ref, wqk_ref, wv_ref, wih_ref, wil_ref, seg_ref, segt_ref, gqk_ref,
                      q_ref, k_ref, v_ref, kb_ref, vb_ref, qi_ref, kiw_ref):
    h = _rms(x_ref[...], g_ref[...])
    hb, h_lo = _split_bf16(h)
    qk = jnp.dot(hb, wqk_ref[...], preferred_element_type=F32)
    ss = jnp.dot((qk * qk).astype(BF16), seg_ref[...], preferred_element_type=F32)
    r = lax.rsqrt(ss * (1.0 / HEAD_DIM) + EPS)
    r_hi = r.astype(BF16)
    r_lo = (r - r_hi.astype(F32)).astype(BF16)
    rx = (jnp.dot(r_hi, segt_ref[...], preferred_element_type=F32)
          + jnp.dot(r_lo, segt_ref[...], preferred_element_type=F32))
    qkn = qk * rx * gqk_ref[...]
    nq = q_ref.shape[1]
    q_ref[...] = (qkn[:, :nq] * Q_SCALE).astype(BF16)
    k = qkn[:, nq:]
    k_ref[...] = k
    kb_ref[...] = k.astype(BF16)
    v = jnp.dot(hb, wv_ref[...], preferred_element_type=F32)
    v_ref[...] = v
    vb_ref[...] = v.astype(BF16)
    idx = _dot3(hb, h_lo, wih_ref[...], wil_ref[...])
    nqi = qi_ref.shape[1]
    qi_ref[...] = idx[:, :nqi]
    kiw_ref[...] = idx[:, nqi:]


def _proj_attn(x1, g, wqk, wv, wih, wil, seg, segt, gqk, tm):
    r, d = x1.shape
    nqk, nv = wqk.shape[1], wv.shape[1]
    nq = N_HEADS * HEAD_DIM
    nqi = IDX_HEADS * IDX_DIM
    row = lambda i: (i, 0)
    outs = [(nq, BF16), (nqk - nq, F32), (nv, F32), (nqk - nq, BF16), (nv, BF16), (nqi, F32), (TILE, F32)]
    return pl.pallas_call(
        _proj_attn_kernel,
        grid=(r // tm,),
        in_specs=[pl.BlockSpec((tm, d), row), _const_spec((1, d)), _const_spec(wqk.shape),
                  _const_spec(wv.shape), _const_spec(wih.shape), _const_spec(wil.shape), _const_spec(seg.shape),
                  _const_spec(segt.shape), _const_spec(gqk.shape)],
        out_specs=[pl.BlockSpec((tm, n), row) for n, _ in outs],
        out_shape=[jax.ShapeDtypeStruct((r, n), dt) for n, dt in outs],
        compiler_params=_cparams(("arbitrary",)),
        name="proj_attn",
    )(x1, g, wqk, wv, wih, wil, seg, segt, gqk)


def _proj_xbc_kernel(h_ref, wx_ref, wdt_ref, xbc_ref, dt_ref):
    h = h_ref[...]
    xbc_ref[...] = jnp.dot(h, wx_ref[...], preferred_element_type=F32)
    dt_ref[...] = jnp.dot(h, wdt_ref[...], preferred_element_type=F32)


def _proj_xbc(hb, wx, wdt, tm):
    r, d = hb.shape
    row = lambda i: (i, 0)
    return pl.pallas_call(
        _proj_xbc_kernel,
        grid=(r // tm,),
        in_specs=[pl.BlockSpec((tm, d), row), _const_spec(wx.shape), _const_spec(wdt.shape)],
        out_specs=[pl.BlockSpec((tm, wx.shape[1]), row), pl.BlockSpec((tm, wdt.shape[1]), row)],
        out_shape=[jax.ShapeDtypeStruct((r, wx.shape[1]), F32),
                   jax.ShapeDtypeStruct((r, wdt.shape[1]), F32)],
        compiler_params=_cparams(("arbitrary",)),
        name="proj_xbc",
    )(hb, wx, wdt)


def _proj_gate_kernel(h_ref, wz_ref, wg_ref, zs_ref, gt_ref):
    h = h_ref[...]
    zs_ref[...] = _silu(jnp.dot(h, wz_ref[...], preferred_element_type=F32)).astype(BF16)
    gt_ref[...] = jax.nn.sigmoid(jnp.dot(h, wg_ref[...], preferred_element_type=F32)).astype(BF16)


def _proj_gate(hb, wz, wg, tm):
    r, d = hb.shape
    row = lambda i: (i, 0)
    return pl.pallas_call(
        _proj_gate_kernel,
        grid=(r // tm,),
        in_specs=[pl.BlockSpec((tm, d), row), _const_spec(wz.shape), _const_spec(wg.shape)],
        out_specs=[pl.BlockSpec((tm, wz.shape[1]), row), pl.BlockSpec((tm, wg.shape[1]), row)],
        out_shape=[jax.ShapeDtypeStruct((r, wz.shape[1]), BF16),
                   jax.ShapeDtypeStruct((r, wg.shape[1]), BF16)],
        compiler_params=_cparams(("arbitrary",)),
        name="proj_gate",
    )(hb, wz, wg)


KEY_GROUP = 4


def _key_value(key):
    mag = jnp.where(key < 0, jnp.where(key == INT_MIN, 0x7F800000, -key), key)
    v = pltpu.bitcast(mag, F32)
    return jnp.where(key < 0, -v, v)


def _tree_sum(x):
    while x.shape[0] > 1:
        h = x.shape[0] // 2
        x = x[:h] + x[h:]
    return x[0]


def _count(sc_ref, n_groups, pred):
    def body(gi, acc):
        s = sc_ref[pl.ds(gi * KEY_GROUP, KEY_GROUP)]
        hit = jnp.where(pred(s, gi * (KEY_GROUP * TILE)), 1.0, 0.0)
        return acc + _tree_sum(hit.reshape(KEY_GROUP * TILE // 8, 8, TILE))
    part = lax.fori_loop(0, n_groups, body, jnp.zeros((8, TILE), F32))
    return jnp.sum(part, axis=0, keepdims=True)


def _any(flag):
    return jnp.max(jnp.where(flag, 1, 0)) > 0


def _kth_bitwise(sc_ref, n_groups, ksel):
    n_ge0 = _count(sc_ref, n_groups, lambda s, r0: s >= 0.0)
    t = jnp.where(n_ge0 >= ksel, jnp.zeros((1, TILE), I32), jnp.full((1, TILE), INT_MIN, I32))

    def bit_step(i, t):
        cand = t | jnp.left_shift(jnp.int32(1), 30 - i)
        cv = _key_value(cand)
        n = _count(sc_ref, n_groups, lambda s, r0: s >= cv)
        return jnp.where(n >= ksel, cand, t)

    return lax.fori_loop(0, 31, bit_step, t)


def _tie_search(sc_ref, n_groups, tv, need, n_bits):
    shape = (KEY_GROUP, TILE, TILE)
    rows = (lax.broadcasted_iota(I32, shape, 0) * TILE + lax.broadcasted_iota(I32, shape, 1)).astype(F32)

    def bit_step(i, p):
        cand = p | jnp.left_shift(jnp.int32(1), n_bits - 1 - i)
        cf = cand.astype(F32)
        n = _count(sc_ref, n_groups, lambda s, r0: (s == tv) & ((rows + jnp.asarray(r0, F32)) < cf))
        return jnp.where(n < need, cand, p)
    return lax.fori_loop(0, n_bits, bit_step, jnp.zeros((1, TILE), I32)) + 1


def _select_topk(sc_ref, n_groups, ksel, n_bits):
    no_limit = jnp.full((1, TILE), float(1 << n_bits), F32)
    t = _kth_bitwise(sc_ref, n_groups, ksel)
    tv = _key_value(t)
    n_gt = _count(sc_ref, n_groups, lambda s, r0: s > tv)
    n_eq = _count(sc_ref, n_groups, lambda s, r0: s == tv)
    need = ksel - n_gt
    conflict = (t != INT_MIN) & (n_eq > need)
    lim = lax.cond(_any(conflict), lambda: _tie_search(sc_ref, n_groups, tv, need, n_bits).astype(F32),
                   lambda: no_limit)
    return tv, jnp.where(conflict, lim, no_limit)


def _split_bf16(x):
    hi = x.astype(BF16)
    return hi, (x - hi.astype(F32)).astype(BF16)


def _dot3(a_hi, a_lo, b_hi, b_lo):
    return (jnp.dot(a_hi, b_hi, preferred_element_type=F32)
            + jnp.dot(a_hi, b_lo, preferred_element_type=F32)
            + jnp.dot(a_lo, b_hi, preferred_element_type=F32))


VROWS = HEAD_DIM + 16


def _dsa_prompt_kernel(q_ref, qi_ref, kiwq_ref, kb_ref, vb_ref, kiw_ref, kbm_ref, vbm_ref, kiwm_ref, o_ref,
                       sc_ref, kk_ref, vt_ref, kihl_ref, rhs_ref, qa_ref,
                       m_ref, acc_ref, *, ksel, n_bits):
    b = pl.program_id(0)
    j = pl.program_id(1)
    nt = vb_ref.shape[0] // TILE + 1
    n_groups = lax.shift_right_logical(j + KEY_GROUP, KEY_GROUP.bit_length() - 1)
    n_pairs = lax.shift_right_logical(j + 2, 1)
    nkv = N_KV_HEADS
    gw = Q_PER_KV * TILE
    eye = (lax.broadcasted_iota(I32, (TILE, TILE), 0) == lax.broadcasted_iota(I32, (TILE, TILE), 1))

    @pl.when((b == 0) & (j == 0))
    def _():
        rhs_ref[...] = jnp.zeros(rhs_ref.shape, rhs_ref.dtype)
        ident = jnp.where(eye, 1.0, 0.0).astype(BF16)
        for g in range(nkv):
            for hh in range(Q_PER_KV):
                rhs_ref[g, 0:TILE, hh * TILE:(hh + 1) * TILE] = ident
        qa_ref[...] = jnp.zeros(qa_ref.shape, qa_ref.dtype)
        kk_ref[...] = jnp.zeros(kk_ref.shape, kk_ref.dtype)
        kihl_ref[...] = jnp.zeros(kihl_ref.shape, kihl_ref.dtype)
        ones_row = jnp.where(lax.broadcasted_iota(I32, (VROWS - HEAD_DIM, TILE), 0) == 0, 1.0, 0.0).astype(BF16)
        for c in range(vt_ref.shape[0]):
            vt_ref[c] = jnp.zeros(vt_ref.shape[1:], vt_ref.dtype)
            for g in range(nkv):
                vt_ref[c, g * VROWS + HEAD_DIM:(g + 1) * VROWS, :] = ones_row

    @pl.when(j == 0)
    def _():
        def stage(c, kb, vb, kiw):
            kk_ref[c] = kb
            vt = vb.astype(F32).T.astype(BF16)
            for g in range(nkv):
                vt_ref[c, g * VROWS:g * VROWS + HEAD_DIM, :] = vt[g * HEAD_DIM:(g + 1) * HEAD_DIM, :]
            ki = kiw[:, :IDX_DIM]
            hi = ki.astype(BF16).astype(F32)
            lo = (ki - hi).astype(BF16).astype(F32)
            kihl_ref[c] = jnp.concatenate([hi, lo, hi, jnp.zeros_like(hi)], axis=1).astype(BF16)

        stage(0, kbm_ref[...], vbm_ref[...], kiwm_ref[...])

        def prep(c, _):
            rows = pl.ds(pl.multiple_of((c - 1) * TILE, TILE), TILE)
            stage(c, kb_ref[rows, :], vb_ref[rows, :], kiw_ref[rows, :])
            return 0
        lax.fori_loop(1, nt, prep, 0)

    qt = q_ref[...].astype(F32).T.astype(BF16)
    for h in range(N_HEADS):
        g, hh = divmod(h, Q_PER_KV)
        r0 = TILE + (g % 2) * HEAD_DIM
        rhs_ref[g, r0:r0 + HEAD_DIM, hh * TILE:(hh + 1) * TILE] = qt[h * HEAD_DIM:(h + 1) * HEAD_DIM, :]
    qit = qi_ref[...].T
    for h in range(IDX_HEADS):
        hi, lo = _split_bf16(qit[h * IDX_DIM:(h + 1) * IDX_DIM, :])
        qa_ref[0:IDX_DIM, h * TILE:(h + 1) * TILE] = hi
        qa_ref[IDX_DIM:2 * IDX_DIM, h * TILE:(h + 1) * TILE] = hi
        qa_ref[2 * IDX_DIM:3 * IDX_DIM, h * TILE:(h + 1) * TILE] = lo
    wq = kiwq_ref[...].T[IDX_DIM:IDX_DIM + 8, :]
    wq = wq * (IDX_HEADS ** -0.5 * IDX_DIM ** -0.5)

    grp_rows = KEY_GROUP * TILE
    q_pos = j * TILE + lax.broadcasted_iota(I32, (grp_rows, TILE), 1)
    k_row = lax.broadcasted_iota(I32, (grp_rows, TILE), 0)

    def score_group(gi, _):
        khl = kihl_ref[pl.ds(gi * KEY_GROUP, KEY_GROUP)].reshape(grp_rows, 2 * TILE)
        d = jnp.dot(khl, qa_ref[...], preferred_element_type=F32)
        s = jnp.zeros((grp_rows, TILE), F32)
        for h in range(IDX_HEADS):
            s = s + wq[h:h + 1, :] * jnp.maximum(d[:, h * TILE:(h + 1) * TILE], 0.0)
        k_pos = k_row + gi * grp_rows
        valid = (k_pos <= q_pos) & (k_pos >= PAD_FRONT)
        sc_ref[pl.ds(gi * KEY_GROUP, KEY_GROUP)] = jnp.where(valid, s, jnp.nan).reshape(KEY_GROUP, TILE, TILE)
        return 0
    lax.fori_loop(0, n_groups, score_group, 0)

    tv, lim = _select_topk(sc_ref, n_groups, ksel, n_bits)

    m_ref[...] = jnp.full(m_ref.shape, NEG_BIG, F32)
    acc_ref[...] = jnp.zeros(acc_ref.shape, F32)

    pair_shape = (2, TILE, TILE)
    pair_row = (lax.broadcasted_iota(I32, pair_shape, 0) * TILE
                + lax.broadcasted_iota(I32, pair_shape, 1)).astype(F32)

    def attend_pair(i, _):
        sc = sc_ref[pl.ds(2 * i, 2)]
        sel = (sc > tv) | ((sc == tv) & ((pair_row + (i * (2 * TILE)).astype(F32)) < lim))
        bias = jnp.where(sel, 0.0, NEG_BIG).astype(BF16).reshape(2 * TILE, TILE)
        kc = kk_ref[pl.ds(2 * i, 2)].reshape(2 * TILE, nkv * HEAD_DIM)
        vt = jnp.concatenate([vt_ref[2 * i], vt_ref[2 * i + 1]], axis=1)
        m_all = m_ref[...]
        s = []
        for g in range(nkv):
            half = (g // 2) * TILE
            lhs = jnp.concatenate([bias, kc[:, half:half + TILE]], axis=1)
            s.append(jnp.dot(lhs, rhs_ref[g], preferred_element_type=F32))
        m_new = []
        for g in range(nkv):
            m_g = jnp.maximum(m_all[g:g + 1, :], jnp.max(s[g], axis=0, keepdims=True))
            alpha = jnp.exp2(m_all[g:g + 1, :] - m_g)
            p = jnp.exp2(s[g] - m_g).astype(BF16)
            pv = jnp.dot(vt[g * VROWS:(g + 1) * VROWS, :], p, preferred_element_type=F32)
            acc_ref[g] = alpha * acc_ref[g] + pv
            m_new.append(m_g)
        m_ref[...] = jnp.concatenate(m_new + [m_all[nkv:, :]], axis=0)
        return 0
    lax.fori_loop(0, n_pairs, attend_pair, 0)

    pieces = []
    for g in range(nkv):
        acc = acc_ref[g]
        o = acc[:HEAD_DIM] / acc[HEAD_DIM:HEAD_DIM + 1]
        for hh in range(Q_PER_KV):
            pieces.append(o[:, hh * TILE:(hh + 1) * TILE])
    o_ref[...] = jnp.concatenate(pieces, axis=0).T.astype(BF16)


def _seq_tile_map(seq, meta_blk):
    n_tok = seq // TILE
    meta = meta_blk if callable(meta_blk) else (lambda b: meta_blk)
    return lambda b, j: (jnp.where(j == 0, meta(b), b * n_tok + j - 1), 0)


def _dsa_prompt(q, qi, kiw, kb, vb, n_batch, seq, meta_blk, ksel):
    lp = seq + TILE
    nt = lp // TILE
    nt_pad = -(-nt // KEY_GROUP) * KEY_GROUP
    qw = N_HEADS * HEAD_DIM
    kvw = N_KV_HEADS * HEAD_DIM
    gw = Q_PER_KV * TILE
    tile = _seq_tile_map(seq, meta_blk)
    toks = lambda b, j: (b, 0)
    meta = lambda b, j: (meta_blk, 0)
    n_bits = max(1, (lp - 1).bit_length())
    return pl.pallas_call(
        functools.partial(_dsa_prompt_kernel, ksel=ksel, n_bits=n_bits),
        grid=(n_batch, nt),
        in_specs=[pl.BlockSpec((TILE, qw), tile), pl.BlockSpec((TILE, IDX_HEADS * IDX_DIM), tile),
                  pl.BlockSpec((TILE, TILE), tile), pl.BlockSpec((seq, kvw), toks),
                  pl.BlockSpec((seq, kvw), toks), pl.BlockSpec((seq, TILE), toks),
                  pl.BlockSpec((TILE, kvw), meta), pl.BlockSpec((TILE, kvw), meta),
                  pl.BlockSpec((TILE, TILE), meta)],
        out_specs=pl.BlockSpec((TILE, qw), _seq_tile_map(seq, lambda b: n_batch * (seq // TILE) + b)),
        out_shape=jax.ShapeDtypeStruct((n_batch * (seq + TILE), qw), BF16),
        scratch_shapes=[pltpu.VMEM((nt_pad, TILE, TILE), F32),
                        pltpu.VMEM((nt_pad, TILE, kvw), BF16),
                        pltpu.VMEM((nt_pad, N_KV_HEADS * VROWS, TILE), BF16),
                        pltpu.VMEM((nt_pad, TILE, 2 * TILE), BF16),
                        pltpu.VMEM((N_KV_HEADS, 2 * TILE, gw), BF16),
                        pltpu.VMEM((2 * TILE, IDX_HEADS * TILE), BF16),
                        pltpu.VMEM((8, gw), F32),
                        pltpu.VMEM((N_KV_HEADS, VROWS, gw), F32)],
        compiler_params=_cparams(("arbitrary", "arbitrary")),
        name="dsa_prompt",
    )(q, qi, kiw, kb, vb, kiw, kb, vb, kiw)


def _expand_heads(v, e_ref):
    n_heads = e_ref.shape[1] // SSD_HEAD_DIM
    lane = lax.broadcasted_iota(I32, (1, v.shape[1]), 1)
    v = jnp.where(lane < n_heads, v, 0.0)
    hi = v.astype(BF16).astype(F32)
    r1 = v - hi
    mid = r1.astype(BF16).astype(F32)
    lo = (r1 - mid).astype(BF16).astype(F32)
    packed = (hi + pltpu.roll(mid, n_heads, 1) + pltpu.roll(lo, 2 * n_heads, 1)).astype(BF16)
    return jnp.dot(packed, e_ref[...], preferred_element_type=F32)


def _softplus(x):
    return jnp.maximum(x, 0.0) + jnp.log1p(jnp.exp(-jnp.abs(x)))


def _ssd_prompt_kernel(xbc_ref, dtr_ref, zs_ref, cw_ref, cb_ref, dtb_ref, alog_ref, dsk_ref, gn_ref,
                       e_ref, y_ref, ssm_ref, conv_ref, xp_ref, ht_ref, *, n_heads):
    c = pl.program_id(1)
    nc = pl.num_programs(1)
    d_inner = n_heads * SSD_HEAD_DIM
    hpg = n_heads // SSD_GROUPS
    gwid = hpg * SSD_HEAD_DIM
    gn_w = d_inner // SSD_GROUPS

    @pl.when(c == 0)
    def _():
        xp_ref[0:8, :] = jnp.zeros((8, xp_ref.shape[1]), F32)
        ht_ref[...] = jnp.zeros(ht_ref.shape, F32)

    xp_ref[8:8 + TILE, :] = xbc_ref[...]
    u = cb_ref[...] + cw_ref[CONV_W - 1:CONV_W, :] * xp_ref[8:8 + TILE, :]
    for k in range(CONV_W - 1):
        off = 8 - (CONV_W - 1) + k
        u = u + cw_ref[k:k + 1, :] * xp_ref[off:off + TILE, :]
    tail = xp_ref[8 + TILE - (CONV_W - 1):8 + TILE, :]
    xp_ref[8 - (CONV_W - 1):8, :] = tail

    row = lax.broadcasted_iota(I32, (TILE, 1), 0)
    live = jnp.where((c > 0) | (row >= PAD_FRONT), 1.0, 0.0)
    act = _silu(u) * live
    xs = act[:, :d_inner]
    bm = act[:, d_inner:d_inner + SSD_GROUPS * D_STATE]
    cm = act[:, d_inner + SSD_GROUPS * D_STATE:]

    dt = _softplus(dtr_ref[...] + dtb_ref[...]) * live
    a = dt * (-jnp.exp(alog_ref[...]))
    ri = lax.broadcasted_iota(I32, (TILE, TILE), 0)
    ci = lax.broadcasted_iota(I32, (TILE, TILE), 1)
    tril = ri >= ci
    acs = jnp.dot(jnp.where(tril, 1.0, 0.0), a, preferred_element_type=F32, precision=HIGHEST)
    acs_t = acs.T
    dt_t = dt.T
    acs_last = acs[TILE - 1:TILE, :]
    w_in = _expand_heads(jnp.exp(acs_last - acs) * dt, e_ref)
    w_out = _expand_heads(jnp.exp(acs), e_ref)
    xw = (xs * w_in).astype(BF16)
    xsb = xs.astype(BF16)
    first_head = lax.broadcasted_iota(I32, (1, 2 * SSD_HEAD_DIM), 1) < SSD_HEAD_DIM

    ys = []
    for g in range(SSD_GROUPS):
        bg = bm[:, g * D_STATE:(g + 1) * D_STATE]
        cg = cm[:, g * D_STATE:(g + 1) * D_STATE].astype(BF16)
        bgt = bg.T.astype(BF16)
        cb = jnp.dot(cg, bgt, preferred_element_type=F32)
        xw_g = xw[:, g * gwid:(g + 1) * gwid]
        xs_g = xsb[:, g * gwid:(g + 1) * gwid]
        h_prev = ht_ref[g]
        y_off = jnp.dot(cg, h_prev.astype(BF16), preferred_element_type=F32)
        y_g = y_off * w_out[:, g * gwid:(g + 1) * gwid]
        pairs = []
        for e0 in range(0, hpg, 2):
            x_pair = xs_g[:, e0 * SSD_HEAD_DIM:(e0 + 2) * SSD_HEAD_DIM]
            prods = []
            for e in (e0, e0 + 1):
                he = g * hpg + e
                seg = acs[:, he:he + 1] - acs_t[he:he + 1, :]
                m = (cb * jnp.exp(jnp.where(tril, seg, -jnp.inf)) * dt_t[he:he + 1, :]).astype(BF16)
                prods.append(jnp.dot(m, x_pair, preferred_element_type=F32))
            pairs.append(jnp.where(first_head, prods[0], prods[1]))
        y_g = y_g + jnp.concatenate(pairs, axis=1)
        states = jnp.dot(bgt, xw_g, preferred_element_type=F32)
        ht_ref[g] = h_prev * w_out[TILE - 1:TILE, g * gwid:(g + 1) * gwid] + states
        ys.append(y_g)
    y = jnp.concatenate(ys, axis=1) + dsk_ref[...] * xs
    y = y * zs_ref[...].astype(F32)
    outs = []
    for g in range(SSD_GROUPS):
        yg = y[:, g * gn_w:(g + 1) * gn_w]
        outs.append(yg * lax.rsqrt(jnp.mean(yg * yg, axis=-1, keepdims=True) + EPS))
    y_ref[...] = (jnp.concatenate(outs, axis=1) * gn_ref[...]).astype(BF16)

    @pl.when(c == nc - 1)
    def _():
        for g in range(SSD_GROUPS):
            ssm_ref[0, g * gwid:(g + 1) * gwid, :] = ht_ref[g].T
        conv_ref[0] = tail


def _ssd_prompt(xbc, dtr, zs, cw, cb, dtb, alog, dsk, gn, e_mat, n_batch, seq, meta_blk, n_heads):
    nt = seq // TILE + 1
    d_inner = n_heads * SSD_HEAD_DIM
    cdim = xbc.shape[1]
    tile = _seq_tile_map(seq, meta_blk)
    fix = lambda b, c: (0, 0)
    return pl.pallas_call(
        functools.partial(_ssd_prompt_kernel, n_heads=n_heads),
        grid=(n_batch, nt),
        in_specs=[pl.BlockSpec((TILE, cdim), tile), pl.BlockSpec((TILE, TILE), tile),
                  pl.BlockSpec((TILE, d_inner), tile),
                  pl.BlockSpec(cw.shape, fix), pl.BlockSpec(cb.shape, fix), pl.BlockSpec(dtb.shape, fix),
                  pl.BlockSpec(alog.shape, fix), pl.BlockSpec(dsk.shape, fix), pl.BlockSpec(gn.shape, fix),
                  pl.BlockSpec(e_mat.shape, fix)],
        out_specs=[pl.BlockSpec((TILE, d_inner), _seq_tile_map(seq, lambda b: n_batch * (seq // TILE) + b)),
                   pl.BlockSpec((1, d_inner, D_STATE), lambda b, c: (b, 0, 0)),
                   pl.BlockSpec((1, CONV_W - 1, cdim), lambda b, c: (b, 0, 0))],
        out_shape=[jax.ShapeDtypeStruct((n_batch * (seq + TILE), d_inner), BF16),
                   jax.ShapeDtypeStruct((n_batch, d_inner, D_STATE), F32),
                   jax.ShapeDtypeStruct((n_batch, CONV_W - 1, cdim), F32)],
        scratch_shapes=[pltpu.VMEM((8 + TILE, cdim), F32),
                        pltpu.VMEM((SSD_GROUPS, D_STATE, d_inner // SSD_GROUPS), F32)],
        compiler_params=_cparams(("arbitrary", "arbitrary")),
        name="ssd_prompt",
    )(xbc, dtr, zs, cw, cb, dtb, alog, dsk, gn, e_mat)


IDX_SEQS = 2


def _idx_sample_kernel(pt_ref, qi_ref, kiw_ref, *rest, n_pages):
    pages, s_ref = rest[:IDX_SEQS * n_pages], rest[IDX_SEQS * n_pages]
    lane = lax.broadcasted_iota(I32, (1, TILE), 1)
    for u in range(IDX_SEQS):
        qi = qi_ref[u]
        kiw = kiw_ref[u]
        q8 = jnp.concatenate([qi[:, h * IDX_DIM:(h + 1) * IDX_DIM] for h in range(IDX_HEADS)]
                             + [jnp.zeros((8 - IDX_HEADS, IDX_DIM), F32)], axis=0)
        w_col = jnp.broadcast_to(kiw, (TILE, TILE)).T[IDX_DIM:IDX_DIM + 8, 0:1]
        w_col = w_col * (IDX_HEADS ** -0.5 * IDX_DIM ** -0.5)
        q_hi, q_lo = _split_bf16(q8)
        q3 = jnp.concatenate([q_hi, q_lo, q_hi], axis=1)
        for p in range(n_pages):
            k_hi, k_lo = _split_bf16(pages[u * n_pages + p][0])
            d = jnp.dot(q3, jnp.concatenate([k_hi, k_hi, k_lo], axis=0), preferred_element_type=F32)
            s_ref[p, u] = jnp.sum(w_col * jnp.maximum(d, 0.0), axis=0, keepdims=True)
        d_new = jnp.sum(q8 * kiw[:, :IDX_DIM], axis=1, keepdims=True)
        s_new = jnp.sum(w_col * jnp.maximum(d_new, 0.0), axis=0, keepdims=True)
        s_ref[n_pages, u] = jnp.where(lane == 0, s_new, 0.0)


def _idx_sample(page_table, qi_s, kiw_s, cache_kidx):
    ns, n_pages = page_table.shape
    n_slots = n_pages + 1
    assert ns % IDX_SEQS == 0
    page = lambda u, p: pl.BlockSpec((1, IDX_DIM, TILE), lambda s, pt: (pt[IDX_SEQS * s + u, p], 0, 0))
    some = lambda s, pt: (s, 0, 0)
    grid_spec = pltpu.PrefetchScalarGridSpec(
        num_scalar_prefetch=1, grid=(ns // IDX_SEQS,),
        in_specs=[pl.BlockSpec((IDX_SEQS, 1, qi_s.shape[2]), some), pl.BlockSpec((IDX_SEQS, 1, TILE), some)]
        + [page(u, p) for u in range(IDX_SEQS) for p in range(n_pages)],
        out_specs=pl.BlockSpec((n_slots, IDX_SEQS, 1, TILE), lambda s, pt: (0, s, 0, 0)))
    return pl.pallas_call(
        functools.partial(_idx_sample_kernel, n_pages=n_pages),
        grid_spec=grid_spec,
        out_shape=jax.ShapeDtypeStruct((n_slots, ns, 1, TILE), F32),
        compiler_params=_cparams(("arbitrary",)),
        name="idx_sample",
    )(page_table, qi_s, kiw_s, *([cache_kidx] * (IDX_SEQS * n_pages)))


def _select_sample_kernel(s_ref, bias_ref, sc_ref, *, n_pages, ksel, n_bits):
    n_slots = n_pages + 1
    rows = lax.broadcasted_iota(I32, (TILE, TILE), 0)
    for p in range(n_slots):
        sc = s_ref[p].T
        if p == n_pages:
            sc = jnp.where(rows == 0, sc, jnp.nan)
        sc_ref[p] = sc
    for p in range(n_slots, sc_ref.shape[0]):
        sc_ref[p] = jnp.full((TILE, TILE), jnp.nan, F32)
    n_groups = sc_ref.shape[0] // KEY_GROUP
    tv, lim = _select_topk(sc_ref, n_groups, ksel, n_bits)
    rows_f = rows.astype(F32)
    for p in range(n_slots):
        sc = sc_ref[p]
        sel = (sc > tv) | ((sc == tv) & ((rows_f + float(p * TILE)) < lim))
        bias_ref[p] = jnp.where(sel, 0.0, NEG_BIG).T


def _select_sample(scores, ksel):
    n_slots, ns, _ = scores.shape
    n_pages = n_slots - 1
    n_bits = (n_slots * TILE - 1).bit_length()
    return pl.pallas_call(
        functools.partial(_select_sample_kernel, n_pages=n_pages, ksel=ksel, n_bits=n_bits),
        out_shape=jax.ShapeDtypeStruct((n_slots, ns, TILE), F32),
        scratch_shapes=[pltpu.VMEM((-(-n_slots // KEY_GROUP) * KEY_GROUP, TILE, TILE), F32)],
        compiler_params=pltpu.CompilerParams(vmem_limit_bytes=VMEM_LIMIT),
        name="select_sample",
    )(scores)


def _attn_sample_kernel(pt_ref, q_ref, kn_ref, vn_ref, bias_ref, *rest, n_pages):
    n_ops = IDX_SEQS * n_pages
    o_ref = rest[2 * n_ops]
    for u in range(IDX_SEQS):
        _attn_one_sample(u, q_ref, kn_ref, vn_ref, bias_ref, rest[u * n_pages:(u + 1) * n_pages],
                         rest[n_ops + u * n_pages:n_ops + (u + 1) * n_pages], o_ref, n_pages)


def _attn_one_sample(u, q_ref, kn_ref, vn_ref, bias_ref, kpages, vpages, o_ref, n_pages):
    kvw = N_KV_HEADS * HEAD_DIM
    q = q_ref[u].astype(F32)
    lane = lax.broadcasted_iota(I32, (N_HEADS, kvw), 1)
    head = lax.broadcasted_iota(I32, (N_HEADS, kvw), 0)
    own = jnp.right_shift(lane, 6) == jnp.right_shift(head, 2)
    qrows = []
    for h in range(N_HEADS):
        qh = q[:, h * HEAD_DIM:(h + 1) * HEAD_DIM]
        qrows.append(jnp.concatenate([qh] * N_KV_HEADS, axis=1))
    qf = jnp.where(own, jnp.concatenate(qrows, axis=0), 0.0)
    qw = qf.astype(BF16)
    nt = (((1,), (1,)), ((), ()))
    scores = []
    for p in range(n_pages):
        kp = kpages[p][0].astype(BF16)
        s = jnp.dot(qw, kp, preferred_element_type=F32)
        scores.append(s + bias_ref[p, u])
    s_new = jnp.sum(qf * kn_ref[u], axis=1, keepdims=True)
    lane1 = lax.broadcasted_iota(I32, (1, TILE), 1)
    scores.append(jnp.where(lane1 == 0, s_new, NEG_BIG) + bias_ref[n_pages, u])
    m = scores[0].max(axis=1, keepdims=True)
    for s in scores[1:]:
        m = jnp.maximum(m, s.max(axis=1, keepdims=True))
    den = jnp.zeros((N_HEADS, 1), F32)
    out = jnp.zeros((N_HEADS, kvw), F32)
    for p in range(n_pages):
        e = jnp.exp2(scores[p] - m)
        den = den + e.sum(axis=1, keepdims=True)
        out = out + lax.dot_general(e.astype(BF16), vpages[p][0].astype(BF16), nt,
                                    preferred_element_type=F32)
    e = jnp.exp2(scores[n_pages] - m)
    den = den + e.sum(axis=1, keepdims=True)
    out = out + e[:, 0:1] * vn_ref[u]
    out = out / den
    rolled = jnp.concatenate([pltpu.roll(out[:, i * TILE:(i + 1) * TILE], HEAD_DIM, 1)
                              for i in range(kvw // TILE)], axis=1)
    half = lax.broadcasted_iota(I32, (1, TILE), 1) < HEAD_DIM
    pieces = []
    for m2 in range(N_HEADS // 2):
        g = (2 * m2) // Q_PER_KV
        blk = (g // 2) * TILE
        a_src, b_src = (rolled, out) if g % 2 == 1 else (out, rolled)
        a = a_src[2 * m2:2 * m2 + 1, blk:blk + TILE]
        b = b_src[2 * m2 + 1:2 * m2 + 2, blk:blk + TILE]
        pieces.append(jnp.where(half, a, b))
    o_ref[u] = jnp.concatenate(pieces, axis=1)


def _attn_sample(page_table, q_s, k_s, v_s, bias, cache_k, cache_v):
    ns, n_pages = page_table.shape
    kvw = N_KV_HEADS * HEAD_DIM
    qw = N_HEADS * HEAD_DIM
    assert ns % IDX_SEQS == 0
    page = lambda u, p: pl.BlockSpec((1, kvw, TILE), lambda s, pt: (pt[IDX_SEQS * s + u, p], 0, 0))
    pages = [page(u, p) for u in range(IDX_SEQS) for p in range(n_pages)]
    some = lambda s, pt: (s, 0, 0)
    grid_spec = pltpu.PrefetchScalarGridSpec(
        num_scalar_prefetch=1, grid=(ns // IDX_SEQS,),
        in_specs=[pl.BlockSpec((IDX_SEQS, 1, qw), some), pl.BlockSpec((IDX_SEQS, 1, kvw), some),
                  pl.BlockSpec((IDX_SEQS, 1, kvw), some),
                  pl.BlockSpec((n_pages + 1, IDX_SEQS, 1, TILE), lambda s, pt: (0, s, 0, 0))]
        + pages * 2,
        out_specs=pl.BlockSpec((IDX_SEQS, 1, qw), some))
    return pl.pallas_call(
        functools.partial(_attn_sample_kernel, n_pages=n_pages),
        grid_spec=grid_spec,
        out_shape=jax.ShapeDtypeStruct((ns, 1, qw), F32),
        compiler_params=_cparams(("arbitrary",)),
        name="attn_sample",
    )(page_table, q_s, k_s, v_s, bias, *([cache_k] * (IDX_SEQS * n_pages)), *([cache_v] * (IDX_SEQS * n_pages)))


def _ssd_sample_kernel(xbc_ref, dtr_ref, sc_ref, h0_ref, cw_ref, cb_ref, dtb_ref, alog_ref, dsk_ref,
                       e_ref, y_ref, h_ref, conv_ref, *, n_heads, sb):
    d_inner = n_heads * SSD_HEAD_DIM
    hpg = n_heads // SSD_GROUPS
    new = xbc_ref[...]
    cdim = new.shape[1]
    u = cb_ref[...] + cw_ref[CONV_W - 1:CONV_W, :] * new
    for k in range(CONV_W - 1):
        u = u + cw_ref[k:k + 1, :] * sc_ref[:, k * cdim:(k + 1) * cdim]
    for k in range(CONV_W - 2):
        conv_ref[:, k * cdim:(k + 1) * cdim] = sc_ref[:, (k + 1) * cdim:(k + 2) * cdim]
    conv_ref[:, (CONV_W - 2) * cdim:] = new
    act = _silu(u)
    xs = act[:, :d_inner]
    bm = act[:, d_inner:d_inner + SSD_GROUPS * D_STATE]
    cm = act[:, d_inner + SSD_GROUPS * D_STATE:]
    dt = _softplus(dtr_ref[...] + dtb_ref[...])
    decay = jnp.exp(dt * (-jnp.exp(alog_ref[...])))
    dec_x = _expand_heads(decay, e_ref)
    xdt = xs * _expand_heads(dt, e_ref)
    for i in range(sb):
        bcols, ccols = [], []
        for g in range(SSD_GROUPS):
            brow = bm[i:i + 1, g * D_STATE:(g + 1) * D_STATE]
            crow = cm[i:i + 1, g * D_STATE:(g + 1) * D_STATE]
            bcols.append(jnp.broadcast_to(brow, (TILE, D_STATE)).T)
            ccols.append(jnp.broadcast_to(crow, (TILE, D_STATE)).T)
        pieces = []
        for pr in range(n_heads // 2):
            g = (2 * pr) // hpg
            sl = slice(pr * TILE, (pr + 1) * TILE)
            ht = h0_ref[i, sl, :].T
            hn = ht * dec_x[i:i + 1, sl] + bcols[g] * xdt[i:i + 1, sl]
            pieces.append(jnp.sum(hn * ccols[g], axis=0, keepdims=True))
            h_ref[i, sl, :] = hn.T
        y_ref[i:i + 1, :] = jnp.concatenate(pieces, axis=1) + dsk_ref[...] * xs[i:i + 1, :]


def _ssd_sample(xbc_s, dtr_s, state_conv, state_ssm, cw, cb, dtb, alog, dsk, e_mat, n_heads, sb):
    ns, cdim = xbc_s.shape
    d_inner = n_heads * SSD_HEAD_DIM
    row = lambda i: (i, 0)
    row3 = lambda i: (i, 0, 0)
    fix = lambda i: (0, 0)
    return pl.pallas_call(
        functools.partial(_ssd_sample_kernel, n_heads=n_heads, sb=sb),
        grid=(ns // sb,),
        in_specs=[pl.BlockSpec((sb, cdim), row), pl.BlockSpec((sb, TILE), row),
                  pl.BlockSpec((sb, (CONV_W - 1) * cdim), row), pl.BlockSpec((sb, d_inner, D_STATE), row3),
                  pl.BlockSpec(cw.shape, fix), pl.BlockSpec(cb.shape, fix), pl.BlockSpec(dtb.shape, fix),
                  pl.BlockSpec(alog.shape, fix), pl.BlockSpec(dsk.shape, fix), pl.BlockSpec(e_mat.shape, fix)],
        out_specs=[pl.BlockSpec((sb, d_inner), row), pl.BlockSpec((sb, d_inner, D_STATE), row3),
                   pl.BlockSpec((sb, (CONV_W - 1) * cdim), row)],
        out_shape=[jax.ShapeDtypeStruct((ns, d_inner), F32),
                   jax.ShapeDtypeStruct((ns, d_inner, D_STATE), F32),
                   jax.ShapeDtypeStruct((ns, (CONV_W - 1) * cdim), F32)],
        compiler_params=_cparams(("arbitrary",)),
        name="ssd_sample",
    )(xbc_s, dtr_s, state_conv, state_ssm, cw, cb, dtb, alog, dsk, e_mat)


def _sample_acts_kernel(a_ref, y_ref, zs_ref, gn_ref, attn_ref, ssd_ref):
    attn_ref[...] = a_ref[...].astype(BF16)
    y = y_ref[...] * zs_ref[...].astype(F32)
    gn_w = y.shape[1] // SSD_GROUPS
    outs = []
    for g in range(SSD_GROUPS):
        yg = y[:, g * gn_w:(g + 1) * gn_w]
        outs.append(yg * lax.rsqrt(jnp.mean(yg * yg, axis=-1, keepdims=True) + EPS))
    ssd_ref[...] = (jnp.concatenate(outs, axis=1) * gn_ref[...]).astype(BF16)


def _sample_acts(attn_s, y_s, zs, gn, blk):
    ns = attn_s.shape[0]
    fix = lambda i: (0, 0)
    return pl.pallas_call(
        _sample_acts_kernel,
        grid=(1,),
        in_specs=[pl.BlockSpec(attn_s.shape, fix), pl.BlockSpec(y_s.shape, fix),
                  pl.BlockSpec((ns, zs.shape[1]), lambda i: (blk, 0)), pl.BlockSpec(gn.shape, fix)],
        out_specs=[pl.BlockSpec(attn_s.shape, fix), pl.BlockSpec(y_s.shape, fix)],
        out_shape=[jax.ShapeDtypeStruct(attn_s.shape, BF16), jax.ShapeDtypeStruct(y_s.shape, BF16)],
        compiler_params=_cparams(("arbitrary",)),
        name="sample_acts",
    )(attn_s, y_s, zs, gn)


def _mix_kernel(x_ref, a_ref, s_ref, gt_ref, wao_ref, wso_ref, wo_ref, o_ref):
    d = x_ref.shape[1]
    gt = gt_ref[...].astype(F32)
    ao = jnp.dot(a_ref[...], wao_ref[...], preferred_element_type=F32)
    so = jnp.dot(s_ref[...], wso_ref[...], preferred_element_type=F32)
    mixed = (gt[:, :d] * ao + gt[:, d:] * so).astype(BF16)
    o_ref[...] = x_ref[...] + jnp.dot(mixed, wo_ref[...], preferred_element_type=F32)


def _mix(x1, attn, ssd, gates, wao, wso, wo, tm, rows, act_row0):
    d = x1.shape[1]
    r0, r = rows
    blk0 = r0 // tm
    act0 = act_row0 // tm
    row = lambda i: (blk0 + i, 0)
    act = lambda i: (act0 + i, 0)
    return pl.pallas_call(
        _mix_kernel,
        grid=(r // tm,),
        in_specs=[pl.BlockSpec((tm, d), row), pl.BlockSpec((tm, attn.shape[1]), act),
                  pl.BlockSpec((tm, ssd.shape[1]), act), pl.BlockSpec((tm, gates.shape[1]), row),
                  _const_spec(wao.shape), _const_spec(wso.shape), _const_spec(wo.shape)],
        out_specs=pl.BlockSpec((tm, d), lambda i: (i, 0)),
        out_shape=jax.ShapeDtypeStruct((r, d), F32),
        compiler_params=_cparams(("arbitrary",)),
        name="mix",
    )(x1, attn, ssd, gates, wao, wso, wo)


def _split_weight_kernel(wt_ref, hi_ref, lo_ref, *, n_cols):
    blk = wt_ref.shape[0]
    row = pl.program_id(0) * blk + lax.broadcasted_iota(I32, (blk, 1), 0)
    w = jnp.where(row < n_cols, wt_ref[...], 0.0)
    hi, lo = _split_bf16(w.T)
    hi_ref[...] = hi
    lo_ref[...] = lo


def _split_weight(wt):
    n_cols, d = wt.shape
    blk = 4 * TILE
    n_blk = pl.cdiv(n_cols, blk)
    return pl.pallas_call(
        functools.partial(_split_weight_kernel, n_cols=n_cols),
        grid=(n_blk,),
        in_specs=[pl.BlockSpec((blk, d), lambda i: (i, 0))],
        out_specs=[pl.BlockSpec((d, blk), lambda i: (0, i)), pl.BlockSpec((d, blk), lambda i: (0, i))],
        out_shape=[jax.ShapeDtypeStruct((d, n_blk * blk), BF16), jax.ShapeDtypeStruct((d, n_blk * blk), BF16)],
        compiler_params=_cparams(("arbitrary",)),
        name="split_weight",
    )(wt)


def _prompt_cache_kernel(k_ref, v_ref, kiw_ref, km_ref, vm_ref, kiwm_ref, kt_ref, vt_ref, kit_ref):
    n_tiles = k_ref.shape[0] // TILE
    for src, meta, dst, width in ((k_ref, km_ref, kt_ref, k_ref.shape[1]), (v_ref, vm_ref, vt_ref, v_ref.shape[1]),
                                  (kiw_ref, kiwm_ref, kit_ref, IDX_DIM)):
        dst[0, :, 0:N_META] = meta[...].T[:width, PAD_FRONT:]
        for j in range(n_tiles):
            dst[0, :, N_META + j * TILE:N_META + (j + 1) * TILE] = src[j * TILE:(j + 1) * TILE, :].T[:width, :]


def _prompt_cache(k, v, kiw, n_batch, seq, meta_blk):
    l_seq = seq + N_META
    kvw = k.shape[1]
    toks = lambda b: (b, 0)
    meta = lambda b: (meta_blk, 0)
    out = lambda b: (b, 0, 0)
    return pl.pallas_call(
        _prompt_cache_kernel,
        grid=(n_batch,),
        in_specs=[pl.BlockSpec((seq, kvw), toks), pl.BlockSpec((seq, kvw), toks), pl.BlockSpec((seq, TILE), toks),
                  pl.BlockSpec((TILE, kvw), meta), pl.BlockSpec((TILE, kvw), meta), pl.BlockSpec((TILE, TILE), meta)],
        out_specs=[pl.BlockSpec((1, kvw, l_seq), out), pl.BlockSpec((1, kvw, l_seq), out),
                   pl.BlockSpec((1, IDX_DIM, l_seq), out)],
        out_shape=[jax.ShapeDtypeStruct((n_batch, kvw, l_seq), F32), jax.ShapeDtypeStruct((n_batch, kvw, l_seq), F32),
                   jax.ShapeDtypeStruct((n_batch, IDX_DIM, l_seq), F32)],
        compiler_params=_cparams(("arbitrary",)),
        name="prompt_cache",
    )(k, v, kiw, k, v, kiw)


def kernel(x_prompt, x_sample, cache_k, cache_v, cache_kidx, state_ssm, state_conv, page_table, meta_tokens,
           g_ffn1, w_ffn1_gate, w_ffn1_up, w_ffn1_down, g_mix, w_in, g_q, g_k, conv_w, conv_b, dt_bias, a_log,
           d_skip, g_ssd_norm, w_attn_out, w_ssd_out, w_o, g_ffn2, w_ffn2_gate, w_ffn2_up, w_ffn2_down):
    depth = w_in.shape[0]
    assert depth == 1, "single-layer step"
    n_batch, seq, d_model = x_prompt.shape
    ns = x_sample.shape[0]
    assert x_sample.shape[1] == 1 and seq % TILE == 0 and ns == TILE
    page = cache_k.shape[2]
    assert page == TILE
    l_seq = seq + N_META
    r_tok = n_batch * seq
    r_total = r_tok + ns + TILE
    meta_blk = (r_tok + ns) // TILE
    n_heads = state_ssm.shape[2]
    d_inner = n_heads * SSD_HEAD_DIM
    assert n_heads <= TILE and n_heads % (2 * SSD_GROUPS) == 0
    cdim = d_inner + 2 * SSD_GROUPS * D_STATE
    nq, nkv = N_HEADS * HEAD_DIM, N_KV_HEADS * HEAD_DIM
    nqi = IDX_HEADS * IDX_DIM
    tm = _row_tile(r_total // TILE, 6)
    tm_tok = _row_tile(r_tok // TILE, 8)

    w_hi, w_lo = _split_weight(jnp.transpose(w_in[0]))
    offs = [0]
    for n in (nq, nkv, nkv, nqi, IDX_DIM, IDX_HEADS, d_inner, cdim, n_heads, d_model, d_model):
        offs.append(offs[-1] + n)
    col = lambda i, w=w_hi: w[:, offs[i]:offs[i + 1]]
    wqk = jnp.concatenate([col(0), col(1)], axis=1)
    wv = col(2)
    idx_cols = lambda w: jnp.concatenate([col(3, w), col(4, w), col(5, w),
                                          jnp.zeros((d_model, TILE - IDX_DIM - IDX_HEADS), BF16)], axis=1)
    wih, wil = idx_cols(w_hi), idx_cols(w_lo)
    wz = col(6)
    wx = col(7)
    wdt = jnp.concatenate([col(8), jnp.zeros((d_model, TILE - n_heads), BF16)], axis=1)
    wgate = jnp.concatenate([col(9), col(10)], axis=1)
    n_seg = (nq + nkv) // HEAD_DIM
    seg = (jnp.arange(nq + nkv)[:, None] // HEAD_DIM == jnp.arange(TILE)[None, :]).astype(BF16)
    segt = seg.T
    del n_seg
    gqk = jnp.concatenate([jnp.tile(g_q[0], N_HEADS), jnp.tile(g_k[0], N_KV_HEADS)])[None, :]
    row1 = lambda v: v.reshape(1, -1)
    pad_heads = lambda v, fill: jnp.concatenate([v, jnp.full((TILE - n_heads,), fill, F32)])[None, :]
    dtb = pad_heads(dt_bias[0], -1e4)
    alog = pad_heads(a_log[0], 0.0)
    dsk = jnp.repeat(d_skip[0], SSD_HEAD_DIM)[None, :]
    assert 3 * n_heads <= TILE
    e_row = jnp.arange(TILE)[:, None]
    e_mat = ((e_row < 3 * n_heads) & (e_row % n_heads == jnp.arange(d_inner)[None, :] // SSD_HEAD_DIM)).astype(BF16)
    gn = row1(g_ssd_norm[0])

    x_all = jnp.concatenate([x_prompt.reshape(r_tok, d_model), x_sample.reshape(ns, d_model),
                             jnp.zeros((PAD_FRONT, d_model), F32), meta_tokens.astype(F32)], axis=0)
    x1, hb = _ffn(x_all, row1(g_ffn1[0]), row1(g_mix[0]), w_ffn1_gate[0].astype(BF16),
                  w_ffn1_up[0].astype(BF16), w_ffn1_down[0].astype(BF16), tm)
    q, k, v, kb, vb, qi, kiw = _proj_attn(x1, row1(g_mix[0]), wqk, wv, wih, wil, seg, segt, gqk, tm)
    xbc, dtr = _proj_xbc(hb, wx, wdt, tm)
    zs, gates = _proj_gate(hb, wz, wgate, tm)

    ksel_p = min(TOP_K_MAX, l_seq // 4)
    attn = _dsa_prompt(q, qi, kiw, kb, vb, n_batch, seq, meta_blk, ksel_p)
    ssd, ssm_p, conv_p = _ssd_prompt(xbc, dtr, zs, conv_w[0], row1(conv_b[0]), dtb, alog, dsk, gn, e_mat,
                                     n_batch, seq, meta_blk, n_heads)

    n_pages = page_table.shape[1]
    ksel_s = min(TOP_K_MAX, (n_pages * page + 1) // 4)
    smp = lambda a: a[r_tok:r_tok + ns]
    kidx_t = jnp.transpose(cache_kidx[0], (0, 2, 1))
    scores = _idx_sample(page_table, smp(qi).reshape(ns, 1, nqi), smp(kiw).reshape(ns, 1, TILE), kidx_t)
    bias = _select_sample(scores.reshape(n_pages + 1, ns, TILE), ksel_s).reshape(n_pages + 1, ns, 1, TILE)
    ck = jnp.transpose(cache_k[0], (0, 2, 3, 1)).reshape(cache_k.shape[1], nkv, page)
    cv = jnp.transpose(cache_v[0], (0, 2, 3, 1)).reshape(cache_v.shape[1], nkv, page)
    attn_s = _attn_sample(page_table, smp(q).reshape(ns, 1, nq), smp(k).reshape(ns, 1, nkv),
                          smp(v).reshape(ns, 1, nkv), bias, ck, cv)
    y_s, ssm_s, conv_s = _ssd_sample(smp(xbc), smp(dtr), state_conv[0].reshape(ns, -1), state_ssm[0].reshape(ns, d_inner, D_STATE),
                                     conv_w[0], row1(conv_b[0]), dtb, alog, dsk, e_mat, n_heads, 8)
    attn_sb, ssd_sb = _sample_acts(attn_s.reshape(ns, nq), y_s, zs, gn, r_tok // ns)

    wao, wso, wo = w_attn_out[0].astype(BF16), w_ssd_out[0].astype(BF16), w_o[0].astype(BF16)
    ffn2 = (row1(g_ffn2[0]), row1(g_ffn2[0]), w_ffn2_gate[0].astype(BF16), w_ffn2_up[0].astype(BF16),
            w_ffn2_down[0].astype(BF16))
    x2 = _mix(x1, attn, ssd, gates, wao, wso, wo, tm_tok, (0, r_tok), 0)
    y_prompt = _ffn(x2, *ffn2, tm_tok, with_norm=False)[0].reshape(n_batch, seq, d_model)
    x2_s = _mix(x1, attn_sb, ssd_sb, gates, wao, wso, wo, ns, (r_tok, ns), 0)
    y_sample = _ffn(x2_s, *ffn2, ns, with_norm=False)[0].reshape(ns, 1, d_model)

    kt, vt, kit = _prompt_cache(k, v, kiw, n_batch, seq, meta_blk)
    heads_last = lambda a: jnp.transpose(a.reshape(1, n_batch, N_KV_HEADS, HEAD_DIM, l_seq), (0, 1, 4, 2, 3))
    k_prompt = heads_last(kt)
    v_prompt = heads_last(vt)
    kidx_prompt = jnp.transpose(kit, (0, 2, 1))[None]
    ssm_prompt = ssm_p.reshape(1, n_batch, n_heads, SSD_HEAD_DIM, D_STATE)
    conv_prompt = conv_p.reshape(1, n_batch, CONV_W - 1, cdim)
    k_sample = smp(k).reshape(1, ns, 1, N_KV_HEADS, HEAD_DIM)
    v_sample = smp(v).reshape(1, ns, 1, N_KV_HEADS, HEAD_DIM)
    kidx_sample = smp(kiw)[:, :IDX_DIM].reshape(1, ns, 1, IDX_DIM)
    ssm_sample = ssm_s.reshape(1, ns, n_heads, SSD_HEAD_DIM, D_STATE)
    conv_sample = conv_s.reshape(1, ns, CONV_W - 1, cdim)
    return (y_prompt, y_sample, k_prompt, v_prompt, kidx_prompt, ssm_prompt, conv_prompt,
            k_sample, v_sample, kidx_sample, ssm_sample, conv_sample)
```

```python
import functools

import jax
import jax.numpy as jnp
from jax import lax
from jax.experimental import pallas as pl
from jax.experimental.pallas import tpu as pltpu

F32 = jnp.float32
BF16 = jnp.bfloat16
I32 = jnp.int32
HIGHEST = lax.Precision.HIGHEST

EPS = 1e-6
N_META = 16
N_HEADS = 16
N_KV_HEADS = 4
HEAD_DIM = 64
Q_PER_KV = N_HEADS // N_KV_HEADS
IDX_HEADS = 4
IDX_DIM = 64
TOP_K_MAX = 256
SSD_HEAD_DIM = 64
SSD_GROUPS = 4
D_STATE = 128
CONV_W = 4
TILE = 128
PAD_FRONT = TILE - N_META
INT_MIN = -(2 ** 31)
NEG_BIG = -(2.0 ** 100)
Q_SCALE = HEAD_DIM ** -0.5 * 1.4426950408889634
VMEM_LIMIT = 56 * 1024 * 1024


def _cparams(sem, flags=None):
    return pltpu.CompilerParams(dimension_semantics=sem, vmem_limit_bytes=VMEM_LIMIT, flags=flags)


def _const_spec(shape):
    nd = len(shape)
    return pl.BlockSpec(shape, lambda *_: (0,) * nd, pipeline_mode=pl.Buffered(1))


def _rms(x, g):
    return x * lax.rsqrt(jnp.mean(x * x, axis=-1, keepdims=True) + EPS) * g


def _silu(x):
    return x * jax.nn.sigmoid(x)


def _row_tile(n_tiles, cap):
    for k in range(cap, 0, -1):
        if n_tiles % k == 0:
            return k * TILE
    return TILE


def _ffn_kernel(x_ref, g_ref, g2_ref, wg_ref, wu_ref, wd_ref, o_ref, h2_ref=None, *, ck):
    x = x_ref[...]
    h = _rms(x, g_ref[...]).astype(BF16)
    acc = jnp.zeros(x.shape, F32)
    for c in range(wg_ref.shape[1] // ck):
        sl = slice(c * ck, (c + 1) * ck)
        gate = jnp.dot(h, wg_ref[:, sl], preferred_element_type=F32)
        up = jnp.dot(h, wu_ref[:, sl], preferred_element_type=F32)
        act = (_silu(gate) * up).astype(BF16)
        acc = acc + jnp.dot(act, wd_ref[sl, :], preferred_element_type=F32)
    y = x + 0.5 * acc
    o_ref[...] = y
    if h2_ref is not None:
        h2_ref[...] = _rms(y, g2_ref[...]).astype(BF16)


def _ffn(x, g, g2, wg, wu, wd, tm, with_norm=True):
    r, d = x.shape
    dff = wg.shape[1]
    row = lambda i: (i, 0)
    out_specs = [pl.BlockSpec((tm, d), row)]
    out_shape = [jax.ShapeDtypeStruct((r, d), F32)]
    if with_norm:
        out_specs.append(pl.BlockSpec((tm, d), row))
        out_shape.append(jax.ShapeDtypeStruct((r, d), BF16))
    return pl.pallas_call(
        functools.partial(_ffn_kernel, ck=256),
        grid=(r // tm,),
        in_specs=[pl.BlockSpec((tm, d), row), _const_spec((1, d)), _const_spec((1, d)),
                  _const_spec((d, dff)), _const_spec((d, dff)), _const_spec((dff, d))],
        out_specs=out_specs,
        out_shape=out_shape,
        compiler_params=_cparams(("arbitrary",)),
        name="ffn",
    )(x, g, g2, wg, wu, wd)


def _proj_attn_kernel(x_ref, g_ref, wqk_ref, wv_ref, wih_ref, wil_ref, seg_ref, segt_ref, gqk_ref,
                      q_ref, k_ref, v_ref, kb_ref, vb_ref, qi_ref, kiw_ref):
    h = _rms(x_ref[...], g_ref[...])
    hb, h_lo = _split_bf16(h)
    qk = jnp.dot(hb, wqk_ref[...], preferred_element_type=F32)
    ss = jnp.dot((qk * qk).astype(BF16), seg_ref[...], preferred_element_type=F32)
    r = lax.rsqrt(ss * (1.0 / HEAD_DIM) + EPS)
    r_hi = r.astype(BF16)
    r_lo = (r - r_hi.astype(F32)).astype(BF16)
    rx = (jnp.dot(r_hi, segt_ref[...], preferred_element_type=F32)
          + jnp.dot(r_lo, segt_ref[...], preferred_element_type=F32))
    qkn = qk * rx * gqk_ref[...]
    nq = q_ref.shape[1]
    q_ref[...] = (qkn[:, :nq] * Q_SCALE).astype(BF16)
    k = qkn[:, nq:]
    k_ref[...] = k
    kb_ref[...] = k.astype(BF16)
    v = jnp.dot(hb, wv_ref[...], preferred_element_type=F32)
    v_ref[...] = v
    vb_ref[...] = v.astype(BF16)
    idx = _dot3(hb, h_lo, wih_ref[...], wil_ref[...])
    nqi = qi_ref.shape[1]
    qi_ref[...] = idx[:, :nqi]
    kiw_ref[...] = idx[:, nqi:]


def _proj_attn(x1, g, wqk, wv, wih, wil, seg, segt, gqk, tm):
    r, d = x1.shape
    nqk, nv = wqk.shape[1], wv.shape[1]
    nq = N_HEADS * HEAD_DIM
    nqi = IDX_HEADS * IDX_DIM
    row = lambda i: (i, 0)
    outs = [(nq, BF16), (nqk - nq, F32), (nv, F32), (nqk - nq, BF16), (nv, BF16), (nqi, F32), (TILE, F32)]
    return pl.pallas_call(
        _proj_attn_kernel,
        grid=(r // tm,),
        in_specs=[pl.BlockSpec((tm, d), row), _const_spec((1, d)), _const_spec(wqk.shape),
                  _const_spec(wv.shape), _const_spec(wih.shape), _const_spec(wil.shape), _const_spec(seg.shape),
                  _const_spec(segt.shape), _const_spec(gqk.shape)],
        out_specs=[pl.BlockSpec((tm, n), row) for n, _ in outs],
        out_shape=[jax.ShapeDtypeStruct((r, n), dt) for n, dt in outs],
        compiler_params=_cparams(("arbitrary",)),
        name="proj_attn",
    )(x1, g, wqk, wv, wih, wil, seg, segt, gqk)


def _proj_xbc_kernel(h_ref, wx_ref, wdt_ref, xbc_ref, dt_ref):
    h = h_ref[...]
    xbc_ref[...] = jnp.dot(h, wx_ref[...], preferred_element_type=F32)
    dt_ref[...] = jnp.dot(h, wdt_ref[...], preferred_element_type=F32)


def _proj_xbc(hb, wx, wdt, tm):
    r, d = hb.shape
    row = lambda i: (i, 0)
    return pl.pallas_call(
        _proj_xbc_kernel,
        grid=(r // tm,),
        in_specs=[pl.BlockSpec((tm, d), row), _const_spec(wx.shape), _const_spec(wdt.shape)],
        out_specs=[pl.BlockSpec((tm, wx.shape[1]), row), pl.BlockSpec((tm, wdt.shape[1]), row)],
        out_shape=[jax.ShapeDtypeStruct((r, wx.shape[1]), F32),
                   jax.ShapeDtypeStruct((r, wdt.shape[1]), F32)],
        compiler_params=_cparams(("arbitrary",)),
        name="proj_xbc",
    )(hb, wx, wdt)


def _proj_gate_kernel(h_ref, wz_ref, wg_ref, zs_ref, gt_ref):
    h = h_ref[...]
    zs_ref[...] = _silu(jnp.dot(h, wz_ref[...], preferred_element_type=F32)).astype(BF16)
    gt_ref[...] = jax.nn.sigmoid(jnp.dot(h, wg_ref[...], preferred_element_type=F32)).astype(BF16)


def _proj_gate(hb, wz, wg, tm):
    r, d = hb.shape
    row = lambda i: (i, 0)
    return pl.pallas_call(
        _proj_gate_kernel,
        grid=(r // tm,),
        in_specs=[pl.BlockSpec((tm, d), row), _const_spec(wz.shape), _const_spec(wg.shape)],
        out_specs=[pl.BlockSpec((tm, wz.shape[1]), row), pl.BlockSpec((tm, wg.shape[1]), row)],
        out_shape=[jax.ShapeDtypeStruct((r, wz.shape[1]), BF16),
                   jax.ShapeDtypeStruct((r, wg.shape[1]), BF16)],
        compiler_params=_cparams(("arbitrary",)),
        name="proj_gate",
    )(hb, wz, wg)


KEY_GROUP = 4


def _key_value(key):
    mag = jnp.where(key < 0, jnp.where(key == INT_MIN, 0x7F800000, -key), key)
    v = pltpu.bitcast(mag, F32)
    return jnp.where(key < 0, -v, v)


def _tree_sum(x):
    while x.shape[0] > 1:
        h = x.shape[0] // 2
        x = x[:h] + x[h:]
    return x[0]


def _count(sc_ref, n_groups, pred):
    def body(gi, acc):
        s = sc_ref[pl.ds(gi * KEY_GROUP, KEY_GROUP)]
        hit = jnp.where(pred(s, gi * (KEY_GROUP * TILE)), 1.0, 0.0)
        return acc + _tree_sum(hit.reshape(KEY_GROUP * TILE // 8, 8, TILE))
    part = lax.fori_loop(0, n_groups, body, jnp.zeros((8, TILE), F32))
    return jnp.sum(part, axis=0, keepdims=True)


def _any(flag):
    return jnp.max(jnp.where(flag, 1, 0)) > 0


def _kth_bitwise(sc_ref, n_groups, ksel):
    n_ge0 = _count(sc_ref, n_groups, lambda s, r0: s >= 0.0)
    t = jnp.where(n_ge0 >= ksel, jnp.zeros((1, TILE), I32), jnp.full((1, TILE), INT_MIN, I32))

    def bit_step(i, t):
        cand = t | jnp.left_shift(jnp.int32(1), 30 - i)
        cv = _key_value(cand)
        n = _count(sc_ref, n_groups, lambda s, r0: s >= cv)
        return jnp.where(n >= ksel, cand, t)

    return lax.fori_loop(0, 31, bit_step, t)


def _tie_search(sc_ref, n_groups, tv, need, n_bits):
    shape = (KEY_GROUP, TILE, TILE)
    rows = (lax.broadcasted_iota(I32, shape, 0) * TILE + lax.broadcasted_iota(I32, shape, 1)).astype(F32)

    def bit_step(i, p):
        cand = p | jnp.left_shift(jnp.int32(1), n_bits - 1 - i)
        cf = cand.astype(F32)
        n = _count(sc_ref, n_groups, lambda s, r0: (s == tv) & ((rows + jnp.asarray(r0, F32)) < cf))
        return jnp.where(n < need, cand, p)
    return lax.fori_loop(0, n_bits, bit_step, jnp.zeros((1, TILE), I32)) + 1


def _select_topk(sc_ref, n_groups, ksel, n_bits):
    no_limit = jnp.full((1, TILE), float(1 << n_bits), F32)
    t = _kth_bitwise(sc_ref, n_groups, ksel)
    tv = _key_value(t)
    n_gt = _count(sc_ref, n_groups, lambda s, r0: s > tv)
    n_eq = _count(sc_ref, n_groups, lambda s, r0: s == tv)
    need = ksel - n_gt
    conflict = (t != INT_MIN) & (n_eq > need)
    lim = lax.cond(_any(conflict), lambda: _tie_search(sc_ref, n_groups, tv, need, n_bits).astype(F32),
                   lambda: no_limit)
    return tv, jnp.where(conflict, lim, no_limit)


def _split_bf16(x):
    hi = x.astype(BF16)
    return hi, (x - hi.astype(F32)).astype(BF16)


def _dot3(a_hi, a_lo, b_hi, b_lo):
    return (jnp.dot(a_hi, b_hi, preferred_element_type=F32)
            + jnp.dot(a_hi, b_lo, preferred_element_type=F32)
            + jnp.dot(a_lo, b_hi, preferred_element_type=F32))


VROWS = HEAD_DIM + 16


def _dsa_prompt_kernel(q_ref, qi_ref, kiwq_ref, kb_ref, vb_ref, kiw_ref, kbm_ref, vbm_ref, kiwm_ref, o_ref,
                       sc_ref, kk_ref, vt_ref, kihl_ref, rhs_ref, qa_ref,
                       m_ref, acc_ref, *, ksel, n_bits):
    b = pl.program_id(0)
    j = pl.program_id(1)
    nt = vb_ref.shape[0] // TILE + 1
    n_groups = lax.shift_right_logical(j + KEY_GROUP, KEY_GROUP.bit_length() - 1)
    n_pairs = lax.shift_right_logical(j + 2, 1)
    nkv = N_KV_HEADS
    gw = Q_PER_KV * TILE
    eye = (lax.broadcasted_iota(I32, (TILE, TILE), 0) == lax.broadcasted_iota(I32, (TILE, TILE), 1))

    @pl.when((b == 0) & (j == 0))
    def _():
        rhs_ref[...] = jnp.zeros(rhs_ref.shape, rhs_ref.dtype)
        ident = jnp.where(eye, 1.0, 0.0).astype(BF16)
        for g in range(nkv):
            for hh in range(Q_PER_KV):
                rhs_ref[g, 0:TILE, hh * TILE:(hh + 1) * TILE] = ident
        qa_ref[...] = jnp.zeros(qa_ref.shape, qa_ref.dtype)
        kk_ref[...] = jnp.zeros(kk_ref.shape, kk_ref.dtype)
        kihl_ref[...] = jnp.zeros(kihl_ref.shape, kihl_ref.dtype)
        ones_row = jnp.where(lax.broadcasted_iota(I32, (VROWS - HEAD_DIM, TILE), 0) == 0, 1.0, 0.0).astype(BF16)
        for c in range(vt_ref.shape[0]):
            vt_ref[c] = jnp.zeros(vt_ref.shape[1:], vt_ref.dtype)
            for g in range(nkv):
                vt_ref[c, g * VROWS + HEAD_DIM:(g + 1) * VROWS, :] = ones_row

    @pl.when(j == 0)
    def _():
        def stage(c, kb, vb, kiw):
            kk_ref[c] = kb
            vt = vb.astype(F32).T.astype(BF16)
            for g in range(nkv):
                vt_ref[c, g * VROWS:g * VROWS + HEAD_DIM, :] = vt[g * HEAD_DIM:(g + 1) * HEAD_DIM, :]
            ki = kiw[:, :IDX_DIM]
            hi = ki.astype(BF16).astype(F32)
            lo = (ki - hi).astype(BF16).astype(F32)
            kihl_ref[c] = jnp.concatenate([hi, lo, hi, jnp.zeros_like(hi)], axis=1).astype(BF16)

        stage(0, kbm_ref[...], vbm_ref[...], kiwm_ref[...])

        def prep(c, _):
            rows = pl.ds(pl.multiple_of((c - 1) * TILE, TILE), TILE)
            stage(c, kb_ref[rows, :], vb_ref[rows, :], kiw_ref[rows, :])
            return 0
        lax.fori_loop(1, nt, prep, 0)

    qt = q_ref[...].astype(F32).T.astype(BF16)
    for h in range(N_HEADS):
        g, hh = divmod(h, Q_PER_KV)
        r0 = TILE + (g % 2) * HEAD_DIM
        rhs_ref[g, r0:r0 + HEAD_DIM, hh * TILE:(hh + 1) * TILE] = qt[h * HEAD_DIM:(h + 1) * HEAD_DIM, :]
    qit = qi_ref[...].T
    for h in range(IDX_HEADS):
        hi, lo = _split_bf16(qit[h * IDX_DIM:(h + 1) * IDX_DIM, :])
        qa_ref[0:IDX_DIM, h * TILE:(h + 1) * TILE] = hi
        qa_ref[IDX_DIM:2 * IDX_DIM, h * TILE:(h + 1) * TILE] = hi
        qa_ref[2 * IDX_DIM:3 * IDX_DIM, h * TILE:(h + 1) * TILE] = lo
    wq = kiwq_ref[...].T[IDX_DIM:IDX_DIM + 8, :]
    wq = wq * (IDX_HEADS ** -0.5 * IDX_DIM ** -0.5)

    grp_rows = KEY_GROUP * TILE
    q_pos = j * TILE + lax.broadcasted_iota(I32, (grp_rows, TILE), 1)
    k_row = lax.broadcasted_iota(I32, (grp_rows, TILE), 0)

    def score_group(gi, _):
        khl = kihl_ref[pl.ds(gi * KEY_GROUP, KEY_GROUP)].reshape(grp_rows, 2 * TILE)
        d = jnp.dot(khl, qa_ref[...], preferred_element_type=F32)
        s = jnp.zeros((grp_rows, TILE), F32)
        for h in range(IDX_HEADS):
            s = s + wq[h:h + 1, :] * jnp.maximum(d[:, h * TILE:(h + 1) * TILE], 0.0)
        k_pos = k_row + gi * grp_rows
        valid = (k_pos <= q_pos) & (k_pos >= PAD_FRONT)
        sc_ref[pl.ds(gi * KEY_GROUP, KEY_GROUP)] = jnp.where(valid, s, jnp.nan).reshape(KEY_GROUP, TILE, TILE)
        return 0
    lax.fori_loop(0, n_groups, score_group, 0)

    tv, lim = _select_topk(sc_ref, n_groups, ksel, n_bits)

    m_ref[...] = jnp.full(m_ref.shape, NEG_BIG, F32)
    acc_ref[...] = jnp.zeros(acc_ref.shape, F32)

    pair_shape = (2, TILE, TILE)
    pair_row = (lax.broadcasted_iota(I32, pair_shape, 0) * TILE
                + lax.broadcasted_iota(I32, pair_shape, 1)).astype(F32)

    def attend_pair(i, _):
        sc = sc_ref[pl.ds(2 * i, 2)]
        sel = (sc > tv) | ((sc == tv) & ((pair_row + (i * (2 * TILE)).astype(F32)) < lim))
        bias = jnp.where(sel, 0.0, NEG_BIG).astype(BF16).reshape(2 * TILE, TILE)
        kc = kk_ref[pl.ds(2 * i, 2)].reshape(2 * TILE, nkv * HEAD_DIM)
        vt = jnp.concatenate([vt_ref[2 * i], vt_ref[2 * i + 1]], axis=1)
        m_all = m_ref[...]
        s = []
        for g in range(nkv):
            half = (g // 2) * TILE
            lhs = jnp.concatenate([bias, kc[:, half:half + TILE]], axis=1)
            s.append(jnp.dot(lhs, rhs_ref[g], preferred_element_type=F32))
        m_new = []
        for g in range(nkv):
            m_g = jnp.maximum(m_all[g:g + 1, :], jnp.max(s[g], axis=0, keepdims=True))
            alpha = jnp.exp2(m_all[g:g + 1, :] - m_g)
            p = jnp.exp2(s[g] - m_g).astype(BF16)
            pv = jnp.dot(vt[g * VROWS:(g + 1) * VROWS, :], p, preferred_element_type=F32)
            acc_ref[g] = alpha * acc_ref[g] + pv
            m_new.append(m_g)
        m_ref[...] = jnp.concatenate(m_new + [m_all[nkv:, :]], axis=0)
        return 0
    lax.fori_loop(0, n_pairs, attend_pair, 0)

    pieces = []
    for g in range(nkv):
        acc = acc_ref[g]
        o = acc[:HEAD_DIM] / acc[HEAD_DIM:HEAD_DIM + 1]
        for hh in range(Q_PER_KV):
            pieces.append(o[:, hh * TILE:(hh + 1) * TILE])
    o_ref[...] = jnp.concatenate(pieces, axis=0).T.astype(BF16)


def _seq_tile_map(seq, meta_blk):
    n_tok = seq // TILE
    meta = meta_blk if callable(meta_blk) else (lambda b: meta_blk)
    return lambda b, j: (jnp.where(j == 0, meta(b), b * n_tok + j - 1), 0)


def _dsa_prompt(q, qi, kiw, kb, vb, n_batch, seq, meta_blk, ksel):
    lp = seq + TILE
    nt = lp // TILE
    nt_pad = -(-nt // KEY_GROUP) * KEY_GROUP
    qw = N_HEADS * HEAD_DIM
    kvw = N_KV_HEADS * HEAD_DIM
    gw = Q_PER_KV * TILE
    tile = _seq_tile_map(seq, meta_blk)
    toks = lambda b, j: (b, 0)
    meta = lambda b, j: (meta_blk, 0)
    n_bits = max(1, (lp - 1).bit_length())
    return pl.pallas_call(
        functools.partial(_dsa_prompt_kernel, ksel=ksel, n_bits=n_bits),
        grid=(n_batch, nt),
        in_specs=[pl.BlockSpec((TILE, qw), tile), pl.BlockSpec((TILE, IDX_HEADS * IDX_DIM), tile),
                  pl.BlockSpec((TILE, TILE), tile), pl.BlockSpec((seq, kvw), toks),
                  pl.BlockSpec((seq, kvw), toks), pl.BlockSpec((seq, TILE), toks),
                  pl.BlockSpec((TILE, kvw), meta), pl.BlockSpec((TILE, kvw), meta),
                  pl.BlockSpec((TILE, TILE), meta)],
        out_specs=pl.BlockSpec((TILE, qw), _seq_tile_map(seq, lambda b: n_batch * (seq // TILE) + b)),
        out_shape=jax.ShapeDtypeStruct((n_batch * (seq + TILE), qw), BF16),
        scratch_shapes=[pltpu.VMEM((nt_pad, TILE, TILE), F32),
                        pltpu.VMEM((nt_pad, TILE, kvw), BF16),
                        pltpu.VMEM((nt_pad, N_KV_HEADS * VROWS, TILE), BF16),
                        pltpu.VMEM((nt_pad, TILE, 2 * TILE), BF16),
                        pltpu.VMEM((N_KV_HEADS, 2 * TILE, gw), BF16),
                        pltpu.VMEM((2 * TILE, IDX_HEADS * TILE), BF16),
                        pltpu.VMEM((8, gw), F32),
                        pltpu.VMEM((N_KV_HEADS, VROWS, gw), F32)],
        compiler_params=_cparams(("arbitrary", "arbitrary")),
        name="dsa_prompt",
    )(q, qi, kiw, kb, vb, kiw, kb, vb, kiw)


def _expand_heads(v, e_ref):
    n_heads = e_ref.shape[1] // SSD_HEAD_DIM
    lane = lax.broadcasted_iota(I32, (1, v.shape[1]), 1)
    v = jnp.where(lane < n_heads, v, 0.0)
    hi = v.astype(BF16).astype(F32)
    r1 = v - hi
    mid = r1.astype(BF16).astype(F32)
    lo = (r1 - mid).astype(BF16).astype(F32)
    packed = (hi + pltpu.roll(mid, n_heads, 1) + pltpu.roll(lo, 2 * n_heads, 1)).astype(BF16)
    return jnp.dot(packed, e_ref[...], preferred_element_type=F32)


def _softplus(x):
    return jnp.maximum(x, 0.0) + jnp.log1p(jnp.exp(-jnp.abs(x)))


def _ssd_prompt_kernel(xbc_ref, dtr_ref, zs_ref, cw_ref, cb_ref, dtb_ref, alog_ref, dsk_ref, gn_ref,
                       e_ref, y_ref, ssm_ref, conv_ref, xp_ref, ht_ref, *, n_heads):
    c = pl.program_id(1)
    nc = pl.num_programs(1)
    d_inner = n_heads * SSD_HEAD_DIM
    hpg = n_heads // SSD_GROUPS
    gwid = hpg * SSD_HEAD_DIM
    gn_w = d_inner // SSD_GROUPS

    @pl.when(c == 0)
    def _():
        xp_ref[0:8, :] = jnp.zeros((8, xp_ref.shape[1]), F32)
        ht_ref[...] = jnp.zeros(ht_ref.shape, F32)

    xp_ref[8:8 + TILE, :] = xbc_ref[...]
    u = cb_ref[...] + cw_ref[CONV_W - 1:CONV_W, :] * xp_ref[8:8 + TILE, :]
    for k in range(CONV_W - 1):
        off = 8 - (CONV_W - 1) + k
        u = u + cw_ref[k:k + 1, :] * xp_ref[off:off + TILE, :]
    tail = xp_ref[8 + TILE - (CONV_W - 1):8 + TILE, :]
    xp_ref[8 - (CONV_W - 1):8, :] = tail

    row = lax.broadcasted_iota(I32, (TILE, 1), 0)
    live = jnp.where((c > 0) | (row >= PAD_FRONT), 1.0, 0.0)
    act = _silu(u) * live
    xs = act[:, :d_inner]
    bm = act[:, d_inner:d_inner + SSD_GROUPS * D_STATE]
    cm = act[:, d_inner + SSD_GROUPS * D_STATE:]

    dt = _softplus(dtr_ref[...] + dtb_ref[...]) * live
    a = dt * (-jnp.exp(alog_ref[...]))
    ri = lax.broadcasted_iota(I32, (TILE, TILE), 0)
    ci = lax.broadcasted_iota(I32, (TILE, TILE), 1)
    tril = ri >= ci
    acs = jnp.dot(jnp.where(tril, 1.0, 0.0), a, preferred_element_type=F32, precision=HIGHEST)
    acs_t = acs.T
    dt_t = dt.T
    acs_last = acs[TILE - 1:TILE, :]
    w_in = _expand_heads(jnp.exp(acs_last - acs) * dt, e_ref)
    w_out = _expand_heads(jnp.exp(acs), e_ref)
    xw = (xs * w_in).astype(BF16)
    xsb = xs.astype(BF16)
    first_head = lax.broadcasted_iota(I32, (1, 2 * SSD_HEAD_DIM), 1) < SSD_HEAD_DIM

    ys = []
    for g in range(SSD_GROUPS):
        bg = bm[:, g * D_STATE:(g + 1) * D_STATE]
        cg = cm[:, g * D_STATE:(g + 1) * D_STATE].astype(BF16)
        bgt = bg.T.astype(BF16)
        cb = jnp.dot(cg, bgt, preferred_element_type=F32)
        xw_g = xw[:, g * gwid:(g + 1) * gwid]
        xs_g = xsb[:, g * gwid:(g + 1) * gwid]
        h_prev = ht_ref[g]
        y_off = jnp.dot(cg, h_prev.astype(BF16), preferred_element_type=F32)
        y_g = y_off * w_out[:, g * gwid:(g + 1) * gwid]
        pairs = []
        for e0 in range(0, hpg, 2):
            x_pair = xs_g[:, e0 * SSD_HEAD_DIM:(e0 + 2) * SSD_HEAD_DIM]
            prods = []
            for e in (e0, e0 + 1):
                he = g * hpg + e
                seg = acs[:, he:he + 1] - acs_t[he:he + 1, :]
                m = (cb * jnp.exp(jnp.where(tril, seg, -jnp.inf)) * dt_t[he:he + 1, :]).astype(BF16)
                prods.append(jnp.dot(m, x_pair, preferred_element_type=F32))
            pairs.append(jnp.where(first_head, prods[0], prods[1]))
        y_g = y_g + jnp.concatenate(pairs, axis=1)
        states = jnp.dot(bgt, xw_g, preferred_element_type=F32)
        ht_ref[g] = h_prev * w_out[TILE - 1:TILE, g * gwid:(g + 1) * gwid] + states
        ys.append(y_g)
    y = jnp.concatenate(ys, axis=1) + dsk_ref[...] * xs
    y = y * zs_ref[...].astype(F32)
    outs = []
    for g in range(SSD_GROUPS):
        yg = y[:, g * gn_w:(g + 1) * gn_w]
        outs.append(yg * lax.rsqrt(jnp.mean(yg * yg, axis=-1, keepdims=True) + EPS))
    y_ref[...] = (jnp.concatenate(outs, axis=1) * gn_ref[...]).astype(BF16)

    @pl.when(c == nc - 1)
    def _():
        for g in range(SSD_GROUPS):
            ssm_ref[0, g * gwid:(g + 1) * gwid, :] = ht_ref[g].T
        conv_ref[0] = tail


def _ssd_prompt(xbc, dtr, zs, cw, cb, dtb, alog, dsk, gn, e_mat, n_batch, seq, meta_blk, n_heads):
    nt = seq // TILE + 1
    d_inner = n_heads * SSD_HEAD_DIM
    cdim = xbc.shape[1]
    tile = _seq_tile_map(seq, meta_blk)
    fix = lambda b, c: (0, 0)
    return pl.pallas_call(
        functools.partial(_ssd_prompt_kernel, n_heads=n_heads),
        grid=(n_batch, nt),
        in_specs=[pl.BlockSpec((TILE, cdim), tile), pl.BlockSpec((TILE, TILE), tile),
                  pl.BlockSpec((TILE, d_inner), tile),
                  pl.BlockSpec(cw.shape, fix), pl.BlockSpec(cb.shape, fix), pl.BlockSpec(dtb.shape, fix),
                  pl.BlockSpec(alog.shape, fix), pl.BlockSpec(dsk.shape, fix), pl.BlockSpec(gn.shape, fix),
                  pl.BlockSpec(e_mat.shape, fix)],
        out_specs=[pl.BlockSpec((TILE, d_inner), _seq_tile_map(seq, lambda b: n_batch * (seq // TILE) + b)),
                   pl.BlockSpec((1, d_inner, D_STATE), lambda b, c: (b, 0, 0)),
                   pl.BlockSpec((1, CONV_W - 1, cdim), lambda b, c: (b, 0, 0))],
        out_shape=[jax.ShapeDtypeStruct((n_batch * (seq + TILE), d_inner), BF16),
                   jax.ShapeDtypeStruct((n_batch, d_inner, D_STATE), F32),
                   jax.ShapeDtypeStruct((n_batch, CONV_W - 1, cdim), F32)],
        scratch_shapes=[pltpu.VMEM((8 + TILE, cdim), F32),
                        pltpu.VMEM((SSD_GROUPS, D_STATE, d_inner // SSD_GROUPS), F32)],
        compiler_params=_cparams(("arbitrary", "arbitrary")),
        name="ssd_prompt",
    )(xbc, dtr, zs, cw, cb, dtb, alog, dsk, gn, e_mat)


IDX_SEQS = 4


def _idx_sample_kernel(pt_ref, qi_ref, kiw_ref, *rest, n_pages):
    pages, s_ref = rest[:IDX_SEQS * n_pages], rest[IDX_SEQS * n_pages]
    lane = lax.broadcasted_iota(I32, (1, TILE), 1)
    for u in range(IDX_SEQS):
        qi = qi_ref[u]
        kiw = kiw_ref[u]
        q8 = jnp.concatenate([qi[:, h * IDX_DIM:(h + 1) * IDX_DIM] for h in range(IDX_HEADS)]
                             + [jnp.zeros((8 - IDX_HEADS, IDX_DIM), F32)], axis=0)
        w_col = jnp.broadcast_to(kiw, (TILE, TILE)).T[IDX_DIM:IDX_DIM + 8, 0:1]
        w_col = w_col * (IDX_HEADS ** -0.5 * IDX_DIM ** -0.5)
        q_hi, q_lo = _split_bf16(q8)
        q3 = jnp.concatenate([q_hi, q_lo, q_hi], axis=1)
        for p in range(n_pages):
            k_hi, k_lo = _split_bf16(pages[u * n_pages + p][0])
            d = jnp.dot(q3, jnp.concatenate([k_hi, k_hi, k_lo], axis=0), preferred_element_type=F32)
            s_ref[p, u] = jnp.sum(w_col * jnp.maximum(d, 0.0), axis=0, keepdims=True)
        d_new = jnp.sum(q8 * kiw[:, :IDX_DIM], axis=1, keepdims=True)
        s_new = jnp.sum(w_col * jnp.maximum(d_new, 0.0), axis=0, keepdims=True)
        s_ref[n_pages, u] = jnp.where(lane == 0, s_new, 0.0)


def _idx_sample(page_table, qi_s, kiw_s, cache_kidx):
    ns, n_pages = page_table.shape
    n_slots = n_pages + 1
    assert ns % IDX_SEQS == 0
    page = lambda u, p: pl.BlockSpec((1, IDX_DIM, TILE), lambda s, pt: (pt[IDX_SEQS * s + u, p], 0, 0))
    some = lambda s, pt: (s, 0, 0)
    grid_spec = pltpu.PrefetchScalarGridSpec(
        num_scalar_prefetch=1, grid=(ns // IDX_SEQS,),
        in_specs=[pl.BlockSpec((IDX_SEQS, 1, qi_s.shape[2]), some), pl.BlockSpec((IDX_SEQS, 1, TILE), some)]
        + [page(u, p) for u in range(IDX_SEQS) for p in range(n_pages)],
        out_specs=pl.BlockSpec((n_slots, IDX_SEQS, 1, TILE), lambda s, pt: (0, s, 0, 0)))
    return pl.pallas_call(
        functools.partial(_idx_sample_kernel, n_pages=n_pages),
        grid_spec=grid_spec,
        out_shape=jax.ShapeDtypeStruct((n_slots, ns, 1, TILE), F32),
        compiler_params=_cparams(("arbitrary",)),
        name="idx_sample",
    )(page_table, qi_s, kiw_s, *([cache_kidx] * (IDX_SEQS * n_pages)))


def _select_sample_kernel(s_ref, bias_ref, sc_ref, *, n_pages, ksel, n_bits):
    n_slots = n_pages + 1
    rows = lax.broadcasted_iota(I32, (TILE, TILE), 0)
    for p in range(n_slots):
        sc = s_ref[p].T
        if p == n_pages:
            sc = jnp.where(rows == 0, sc, jnp.nan)
        sc_ref[p] = sc
    for p in range(n_slots, sc_ref.shape[0]):
        sc_ref[p] = jnp.full((TILE, TILE), jnp.nan, F32)
    n_groups = sc_ref.shape[0] // KEY_GROUP
    tv, lim = _select_topk(sc_ref, n_groups, ksel, n_bits)
    rows_f = rows.astype(F32)
    for p in range(n_slots):
        sc = sc_ref[p]
        sel = (sc > tv) | ((sc == tv) & ((rows_f + float(p * TILE)) < lim))
        bias_ref[p] = jnp.where(sel, 0.0, NEG_BIG).T


def _select_sample(scores, ksel):
    n_slots, ns, _ = scores.shape
    n_pages = n_slots - 1
    n_bits = (n_slots * TILE - 1).bit_length()
    return pl.pallas_call(
        functools.partial(_select_sample_kernel, n_pages=n_pages, ksel=ksel, n_bits=n_bits),
        out_shape=jax.ShapeDtypeStruct((n_slots, ns, TILE), F32),
        scratch_shapes=[pltpu.VMEM((-(-n_slots // KEY_GROUP) * KEY_GROUP, TILE, TILE), F32)],
        compiler_params=pltpu.CompilerParams(vmem_limit_bytes=VMEM_LIMIT),
        name="select_sample",
    )(scores)


def _attn_sample_kernel(pt_ref, q_ref, kn_ref, vn_ref, bias_ref, *rest, n_pages):
    n_ops = IDX_SEQS * n_pages
    o_ref = rest[2 * n_ops]
    for u in range(IDX_SEQS):
        _attn_one_sample(u, q_ref, kn_ref, vn_ref, bias_ref, rest[u * n_pages:(u + 1) * n_pages],
                         rest[n_ops + u * n_pages:n_ops + (u + 1) * n_pages], o_ref, n_pages)


def _attn_one_sample(u, q_ref, kn_ref, vn_ref, bias_ref, kpages, vpages, o_ref, n_pages):
    kvw = N_KV_HEADS * HEAD_DIM
    q = q_ref[u].astype(F32)
    lane = lax.broadcasted_iota(I32, (N_HEADS, kvw), 1)
    head = lax.broadcasted_iota(I32, (N_HEADS, kvw), 0)
    own = jnp.right_shift(lane, 6) == jnp.right_shift(head, 2)
    qrows = []
    for h in range(N_HEADS):
        qh = q[:, h * HEAD_DIM:(h + 1) * HEAD_DIM]
        qrows.append(jnp.concatenate([qh] * N_KV_HEADS, axis=1))
    qf = jnp.where(own, jnp.concatenate(qrows, axis=0), 0.0)
    qw = qf.astype(BF16)
    nt = (((1,), (1,)), ((), ()))
    scores = []
    for p in range(n_pages):
        kp = kpages[p][0].astype(BF16)
        s = jnp.dot(qw, kp, preferred_element_type=F32)
        scores.append(s + bias_ref[p, u])
    s_new = jnp.sum(qf * kn_ref[u], axis=1, keepdims=True)
    lane1 = lax.broadcasted_iota(I32, (1, TILE), 1)
    scores.append(jnp.where(lane1 == 0, s_new, NEG_BIG) + bias_ref[n_pages, u])
    m = scores[0].max(axis=1, keepdims=True)
    for s in scores[1:]:
        m = jnp.maximum(m, s.max(axis=1, keepdims=True))
    den = jnp.zeros((N_HEADS, 1), F32)
    out = jnp.zeros((N_HEADS, kvw), F32)
    for p in range(n_pages):
        e = jnp.exp2(scores[p] - m)
        den = den + e.sum(axis=1, keepdims=True)
        out = out + lax.dot_general(e.astype(BF16), vpages[p][0].astype(BF16), nt,
                                    preferred_element_type=F32)
    e = jnp.exp2(scores[n_pages] - m)
    den = den + e.sum(axis=1, keepdims=True)
    out = out + e[:, 0:1] * vn_ref[u]
    out = out / den
    rolled = jnp.concatenate([pltpu.roll(out[:, i * TILE:(i + 1) * TILE], HEAD_DIM, 1)
                              for i in range(kvw // TILE)], axis=1)
    half = lax.broadcasted_iota(I32, (1, TILE), 1) < HEAD_DIM
    pieces = []
    for m2 in range(N_HEADS // 2):
        g = (2 * m2) // Q_PER_KV
        blk = (g // 2) * TILE
        a_src, b_src = (rolled, out) if g % 2 == 1 else (out, rolled)
        a = a_src[2 * m2:2 * m2 + 1, blk:blk + TILE]
        b = b_src[2 * m2 + 1:2 * m2 + 2, blk:blk + TILE]
        pieces.append(jnp.where(half, a, b))
    o_ref[u] = jnp.concatenate(pieces, axis=1)


def _attn_sample(page_table, q_s, k_s, v_s, bias, cache_k, cache_v):
    ns, n_pages = page_table.shape
    kvw = N_KV_HEADS * HEAD_DIM
    qw = N_HEADS * HEAD_DIM
    assert ns % IDX_SEQS == 0
    page = lambda u, p: pl.BlockSpec((1, kvw, TILE), lambda s, pt: (pt[IDX_SEQS * s + u, p], 0, 0))
    pages = [page(u, p) for u in range(IDX_SEQS) for p in range(n_pages)]
    some = lambda s, pt: (s, 0, 0)
    grid_spec = pltpu.PrefetchScalarGridSpec(
        num_scalar_prefetch=1, grid=(ns // IDX_SEQS,),
        in_specs=[pl.BlockSpec((IDX_SEQS, 1, qw), some), pl.BlockSpec((IDX_SEQS, 1, kvw), some),
                  pl.BlockSpec((IDX_SEQS, 1, kvw), some),
                  pl.BlockSpec((n_pages + 1, IDX_SEQS, 1, TILE), lambda s, pt: (0, s, 0, 0))]
        + pages * 2,
        out_specs=pl.BlockSpec((IDX_SEQS, 1, qw), some))
    return pl.pallas_call(
        functools.partial(_attn_sample_kernel, n_pages=n_pages),
        grid_spec=grid_spec,
        out_shape=jax.ShapeDtypeStruct((ns, 1, qw), F32),
        compiler_params=_cparams(("arbitrary",)),
        name="attn_sample",
    )(page_table, q_s, k_s, v_s, bias, *([cache_k] * (IDX_SEQS * n_pages)), *([cache_v] * (IDX_SEQS * n_pages)))


def _ssd_sample_kernel(xbc_ref, dtr_ref, sc_ref, h0_ref, cw_ref, cb_ref, dtb_ref, alog_ref, dsk_ref,
                       e_ref, y_ref, h_ref, conv_ref, *, n_heads, sb):
    d_inner = n_heads * SSD_HEAD_DIM
    hpg = n_heads // SSD_GROUPS
    new = xbc_ref[...]
    cdim = new.shape[1]
    u = cb_ref[...] + cw_ref[CONV_W - 1:CONV_W, :] * new
    for k in range(CONV_W - 1):
        u = u + cw_ref[k:k + 1, :] * sc_ref[:, k * cdim:(k + 1) * cdim]
    for k in range(CONV_W - 2):
        conv_ref[:, k * cdim:(k + 1) * cdim] = sc_ref[:, (k + 1) * cdim:(k + 2) * cdim]
    conv_ref[:, (CONV_W - 2) * cdim:] = new
    act = _silu(u)
    xs = act[:, :d_inner]
    bm = act[:, d_inner:d_inner + SSD_GROUPS * D_STATE]
    cm = act[:, d_inner + SSD_GROUPS * D_STATE:]
    dt = _softplus(dtr_ref[...] + dtb_ref[...])
    decay = jnp.exp(dt * (-jnp.exp(alog_ref[...])))
    dec_x = _expand_heads(decay, e_ref)
    xdt = xs * _expand_heads(dt, e_ref)
    for i in range(sb):
        bcols, ccols = [], []
        for g in range(SSD_GROUPS):
            brow = bm[i:i + 1, g * D_STATE:(g + 1) * D_STATE]
            crow = cm[i:i + 1, g * D_STATE:(g + 1) * D_STATE]
            bcols.append(jnp.broadcast_to(brow, (TILE, D_STATE)).T)
            ccols.append(jnp.broadcast_to(crow, (TILE, D_STATE)).T)
        pieces = []
        for pr in range(n_heads // 2):
            g = (2 * pr) // hpg
            sl = slice(pr * TILE, (pr + 1) * TILE)
            ht = h0_ref[i, sl, :].T
            hn = ht * dec_x[i:i + 1, sl] + bcols[g] * xdt[i:i + 1, sl]
            pieces.append(jnp.sum(hn * ccols[g], axis=0, keepdims=True))
            h_ref[i, sl, :] = hn.T
        y_ref[i:i + 1, :] = jnp.concatenate(pieces, axis=1) + dsk_ref[...] * xs[i:i + 1, :]


def _ssd_sample(xbc_s, dtr_s, state_conv, state_ssm, cw, cb, dtb, alog, dsk, e_mat, n_heads, sb):
    ns, cdim = xbc_s.shape
    d_inner = n_heads * SSD_HEAD_DIM
    row = lambda i: (i, 0)
    row3 = lambda i: (i, 0, 0)
    fix = lambda i: (0, 0)
    return pl.pallas_call(
        functools.partial(_ssd_sample_kernel, n_heads=n_heads, sb=sb),
        grid=(ns // sb,),
        in_specs=[pl.BlockSpec((sb, cdim), row), pl.BlockSpec((sb, TILE), row),
                  pl.BlockSpec((sb, (CONV_W - 1) * cdim), row), pl.BlockSpec((sb, d_inner, D_STATE), row3),
                  pl.BlockSpec(cw.shape, fix), pl.BlockSpec(cb.shape, fix), pl.BlockSpec(dtb.shape, fix),
                  pl.BlockSpec(alog.shape, fix), pl.BlockSpec(dsk.shape, fix), pl.BlockSpec(e_mat.shape, fix)],
        out_specs=[pl.BlockSpec((sb, d_inner), row), pl.BlockSpec((sb, d_inner, D_STATE), row3),
                   pl.BlockSpec((sb, (CONV_W - 1) * cdim), row)],
        out_shape=[jax.ShapeDtypeStruct((ns, d_inner), F32),
                   jax.ShapeDtypeStruct((ns, d_inner, D_STATE), F32),
                   jax.ShapeDtypeStruct((ns, (CONV_W - 1) * cdim), F32)],
        compiler_params=_cparams(("arbitrary",)),
        name="ssd_sample",
    )(xbc_s, dtr_s, state_conv, state_ssm, cw, cb, dtb, alog, dsk, e_mat)


def _sample_acts_kernel(a_ref, y_ref, zs_ref, gn_ref, attn_ref, ssd_ref):
    attn_ref[...] = a_ref[...].astype(BF16)
    y = y_ref[...] * zs_ref[...].astype(F32)
    gn_w = y.shape[1] // SSD_GROUPS
    outs = []
    for g in range(SSD_GROUPS):
        yg = y[:, g * gn_w:(g + 1) * gn_w]
        outs.append(yg * lax.rsqrt(jnp.mean(yg * yg, axis=-1, keepdims=True) + EPS))
    ssd_ref[...] = (jnp.concatenate(outs, axis=1) * gn_ref[...]).astype(BF16)


def _sample_acts(attn_s, y_s, zs, gn, blk):
    ns = attn_s.shape[0]
    fix = lambda i: (0, 0)
    return pl.pallas_call(
        _sample_acts_kernel,
        grid=(1,),
        in_specs=[pl.BlockSpec(attn_s.shape, fix), pl.BlockSpec(y_s.shape, fix),
                  pl.BlockSpec((ns, zs.shape[1]), lambda i: (blk, 0)), pl.BlockSpec(gn.shape, fix)],
        out_specs=[pl.BlockSpec(attn_s.shape, fix), pl.BlockSpec(y_s.shape, fix)],
        out_shape=[jax.ShapeDtypeStruct(attn_s.shape, BF16), jax.ShapeDtypeStruct(y_s.shape, BF16)],
        compiler_params=_cparams(("arbitrary",)),
        name="sample_acts",
    )(attn_s, y_s, zs, gn)


def _mix_kernel(x_ref, a_ref, s_ref, gt_ref, wao_ref, wso_ref, wo_ref, o_ref):
    d = x_ref.shape[1]
    gt = gt_ref[...].astype(F32)
    ao = jnp.dot(a_ref[...], wao_ref[...], preferred_element_type=F32)
    so = jnp.dot(s_ref[...], wso_ref[...], preferred_element_type=F32)
    mixed = (gt[:, :d] * ao + gt[:, d:] * so).astype(BF16)
    o_ref[...] = x_ref[...] + jnp.dot(mixed, wo_ref[...], preferred_element_type=F32)


def _mix(x1, attn, ssd, gates, wao, wso, wo, tm, rows, act_row0):
    d = x1.shape[1]
    r0, r = rows
    blk0 = r0 // tm
    act0 = act_row0 // tm
    row = lambda i: (blk0 + i, 0)
    act = lambda i: (act0 + i, 0)
    return pl.pallas_call(
        _mix_kernel,
        grid=(r // tm,),
        in_specs=[pl.BlockSpec((tm, d), row), pl.BlockSpec((tm, attn.shape[1]), act),
                  pl.BlockSpec((tm, ssd.shape[1]), act), pl.BlockSpec((tm, gates.shape[1]), row),
                  _const_spec(wao.shape), _const_spec(wso.shape), _const_spec(wo.shape)],
        out_specs=pl.BlockSpec((tm, d), lambda i: (i, 0)),
        out_shape=jax.ShapeDtypeStruct((r, d), F32),
        compiler_params=_cparams(("arbitrary",)),
        name="mix",
    )(x1, attn, ssd, gates, wao, wso, wo)


def _split_weight_kernel(wt_ref, hi_ref, lo_ref, *, n_cols):
    blk = wt_ref.shape[0]
    row = pl.program_id(0) * blk + lax.broadcasted_iota(I32, (blk, 1), 0)
    w = jnp.where(row < n_cols, wt_ref[...], 0.0)
    hi, lo = _split_bf16(w.T)
    hi_ref[...] = hi
    lo_ref[...] = lo


def _split_weight(wt):
    n_cols, d = wt.shape
    blk = 4 * TILE
    n_blk = pl.cdiv(n_cols, blk)
    return pl.pallas_call(
        functools.partial(_split_weight_kernel, n_cols=n_cols),
        grid=(n_blk,),
        in_specs=[pl.BlockSpec((blk, d), lambda i: (i, 0))],
        out_specs=[pl.BlockSpec((d, blk), lambda i: (0, i)), pl.BlockSpec((d, blk), lambda i: (0, i))],
        out_shape=[jax.ShapeDtypeStruct((d, n_blk * blk), BF16), jax.ShapeDtypeStruct((d, n_blk * blk), BF16)],
        compiler_params=_cparams(("arbitrary",)),
        name="split_weight",
    )(wt)


def _prompt_cache_kernel(k_ref, v_ref, kiw_ref, km_ref, vm_ref, kiwm_ref, kt_ref, vt_ref, kit_ref):
    n_tiles = k_ref.shape[0] // TILE
    for src, meta, dst, width in ((k_ref, km_ref, kt_ref, k_ref.shape[1]), (v_ref, vm_ref, vt_ref, v_ref.shape[1]),
                                  (kiw_ref, kiwm_ref, kit_ref, IDX_DIM)):
        dst[0, :, 0:N_META] = meta[...].T[:width, PAD_FRONT:]
        for j in range(n_tiles):
            dst[0, :, N_META + j * TILE:N_META + (j + 1) * TILE] = src[j * TILE:(j + 1) * TILE, :].T[:width, :]


def _prompt_cache(k, v, kiw, n_batch, seq, meta_blk):
    l_seq = seq + N_META
    kvw = k.shape[1]
    toks = lambda b: (b, 0)
    meta = lambda b: (meta_blk, 0)
    out = lambda b: (b, 0, 0)
    return pl.pallas_call(
        _prompt_cache_kernel,
        grid=(n_batch,),
        in_specs=[pl.BlockSpec((seq, kvw), toks), pl.BlockSpec((seq, kvw), toks), pl.BlockSpec((seq, TILE), toks),
                  pl.BlockSpec((TILE, kvw), meta), pl.BlockSpec((TILE, kvw), meta), pl.BlockSpec((TILE, TILE), meta)],
        out_specs=[pl.BlockSpec((1, kvw, l_seq), out), pl.BlockSpec((1, kvw, l_seq), out),
                   pl.BlockSpec((1, IDX_DIM, l_seq), out)],
        out_shape=[jax.ShapeDtypeStruct((n_batch, kvw, l_seq), F32), jax.ShapeDtypeStruct((n_batch, kvw, l_seq), F32),
                   jax.ShapeDtypeStruct((n_batch, IDX_DIM, l_seq), F32)],
        compiler_params=_cparams(("arbitrary",)),
        name="prompt_cache",
    )(k, v, kiw, k, v, kiw)


def kernel(x_prompt, x_sample, cache_k, cache_v, cache_kidx, state_ssm, state_conv, page_table, meta_tokens,
           g_ffn1, w_ffn1_gate, w_ffn1_up, w_ffn1_down, g_mix, w_in, g_q, g_k, conv_w, conv_b, dt_bias, a_log,
           d_skip, g_ssd_norm, w_attn_out, w_ssd_out, w_o, g_ffn2, w_ffn2_gate, w_ffn2_up, w_ffn2_down):
    depth = w_in.shape[0]
    assert depth == 1, "single-layer step"
    n_batch, seq, d_model = x_prompt.shape
    ns = x_sample.shape[0]
    assert x_sample.shape[1] == 1 and seq % TILE == 0 and ns == TILE
    page = cache_k.shape[2]
    assert page == TILE
    l_seq = seq + N_META
    r_tok = n_batch * seq
    r_total = r_tok + ns + TILE
    meta_blk = (r_tok + ns) // TILE
    n_heads = state_ssm.shape[2]
    d_inner = n_heads * SSD_HEAD_DIM
    assert n_heads <= TILE and n_heads % (2 * SSD_GROUPS) == 0
    cdim = d_inner + 2 * SSD_GROUPS * D_STATE
    nq, nkv = N_HEADS * HEAD_DIM, N_KV_HEADS * HEAD_DIM
    nqi = IDX_HEADS * IDX_DIM
    tm = _row_tile(r_total // TILE, 6)
    tm_tok = _row_tile(r_tok // TILE, 8)

    w_hi, w_lo = _split_weight(jnp.transpose(w_in[0]))
    offs = [0]
    for n in (nq, nkv, nkv, nqi, IDX_DIM, IDX_HEADS, d_inner, cdim, n_heads, d_model, d_model):
        offs.append(offs[-1] + n)
    col = lambda i, w=w_hi: w[:, offs[i]:offs[i + 1]]
    wqk = jnp.concatenate([col(0), col(1)], axis=1)
    wv = col(2)
    idx_cols = lambda w: jnp.concatenate([col(3, w), col(4, w), col(5, w),
                                          jnp.zeros((d_model, TILE - IDX_DIM - IDX_HEADS), BF16)], axis=1)
    wih, wil = idx_cols(w_hi), idx_cols(w_lo)
    wz = col(6)
    wx = col(7)
    wdt = jnp.concatenate([col(8), jnp.zeros((d_model, TILE - n_heads), BF16)], axis=1)
    wgate = jnp.concatenate([col(9), col(10)], axis=1)
    n_seg = (nq + nkv) // HEAD_DIM
    seg = (jnp.arange(nq + nkv)[:, None] // HEAD_DIM == jnp.arange(TILE)[None, :]).astype(BF16)
    segt = seg.T
    del n_seg
    gqk = jnp.concatenate([jnp.tile(g_q[0], N_HEADS), jnp.tile(g_k[0], N_KV_HEADS)])[None, :]
    row1 = lambda v: v.reshape(1, -1)
    pad_heads = lambda v, fill: jnp.concatenate([v, jnp.full((TILE - n_heads,), fill, F32)])[None, :]
    dtb = pad_heads(dt_bias[0], -1e4)
    alog = pad_heads(a_log[0], 0.0)
    dsk = jnp.repeat(d_skip[0], SSD_HEAD_DIM)[None, :]
    assert 3 * n_heads <= TILE
    e_row = jnp.arange(TILE)[:, None]
    e_mat = ((e_row < 3 * n_heads) & (e_row % n_heads == jnp.arange(d_inner)[None, :] // SSD_HEAD_DIM)).astype(BF16)
    gn = row1(g_ssd_norm[0])

    x_all = jnp.concatenate([x_prompt.reshape(r_tok, d_model), x_sample.reshape(ns, d_model),
                             jnp.zeros((PAD_FRONT, d_model), F32), meta_tokens.astype(F32)], axis=0)
    x1, hb = _ffn(x_all, row1(g_ffn1[0]), row1(g_mix[0]), w_ffn1_gate[0].astype(BF16),
                  w_ffn1_up[0].astype(BF16), w_ffn1_down[0].astype(BF16), tm)
    q, k, v, kb, vb, qi, kiw = _proj_attn(x1, row1(g_mix[0]), wqk, wv, wih, wil, seg, segt, gqk, tm)
    xbc, dtr = _proj_xbc(hb, wx, wdt, tm)
    zs, gates = _proj_gate(hb, wz, wgate, tm)

    ksel_p = min(TOP_K_MAX, l_seq // 4)
    attn = _dsa_prompt(q, qi, kiw, kb, vb, n_batch, seq, meta_blk, ksel_p)
    ssd, ssm_p, conv_p = _ssd_prompt(xbc, dtr, zs, conv_w[0], row1(conv_b[0]), dtb, alog, dsk, gn, e_mat,
                                     n_batch, seq, meta_blk, n_heads)

    n_pages = page_table.shape[1]
    ksel_s = min(TOP_K_MAX, (n_pages * page + 1) // 4)
    smp = lambda a: a[r_tok:r_tok + ns]
    kidx_t = jnp.transpose(cache_kidx[0], (0, 2, 1))
    scores = _idx_sample(page_table, smp(qi).reshape(ns, 1, nqi), smp(kiw).reshape(ns, 1, TILE), kidx_t)
    bias = _select_sample(scores.reshape(n_pages + 1, ns, TILE), ksel_s).reshape(n_pages + 1, ns, 1, TILE)
    ck = jnp.transpose(cache_k[0], (0, 2, 3, 1)).reshape(cache_k.shape[1], nkv, page)
    cv = jnp.transpose(cache_v[0], (0, 2, 3, 1)).reshape(cache_v.shape[1], nkv, page)
    attn_s = _attn_sample(page_table, smp(q).reshape(ns, 1, nq), smp(k).reshape(ns, 1, nkv),
                          smp(v).reshape(ns, 1, nkv), bias, ck, cv)
    y_s, ssm_s, conv_s = _ssd_sample(smp(xbc), smp(dtr), state_conv[0].reshape(ns, -1), state_ssm[0].reshape(ns, d_inner, D_STATE),
                                     conv_w[0], row1(conv_b[0]), dtb, alog, dsk, e_mat, n_heads, 8)
    attn_sb, ssd_sb = _sample_acts(attn_s.reshape(ns, nq), y_s, zs, gn, r_tok // ns)

    wao, wso, wo = w_attn_out[0].astype(BF16), w_ssd_out[0].astype(BF16), w_o[0].astype(BF16)
    ffn2 = (row1(g_ffn2[0]), row1(g_ffn2[0]), w_ffn2_gate[0].astype(BF16), w_ffn2_up[0].astype(BF16),
            w_ffn2_down[0].astype(BF16))
    x2 = _mix(x1, attn, ssd, gates, wao, wso, wo, tm_tok, (0, r_tok), 0)
    y_prompt = _ffn(x2, *ffn2, tm_tok, with_norm=False)[0].reshape(n_batch, seq, d_model)
    x2_s = _mix(x1, attn_sb, ssd_sb, gates, wao, wso, wo, ns, (r_tok, ns), 0)
    y_sample = _ffn(x2_s, *ffn2, ns, with_norm=False)[0].reshape(ns, 1, d_model)

    kt, vt, kit = _prompt_cache(k, v, kiw, n_batch, seq, meta_blk)
    heads_last = lambda a: jnp.transpose(a.reshape(1, n_batch, N_KV_HEADS, HEAD_DIM, l_seq), (0, 1, 4, 2, 3))
    k_prompt = heads_last(kt)
    v_prompt = heads_last(vt)
    kidx_prompt = jnp.transpose(kit, (0, 2, 1))[None]
    ssm_prompt = ssm_p.reshape(1, n_batch, n_heads, SSD_HEAD_DIM, D_STATE)
    conv_prompt = conv_p.reshape(1, n_batch, CONV_W - 1, cdim)
    k_sample = smp(k).reshape(1, ns, 1, N_KV_HEADS, HEAD_DIM)
    v_sample = smp(v).reshape(1, ns, 1, N_KV_HEADS, HEAD_DIM)
    kidx_sample = smp(kiw)[:, :IDX_DIM].reshape(1, ns, 1, IDX_DIM)
    ssm_sample = ssm_s.reshape(1, ns, n_heads, SSD_HEAD_DIM, D_STATE)
    conv_sample = conv_s.reshape(1, ns, CONV_W - 1, cdim)
    return (y_prompt, y_sample, k_prompt, v_prompt, kidx_prompt, ssm_prompt, conv_prompt,
            k_sample, v_sample, kidx_sample, ssm_sample, conv_sample)
```
